```python
import math
import jax, jax.numpy as jnp
from jax import lax
import numpy as np

D_MODEL = 1024
BATCH = 8
SEQ = 2048
DEPTH = 2

GRID_W = 64
CTX_LEN = 256
BRANCH_W = D_MODEL
CHUNK = 128
EPS = 1e-6

SSD_INNER = BRANCH_W
SSD_HEADDIM = 64
SSD_HEADS = SSD_INNER // SSD_HEADDIM
SSD_STATE = 128
SSD_GROUPS = 2
SSD_CONV = 5
SSD_CONV_CH = SSD_INNER + 2 * SSD_GROUPS * SSD_STATE
SSD_COLS = SSD_CONV_CH + SSD_INNER + 2 * SSD_HEADS

ML_INNER = BRANCH_W
ML_HEADS = 8
ML_HEADDIM = ML_INNER // ML_HEADS
ML_CONV = 5
ML_COLS = 4 * ML_INNER + 4 * ML_HEADS

HY_INNER = BRANCH_W
HY_ORDER = 2
HY_SHORT = 3
HY_BANDS = 16
HY_FEAT = 1 + 2 * HY_BANDS
HY_FFN = 64
HY_COLS = (HY_ORDER + 1) * HY_INNER

N_BRANCH = 3
REC_COLS = SSD_COLS + ML_COLS
IN_COLS = REC_COLS + HY_COLS + N_BRANCH * D_MODEL
MLP_HIDDEN = 4 * D_MODEL

kernel_name = 'hybrid_ssd_mlstm_hyena_dit_block'


def rmsnorm(x, w):
    xf = x.astype(jnp.float32)
    xf = xf * lax.rsqrt(jnp.mean(xf * xf, axis=-1, keepdims=True) + EPS)
    return (xf * w.astype(jnp.float32)).astype(x.dtype)


def modulate(h, shift, scale):
    return h * (1.0 + scale) + shift


def dwconv(u, w, b):
    k = w.shape[0]
    y = lax.conv_general_dilated(u, w[:, None, :].astype(u.dtype), (1,), [(k // 2, k // 2)],
                                 dimension_numbers=('NWC', 'WIO', 'NWC'),
                                 feature_group_count=u.shape[-1])
    return y + b.astype(u.dtype)


def to_colmajor(t, rows):
    b, L = t.shape[:2]
    rest = t.shape[2:]
    return t.reshape(b, rows, GRID_W, *rest).swapaxes(1, 2).reshape(b, L, *rest)


def from_colmajor(t, rows):
    b, L = t.shape[:2]
    rest = t.shape[2:]
    return t.reshape(b, GRID_W, rows, *rest).swapaxes(1, 2).reshape(b, L, *rest)


def chunked_scan(q, k, v, log_a, log_w, s0, m0, need_y=True):
    f32 = jnp.float32
    bsz, L, H, dk = q.shape
    dv = v.shape[-1]
    nc = L // CHUNK
    q = q.astype(f32).reshape(bsz, nc, CHUNK, H, dk)
    k = k.astype(f32).reshape(bsz, nc, CHUNK, H, dk)
    v = v.astype(f32).reshape(bsz, nc, CHUNK, H, dv)
    cum = jnp.cumsum(log_a.astype(f32).reshape(bsz, nc, CHUNK, H), axis=2).transpose(0, 1, 3, 2)
    w = log_w.astype(f32).reshape(bsz, nc, CHUNK, H).transpose(0, 1, 3, 2)
    last = cum[..., -1]
    g = last[..., None] - cum + w
    m_loc = jnp.max(g, axis=-1)
    local = jnp.einsum('bchs,bcshv,bcshk->bchvk', jnp.exp(g - m_loc[..., None]), v, k)

    def step(carry, inp):
        s, m = carry
        a, ml, loc = inp
        m_new = jnp.maximum(a + m, ml)
        s_new = jnp.exp(a + m - m_new)[..., None, None] * s + jnp.exp(ml - m_new)[..., None, None] * loc
        return (s_new, m_new), (s, m)

    (s_fin, m_fin), (s_prev, m_prev) = lax.scan(
        step, (s0.astype(f32), m0.astype(f32)),
        (last.transpose(1, 0, 2), m_loc.transpose(1, 0, 2), local.transpose(1, 0, 2, 3, 4)))
    if not need_y:
        return None, None, (s_fin, m_fin)
    s_prev = s_prev.transpose(1, 0, 2, 3, 4)
    m_prev = m_prev.transpose(1, 0, 2)
    inter = cum + m_prev[..., None]
    causal = jnp.tril(jnp.ones((CHUNK, CHUNK), dtype=bool))
    dlog = jnp.where(causal, cum[..., :, None] - cum[..., None, :] + w[..., None, :], -jnp.inf)
    m_row = jnp.maximum(inter, jnp.max(dlog, axis=-1))
    p = jnp.exp(dlog - m_row[..., None]) * jnp.einsum('bcthk,bcshk->bchts', q, k)
    y = (jnp.einsum('bchts,bcshv->bcthv', p, v)
         + jnp.einsum('bcthk,bchvk->bcthv', q, s_prev)
         * jnp.exp(inter - m_row).transpose(0, 1, 3, 2)[..., None])
    return (y.reshape(bsz, L, H, dv), m_row.transpose(0, 1, 3, 2).reshape(bsz, L, H), (s_fin, m_fin))


def bidir_scan(q, k, v, log_a, log_w, init_f, init_b, need_y=True):
    fl = lambda t: jnp.flip(t, axis=1)
    fwd = chunked_scan(q, k, v, log_a[:, :, 0], log_w[:, :, 0], *init_f, need_y=need_y)
    bwd = chunked_scan(fl(q), fl(k), fl(v), fl(log_a[:, :, 1]), fl(log_w[:, :, 1]), *init_b, need_y=need_y)
    if need_y:
        bwd = (fl(bwd[0]), fl(bwd[1]), bwd[2])
    return fwd, bwd


def ssd_prep(u, conv_w, conv_b, dt_bias, a_log):
    bsz, L, _ = u.shape
    xbc, z, dt_raw = jnp.split(u, [SSD_CONV_CH, SSD_CONV_CH + SSD_INNER], axis=-1)
    xbc = jax.nn.silu(dwconv(xbc, conv_w, conv_b))
    xs, bm, cm = jnp.split(xbc, [SSD_INNER, SSD_INNER + SSD_GROUPS * SSD_STATE], axis=-1)
    xs = xs.reshape(bsz, L, SSD_HEADS, SSD_HEADDIM)
    rep = SSD_HEADS // SSD_GROUPS
    bm = jnp.repeat(bm.reshape(bsz, L, SSD_GROUPS, SSD_STATE), rep, axis=2)
    cm = jnp.repeat(cm.reshape(bsz, L, SSD_GROUPS, SSD_STATE), rep, axis=2)
    dt = jax.nn.softplus(dt_raw.astype(jnp.float32).reshape(bsz, L, 2, SSD_HEADS) + dt_bias.astype(jnp.float32))
    log_a = -dt * jnp.exp(a_log.astype(jnp.float32))
    return xs, bm, cm, z, dt, log_a


def ssd_out(xs, z, fwd, bwd, d_skip):
    bsz, L = xs.shape[:2]
    y = (fwd[0] * jnp.exp(fwd[1])[..., None] + bwd[0] * jnp.exp(bwd[1])[..., None]
         + d_skip.astype(jnp.float32)[:, None] * xs.astype(jnp.float32))
    return y.reshape(bsz, L, SSD_INNER).astype(z.dtype) * jax.nn.silu(z)


def ssd_branch(u_ctx, u_lat, need_ctx, conv_w, conv_b, dt_bias, a_log, d_skip, norm_w):
    bsz = u_lat.shape[0]
    zero = (jnp.zeros((bsz, SSD_HEADS, SSD_HEADDIM, SSD_STATE), jnp.float32),
            jnp.zeros((bsz, SSD_HEADS), jnp.float32))
    xc, bc, cc, zc, dtc, lac = ssd_prep(u_ctx, conv_w, conv_b, dt_bias, a_log)
    fc, bwc = bidir_scan(cc, bc, xc, lac, jnp.log(dtc), zero, zero, need_y=need_ctx)
    y_ctx = rmsnorm(ssd_out(xc, zc, fc, bwc, d_skip), norm_w) if need_ctx else None
    xl, bl, cl, zl, dtl, lal = ssd_prep(u_lat, conv_w, conv_b, dt_bias, a_log)
    fw, bw = bidir_scan(cl, bl, xl, lal, jnp.log(dtl), fc[2], bwc[2])
    return y_ctx, rmsnorm(ssd_out(xl, zl, fw, bw, d_skip), norm_w)


def ml_prep(u, conv_w, conv_b, gate_b):
    bsz, L, _ = u.shape
    qk, vv, o, gates = jnp.split(u, [2 * ML_INNER, 3 * ML_INNER, 4 * ML_INNER], axis=-1)
    qk = jax.nn.silu(dwconv(qk, conv_w, conv_b))
    q, k = jnp.split(qk, 2, axis=-1)
    q = q.reshape(bsz, L, ML_HEADS, ML_HEADDIM)
    k = k.reshape(bsz, L, ML_HEADS, ML_HEADDIM) * (ML_HEADDIM ** -0.5)
    vv = vv.reshape(bsz, L, ML_HEADS, ML_HEADDIM)
    v_aug = jnp.concatenate([vv, jnp.ones_like(vv[..., :1])], axis=-1)
    gates = gates.astype(jnp.float32).reshape(bsz, L, 2, 2, ML_HEADS) + gate_b.astype(jnp.float32)
    log_w = gates[:, :, :, 0]
    log_a = jax.nn.log_sigmoid(gates[:, :, :, 1])
    return q, k, v_aug, o, log_a, log_w


def ml_out(o, fwd, bwd, norm_w):
    bsz, L = o.shape[:2]

    def cell(r):
        y, m, _ = r
        return y[..., :ML_HEADDIM] / jnp.maximum(jnp.abs(y[..., ML_HEADDIM:]), jnp.exp(-m)[..., None])

    h = cell(fwd) + cell(bwd)
    h = rmsnorm(h, norm_w.reshape(ML_HEADS, ML_HEADDIM)).reshape(bsz, L, ML_INNER)
    return (jax.nn.sigmoid(o.astype(jnp.float32)) * h).astype(o.dtype)


def ml_branch(u_ctx, u_lat, rows, need_ctx, conv_w, conv_b, gate_b, norm_w):
    bsz = u_lat.shape[0]
    zero = (jnp.zeros((bsz, ML_HEADS, ML_HEADDIM + 1, ML_HEADDIM), jnp.float32),
            jnp.zeros((bsz, ML_HEADS), jnp.float32))
    qc, kc, vc, oc, lac, lwc = ml_prep(u_ctx, conv_w, conv_b, gate_b)
    fc, bwc = bidir_scan(qc, kc, vc, lac, lwc, zero, zero, need_y=need_ctx)
    y_ctx = ml_out(oc, fc, bwc, norm_w) if need_ctx else None
    ql, kl, vl, ol, lal, lwl = ml_prep(to_colmajor(u_lat, rows), conv_w, conv_b, gate_b)
    fw, bw = bidir_scan(ql, kl, vl, lal, lwl, fc[2], bwc[2])
    return y_ctx, from_colmajor(ml_out(ol, fw, bw, norm_w), rows)


def hyena_filters(L, w1, b1, w2, b2, w3, decay):
    f32 = jnp.float32
    t = jnp.arange(L, dtype=f32)
    t_norm = t / L
    bands = jnp.linspace(1e-4, HY_BANDS - 1, HY_BANDS, dtype=f32)
    ang = (2.0 * math.pi / L) * t[:, None] * bands[None, :]
    feats = jnp.concatenate([t_norm[:, None], jnp.cos(ang), -jnp.sin(ang)], axis=-1)
    hid = jnp.sin(feats @ w1.astype(f32) + b1.astype(f32))
    hid = jnp.sin(hid @ w2.astype(f32) + b2.astype(f32))
    h = (hid @ w3.astype(f32)).reshape(L, HY_ORDER, 2, HY_INNER)
    h = h * jnp.exp(-t_norm[:, None, None, None] * jnp.abs(decay.astype(f32)))
    h_f, h_b = h[:, :, 0], h[:, :, 1]
    filt = jnp.concatenate([h_f.at[0].add(h_b[0]), jnp.zeros((1, HY_ORDER, HY_INNER), f32),
                            jnp.flip(h_b[1:], axis=0)], axis=0)
    return jnp.fft.rfft(filt, axis=0)


def fftconv(u, filt_f):
    L = u.shape[1]
    uf = jnp.fft.rfft(u.astype(jnp.float32), n=2 * L, axis=1)
    return jnp.fft.irfft(uf * filt_f[None], n=2 * L, axis=1)[:, :L]


def hyena_mix(u, conv_w, conv_b, w1, b1, w2, b2, w3, decay, skip):
    L = u.shape[1]
    filt_f = hyena_filters(L, w1, b1, w2, b2, w3, decay)
    z, x1, x2 = jnp.split(dwconv(u, conv_w, conv_b).astype(jnp.float32), 3, axis=-1)
    skip = skip.astype(jnp.float32)
    for n, gate in enumerate((x1, x2)):
        z = gate * (fftconv(z, filt_f[:, n]) + skip[n] * z)
    return z.astype(u.dtype)


def merge(g, ys, w_branch, w_out):
    bsz, L, _ = g.shape
    gate = jax.nn.sigmoid(g.reshape(bsz, L, N_BRANCH, D_MODEL))
    p = jnp.einsum('blnw,nwd->blnd', jnp.stack(ys, axis=2), w_branch)
    return jnp.sum(gate * p, axis=2) @ w_out


def token_mixer(h_ctx, h_lat, rows, need_ctx, w_in, ssd_p, ml_p, hy_p, w_branch, w_out):
    p_lat = h_lat @ w_in
    p_ctx = h_ctx @ (w_in if need_ctx else w_in[:, :REC_COLS])
    ssd_l, ml_l, hy_l, g_l = jnp.split(p_lat, [SSD_COLS, REC_COLS, REC_COLS + HY_COLS], axis=-1)
    ys_c, ys_l = ssd_branch(p_ctx[..., :SSD_COLS], ssd_l, need_ctx, *ssd_p)
    ym_c, ym_l = ml_branch(p_ctx[..., SSD_COLS:REC_COLS], ml_l, rows, need_ctx, *ml_p)
    yh_l = hyena_mix(hy_l, *hy_p)
    out_lat = merge(g_l, (ys_l, ym_l, yh_l), w_branch, w_out)
    if need_ctx:
        yh_c = hyena_mix(p_ctx[..., REC_COLS:REC_COLS + HY_COLS], *hy_p)
        out_ctx = merge(p_ctx[..., REC_COLS + HY_COLS:], (ys_c, ym_c, yh_c), w_branch, w_out)
    else:
        out_ctx = None
    return out_ctx, out_lat


def sq_relu_mlp(h, w1, w2):
    return jnp.square(jax.nn.relu(h @ w1)) @ w2


def setup_inputs(seed: int = 0) -> dict:
    key = jax.random.key(seed)
    ks = iter(jax.random.split(key, 48))
    f32 = jnp.float32

    def nrm(shape, scale):
        return scale * jax.random.normal(next(ks), shape, f32)

    def unif(shape, lo, hi):
        return jax.random.uniform(next(ks), shape, f32, lo, hi)

    D = D_MODEL
    x = nrm((BATCH, SEQ, D), 1.0)
    c = nrm((BATCH, D), 1.0)
    ctx = nrm((BATCH, CTX_LEN, D), 1.0)
    c_ctx = nrm((D,), 1.0)
    norm1_w = 1.0 + nrm((DEPTH, D), 0.02)
    mod_w = nrm((DEPTH, D, 6 * D), 0.5 * D ** -0.5)
    mod_b = nrm((DEPTH, 6 * D), 0.01)
    w_in = nrm((DEPTH, D, IN_COLS), D ** -0.5)
    ssd_conv_w = nrm((DEPTH, SSD_CONV, SSD_CONV_CH), SSD_CONV ** -0.5)
    ssd_conv_b = nrm((DEPTH, SSD_CONV_CH), 0.01)
    dt0 = jnp.exp(unif((DEPTH, 2, SSD_HEADS), math.log(1e-3), math.log(1e-1)))
    ssd_dt_bias = dt0 + jnp.log(-jnp.expm1(-dt0))
    ssd_a_log = jnp.log(unif((DEPTH, 2, SSD_HEADS), 1.0, 16.0))
    ssd_d = 1.0 + nrm((DEPTH, SSD_HEADS), 0.1)
    ssd_norm_w = 1.0 + nrm((DEPTH, SSD_INNER), 0.02)
    ml_conv_w = nrm((DEPTH, ML_CONV, 2 * ML_INNER), ML_CONV ** -0.5)
    ml_conv_b = nrm((DEPTH, 2 * ML_INNER), 0.01)
    ml_gate_b = nrm((DEPTH, 2, 2, ML_HEADS), 0.1).at[:, :, 1].add(jnp.linspace(3.0, 6.0, ML_HEADS, dtype=f32))
    ml_norm_w = 1.0 + nrm((DEPTH, ML_INNER), 0.02)
    hy_conv_w = nrm((DEPTH, HY_SHORT, HY_COLS), HY_SHORT ** -0.5)
    hy_conv_b = nrm((DEPTH, HY_COLS), 0.01)
    hy_ffn_w1 = nrm((DEPTH, HY_FEAT, HY_FFN), HY_FEAT ** -0.5)
    hy_ffn_b1 = nrm((DEPTH, HY_FFN), 0.1)
    hy_ffn_w2 = nrm((DEPTH, HY_FFN, HY_FFN), HY_FFN ** -0.5)
    hy_ffn_b2 = nrm((DEPTH, HY_FFN), 0.1)
    hy_ffn_w3 = nrm((DEPTH, HY_FFN, HY_ORDER * 2 * HY_INNER), 0.1 * HY_FFN ** -0.5)
    hy_decay = unif((DEPTH, HY_ORDER, 2, HY_INNER), 3.0, 15.0)
    hy_skip = nrm((DEPTH, HY_ORDER, HY_INNER), 1.0)
    w_branch = nrm((DEPTH, N_BRANCH, BRANCH_W, D), BRANCH_W ** -0.5)
    w_out = nrm((DEPTH, D, D), D ** -0.5)
    norm2_w = 1.0 + nrm((DEPTH, D), 0.02)
    mlp_w1 = nrm((DEPTH, D, MLP_HIDDEN), D ** -0.5)
    mlp_w2 = nrm((DEPTH, MLP_HIDDEN, D), MLP_HIDDEN ** -0.5)
    norm_f_w = 1.0 + nrm((D,), 0.02)
    return {'x': x, 'c': c, 'ctx': ctx, 'c_ctx': c_ctx, 'norm1_w': norm1_w, 'mod_w': mod_w,
            'mod_b': mod_b, 'w_in': w_in, 'ssd_conv_w': ssd_conv_w, 'ssd_conv_b': ssd_conv_b,
            'ssd_dt_bias': ssd_dt_bias, 'ssd_a_log': ssd_a_log, 'ssd_d': ssd_d, 'ssd_norm_w': ssd_norm_w,
            'ml_conv_w': ml_conv_w, 'ml_conv_b': ml_conv_b, 'ml_gate_b': ml_gate_b, 'ml_norm_w': ml_norm_w,
            'hy_conv_w': hy_conv_w, 'hy_conv_b': hy_conv_b, 'hy_ffn_w1': hy_ffn_w1, 'hy_ffn_b1': hy_ffn_b1,
            'hy_ffn_w2': hy_ffn_w2, 'hy_ffn_b2': hy_ffn_b2, 'hy_ffn_w3': hy_ffn_w3, 'hy_decay': hy_decay,
            'hy_skip': hy_skip, 'w_branch': w_branch, 'w_out': w_out, 'norm2_w': norm2_w,
            'mlp_w1': mlp_w1, 'mlp_w2': mlp_w2, 'norm_f_w': norm_f_w}


def reference(x, c, ctx, c_ctx, norm1_w, mod_w, mod_b, w_in, ssd_conv_w, ssd_conv_b, ssd_dt_bias,
              ssd_a_log, ssd_d, ssd_norm_w, ml_conv_w, ml_conv_b, ml_gate_b, ml_norm_w, hy_conv_w,
              hy_conv_b, hy_ffn_w1, hy_ffn_b1, hy_ffn_w2, hy_ffn_b2, hy_ffn_w3, hy_decay, hy_skip,
              w_branch, w_out, norm2_w, mlp_w1, mlp_w2, norm_f_w):
    rows = x.shape[1] // GRID_W
    xc = ctx
    for l in range(DEPTH):
        need_ctx = l < DEPTH - 1
        mod_lat = jnp.split((jax.nn.silu(c) @ mod_w[l] + mod_b[l])[:, None, :], 6, axis=-1)
        mod_ctx = jnp.split((jax.nn.silu(c_ctx) @ mod_w[l] + mod_b[l])[None, None, :], 6, axis=-1)
        h_lat = modulate(rmsnorm(x, norm1_w[l]), mod_lat[0], mod_lat[1])
        h_ctx = modulate(rmsnorm(xc, norm1_w[l]), mod_ctx[0], mod_ctx[1])
        o_ctx, o_lat = token_mixer(
            h_ctx, h_lat, rows, need_ctx, w_in[l],
            (ssd_conv_w[l], ssd_conv_b[l], ssd_dt_bias[l], ssd_a_log[l], ssd_d[l], ssd_norm_w[l]),
            (ml_conv_w[l], ml_conv_b[l], ml_gate_b[l], ml_norm_w[l]),
            (hy_conv_w[l], hy_conv_b[l], hy_ffn_w1[l], hy_ffn_b1[l], hy_ffn_w2[l], hy_ffn_b2[l],
             hy_ffn_w3[l], hy_decay[l], hy_skip[l]),
            w_branch[l], w_out[l])
        x = x + mod_lat[2] * o_lat
        x = x + mod_lat[5] * sq_relu_mlp(modulate(rmsnorm(x, norm2_w[l]), mod_lat[3], mod_lat[4]),
                                         mlp_w1[l], mlp_w2[l])
        if need_ctx:
            xc = xc + mod_ctx[2] * o_ctx
            xc = xc + mod_ctx[5] * sq_relu_mlp(modulate(rmsnorm(xc, norm2_w[l]), mod_ctx[3], mod_ctx[4]),
                                               mlp_w1[l], mlp_w2[l])
    return rmsnorm(x, norm_f_w)
```

```python
import functools
import math

import jax
import jax.numpy as jnp
import numpy as np
from jax import lax
from jax.experimental import pallas as pl
from jax.experimental.pallas import tpu as pltpu

F32 = jnp.float32
BF16 = jnp.bfloat16
HIGHEST = lax.Precision.HIGHEST

GRID_W = 64
CHUNK = 128
EPS = 1e-6
SSD_GROUPS = 2
SSD_CONV = 5
ML_HEADDIM = 128
ML_CONV = 5
HY_ORDER = 2
HY_SHORT = 3
HY_BANDS = 16
N_BRANCH = 3

TOKEN_TILE = 256
LANE_TILE = 256
HALO = 16
HY_ROW_BLOCK = 1024
VMEM_LIMIT = 56 * 1024 * 1024

_hdot = functools.partial(jnp.dot, precision=HIGHEST, preferred_element_type=F32)
_dot = functools.partial(jnp.dot, preferred_element_type=F32)


def _params(sem, vmem=None):
    return pltpu.CompilerParams(dimension_semantics=sem, vmem_limit_bytes=vmem or VMEM_LIMIT)


def _softplus(x):
    return jnp.maximum(x, 0.0) + jnp.log(1.0 + jnp.exp(-jnp.abs(x)))


def _silu(x):
    return x * jax.nn.sigmoid(x)


def _mod_kernel(c_ref, w_ref, b_ref, o_ref):
    o_ref[...] = _hdot(_silu(c_ref[...]), w_ref[...]) + b_ref[...]


def _mod_vectors(c_all, mod_w, mod_b, layer):
    r, d = c_all.shape
    n = mod_w.shape[-1]
    tn = n // 6
    return pl.pallas_call(
        _mod_kernel,
        out_shape=jax.ShapeDtypeStruct((r, n), F32),
        grid=(n // tn,),
        in_specs=[pl.BlockSpec((r, d), lambda j: (0, 0)),
                  pl.BlockSpec((None, d, tn), lambda j: (layer, 0, j)),
                  pl.BlockSpec((None, 1, tn), lambda j: (layer, 0, j))],
        out_specs=pl.BlockSpec((r, tn), lambda j: (0, j)),
        compiler_params=_params(("parallel",)),
        name="mod_vectors",
    )(c_all, mod_w, mod_b)


def _normmod_kernel(x_ref, nw_ref, mod_ref, o_ref, *, si):
    x = x_ref[...]
    h = x * lax.rsqrt(jnp.mean(x * x, axis=-1, keepdims=True) + EPS) * nw_ref[...]
    m = mod_ref[...]
    o_ref[...] = (h * (1.0 + m[si + 1:si + 2]) + m[si:si + 1]).astype(o_ref.dtype)


def _mod_row(nb, n_lat_tiles):
    return lambda b, i: (jnp.where(i < n_lat_tiles, b, nb), 0, 0)


def _normmod(x, norm_w, layer, mod, si, n_lat_tiles, n_tiles):
    nb, s, d = x.shape
    return pl.pallas_call(
        functools.partial(_normmod_kernel, si=si),
        out_shape=jax.ShapeDtypeStruct((nb, s, d), BF16),
        grid=(nb, n_tiles),
        in_specs=[pl.BlockSpec((None, TOKEN_TILE, d), lambda b, i: (b, i, 0)),
                  pl.BlockSpec((None, 1, d), lambda b, i: (layer, 0, 0)),
                  pl.BlockSpec((None, 6, d), _mod_row(nb, n_lat_tiles))],
        out_specs=pl.BlockSpec((None, TOKEN_TILE, d), lambda b, i: (b, i, 0)),
        compiler_params=_params(("parallel", "parallel")),
        name="normmod",
    )(x, norm_w, mod)


def _mm_kernel(a_ref, w_ref, o_ref):
    o_ref[...] = _dot(a_ref[...], w_ref[...]).astype(o_ref.dtype)


def _pick(n, cands):
    for c in cands:
        if n % c == 0:
            return c
    return n


def _matmul(a, w, out_dtype, name):
    t, k = a.shape
    n = w.shape[1]
    tm = _pick(t, (1024, 768, 512, 256))
    tn = _pick(n, (512, 256, 128))
    return pl.pallas_call(
        _mm_kernel,
        out_shape=jax.ShapeDtypeStruct((t, n), out_dtype),
        grid=(t // tm, n // tn),
        in_specs=[pl.BlockSpec((tm, k), lambda i, j: (i, 0)),
                  pl.BlockSpec((k, tn), lambda i, j: (0, j))],
        out_specs=pl.BlockSpec((tm, tn), lambda i, j: (i, j)),
        compiler_params=_params(("parallel", "parallel")),
        name=name,
    )(a, w)


def _dwconv_kernel(u_ref, p_ref, n_ref, w_ref, b_ref, s_ref, o_ref, *, taps, act, bounds):
    i = pl.program_id(1)
    tm = u_ref.shape[0]
    pad = taps // 2
    lv, rv = jnp.float32(1.0), jnp.float32(1.0)
    for e in bounds:
        lv = jnp.where(i == e, 0.0, lv)
        rv = jnp.where(i == e - 1, 0.0, rv)
    ext = jnp.concatenate([p_ref[...].astype(F32) * lv, u_ref[...].astype(F32),
                           n_ref[...].astype(F32) * rv], axis=0)
    rows = ext.shape[0]
    w = w_ref[...]
    acc = jnp.zeros((tm, u_ref.shape[1]), F32) + b_ref[...]
    for j in range(taps):
        sh = (pad - j) % rows
        shifted = ext if sh == 0 else pltpu.roll(ext, sh, 0)
        acc = acc + shifted[HALO:HALO + tm] * w[j:j + 1]
    if act:
        acc = _silu(acc)
    o_ref[...] = (acc * s_ref[...]).astype(o_ref.dtype)


def _dwconv(u, col0, ncols, w, b, scale, act, n_lat_tiles, n_tiles, name):
    nb, s, _ = u.shape
    taps = w.shape[0]
    tc = _pick(ncols, (512, 256))
    assert col0 % tc == 0
    c0 = col0 // tc
    hb = TOKEN_TILE // HALO
    last = s // HALO - 1
    bounds = (0, n_lat_tiles, n_tiles)
    return pl.pallas_call(
        functools.partial(_dwconv_kernel, taps=taps, act=act, bounds=bounds),
        out_shape=jax.ShapeDtypeStruct((nb, s, ncols), BF16),
        grid=(nb, n_tiles, ncols // tc),
        in_specs=[pl.BlockSpec((None, TOKEN_TILE, tc), lambda bb, i, c: (bb, i, c0 + c)),
                  pl.BlockSpec((None, HALO, tc), lambda bb, i, c: (bb, jnp.maximum(i * hb - 1, 0), c0 + c)),
                  pl.BlockSpec((None, HALO, tc), lambda bb, i, c: (bb, jnp.minimum((i + 1) * hb, last), c0 + c)),
                  pl.BlockSpec((taps, tc), lambda bb, i, c: (0, c)),
                  pl.BlockSpec((1, tc), lambda bb, i, c: (0, c)),
                  pl.BlockSpec((1, tc), lambda bb, i, c: (0, c))],
        out_specs=pl.BlockSpec((None, TOKEN_TILE, tc), lambda bb, i, c: (bb, i, c)),
        compiler_params=_params(("parallel", "parallel", "parallel")),
        name=name,
    )(u, u, u, w, b, scale)


def _chunk_index(n_lat_ch, n_ch):
    return lambda d, j: jnp.where(d == 0, (j + n_lat_ch) % n_ch, n_ch - 1 - j)


def _scan_masks(d, t):
    ii = lax.broadcasted_iota(jnp.int32, (t, t), 0)
    jj = lax.broadcasted_iota(jnp.int32, (t, t), 1)
    sgn = jnp.where(d == 0, 1, -1)
    mask = (ii - jj) * sgn >= 0
    tri_t = jnp.where((jj - ii) * sgn >= 0, 1.0, 0.0).astype(F32)
    return mask, tri_t


def _gate_rows(la_r, lw_r, tri_t, m_prev):
    cum_r = _hdot(la_r, tri_t)
    last = jnp.sum(la_r, axis=1, keepdims=True)
    g_r = last - cum_r + lw_r
    m_loc = jnp.max(g_r, axis=1, keepdims=True)
    e_r = jnp.exp(g_r - m_loc)
    m_new = jnp.maximum(last + m_prev, m_loc)
    s_old = jnp.exp(last + m_prev - m_new)
    s_new = jnp.exp(m_loc - m_new)
    return cum_r, cum_r - lw_r, e_r, m_new, s_old, s_new


def _head_probs(cum_row, crow_row, m_prev_h, mask, qk):
    t = qk.shape[0]
    colb = jnp.broadcast_to(cum_row, (t, t)).T
    dlog = jnp.where(mask, colb - crow_row, -jnp.inf)
    inter = colb[:, 0:1] + m_prev_h
    m_row = jnp.maximum(inter, jnp.max(dlog, axis=1, keepdims=True))
    p = jnp.exp(dlog - m_row) * qk
    return p.astype(BF16), jnp.exp(inter - m_row), m_row


def _ssd_scan_kernel(xs_ref, b_ref, c_ref, dt_ref, bias_ref, alog_ref, o_ref, st_ref, m_ref,
                     *, heads, groups, hd, ds):
    d = pl.program_id(1)
    j = pl.program_id(2)
    t = xs_ref.shape[0]
    hpg = heads // groups

    @pl.when(j == 0)
    def _():
        st_ref[...] = jnp.zeros_like(st_ref)
        m_ref[...] = jnp.zeros_like(m_ref)

    mask, tri_t = _scan_masks(d, t)
    dt = _softplus(dt_ref[...] + bias_ref[...])
    la_r = -dt * jnp.exp(alog_ref[...])
    lw_r = jnp.log(dt)
    m_prev = m_ref[:, 0:1]
    cum_r, crow, e_r, m_new, s_old, s_new = _gate_rows(la_r, lw_r, tri_t, m_prev)

    for g in range(groups):
        q = c_ref[:, g * ds:(g + 1) * ds]
        k_t = b_ref[:, g * ds:(g + 1) * ds].astype(F32).T
        qk = _dot(q, k_t.astype(BF16))
        w0 = g * hpg * hd
        qs = _dot(q, st_ref[:, w0:w0 + hpg * hd].astype(BF16))
        for i in range(hpg):
            h = g * hpg + i
            c0 = h * hd
            v = xs_ref[:, c0:c0 + hd]
            p, cf, m_row = _head_probs(cum_r[h:h + 1], crow[h:h + 1], m_prev[h:h + 1], mask, qk)
            y = _dot(p, v) + qs[:, i * hd:(i + 1) * hd] * cf
            o_ref[:, c0:c0 + hd] = y * jnp.exp(m_row)
            local_t = _dot((k_t * e_r[h:h + 1]).astype(BF16), v)
            st_ref[:, c0:c0 + hd] = s_old[h:h + 1] * st_ref[:, c0:c0 + hd] + s_new[h:h + 1] * local_t
    m_ref[...] = jnp.broadcast_to(m_new, m_ref.shape)


def _ssd_scan(xbc, dt_r, bias_c, alog_c, heads, hd, ds, n_lat_ch, n_ch):
    nb, s, _ = xbc.shape
    inner = heads * hd
    gn = SSD_GROUPS * ds
    cidx = _chunk_index(n_lat_ch, n_ch)
    return pl.pallas_call(
        functools.partial(_ssd_scan_kernel, heads=heads, groups=SSD_GROUPS, hd=hd, ds=ds),
        out_shape=jax.ShapeDtypeStruct((2, nb, s, inner), F32),
        grid=(nb, 2, n_ch),
        in_specs=[pl.BlockSpec((None, CHUNK, inner), lambda b, d, j: (b, cidx(d, j), 0)),
                  pl.BlockSpec((None, CHUNK, gn), lambda b, d, j: (b, cidx(d, j), inner // gn)),
                  pl.BlockSpec((None, CHUNK, gn), lambda b, d, j: (b, cidx(d, j), inner // gn + 1)),
                  pl.BlockSpec((None, None, heads, CHUNK), lambda b, d, j: (d, b, 0, cidx(d, j))),
                  pl.BlockSpec((None, heads, 1), lambda b, d, j: (d, 0, 0)),
                  pl.BlockSpec((None, heads, 1), lambda b, d, j: (d, 0, 0))],
        out_specs=pl.BlockSpec((None, None, CHUNK, inner), lambda b, d, j: (d, b, cidx(d, j), 0)),
        scratch_shapes=[pltpu.VMEM((ds, inner), F32), pltpu.VMEM((heads, 128), F32)],
        compiler_params=_params(("parallel", "parallel", "arbitrary")),
        name="ssd_scan",
    )(xbc, xbc, xbc, dt_r, bias_c, alog_c)


def _ml_scan_kernel(q_ref, k_ref, v_ref, gi_ref, gf_ref, bi_ref, bf_ref, o_ref, st_ref, m_ref, *, heads):
    d = pl.program_id(1)
    j = pl.program_id(2)
    t = q_ref.shape[0]
    dh = ML_HEADDIM

    @pl.when(j == 0)
    def _():
        st_ref[...] = jnp.zeros_like(st_ref)
        m_ref[...] = jnp.zeros_like(m_ref)

    mask, tri_t = _scan_masks(d, t)
    lw_r = gi_ref[...] + bi_ref[...]
    f = gf_ref[...] + bf_ref[...]
    la_r = jnp.minimum(f, 0.0) - jnp.log(1.0 + jnp.exp(-jnp.abs(f)))
    m_prev = m_ref[:, 0:1]
    cum_r, crow, e_r, m_new, s_old, s_new = _gate_rows(la_r, lw_r, tri_t, m_prev)
    ones = jnp.ones((t, dh), BF16)

    for h in range(heads):
        c0 = h * dh
        q = q_ref[:, c0:c0 + dh]
        k_t = k_ref[:, c0:c0 + dh].astype(F32).T
        v_aug = jnp.concatenate([v_ref[:, c0:c0 + dh], ones], axis=1)
        qk = _dot(q, k_t.astype(BF16))
        s0 = 2 * c0
        qs = _dot(q, st_ref[:, s0:s0 + 2 * dh].astype(BF16))
        p, cf, m_row = _head_probs(cum_r[h:h + 1], crow[h:h + 1], m_prev[h:h + 1], mask, qk)
        y = _dot(p, v_aug) + qs * cf
        o_ref[:, c0:c0 + dh] = y[:, :dh] / jnp.maximum(jnp.abs(y[:, dh:]), jnp.exp(-m_row))
        local_t = _dot((k_t * e_r[h:h + 1]).astype(BF16), v_aug)
        st_ref[:, s0:s0 + 2 * dh] = s_old[h:h + 1] * st_ref[:, s0:s0 + 2 * dh] + s_new[h:h + 1] * local_t
    m_ref[...] = jnp.broadcast_to(m_new, m_ref.shape)


def _ml_scan(qk, pml, gi_r, gf_r, bi_c, bf_c, heads, n_lat_ch, n_ch):
    nb, s, _ = qk.shape
    inner = heads * ML_HEADDIM
    cidx = _chunk_index(n_lat_ch, n_ch)
    gspec = pl.BlockSpec((None, None, heads, CHUNK), lambda b, d, j: (d, b, 0, cidx(d, j)))
    bspec = pl.BlockSpec((None, heads, 1), lambda b, d, j: (d, 0, 0))
    return pl.pallas_call(
        functools.partial(_ml_scan_kernel, heads=heads),
        out_shape=jax.ShapeDtypeStruct((2, nb, s, inner), F32),
        grid=(nb, 2, n_ch),
        in_specs=[pl.BlockSpec((None, CHUNK, inner), lambda b, d, j: (b, cidx(d, j), 0)),
                  pl.BlockSpec((None, CHUNK, inner), lambda b, d, j: (b, cidx(d, j), 1)),
                  pl.BlockSpec((None, CHUNK, inner), lambda b, d, j: (b, cidx(d, j), 2)),
                  gspec, gspec, bspec, bspec],
        out_specs=pl.BlockSpec((None, None, CHUNK, inner), lambda b, d, j: (d, b, cidx(d, j), 0)),
        scratch_shapes=[pltpu.VMEM((ML_HEADDIM, 2 * inner), F32), pltpu.VMEM((heads, 128), F32)],
        compiler_params=_params(("parallel", "parallel", "arbitrary")),
        name="ml_scan",
    )(qk, qk, pml, gi_r, gf_r, bi_c, bf_c)


def _ssd_out_kernel(yf_ref, yb_ref, xs_ref, z_ref, d_ref, nw_ref, o_ref):
    y = yf_ref[...] + yb_ref[...] + d_ref[...] * xs_ref[...].astype(F32)
    y = y * _silu(z_ref[...].astype(F32))
    y = y * lax.rsqrt(jnp.mean(y * y, axis=-1, keepdims=True) + EPS) * nw_ref[...]
    o_ref[...] = y.astype(o_ref.dtype)


def _ssd_out(y, xbc, pnat, d_full, norm_w, layer, n_tiles):
    _, nb, s, inner = y.shape
    row = pl.BlockSpec((None, TOKEN_TILE, inner), lambda b, i: (b, i, 0))
    return pl.pallas_call(
        _ssd_out_kernel,
        out_shape=jax.ShapeDtypeStruct((nb, s, inner), BF16),
        grid=(nb, n_tiles),
        in_specs=[pl.BlockSpec((None, None, TOKEN_TILE, inner), lambda b, i: (0, b, i, 0)),
                  pl.BlockSpec((None, None, TOKEN_TILE, inner), lambda b, i: (1, b, i, 0)),
                  row, row,
                  pl.BlockSpec((None, 1, inner), lambda b, i: (layer, 0, 0)),
                  pl.BlockSpec((None, 1, inner), lambda b, i: (layer, 0, 0))],
        out_specs=row,
        compiler_params=_params(("parallel", "parallel")),
        name="ssd_out",
    )(y, y, xbc, pnat, d_full, norm_w)


def _ml_out_kernel(cf_ref, cb_ref, o_in_ref, nw_ref, o_ref, *, heads):
    dh = ML_HEADDIM
    for h in range(heads):
        sl = slice(h * dh, (h + 1) * dh)
        c = cf_ref[:, sl] + cb_ref[:, sl]
        c = c * lax.rsqrt(jnp.mean(c * c, axis=-1, keepdims=True) + EPS) * nw_ref[:, sl]
        o_ref[:, sl] = (jax.nn.sigmoid(o_in_ref[:, sl].astype(F32)) * c).astype(o_ref.dtype)


def _ml_out(cell, pml, norm_w, layer, heads, n_tiles):
    _, nb, s, inner = cell.shape
    return pl.pallas_call(
        functools.partial(_ml_out_kernel, heads=heads),
        out_shape=jax.ShapeDtypeStruct((nb, s, inner), BF16),
        grid=(nb, n_tiles),
        in_specs=[pl.BlockSpec((None, None, TOKEN_TILE, inner), lambda b, i: (0, b, i, 0)),
                  pl.BlockSpec((None, None, TOKEN_TILE, inner), lambda b, i: (1, b, i, 0)),
                  pl.BlockSpec((None, TOKEN_TILE, inner), lambda b, i: (b, i, 3)),
                  pl.BlockSpec((None, 1, inner), lambda b, i: (layer, 0, 0))],
        out_specs=pl.BlockSpec((None, TOKEN_TILE, inner), lambda b, i: (b, i, 0)),
        compiler_params=_params(("parallel", "parallel")),
        name="ml_out",
    )(cell, cell, pml, norm_w)


def _dft_tables(length):
    k = np.arange(length, dtype=np.int64)
    prod = (k[:, None] * k[None, :]) % (2 * length)
    ang = prod.astype(np.float64) * (np.pi / length)
    alt = np.where(np.arange(length) % 2 == 0, 1.0, -1.0)
    alt8 = np.broadcast_to(alt[None, :], (8, length))
    return (jnp.asarray(np.cos(ang), BF16), jnp.asarray(np.sin(ang), BF16), jnp.asarray(alt8, BF16))


def _hy_feats(length):
    t = jnp.arange(length, dtype=F32)
    t_norm = t / length
    bands = jnp.linspace(1e-4, HY_BANDS - 1, HY_BANDS, dtype=F32)
    ang = (2.0 * math.pi / length) * t[:, None] * bands[None, :]
    return jnp.concatenate([t_norm[:, None], jnp.cos(ang), -jnp.sin(ang)], axis=-1), t_norm[:, None]


def _hy_filter_kernel(feats_ref, tn_ref, w1_ref, b1_ref, w2_ref, b2_ref, w3f_ref, w3b_ref, df_ref, db_ref,
                      cm_ref, sm_ref, alt_ref, ha_ref, hb_ref, hn_ref):
    hid = jnp.sin(_hdot(feats_ref[...], w1_ref[...]) + b1_ref[...])
    hid = jnp.sin(_hdot(hid, w2_ref[...]) + b2_ref[...])
    tn = tn_ref[...]
    h_f = _hdot(hid, w3f_ref[...]) * jnp.exp(-tn * jnp.abs(df_ref[...]))
    h_b = _hdot(hid, w3b_ref[...]) * jnp.exp(-tn * jnp.abs(db_ref[...]))
    hsum = (h_f + h_b).astype(BF16)
    hdif = (h_f - h_b).astype(BF16)
    ha_ref[...] = _dot(cm_ref[...], hsum)
    hb_ref[...] = _dot(sm_ref[...], hdif)
    hn_ref[...] = _dot(alt_ref[...], hsum)


def _hy_filters(length, layer, w1, b1, w2, b2, w3, decay, tables):
    cm, sm, alt8 = tables
    feats, tn = _hy_feats(length)
    nfeat, nf = w1.shape[-2:]
    ch = decay.shape[-1]
    nct = ch // LANE_TILE
    const = lambda shape: pl.BlockSpec(shape, lambda n, c: (0,) * len(shape))
    single = dict(pipeline_mode=pl.Buffered(1))
    w3spec = lambda dr: pl.BlockSpec((None, None, None, nf, LANE_TILE), lambda n, c: (layer, n, dr, 0, c))
    dspec = lambda dr: pl.BlockSpec((None, None, None, 1, LANE_TILE), lambda n, c: (layer, n, dr, 0, c))
    lay = lambda a, b: pl.BlockSpec((None, a, b), lambda n, c: (layer, 0, 0))
    out = lambda rows: pl.BlockSpec((None, rows, LANE_TILE), lambda n, c: (n, 0, c))
    return pl.pallas_call(
        _hy_filter_kernel,
        out_shape=(jax.ShapeDtypeStruct((HY_ORDER, length, ch), F32),
                   jax.ShapeDtypeStruct((HY_ORDER, length, ch), F32),
                   jax.ShapeDtypeStruct((HY_ORDER, 8, ch), F32)),
        grid=(HY_ORDER, nct),
        in_specs=[const((length, nfeat)), const((length, 1)),
                  lay(nfeat, nf), lay(1, nf), lay(nf, nf), lay(1, nf),
                  w3spec(0), w3spec(1), dspec(0), dspec(1),
                  pl.BlockSpec((length, length), lambda n, c: (0, 0), **single),
                  pl.BlockSpec((length, length), lambda n, c: (0, 0), **single),
                  const((8, length))],
        out_specs=(out(length), out(length), out(8)),
        compiler_params=_params(("parallel", "parallel")),
        name="hyena_filters_%d" % length,
    )(feats, tn, w1, b1, w2, b2, w3, w3, decay, decay, cm, sm, alt8)


def _hy_conv_kernel(z_ref, g_ref, ha_ref, hb_ref, hn_ref, skip_ref, cm_ref, sm_ref, alt_ref, o_ref,
                    zc_ref, yr_ref, ym_ref):
    n = pl.program_id(2)
    length, width = zc_ref.shape
    rb = min(length, HY_ROW_BLOCK)

    @pl.when(n == 0)
    def _():
        zc_ref[...] = z_ref[...].astype(F32)

    zb = zc_ref[...].astype(BF16)
    nyq = 0.5 * _dot(alt_ref[...], zb)[0:1] * hn_ref[0:1]
    row = lax.broadcasted_iota(jnp.int32, (rb, width), 0)
    for r in range(length // rb):
        sl = slice(r * rb, (r + 1) * rb)
        a = _dot(cm_ref[sl, :], zb)
        b = _dot(sm_ref[sl, :], zb)
        ha = ha_ref[sl, :]
        hb = hb_ref[sl, :]
        yr = a * ha - b * hb
        if r == 0:
            yr = yr * jnp.where(row == 0, 0.5, 1.0)
        yr_ref[sl, :] = yr.astype(BF16)
        ym_ref[sl, :] = (a * hb + b * ha).astype(BF16)
    alt = jnp.where((row & 1) == 0, 1.0, -1.0)
    for r in range(length // rb):
        sl = slice(r * rb, (r + 1) * rb)
        y = _dot(cm_ref[sl, :], yr_ref[...]) + _dot(sm_ref[sl, :], ym_ref[...])
        y = (y + alt * nyq) * (1.0 / length)
        z_new = g_ref[sl, :].astype(F32) * (y + skip_ref[...] * zc_ref[sl, :])

        @pl.when(n == 0)
        def _():
            zc_ref[sl, :] = z_new

        @pl.when(n == HY_ORDER - 1)
        def _():
            o_ref[sl, :] = z_new.astype(o_ref.dtype)


def _hy_conv(u, out_prev, row_block, length, layer, ha, hb, hn, skip, tables):
    cm, sm, alt8 = tables
    nb, s, c3 = u.shape
    ch = c3 // (HY_ORDER + 1)
    nct = ch // LANE_TILE
    single = dict(pipeline_mode=pl.Buffered(1))
    hspec = lambda rows: pl.BlockSpec((None, rows, LANE_TILE), lambda c, b, n: (n, 0, c))
    in_specs = [pl.BlockSpec((None, length, LANE_TILE), lambda c, b, n: (b, row_block, c)),
                pl.BlockSpec((None, length, LANE_TILE), lambda c, b, n: (b, row_block, (1 + n) * nct + c)),
                hspec(length), hspec(length), hspec(8),
                pl.BlockSpec((None, None, 1, LANE_TILE), lambda c, b, n: (layer, n, 0, c)),
                pl.BlockSpec((length, length), lambda c, b, n: (0, 0), **single),
                pl.BlockSpec((length, length), lambda c, b, n: (0, 0), **single),
                pl.BlockSpec((8, length), lambda c, b, n: (0, 0))]
    args = [u, u, ha, hb, hn, skip, cm, sm, alt8]
    aliases = {}
    if out_prev is not None:
        in_specs.append(pl.BlockSpec(memory_space=pl.ANY))
        args.append(out_prev)
        aliases = {len(args) - 1: 0}

    def body(*refs):
        _hy_conv_kernel(*refs[:9], *refs[-4:])

    return pl.pallas_call(
        body,
        out_shape=jax.ShapeDtypeStruct((nb, s, ch), BF16),
        grid=(nct, nb, HY_ORDER),
        in_specs=in_specs,
        out_specs=pl.BlockSpec((None, length, LANE_TILE), lambda c, b, n: (b, row_block, c)),
        scratch_shapes=[pltpu.VMEM((length, LANE_TILE), F32), pltpu.VMEM((length, LANE_TILE), BF16),
                        pltpu.VMEM((length, LANE_TILE), BF16)],
        input_output_aliases=aliases,
        compiler_params=_params(("parallel", "parallel", "arbitrary")),
        name="hyena_conv_%d" % length,
    )(*args)


def _merge_kernel(ys_ref, ym_ref, yh_ref, g0_ref, g1_ref, g2_ref, wb_ref, wo_ref, x_ref, mod_ref, o_ref):
    acc = None
    for n, (y_ref, g_ref) in enumerate(((ys_ref, g0_ref), (ym_ref, g1_ref), (yh_ref, g2_ref))):
        term = jax.nn.sigmoid(g_ref[...].astype(F32)) * _dot(y_ref[...], wb_ref[n])
        acc = term if acc is None else acc + term
    o = _dot(acc.astype(BF16), wo_ref[...])
    o_ref[...] = x_ref[...] + mod_ref[2:3] * o


def _merge(ys, ym, yh, pnat, wb, wo, x, mod, n_lat_tiles, n_tiles):
    nb, s, d = x.shape
    bw = ys.shape[-1]
    row = pl.BlockSpec((None, TOKEN_TILE, bw), lambda b, i: (b, i, 0))
    gate = lambda n: pl.BlockSpec((None, TOKEN_TILE, d), lambda b, i: (b, i, 1 + n))
    xrow = pl.BlockSpec((None, TOKEN_TILE, d), lambda b, i: (b, i, 0))
    single = dict(pipeline_mode=pl.Buffered(1))
    return pl.pallas_call(
        _merge_kernel,
        out_shape=jax.ShapeDtypeStruct((nb, s, d), F32),
        grid=(nb, n_tiles),
        in_specs=[row, row, row, gate(0), gate(1), gate(2),
                  pl.BlockSpec((N_BRANCH, bw, d), lambda b, i: (0, 0, 0), **single),
                  pl.BlockSpec((d, d), lambda b, i: (0, 0), **single),
                  xrow,
                  pl.BlockSpec((None, 6, d), _mod_row(nb, n_lat_tiles))],
        out_specs=xrow,
        input_output_aliases={8: 0},
        compiler_params=_params(("parallel", "parallel")),
        name="merge",
    )(ys, ym, yh, pnat, pnat, pnat, wb, wo, x, mod)


def _mlp_kernel(x_ref, nw_ref, mod_ref, w1_ref, w2_ref, o_ref):
    x = x_ref[...]
    m = mod_ref[...]
    h = x * lax.rsqrt(jnp.mean(x * x, axis=-1, keepdims=True) + EPS) * nw_ref[...]
    h = (h * (1.0 + m[4:5]) + m[3:4]).astype(BF16)
    a = jnp.maximum(_dot(h, w1_ref[...]), 0.0)
    o_ref[...] = x + m[5:6] * _dot((a * a).astype(BF16), w2_ref[...])


def _mlp(x, norm_w, layer, mod, w1, w2, n_lat_tiles, n_tiles):
    nb, s, d = x.shape
    hidden = w1.shape[1]
    xrow = pl.BlockSpec((None, TOKEN_TILE, d), lambda b, i: (b, i, 0))
    single = dict(pipeline_mode=pl.Buffered(1))
    return pl.pallas_call(
        _mlp_kernel,
        out_shape=jax.ShapeDtypeStruct((nb, s, d), F32),
        grid=(nb, n_tiles),
        in_specs=[xrow,
                  pl.BlockSpec((None, 1, d), lambda b, i: (layer, 0, 0)),
                  pl.BlockSpec((None, 6, d), _mod_row(nb, n_lat_tiles)),
                  pl.BlockSpec((d, hidden), lambda b, i: (0, 0), **single),
                  pl.BlockSpec((hidden, d), lambda b, i: (0, 0), **single)],
        out_specs=xrow,
        input_output_aliases={0: 0},
        compiler_params=_params(("parallel", "parallel")),
        name="mlp",
    )(x, norm_w, mod, w1, w2)


def _final_norm_kernel(x_ref, w_ref, o_ref):
    x = x_ref[...]
    o_ref[...] = x * lax.rsqrt(jnp.mean(x * x, axis=-1, keepdims=True) + EPS) * w_ref[...]


def _final_norm(x, w, seq):
    nb, _, d = x.shape
    return pl.pallas_call(
        _final_norm_kernel,
        out_shape=jax.ShapeDtypeStruct((nb, seq, d), F32),
        grid=(nb, seq // TOKEN_TILE),
        in_specs=[pl.BlockSpec((None, TOKEN_TILE, d), lambda b, i: (b, i, 0)),
                  pl.BlockSpec((1, d), lambda b, i: (0, 0))],
        out_specs=pl.BlockSpec((None, TOKEN_TILE, d), lambda b, i: (b, i, 0)),
        compiler_params=_params(("parallel", "parallel")),
        name="final_norm",
    )(x, w)


def _lat_colmajor(t, seq):
    nb, _, ch = t.shape
    rows = seq // GRID_W
    lat = t[:, :seq].reshape(nb, rows, GRID_W, ch).swapaxes(1, 2).reshape(nb, seq, ch)
    return jnp.concatenate([lat, t[:, seq:]], axis=1)


def _lat_rowmajor(t, seq):
    nb, _, ch = t.shape
    rows = seq // GRID_W
    lat = t[:, :seq].reshape(nb, GRID_W, rows, ch).swapaxes(1, 2).reshape(nb, seq, ch)
    return jnp.concatenate([lat, t[:, seq:]], axis=1)


def _dir_rows(g, nb, s, per_dir):
    g = g[:, :2 * per_dir].reshape(nb, s, 2, per_dir)
    return jnp.transpose(g, (2, 0, 3, 1))


def kernel(x, c, ctx, c_ctx, norm1_w, mod_w, mod_b, w_in, ssd_conv_w, ssd_conv_b, ssd_dt_bias, ssd_a_log,
           ssd_d, ssd_norm_w, ml_conv_w, ml_conv_b, ml_gate_b, ml_norm_w, hy_conv_w, hy_conv_b, hy_ffn_w1,
           hy_ffn_b1, hy_ffn_w2, hy_ffn_b2, hy_ffn_w3, hy_decay, hy_skip, w_branch, w_out, norm2_w,
           mlp_w1, mlp_w2, norm_f_w):
    nb, seq, d = x.shape
    ctx_len = ctx.shape[1]
    depth = w_in.shape[0]
    s = seq + ctx_len
    assert seq % ctx_len == 0 and ctx_len % TOKEN_TILE == 0 and seq % GRID_W == 0
    n_lat_tiles, n_tiles = seq // TOKEN_TILE, s // TOKEN_TILE
    n_lat_ch, n_ch = seq // CHUNK, s // CHUNK

    ssd_heads = ssd_d.shape[-1]
    ssd_inner = ssd_norm_w.shape[-1]
    ssd_hd = ssd_inner // ssd_heads
    ssd_conv_ch = ssd_conv_w.shape[-1]
    ssd_ds = (ssd_conv_ch - ssd_inner) // (2 * SSD_GROUPS)
    ml_heads = ml_gate_b.shape[-1]
    ml_inner = ml_heads * ML_HEADDIM
    hy_inner = hy_skip.shape[-1]
    ssd_cols = ssd_conv_ch + ssd_inner + 2 * ssd_heads
    ml_cols = 4 * ml_inner + 4 * ml_heads
    rec_cols = ssd_cols + ml_cols
    hy_cols = (HY_ORDER + 1) * hy_inner

    o_z = ssd_conv_ch
    o_dt = ssd_conv_ch + ssd_inner
    o_ml = ssd_cols
    o_mlg = ssd_cols + 4 * ml_inner
    o_hy = rec_cols
    o_g = rec_cols + hy_cols

    xa = jnp.concatenate([x, ctx], axis=1)
    rpad = (-(nb + 1)) % 8
    c_all = jnp.concatenate([c, c_ctx[None], jnp.zeros((rpad, d), F32)], axis=0)

    tab_lat = _dft_tables(seq)
    tab_ctx = _dft_tables(ctx_len)
    k_scale = jnp.concatenate([jnp.ones((1, ml_inner), F32),
                               jnp.full((1, ml_inner), ML_HEADDIM ** -0.5, F32)], axis=1)
    ones_row = lambda n: jnp.ones((1, n), F32)

    norm1 = norm1_w[:, None, :]
    norm2 = norm2_w[:, None, :]
    ssd_norm = ssd_norm_w[:, None, :]
    ml_norm = ml_norm_w[:, None, :]
    d_full = jnp.repeat(ssd_d, ssd_hd, axis=-1)[:, None, :]
    w3 = hy_ffn_w3.reshape(depth, hy_ffn_w3.shape[1], HY_ORDER, 2, hy_inner).transpose(0, 2, 3, 1, 4)
    decay = hy_decay[:, :, :, None, :]
    skip = hy_skip[:, :, None, :]
    hb1 = hy_ffn_b1[:, None, :]
    hb2 = hy_ffn_b2[:, None, :]

    for l in range(depth):
        need_ctx = l < depth - 1
        used_tiles = n_tiles if need_ctx else n_lat_tiles
        mod = _mod_vectors(c_all, mod_w, mod_b[:, None, :], l).reshape(-1, 6, d)

        wl = w_in[l]
        w_nat = jnp.concatenate([wl[:, o_z:o_dt], wl[:, o_g:], wl[:, o_hy:o_g], wl[:, :o_z]], axis=1).astype(BF16)
        w_ml = wl[:, o_ml:o_mlg].astype(BF16)
        gpad = lambda w: jnp.pad(w, ((0, 0), (0, 128 - w.shape[1]))).astype(BF16)
        hn = _normmod(xa, norm1, l, mod, 0, n_lat_tiles, n_tiles)
        hn_cm = _lat_colmajor(hn, seq)
        hn2 = hn.reshape(nb * s, d)
        hn_cm2 = hn_cm.reshape(nb * s, d)
        pnat = _matmul(hn2, w_nat, BF16, "proj_nat").reshape(nb, s, -1)
        pml = _matmul(hn_cm2, w_ml, BF16, "proj_ml").reshape(nb, s, -1)
        p_dt = _matmul(hn2, gpad(wl[:, o_dt:o_ml]), F32, "proj_dt")
        p_mlg = _matmul(hn_cm2, gpad(wl[:, o_mlg:o_hy]), F32, "proj_mlg")
        c_g = ssd_inner
        c_hy = c_g + N_BRANCH * d
        c_xbc = c_hy + hy_cols

        xbc = _dwconv(pnat, c_xbc, ssd_conv_ch, ssd_conv_w[l], ssd_conv_b[l][None], ones_row(ssd_conv_ch),
                      True, n_lat_tiles, n_tiles, "ssd_conv")
        dt_r = _dir_rows(p_dt, nb, s, ssd_heads)
        y_ssd = _ssd_scan(xbc, dt_r, ssd_dt_bias[l][:, :, None], ssd_a_log[l][:, :, None],
                          ssd_heads, ssd_hd, ssd_ds, n_lat_ch, n_ch)
        ys = _ssd_out(y_ssd, xbc, pnat, d_full, ssd_norm, l, used_tiles)

        qk = _dwconv(pml, 0, 2 * ml_inner, ml_conv_w[l], ml_conv_b[l][None], k_scale,
                     True, n_lat_tiles, n_tiles, "ml_conv")
        g_r = _dir_rows(p_mlg, nb, s, 2 * ml_heads)
        gate_b = ml_gate_b[l]
        cell = _ml_scan(qk, pml, g_r[:, :, :ml_heads], g_r[:, :, ml_heads:],
                        gate_b[:, 0, :, None], gate_b[:, 1, :, None], ml_heads, n_lat_ch, n_ch)
        ym = _lat_rowmajor(_ml_out(cell, pml, ml_norm, l, ml_heads, used_tiles), seq)

        uh = _dwconv(pnat, c_hy, hy_cols, hy_conv_w[l], hy_conv_b[l][None], ones_row(hy_cols),
                     False, n_lat_tiles, n_tiles, "hy_conv")
        fl = _hy_filters(seq, l, hy_ffn_w1, hb1, hy_ffn_w2, hb2, w3, decay, tab_lat)
        yh = _hy_conv(uh, None, 0, seq, l, *fl, skip, tab_lat)
        if need_ctx:
            fc = _hy_filters(ctx_len, l, hy_ffn_w1, hb1, hy_ffn_w2, hb2, w3, decay, tab_ctx)
            yh = _hy_conv(uh, yh, seq // ctx_len, ctx_len, l, *fc, skip, tab_ctx)

        xa = _merge(ys, ym, yh, pnat, w_branch[l].astype(BF16), w_out[l].astype(BF16), xa, mod,
                    n_lat_tiles, used_tiles)
        xa = _mlp(xa, norm2, l, mod, mlp_w1[l].astype(BF16), mlp_w2[l].astype(BF16), n_lat_tiles, used_tiles)

    return _final_norm(xa, norm_f_w[None], seq)
```

```python
import functools
import math

import jax
import jax.numpy as jnp
import numpy as np
from jax import lax
from jax.experimental import pallas as pl
from jax.experimental.pallas import tpu as pltpu

F32 = jnp.float32
BF16 = jnp.bfloat16
HIGHEST = lax.Precision.HIGHEST

GRID_W = 64
CHUNK = 128
EPS = 1e-6
SSD_GROUPS = 2
SSD_CONV = 5
ML_HEADDIM = 128
ML_CONV = 5
HY_ORDER = 2
HY_SHORT = 3
HY_BANDS = 16
N_BRANCH = 3

TOKEN_TILE = 256
LANE_TILE = 256
HALO = 16
HY_ROW_BLOCK = 1024
VMEM_LIMIT = 56 * 1024 * 1024

_hdot = functools.partial(jnp.dot, precision=HIGHEST, preferred_element_type=F32)
_dot = functools.partial(jnp.dot, preferred_element_type=F32)


def _params(sem, vmem=None):
    return pltpu.CompilerParams(dimension_semantics=sem, vmem_limit_bytes=vmem or VMEM_LIMIT)


def _softplus(x):
    return jnp.maximum(x, 0.0) + jnp.log(1.0 + jnp.exp(-jnp.abs(x)))


def _silu(x):
    return x * jax.nn.sigmoid(x)


def _mod_kernel(c_ref, w_ref, b_ref, o_ref):
    o_ref[...] = _hdot(_silu(c_ref[...]), w_ref[...]) + b_ref[...]


def _mod_vectors(c_all, mod_w, mod_b, layer):
    r, d = c_all.shape
    n = mod_w.shape[-1]
    tn = n // 6
    return pl.pallas_call(
        _mod_kernel,
        out_shape=jax.ShapeDtypeStruct((r, n), F32),
        grid=(n // tn,),
        in_specs=[pl.BlockSpec((r, d), lambda j: (0, 0)),
                  pl.BlockSpec((None, d, tn), lambda j: (layer, 0, j)),
                  pl.BlockSpec((None, 1, tn), lambda j: (layer, 0, j))],
        out_specs=pl.BlockSpec((r, tn), lambda j: (0, j)),
        compiler_params=_params(("parallel",)),
        name="mod_vectors",
    )(c_all, mod_w, mod_b)


def _normmod_kernel(x_ref, nw_ref, mod_ref, o_ref, *, si):
    x = x_ref[...]
    h = x * lax.rsqrt(jnp.mean(x * x, axis=-1, keepdims=True) + EPS) * nw_ref[...]
    m = mod_ref[...]
    o_ref[...] = (h * (1.0 + m[si + 1:si + 2]) + m[si:si + 1]).astype(o_ref.dtype)


def _mod_row(nb, n_lat_tiles):
    return lambda b, i: (jnp.where(i < n_lat_tiles, b, nb), 0, 0)


def _normmod(x, norm_w, layer, mod, si, n_lat_tiles, n_tiles):
    nb, s, d = x.shape
    return pl.pallas_call(
        functools.partial(_normmod_kernel, si=si),
        out_shape=jax.ShapeDtypeStruct((nb, s, d), BF16),
        grid=(nb, n_tiles),
        in_specs=[pl.BlockSpec((None, TOKEN_TILE, d), lambda b, i: (b, i, 0)),
                  pl.BlockSpec((None, 1, d), lambda b, i: (layer, 0, 0)),
                  pl.BlockSpec((None, 6, d), _mod_row(nb, n_lat_tiles))],
        out_specs=pl.BlockSpec((None, TOKEN_TILE, d), lambda b, i: (b, i, 0)),
        compiler_params=_params(("parallel", "parallel")),
        name="normmod",
    )(x, norm_w, mod)


def _mm_kernel(a_ref, w_ref, o_ref):
    o_ref[...] = _dot(a_ref[...], w_ref[...]).astype(o_ref.dtype)


def _pick(n, cands):
    for c in cands:
        if n % c == 0:
            return c
    return n


def _matmul(a, w, out_dtype, name):
    t, k = a.shape
    n = w.shape[1]
    tm = _pick(t, (1024, 768, 512, 256))
    tn = _pick(n, (1024, 512, 256, 128))
    return pl.pallas_call(
        _mm_kernel,
        out_shape=jax.ShapeDtypeStruct((t, n), out_dtype),
        grid=(t // tm, n // tn),
        in_specs=[pl.BlockSpec((tm, k), lambda i, j: (i, 0)),
                  pl.BlockSpec((k, tn), lambda i, j: (0, j))],
        out_specs=pl.BlockSpec((tm, tn), lambda i, j: (i, j)),
        compiler_params=_params(("parallel", "parallel")),
        name=name,
    )(a, w)


def _dwconv_kernel(u_ref, p_ref, n_ref, w_ref, b_ref, s_ref, o_ref, *, taps, act, bounds):
    i = pl.program_id(1)
    tm = u_ref.shape[0]
    pad = taps // 2
    lv, rv = jnp.float32(1.0), jnp.float32(1.0)
    for e in bounds:
        lv = jnp.where(i == e, 0.0, lv)
        rv = jnp.where(i == e - 1, 0.0, rv)
    ext = jnp.concatenate([p_ref[...].astype(F32) * lv, u_ref[...].astype(F32),
                           n_ref[...].astype(F32) * rv], axis=0)
    rows = ext.shape[0]
    w = w_ref[...]
    acc = jnp.zeros((tm, u_ref.shape[1]), F32) + b_ref[...]
    for j in range(taps):
        sh = (pad - j) % rows
        shifted = ext if sh == 0 else pltpu.roll(ext, sh, 0)
        acc = acc + shifted[HALO:HALO + tm] * w[j:j + 1]
    if act:
        acc = _silu(acc)
    o_ref[...] = (acc * s_ref[...]).astype(o_ref.dtype)


def _dwconv(u, col0, ncols, w, b, scale, act, n_lat_tiles, n_tiles, name):
    nb, s, _ = u.shape
    taps = w.shape[0]
    tc = _pick(ncols, (2048, 1536, 1024, 512, 256))
    assert col0 % tc == 0
    c0 = col0 // tc
    hb = TOKEN_TILE // HALO
    last = s // HALO - 1
    bounds = (0, n_lat_tiles, n_tiles)
    return pl.pallas_call(
        functools.partial(_dwconv_kernel, taps=taps, act=act, bounds=bounds),
        out_shape=jax.ShapeDtypeStruct((nb, s, ncols), BF16),
        grid=(nb, n_tiles, ncols // tc),
        in_specs=[pl.BlockSpec((None, TOKEN_TILE, tc), lambda bb, i, c: (bb, i, c0 + c)),
                  pl.BlockSpec((None, HALO, tc), lambda bb, i, c: (bb, jnp.maximum(i * hb - 1, 0), c0 + c)),
                  pl.BlockSpec((None, HALO, tc), lambda bb, i, c: (bb, jnp.minimum((i + 1) * hb, last), c0 + c)),
                  pl.BlockSpec((taps, tc), lambda bb, i, c: (0, c)),
                  pl.BlockSpec((1, tc), lambda bb, i, c: (0, c)),
                  pl.BlockSpec((1, tc), lambda bb, i, c: (0, c))],
        out_specs=pl.BlockSpec((None, TOKEN_TILE, tc), lambda bb, i, c: (bb, i, c)),
        compiler_params=_params(("parallel", "parallel", "parallel")),
        name=name,
    )(u, u, u, w, b, scale)


def _chunk_index(n_lat_ch, n_ch):
    return lambda d, j: jnp.where(d == 0, (j + n_lat_ch) % n_ch, n_ch - 1 - j)


def _scan_masks(d, t):
    ii = lax.broadcasted_iota(jnp.int32, (t, t), 0)
    jj = lax.broadcasted_iota(jnp.int32, (t, t), 1)
    sgn = jnp.where(d == 0, 1, -1)
    mask = (ii - jj) * sgn >= 0
    tri_t = jnp.where((jj - ii) * sgn >= 0, 1.0, 0.0).astype(F32)
    return mask, tri_t


def _gate_rows(la_r, lw_r, tri_t, m_prev):
    cum_r = _hdot(la_r, tri_t)
    last = jnp.sum(la_r, axis=1, keepdims=True)
    g_r = last - cum_r + lw_r
    m_loc = jnp.max(g_r, axis=1, keepdims=True)
    e_r = jnp.exp(g_r - m_loc)
    m_new = jnp.maximum(last + m_prev, m_loc)
    s_old = jnp.exp(last + m_prev - m_new)
    s_new = jnp.exp(m_loc - m_new)
    return cum_r, cum_r - lw_r, e_r, m_new, s_old, s_new


def _head_probs(cum_row, crow_row, m_prev_h, mask, qk):
    t = qk.shape[0]
    colb = jnp.broadcast_to(cum_row, (t, t)).T
    dlog = jnp.where(mask, colb - crow_row, -jnp.inf)
    inter = colb[:, 0:1] + m_prev_h
    m_row = jnp.maximum(inter, jnp.max(dlog, axis=1, keepdims=True))
    p = jnp.exp(dlog - m_row) * qk
    return p.astype(BF16), jnp.exp(inter - m_row), m_row


def _ssd_scan_kernel(xs_ref, b_ref, c_ref, dt_ref, bias_ref, alog_ref, o_ref, st_ref,
                     *, heads, groups, hd, ds):
    d = pl.program_id(1)
    j = pl.program_id(2)
    t = xs_ref.shape[0]
    hpg = heads // groups
    pw = 2 * hd

    @pl.when(j == 0)
    def _():
        st_ref[...] = jnp.zeros_like(st_ref)

    mask, tri_t = _scan_masks(d, t)
    dt = _softplus(dt_ref[...] + bias_ref[...])
    la_r = -dt * jnp.exp(alog_ref[...])
    cum_r = _hdot(la_r, tri_t)
    last = jnp.sum(la_r, axis=1, keepdims=True)
    crow = cum_r - jnp.log(dt)
    e_r = jnp.exp(last - crow)
    e_last = jnp.exp(last)
    lo = lax.broadcasted_iota(jnp.int32, (1, pw), 1) < hd

    for g in range(groups):
        q = c_ref[:, g * ds:(g + 1) * ds]
        k_t = b_ref[:, g * ds:(g + 1) * ds].astype(F32).T
        qk = _dot(q, k_t.astype(BF16))
        w0 = g * hpg * hd
        qs = _dot(q, st_ref[:, w0:w0 + hpg * hd].astype(BF16))
        for i in range(hpg // 2):
            h0 = g * hpg + 2 * i
            c0 = h0 * hd
            vp = xs_ref[:, c0:c0 + pw]
            zero = jnp.zeros_like(vp)
            v_bd = jnp.concatenate([jnp.where(lo, vp, zero), jnp.where(lo, zero, vp)], axis=0)
            probs, carry, kte = [], [], []
            for h in (h0, h0 + 1):
                colb = jnp.broadcast_to(cum_r[h:h + 1], (t, t)).T
                dlog = jnp.where(mask, colb - crow[h:h + 1], -jnp.inf)
                probs.append((jnp.exp(dlog) * qk).astype(BF16))
                carry.append(jnp.exp(colb))
                kte.append((k_t * e_r[h:h + 1]).astype(BF16))
            lhs = jnp.concatenate([jnp.concatenate(probs, axis=1), jnp.concatenate(kte, axis=1)], axis=0)
            res = _dot(lhs, v_bd)
            o_ref[:, c0:c0 + pw] = res[:t] + qs[:, 2 * i * hd:2 * i * hd + pw] * jnp.where(lo, carry[0], carry[1])
            decay = jnp.where(lo, e_last[h0:h0 + 1], e_last[h0 + 1:h0 + 2])
            st_ref[:, c0:c0 + pw] = decay * st_ref[:, c0:c0 + pw] + res[t:]


def _ssd_scan(xbc, dt_r, bias_c, alog_c, heads, hd, ds, n_lat_ch, n_ch):
    nb, s, _ = xbc.shape
    inner = heads * hd
    gn = SSD_GROUPS * ds
    assert 2 * hd == CHUNK and (heads // SSD_GROUPS) % 2 == 0 and inner % gn == 0
    cidx = _chunk_index(n_lat_ch, n_ch)
    return pl.pallas_call(
        functools.partial(_ssd_scan_kernel, heads=heads, groups=SSD_GROUPS, hd=hd, ds=ds),
        out_shape=jax.ShapeDtypeStruct((2, nb, s, inner), F32),
        grid=(nb, 2, n_ch),
        in_specs=[pl.BlockSpec((None, CHUNK, inner), lambda b, d, j: (b, cidx(d, j), 0)),
                  pl.BlockSpec((None, CHUNK, gn), lambda b, d, j: (b, cidx(d, j), inner // gn)),
                  pl.BlockSpec((None, CHUNK, gn), lambda b, d, j: (b, cidx(d, j), inner // gn + 1)),
                  pl.BlockSpec((None, None, heads, CHUNK), lambda b, d, j: (d, b, 0, cidx(d, j))),
                  pl.BlockSpec((None, heads, 1), lambda b, d, j: (d, 0, 0)),
                  pl.BlockSpec((None, heads, 1), lambda b, d, j: (d, 0, 0))],
        out_specs=pl.BlockSpec((None, None, CHUNK, inner), lambda b, d, j: (d, b, cidx(d, j), 0)),
        scratch_shapes=[pltpu.VMEM((ds, inner), F32)],
        compiler_params=_params(("parallel", "parallel", "arbitrary")),
        name="ssd_scan",
    )(xbc, xbc, xbc, dt_r, bias_c, alog_c)


def _ml_scan_kernel(q_ref, k_ref, v_ref, gi_ref, gf_ref, bi_ref, bf_ref, o_ref, st_ref, m_ref, *, heads):
    d = pl.program_id(1)
    j = pl.program_id(2)
    t = q_ref.shape[0]
    dh = ML_HEADDIM

    @pl.when(j == 0)
    def _():
        st_ref[...] = jnp.zeros_like(st_ref)
        m_ref[...] = jnp.zeros_like(m_ref)

    mask, tri_t = _scan_masks(d, t)
    lw_r = gi_ref[...] + bi_ref[...]
    f = gf_ref[...] + bf_ref[...]
    la_r = jnp.minimum(f, 0.0) - jnp.log(1.0 + jnp.exp(-jnp.abs(f)))
    m_prev = m_ref[:, 0:1]
    cum_r, crow, e_r, m_new, s_old, s_new = _gate_rows(la_r, lw_r, tri_t, m_prev)
    ones = jnp.ones((t, dh), BF16)

    for h in range(heads):
        c0 = h * dh
        q = q_ref[:, c0:c0 + dh]
        k_t = k_ref[:, c0:c0 + dh].astype(F32).T
        v_aug = jnp.concatenate([v_ref[:, c0:c0 + dh], ones], axis=1)
        qk = _dot(q, k_t.astype(BF16))
        s0 = 2 * c0
        qs = _dot(q, st_ref[:, s0:s0 + 2 * dh].astype(BF16))
        p, cf, m_row = _head_probs(cum_r[h:h + 1], crow[h:h + 1], m_prev[h:h + 1], mask, qk)
        y = _dot(p, v_aug) + qs * cf
        o_ref[:, c0:c0 + dh] = y[:, :dh] / jnp.maximum(jnp.abs(y[:, dh:]), jnp.exp(-m_row))
        local_t = _dot((k_t * e_r[h:h + 1]).astype(BF16), v_aug)
        st_ref[:, s0:s0 + 2 * dh] = s_old[h:h + 1] * st_ref[:, s0:s0 + 2 * dh] + s_new[h:h + 1] * local_t
    m_ref[...] = jnp.broadcast_to(m_new, m_ref.shape)


def _ml_scan(qk, pml, gi_r, gf_r, bi_c, bf_c, heads, n_lat_ch, n_ch):
    nb, s, _ = qk.shape
    inner = heads * ML_HEADDIM
    cidx = _chunk_index(n_lat_ch, n_ch)
    gspec = pl.BlockSpec((None, None, heads, CHUNK), lambda b, d, j: (d, b, 0, cidx(d, j)))
    bspec = pl.BlockSpec((None, heads, 1), lambda b, d, j: (d, 0, 0))
    return pl.pallas_call(
        functools.partial(_ml_scan_kernel, heads=heads),
        out_shape=jax.ShapeDtypeStruct((2, nb, s, inner), F32),
        grid=(nb, 2, n_ch),
        in_specs=[pl.BlockSpec((None, CHUNK, inner), lambda b, d, j: (b, cidx(d, j), 0)),
                  pl.BlockSpec((None, CHUNK, inner), lambda b, d, j: (b, cidx(d, j), 1)),
                  pl.BlockSpec((None, CHUNK, inner), lambda b, d, j: (b, cidx(d, j), 2)),
                  gspec, gspec, bspec, bspec],
        out_specs=pl.BlockSpec((None, None, CHUNK, inner), lambda b, d, j: (d, b, cidx(d, j), 0)),
        scratch_shapes=[pltpu.VMEM((ML_HEADDIM, 2 * inner), F32), pltpu.VMEM((heads, 128), F32)],
        compiler_params=_params(("parallel", "parallel", "arbitrary")),
        name="ml_scan",
    )(qk, qk, pml, gi_r, gf_r, bi_c, bf_c)


def _ssd_out_kernel(yf_ref, yb_ref, xs_ref, z_ref, d_ref, nw_ref, o_ref):
    y = yf_ref[...] + yb_ref[...] + d_ref[...] * xs_ref[...].astype(F32)
    y = y * _silu(z_ref[...].astype(F32))
    y = y * lax.rsqrt(jnp.mean(y * y, axis=-1, keepdims=True) + EPS) * nw_ref[...]
    o_ref[...] = y.astype(o_ref.dtype)


def _ssd_out(y, xbc, pnat, d_full, norm_w, layer, n_tiles):
    _, nb, s, inner = y.shape
    row = pl.BlockSpec((None, TOKEN_TILE, inner), lambda b, i: (b, i, 0))
    return pl.pallas_call(
        _ssd_out_kernel,
        out_shape=jax.ShapeDtypeStruct((nb, s, inner), BF16),
        grid=(nb, n_tiles),
        in_specs=[pl.BlockSpec((None, None, TOKEN_TILE, inner), lambda b, i: (0, b, i, 0)),
                  pl.BlockSpec((None, None, TOKEN_TILE, inner), lambda b, i: (1, b, i, 0)),
                  row, row,
                  pl.BlockSpec((None, 1, inner), lambda b, i: (layer, 0, 0)),
                  pl.BlockSpec((None, 1, inner), lambda b, i: (layer, 0, 0))],
        out_specs=row,
        compiler_params=_params(("parallel", "parallel")),
        name="ssd_out",
    )(y, y, xbc, pnat, d_full, norm_w)


def _ml_out_kernel(cf_ref, cb_ref, o_in_ref, nw_ref, o_ref, *, heads):
    dh = ML_HEADDIM
    for h in range(heads):
        sl = slice(h * dh, (h + 1) * dh)
        c = cf_ref[:, sl] + cb_ref[:, sl]
        c = c * lax.rsqrt(jnp.mean(c * c, axis=-1, keepdims=True) + EPS) * nw_ref[:, sl]
        o_ref[:, sl] = (jax.nn.sigmoid(o_in_ref[:, sl].astype(F32)) * c).astype(o_ref.dtype)


def _ml_out(cell, pml, norm_w, layer, heads, n_tiles):
    _, nb, s, inner = cell.shape
    return pl.pallas_call(
        functools.partial(_ml_out_kernel, heads=heads),
        out_shape=jax.ShapeDtypeStruct((nb, s, inner), BF16),
        grid=(nb, n_tiles),
        in_specs=[pl.BlockSpec((None, None, TOKEN_TILE, inner), lambda b, i: (0, b, i, 0)),
                  pl.BlockSpec((None, None, TOKEN_TILE, inner), lambda b, i: (1, b, i, 0)),
                  pl.BlockSpec((None, TOKEN_TILE, inner), lambda b, i: (b, i, 3)),
                  pl.BlockSpec((None, 1, inner), lambda b, i: (layer, 0, 0))],
        out_specs=pl.BlockSpec((None, TOKEN_TILE, inner), lambda b, i: (b, i, 0)),
        compiler_params=_params(("parallel", "parallel")),
        name="ml_out",
    )(cell, cell, pml, norm_w)


def _dft_tables(length):
    k = np.arange(length, dtype=np.int64)
    prod = (k[:, None] * k[None, :]) % (2 * length)
    ang = prod.astype(np.float64) * (np.pi / length)
    alt = np.where(np.arange(length) % 2 == 0, 1.0, -1.0)
    alt8 = np.broadcast_to(alt[None, :], (8, length))
    return (jnp.asarray(np.cos(ang), BF16), jnp.asarray(np.sin(ang), BF16), jnp.asarray(alt8, BF16))


def _hy_feats(length):
    t = jnp.arange(length, dtype=F32)
    t_norm = t / length
    bands = jnp.linspace(1e-4, HY_BANDS - 1, HY_BANDS, dtype=F32)
    ang = (2.0 * math.pi / length) * t[:, None] * bands[None, :]
    return jnp.concatenate([t_norm[:, None], jnp.cos(ang), -jnp.sin(ang)], axis=-1), t_norm[:, None]


def _hy_filter_kernel(feats_ref, tn_ref, w1_ref, b1_ref, w2_ref, b2_ref, w3f_ref, w3b_ref, df_ref, db_ref,
                      cm_ref, sm_ref, alt_ref, ha_ref, hb_ref, hn_ref):
    hid = jnp.sin(_hdot(feats_ref[...], w1_ref[...]) + b1_ref[...])
    hid = jnp.sin(_hdot(hid, w2_ref[...]) + b2_ref[...])
    tn = tn_ref[...]
    h_f = _hdot(hid, w3f_ref[...]) * jnp.exp(-tn * jnp.abs(df_ref[...]))
    h_b = _hdot(hid, w3b_ref[...]) * jnp.exp(-tn * jnp.abs(db_ref[...]))
    hsum = (h_f + h_b).astype(BF16)
    hdif = (h_f - h_b).astype(BF16)
    ha_ref[...] = _dot(cm_ref[...], hsum)
    hb_ref[...] = _dot(sm_ref[...], hdif)
    hn_ref[...] = _dot(alt_ref[...], hsum)


def _hy_filters(length, layer, w1, b1, w2, b2, w3, decay, tables):
    cm, sm, alt8 = tables
    feats, tn = _hy_feats(length)
    nfeat, nf = w1.shape[-2:]
    ch = decay.shape[-1]
    nct = ch // LANE_TILE
    const = lambda shape: pl.BlockSpec(shape, lambda n, c: (0,) * len(shape))
    single = dict(pipeline_mode=pl.Buffered(1))
    w3spec = lambda dr: pl.BlockSpec((None, None, None, nf, LANE_TILE), lambda n, c: (layer, n, dr, 0, c))
    dspec = lambda dr: pl.BlockSpec((None, None, None, 1, LANE_TILE), lambda n, c: (layer, n, dr, 0, c))
    lay = lambda a, b: pl.BlockSpec((None, a, b), lambda n, c: (layer, 0, 0))
    out = lambda rows: pl.BlockSpec((None, rows, LANE_TILE), lambda n, c: (n, 0, c))
    return pl.pallas_call(
        _hy_filter_kernel,
        out_shape=(jax.ShapeDtypeStruct((HY_ORDER, length, ch), F32),
                   jax.ShapeDtypeStruct((HY_ORDER, length, ch), F32),
                   jax.ShapeDtypeStruct((HY_ORDER, 8, ch), F32)),
        grid=(HY_ORDER, nct),
        in_specs=[const((length, nfeat)), const((length, 1)),
                  lay(nfeat, nf), lay(1, nf), lay(nf, nf), lay(1, nf),
                  w3spec(0), w3spec(1), dspec(0), dspec(1),
                  pl.BlockSpec((length, length), lambda n, c: (0, 0), **single),
                  pl.BlockSpec((length, length), lambda n, c: (0, 0), **single),
                  const((8, length))],
        out_specs=(out(length), out(length), out(8)),
        compiler_params=_params(("parallel", "parallel")),
        name="hyena_filters_%d" % length,
    )(feats, tn, w1, b1, w2, b2, w3, w3, decay, decay, cm, sm, alt8)


def _short_conv(u, w, b):
    length = u.shape[0]
    taps = w.shape[0]
    pad = taps // 2
    row = lax.broadcasted_iota(jnp.int32, u.shape, 0)
    acc = b + u * w[pad:pad + 1]
    for j in range(taps):
        off = j - pad
        if off != 0:
            moved = pltpu.roll(u, (-off) % length, 0)
            valid = (row + off >= 0) & (row + off < length)
            acc = acc + jnp.where(valid, moved, 0.0) * w[j:j + 1]
    return acc


def _hy_conv_kernel(z_ref, g_ref, wz_ref, bz_ref, wg_ref, bg_ref, ha_ref, hb_ref, hn_ref, skip_ref,
                    cm_ref, sm_ref, alt_ref, o_ref, zc_ref, gc_ref, yr_ref, ym_ref):
    n = pl.program_id(2)
    length, width = zc_ref.shape
    rb = min(length, HY_ROW_BLOCK)

    @pl.when(n == 0)
    def _():
        zc_ref[...] = _short_conv(z_ref[...].astype(F32), wz_ref[...], bz_ref[...])

    gc_ref[...] = _short_conv(g_ref[...].astype(F32), wg_ref[...], bg_ref[...])
    zb = zc_ref[...].astype(BF16)
    nyq = 0.5 * _dot(alt_ref[...], zb)[0:1] * hn_ref[0:1]
    row = lax.broadcasted_iota(jnp.int32, (rb, width), 0)
    for r in range(length // rb):
        sl = slice(r * rb, (r + 1) * rb)
        a = _dot(cm_ref[sl, :], zb)
        b = _dot(sm_ref[sl, :], zb)
        ha = ha_ref[sl, :]
        hb = hb_ref[sl, :]
        yr = a * ha - b * hb
        if r == 0:
            yr = yr * jnp.where(row == 0, 0.5, 1.0)
        yr_ref[sl, :] = yr.astype(BF16)
        ym_ref[sl, :] = (a * hb + b * ha).astype(BF16)
    alt = jnp.where((row & 1) == 0, 1.0, -1.0)
    for r in range(length // rb):
        sl = slice(r * rb, (r + 1) * rb)
        y = _dot(cm_ref[sl, :], yr_ref[...]) + _dot(sm_ref[sl, :], ym_ref[...])
        y = (y + alt * nyq) * (1.0 / length)
        z_new = gc_ref[sl, :] * (y + skip_ref[...] * zc_ref[sl, :])

        @pl.when(n == 0)
        def _():
            zc_ref[sl, :] = z_new

        @pl.when(n == HY_ORDER - 1)
        def _():
            o_ref[sl, :] = z_new.astype(o_ref.dtype)


def _hy_conv(u, col0, ch, conv_w, conv_b, out_prev, row_block, length, layer, ha, hb, hn, skip, tables):
    cm, sm, alt8 = tables
    nb, s, _ = u.shape
    nct = ch // LANE_TILE
    assert col0 % LANE_TILE == 0
    cb = col0 // LANE_TILE
    taps = conv_w.shape[0]
    single = dict(pipeline_mode=pl.Buffered(1))
    hspec = lambda rows: pl.BlockSpec((None, rows, LANE_TILE), lambda c, b, n: (n, 0, c))
    zcol = lambda c, n: c
    gcol = lambda c, n: (1 + n) * nct + c
    wspec = lambda rows, col: pl.BlockSpec((rows, LANE_TILE), lambda c, b, n: (0, col(c, n)))
    in_specs = [pl.BlockSpec((None, length, LANE_TILE), lambda c, b, n: (b, row_block, cb + zcol(c, n))),
                pl.BlockSpec((None, length, LANE_TILE), lambda c, b, n: (b, row_block, cb + gcol(c, n))),
                wspec(taps, zcol), wspec(1, zcol), wspec(taps, gcol), wspec(1, gcol),
                hspec(length), hspec(length), hspec(8),
                pl.BlockSpec((None, None, 1, LANE_TILE), lambda c, b, n: (layer, n, 0, c)),
                pl.BlockSpec((length, length), lambda c, b, n: (0, 0), **single),
                pl.BlockSpec((length, length), lambda c, b, n: (0, 0), **single),
                pl.BlockSpec((8, length), lambda c, b, n: (0, 0))]
    args = [u, u, conv_w, conv_b, conv_w, conv_b, ha, hb, hn, skip, cm, sm, alt8]
    n_in = len(args)
    aliases = {}
    if out_prev is not None:
        in_specs.append(pl.BlockSpec(memory_space=pl.ANY))
        args.append(out_prev)
        aliases = {len(args) - 1: 0}

    def body(*refs):
        _hy_conv_kernel(*refs[:n_in], *refs[-5:])

    return pl.pallas_call(
        body,
        out_shape=jax.ShapeDtypeStruct((nb, s, ch), BF16),
        grid=(nct, nb, HY_ORDER),
        in_specs=in_specs,
        out_specs=pl.BlockSpec((None, length, LANE_TILE), lambda c, b, n: (b, row_block, c)),
        scratch_shapes=[pltpu.VMEM((length, LANE_TILE), F32), pltpu.VMEM((length, LANE_TILE), F32),
                        pltpu.VMEM((length, LANE_TILE), BF16), pltpu.VMEM((length, LANE_TILE), BF16)],
        input_output_aliases=aliases,
        compiler_params=_params(("parallel", "parallel", "arbitrary")),
        name="hyena_conv_%d" % length,
    )(*args)


def _merge_kernel(ys_ref, ym_ref, yh_ref, g0_ref, g1_ref, g2_ref, wb_ref, wo_ref, x_ref, mod_ref, o_ref):
    acc = None
    for n, (y_ref, g_ref) in enumerate(((ys_ref, g0_ref), (ym_ref, g1_ref), (yh_ref, g2_ref))):
        term = jax.nn.sigmoid(g_ref[...].astype(F32)) * _dot(y_ref[...], wb_ref[n])
        acc = term if acc is None else acc + term
    o = _dot(acc.astype(BF16), wo_ref[...])
    o_ref[...] = x_ref[...] + mod_ref[2:3] * o


def _merge(ys, ym, yh, pnat, wb, wo, x, mod, n_lat_tiles, n_tiles):
    nb, s, d = x.shape
    bw = ys.shape[-1]
    row = pl.BlockSpec((None, TOKEN_TILE, bw), lambda b, i: (b, i, 0))
    gate = lambda n: pl.BlockSpec((None, TOKEN_TILE, d), lambda b, i: (b, i, 1 + n))
    xrow = pl.BlockSpec((None, TOKEN_TILE, d), lambda b, i: (b, i, 0))
    single = dict(pipeline_mode=pl.Buffered(1))
    return pl.pallas_call(
        _merge_kernel,
        out_shape=jax.ShapeDtypeStruct((nb, s, d), F32),
        grid=(nb, n_tiles),
        in_specs=[row, row, row, gate(0), gate(1), gate(2),
                  pl.BlockSpec((N_BRANCH, bw, d), lambda b, i: (0, 0, 0), **single),
                  pl.BlockSpec((d, d), lambda b, i: (0, 0), **single),
                  xrow,
                  pl.BlockSpec((None, 6, d), _mod_row(nb, n_lat_tiles))],
        out_specs=xrow,
        input_output_aliases={8: 0},
        compiler_params=_params(("parallel", "parallel")),
        name="merge",
    )(ys, ym, yh, pnat, pnat, pnat, wb, wo, x, mod)


def _mlp_kernel(x_ref, nw_ref, mod_ref, w1_ref, w2_ref, o_ref):
    x = x_ref[...]
    m = mod_ref[...]
    h = x * lax.rsqrt(jnp.mean(x * x, axis=-1, keepdims=True) + EPS) * nw_ref[...]
    h = (h * (1.0 + m[4:5]) + m[3:4]).astype(BF16)
    a = jnp.maximum(_dot(h, w1_ref[...]), 0.0)
    o_ref[...] = x + m[5:6] * _dot((a * a).astype(BF16), w2_ref[...])


def _mlp(x, norm_w, layer, mod, w1, w2, n_lat_tiles, n_tiles):
    nb, s, d = x.shape
    hidden = w1.shape[1]
    xrow = pl.BlockSpec((None, TOKEN_TILE, d), lambda b, i: (b, i, 0))
    single = dict(pipeline_mode=pl.Buffered(1))
    return pl.pallas_call(
        _mlp_kernel,
        out_shape=jax.ShapeDtypeStruct((nb, s, d), F32),
        grid=(nb, n_tiles),
        in_specs=[xrow,
                  pl.BlockSpec((None, 1, d), lambda b, i: (layer, 0, 0)),
                  pl.BlockSpec((None, 6, d), _mod_row(nb, n_lat_tiles)),
                  pl.BlockSpec((d, hidden), lambda b, i: (0, 0), **single),
                  pl.BlockSpec((hidden, d), lambda b, i: (0, 0), **single)],
        out_specs=xrow,
        input_output_aliases={0: 0},
        compiler_params=_params(("parallel", "parallel")),
        name="mlp",
    )(x, norm_w, mod, w1, w2)


def _final_norm_kernel(x_ref, w_ref, o_ref):
    x = x_ref[...]
    o_ref[...] = x * lax.rsqrt(jnp.mean(x * x, axis=-1, keepdims=True) + EPS) * w_ref[...]


def _final_norm(x, w, seq):
    nb, _, d = x.shape
    return pl.pallas_call(
        _final_norm_kernel,
        out_shape=jax.ShapeDtypeStruct((nb, seq, d), F32),
        grid=(nb, seq // TOKEN_TILE),
        in_specs=[pl.BlockSpec((None, TOKEN_TILE, d), lambda b, i: (b, i, 0)),
                  pl.BlockSpec((1, d), lambda b, i: (0, 0))],
        out_specs=pl.BlockSpec((None, TOKEN_TILE, d), lambda b, i: (b, i, 0)),
        compiler_params=_params(("parallel", "parallel")),
        name="final_norm",
    )(x, w)


def _lat_colmajor(t, seq):
    nb, _, ch = t.shape
    rows = seq // GRID_W
    lat = t[:, :seq].reshape(nb, rows, GRID_W, ch).swapaxes(1, 2).reshape(nb, seq, ch)
    return jnp.concatenate([lat, t[:, seq:]], axis=1)


def _lat_rowmajor(t, seq):
    nb, _, ch = t.shape
    rows = seq // GRID_W
    lat = t[:, :seq].reshape(nb, GRID_W, rows, ch).swapaxes(1, 2).reshape(nb, seq, ch)
    return jnp.concatenate([lat, t[:, seq:]], axis=1)


def _dir_rows(g, nb, s, per_dir):
    g = g[:, :2 * per_dir].reshape(nb, s, 2, per_dir)
    return jnp.transpose(g, (2, 0, 3, 1))


def kernel(x, c, ctx, c_ctx, norm1_w, mod_w, mod_b, w_in, ssd_conv_w, ssd_conv_b, ssd_dt_bias, ssd_a_log,
           ssd_d, ssd_norm_w, ml_conv_w, ml_conv_b, ml_gate_b, ml_norm_w, hy_conv_w, hy_conv_b, hy_ffn_w1,
           hy_ffn_b1, hy_ffn_w2, hy_ffn_b2, hy_ffn_w3, hy_decay, hy_skip, w_branch, w_out, norm2_w,
           mlp_w1, mlp_w2, norm_f_w):
    nb, seq, d = x.shape
    ctx_len = ctx.shape[1]
    depth = w_in.shape[0]
    s = seq + ctx_len
    assert seq % ctx_len == 0 and ctx_len % TOKEN_TILE == 0 and seq % GRID_W == 0
    n_lat_tiles, n_tiles = seq // TOKEN_TILE, s // TOKEN_TILE
    n_lat_ch, n_ch = seq // CHUNK, s // CHUNK

    ssd_heads = ssd_d.shape[-1]
    ssd_inner = ssd_norm_w.shape[-1]
    ssd_hd = ssd_inner // ssd_heads
    ssd_conv_ch = ssd_conv_w.shape[-1]
    ssd_ds = (ssd_conv_ch - ssd_inner) // (2 * SSD_GROUPS)
    ml_heads = ml_gate_b.shape[-1]
    ml_inner = ml_heads * ML_HEADDIM
    hy_inner = hy_skip.shape[-1]
    ssd_cols = ssd_conv_ch + ssd_inner + 2 * ssd_heads
    ml_cols = 4 * ml_inner + 4 * ml_heads
    rec_cols = ssd_cols + ml_cols
    hy_cols = (HY_ORDER + 1) * hy_inner

    o_z = ssd_conv_ch
    o_dt = ssd_conv_ch + ssd_inner
    o_ml = ssd_cols
    o_mlg = ssd_cols + 4 * ml_inner
    o_hy = rec_cols
    o_g = rec_cols + hy_cols

    xa = jnp.concatenate([x, ctx], axis=1)
    rpad = (-(nb + 1)) % 8
    c_all = jnp.concatenate([c, c_ctx[None], jnp.zeros((rpad, d), F32)], axis=0)

    tab_lat = _dft_tables(seq)
    tab_ctx = _dft_tables(ctx_len)
    k_scale = jnp.concatenate([jnp.ones((1, ml_inner), F32),
                               jnp.full((1, ml_inner), ML_HEADDIM ** -0.5, F32)], axis=1)
    ones_row = lambda n: jnp.ones((1, n), F32)

    norm1 = norm1_w[:, None, :]
    norm2 = norm2_w[:, None, :]
    ssd_norm = ssd_norm_w[:, None, :]
    ml_norm = ml_norm_w[:, None, :]
    d_full = jnp.repeat(ssd_d, ssd_hd, axis=-1)[:, None, :]
    w3 = hy_ffn_w3.reshape(depth, hy_ffn_w3.shape[1], HY_ORDER, 2, hy_inner).transpose(0, 2, 3, 1, 4)
    decay = hy_decay[:, :, :, None, :]
    skip = hy_skip[:, :, None, :]
    hb1 = hy_ffn_b1[:, None, :]
    hb2 = hy_ffn_b2[:, None, :]

    for l in range(depth):
        need_ctx = l < depth - 1
        used_tiles = n_tiles if need_ctx else n_lat_tiles
        mod = _mod_vectors(c_all, mod_w, mod_b[:, None, :], l).reshape(-1, 6, d)

        wl = w_in[l]
        w_nat = jnp.concatenate([wl[:, o_z:o_dt], wl[:, o_g:], wl[:, o_hy:o_g]], axis=1).astype(BF16)
        w_xbc = wl[:, :o_z].astype(BF16)
        w_ml = wl[:, o_ml:o_mlg].astype(BF16)
        gpad = lambda w: jnp.pad(w, ((0, 0), (0, 128 - w.shape[1]))).astype(BF16)
        hn = _normmod(xa, norm1, l, mod, 0, n_lat_tiles, n_tiles)
        hn_cm = _lat_colmajor(hn, seq)
        hn2 = hn.reshape(nb * s, d)
        hn_cm2 = hn_cm.reshape(nb * s, d)
        pnat = _matmul(hn2, w_nat, BF16, "proj_nat").reshape(nb, s, -1)
        pml = _matmul(hn_cm2, w_ml, BF16, "proj_ml").reshape(nb, s, -1)
        p_dt = _matmul(hn2, gpad(wl[:, o_dt:o_ml]), F32, "proj_dt")
        p_mlg = _matmul(hn_cm2, gpad(wl[:, o_mlg:o_hy]), F32, "proj_mlg")
        pxbc = _matmul(hn2, w_xbc, BF16, "proj_xbc").reshape(nb, s, -1)
        c_g = ssd_inner
        c_hy = c_g + N_BRANCH * d

        xbc = _dwconv(pxbc, 0, ssd_conv_ch, ssd_conv_w[l], ssd_conv_b[l][None], ones_row(ssd_conv_ch),
                      True, n_lat_tiles, n_tiles, "ssd_conv")
        dt_r = _dir_rows(p_dt, nb, s, ssd_heads)
        y_ssd = _ssd_scan(xbc, dt_r, ssd_dt_bias[l][:, :, None], ssd_a_log[l][:, :, None],
                          ssd_heads, ssd_hd, ssd_ds, n_lat_ch, n_ch)
        ys = _ssd_out(y_ssd, xbc, pnat, d_full, ssd_norm, l, used_tiles)

        qk = _dwconv(pml, 0, 2 * ml_inner, ml_conv_w[l], ml_conv_b[l][None], k_scale,
                     True, n_lat_tiles, n_tiles, "ml_conv")
        g_r = _dir_rows(p_mlg, nb, s, 2 * ml_heads)
        gate_b = ml_gate_b[l]
        cell = _ml_scan(qk, pml, g_r[:, :, :ml_heads], g_r[:, :, ml_heads:],
                        gate_b[:, 0, :, None], gate_b[:, 1, :, None], ml_heads, n_lat_ch, n_ch)
        ym = _lat_rowmajor(_ml_out(cell, pml, ml_norm, l, ml_heads, used_tiles), seq)

        hcw, hcb = hy_conv_w[l], hy_conv_b[l][None]
        fl = _hy_filters(seq, l, hy_ffn_w1, hb1, hy_ffn_w2, hb2, w3, decay, tab_lat)
        yh = _hy_conv(pnat, c_hy, hy_inner, hcw, hcb, None, 0, seq, l, *fl, skip, tab_lat)
        if need_ctx:
            fc = _hy_filters(ctx_len, l, hy_ffn_w1, hb1, hy_ffn_w2, hb2, w3, decay, tab_ctx)
            yh = _hy_conv(pnat, c_hy, hy_inner, hcw, hcb, yh, seq // ctx_len, ctx_len, l, *fc, skip, tab_ctx)

        xa = _merge(ys, ym, yh, pnat, w_branch[l].astype(BF16), w_out[l].astype(BF16), xa, mod,
                    n_lat_tiles, used_tiles)
        xa = _mlp(xa, norm2, l, mod, mlp_w1[l].astype(BF16), mlp_w2[l].astype(BF16), n_lat_tiles, used_tiles)

    return _final_norm(xa, norm_f_w[None], seq)
```

```python
import functools
import math

import jax
import jax.numpy as jnp
import numpy as np
from jax import lax
from jax.experimental import pallas as pl
from jax.experimental.pallas import tpu as pltpu

F32 = jnp.float32
BF16 = jnp.bfloat16
HIGHEST = lax.Precision.HIGHEST

GRID_W = 64
CHUNK = 128
EPS = 1e-6
SSD_GROUPS = 2
SSD_CONV = 5
ML_HEADDIM = 128
ML_CONV = 5
HY_ORDER = 2
HY_SHORT = 3
HY_BANDS = 16
N_BRANCH = 3

TOKEN_TILE = 256
LANE_TILE = 256
HALO = 16
VMEM_LIMIT = 56 * 1024 * 1024

_hdot = functools.partial(jnp.dot, precision=HIGHEST, preferred_element_type=F32)
_dot = functools.partial(jnp.dot, preferred_element_type=F32)


def _params(sem, vmem=None):
    return pltpu.CompilerParams(dimension_semantics=sem, vmem_limit_bytes=vmem or VMEM_LIMIT)


def _softplus(x):
    return jnp.maximum(x, 0.0) + jnp.log(1.0 + jnp.exp(-jnp.abs(x)))


def _silu(x):
    return x * jax.nn.sigmoid(x)


def _mod_kernel(c_ref, w_ref, b_ref, o_ref):
    o_ref[...] = _hdot(_silu(c_ref[...]), w_ref[...]) + b_ref[...]


def _mod_vectors(c_all, mod_w, mod_b, layer):
    r, d = c_all.shape
    n = mod_w.shape[-1]
    tn = n // 6
    return pl.pallas_call(
        _mod_kernel,
        out_shape=jax.ShapeDtypeStruct((r, n), F32),
        grid=(n // tn,),
        in_specs=[pl.BlockSpec((r, d), lambda j: (0, 0)),
                  pl.BlockSpec((None, d, tn), lambda j: (layer, 0, j)),
                  pl.BlockSpec((None, 1, tn), lambda j: (layer, 0, j))],
        out_specs=pl.BlockSpec((r, tn), lambda j: (0, j)),
        compiler_params=_params(("parallel",)),
        name="mod_vectors",
    )(c_all, mod_w, mod_b)


def _normmod_kernel(x_ref, nw_ref, mod_ref, o_ref, *, si):
    x = x_ref[...]
    h = x * lax.rsqrt(jnp.mean(x * x, axis=-1, keepdims=True) + EPS) * nw_ref[...]
    m = mod_ref[...]
    o_ref[...] = (h * (1.0 + m[si + 1:si + 2]) + m[si:si + 1]).astype(o_ref.dtype)


def _mod_row(nb, n_lat_tiles):
    return lambda b, i: (jnp.where(i < n_lat_tiles, b, nb), 0, 0)


def _normmod(x, norm_w, layer, mod, si, n_lat_tiles, n_tiles):
    nb, s, d = x.shape
    return pl.pallas_call(
        functools.partial(_normmod_kernel, si=si),
        out_shape=jax.ShapeDtypeStruct((nb, s, d), BF16),
        grid=(nb, n_tiles),
        in_specs=[pl.BlockSpec((None, TOKEN_TILE, d), lambda b, i: (b, i, 0)),
                  pl.BlockSpec((None, 1, d), lambda b, i: (layer, 0, 0)),
                  pl.BlockSpec((None, 6, d), _mod_row(nb, n_lat_tiles))],
        out_specs=pl.BlockSpec((None, TOKEN_TILE, d), lambda b, i: (b, i, 0)),
        compiler_params=_params(("parallel", "parallel")),
        name="normmod",
    )(x, norm_w, mod)


def _mm_kernel(a_ref, w_ref, o_ref):
    o_ref[...] = _dot(a_ref[...], w_ref[...]).astype(o_ref.dtype)


def _pick(n, cands):
    for c in cands:
        if n % c == 0:
            return c
    return n


def _matmul(a, w, out_dtype, name):
    t, k = a.shape
    n = w.shape[1]
    tm = _pick(t, (1024, 768, 512, 256))
    tn = _pick(n, (1024, 512, 256, 128))
    return pl.pallas_call(
        _mm_kernel,
        out_shape=jax.ShapeDtypeStruct((t, n), out_dtype),
        grid=(t // tm, n // tn),
        in_specs=[pl.BlockSpec((tm, k), lambda i, j: (i, 0)),
                  pl.BlockSpec((k, tn), lambda i, j: (0, j))],
        out_specs=pl.BlockSpec((tm, tn), lambda i, j: (i, j)),
        compiler_params=_params(("parallel", "parallel")),
        name=name,
    )(a, w)


def _dwconv_kernel(u_ref, p_ref, n_ref, w_ref, b_ref, s_ref, o_ref, *, taps, act, bounds):
    i = pl.program_id(1)
    tm = u_ref.shape[0]
    pad = taps // 2
    lv, rv = jnp.float32(1.0), jnp.float32(1.0)
    for e in bounds:
        lv = jnp.where(i == e, 0.0, lv)
        rv = jnp.where(i == e - 1, 0.0, rv)
    ext = jnp.concatenate([p_ref[...].astype(F32) * lv, u_ref[...].astype(F32),
                           n_ref[...].astype(F32) * rv], axis=0)
    rows = ext.shape[0]
    w = w_ref[...]
    acc = jnp.zeros((tm, u_ref.shape[1]), F32) + b_ref[...]
    for j in range(taps):
        sh = (pad - j) % rows
        shifted = ext if sh == 0 else pltpu.roll(ext, sh, 0)
        acc = acc + shifted[HALO:HALO + tm] * w[j:j + 1]
    if act:
        acc = _silu(acc)
    o_ref[...] = (acc * s_ref[...]).astype(o_ref.dtype)


def _dwconv(u, col0, ncols, w, b, scale, act, n_lat_tiles, n_tiles, name):
    nb, s, _ = u.shape
    taps = w.shape[0]
    tc = _pick(ncols, (2048, 1536, 1024, 512, 256))
    assert col0 % tc == 0
    c0 = col0 // tc
    hb = TOKEN_TILE // HALO
    last = s // HALO - 1
    bounds = (0, n_lat_tiles, n_tiles)
    return pl.pallas_call(
        functools.partial(_dwconv_kernel, taps=taps, act=act, bounds=bounds),
        out_shape=jax.ShapeDtypeStruct((nb, s, ncols), BF16),
        grid=(nb, n_tiles, ncols // tc),
        in_specs=[pl.BlockSpec((None, TOKEN_TILE, tc), lambda bb, i, c: (bb, i, c0 + c)),
                  pl.BlockSpec((None, HALO, tc), lambda bb, i, c: (bb, jnp.maximum(i * hb - 1, 0), c0 + c)),
                  pl.BlockSpec((None, HALO, tc), lambda bb, i, c: (bb, jnp.minimum((i + 1) * hb, last), c0 + c)),
                  pl.BlockSpec((taps, tc), lambda bb, i, c: (0, c)),
                  pl.BlockSpec((1, tc), lambda bb, i, c: (0, c)),
                  pl.BlockSpec((1, tc), lambda bb, i, c: (0, c))],
        out_specs=pl.BlockSpec((None, TOKEN_TILE, tc), lambda bb, i, c: (bb, i, c)),
        compiler_params=_params(("parallel", "parallel", "parallel")),
        name=name,
    )(u, u, u, w, b, scale)


def _chunk_index(n_lat_ch, n_ch):
    return lambda d, j: jnp.where(d == 0, (j + n_lat_ch) % n_ch, n_ch - 1 - j)


def _scan_masks(d, t):
    ii = lax.broadcasted_iota(jnp.int32, (t, t), 0)
    jj = lax.broadcasted_iota(jnp.int32, (t, t), 1)
    sgn = jnp.where(d == 0, 1, -1)
    mask = (ii - jj) * sgn >= 0
    tri_t = jnp.where((jj - ii) * sgn >= 0, 1.0, 0.0).astype(F32)
    return mask, tri_t


def _gate_rows(la_r, lw_r, tri_t, m_prev):
    cum_r = _hdot(la_r, tri_t)
    last = jnp.sum(la_r, axis=1, keepdims=True)
    g_r = last - cum_r + lw_r
    m_loc = jnp.max(g_r, axis=1, keepdims=True)
    e_r = jnp.exp(g_r - m_loc)
    m_new = jnp.maximum(last + m_prev, m_loc)
    s_old = jnp.exp(last + m_prev - m_new)
    s_new = jnp.exp(m_loc - m_new)
    return cum_r, cum_r - lw_r, e_r, m_new, s_old, s_new


def _head_probs(cum_row, crow_row, m_prev_h, mask, qk):
    t = qk.shape[0]
    colb = jnp.broadcast_to(cum_row, (t, t)).T
    dlog = jnp.where(mask, colb - crow_row, -jnp.inf)
    inter = colb[:, 0:1] + m_prev_h
    m_row = jnp.maximum(inter, jnp.max(dlog, axis=1, keepdims=True))
    p = jnp.exp(dlog - m_row) * qk
    return p.astype(BF16), jnp.exp(inter - m_row), m_row


def _ssd_scan_kernel(xs_ref, b_ref, c_ref, dt_ref, bias_ref, alog_ref, o_ref, st_ref,
                     *, heads, groups, hd, ds):
    d = pl.program_id(1)
    j = pl.program_id(2)
    t = xs_ref.shape[0]
    hpg = heads // groups
    pw = 2 * hd

    @pl.when(j == 0)
    def _():
        st_ref[...] = jnp.zeros_like(st_ref)

    mask, tri_t = _scan_masks(d, t)
    dt = _softplus(dt_ref[...] + bias_ref[...])
    la_r = -dt * jnp.exp(alog_ref[...])
    cum_r = _hdot(la_r, tri_t)
    last = jnp.sum(la_r, axis=1, keepdims=True)
    crow = cum_r - jnp.log(dt)
    e_r = jnp.exp(last - crow)
    e_last = jnp.exp(last)
    lo = lax.broadcasted_iota(jnp.int32, (1, pw), 1) < hd

    for g in range(groups):
        q = c_ref[:, g * ds:(g + 1) * ds]
        k_t = b_ref[:, g * ds:(g + 1) * ds].astype(F32).T
        qk = _dot(q, k_t.astype(BF16))
        w0 = g * hpg * hd
        qs = _dot(q, st_ref[:, w0:w0 + hpg * hd].astype(BF16))
        for i in range(hpg // 2):
            h0 = g * hpg + 2 * i
            c0 = h0 * hd
            vp = xs_ref[:, c0:c0 + pw]
            zero = jnp.zeros_like(vp)
            v_bd = jnp.concatenate([jnp.where(lo, vp, zero), jnp.where(lo, zero, vp)], axis=0)
            probs, carry, kte = [], [], []
            for h in (h0, h0 + 1):
                colb = jnp.broadcast_to(cum_r[h:h + 1], (t, t)).T
                dlog = jnp.where(mask, colb - crow[h:h + 1], -jnp.inf)
                probs.append((jnp.exp(dlog) * qk).astype(BF16))
                carry.append(jnp.exp(colb))
                kte.append((k_t * e_r[h:h + 1]).astype(BF16))
            lhs = jnp.concatenate([jnp.concatenate(probs, axis=1), jnp.concatenate(kte, axis=1)], axis=0)
            res = _dot(lhs, v_bd)
            o_ref[:, c0:c0 + pw] = res[:t] + qs[:, 2 * i * hd:2 * i * hd + pw] * jnp.where(lo, carry[0], carry[1])
            decay = jnp.where(lo, e_last[h0:h0 + 1], e_last[h0 + 1:h0 + 2])
            st_ref[:, c0:c0 + pw] = decay * st_ref[:, c0:c0 + pw] + res[t:]


def _ssd_scan(xbc, dt_r, bias_c, alog_c, heads, hd, ds, n_lat_ch, n_ch):
    nb, s, _ = xbc.shape
    inner = heads * hd
    gn = SSD_GROUPS * ds
    assert 2 * hd == CHUNK and (heads // SSD_GROUPS) % 2 == 0 and inner % gn == 0
    cidx = _chunk_index(n_lat_ch, n_ch)
    return pl.pallas_call(
        functools.partial(_ssd_scan_kernel, heads=heads, groups=SSD_GROUPS, hd=hd, ds=ds),
        out_shape=jax.ShapeDtypeStruct((2, nb, s, inner), F32),
        grid=(nb, 2, n_ch),
        in_specs=[pl.BlockSpec((None, CHUNK, inner), lambda b, d, j: (b, cidx(d, j), 0)),
                  pl.BlockSpec((None, CHUNK, gn), lambda b, d, j: (b, cidx(d, j), inner // gn)),
                  pl.BlockSpec((None, CHUNK, gn), lambda b, d, j: (b, cidx(d, j), inner // gn + 1)),
                  pl.BlockSpec((None, None, heads, CHUNK), lambda b, d, j: (d, b, 0, cidx(d, j))),
                  pl.BlockSpec((None, heads, 1), lambda b, d, j: (d, 0, 0)),
                  pl.BlockSpec((None, heads, 1), lambda b, d, j: (d, 0, 0))],
        out_specs=pl.BlockSpec((None, None, CHUNK, inner), lambda b, d, j: (d, b, cidx(d, j), 0)),
        scratch_shapes=[pltpu.VMEM((ds, inner), F32)],
        compiler_params=_params(("parallel", "parallel", "arbitrary")),
        name="ssd_scan",
    )(xbc, xbc, xbc, dt_r, bias_c, alog_c)


def _ml_scan_kernel(q_ref, k_ref, v_ref, gi_ref, gf_ref, bi_ref, bf_ref, o_ref, st_ref, m_ref, *, heads):
    d = pl.program_id(1)
    j = pl.program_id(2)
    t = q_ref.shape[0]
    dh = ML_HEADDIM

    @pl.when(j == 0)
    def _():
        st_ref[...] = jnp.zeros_like(st_ref)
        m_ref[...] = jnp.zeros_like(m_ref)

    mask, tri_t = _scan_masks(d, t)
    lw_r = gi_ref[...] + bi_ref[...]
    f = gf_ref[...] + bf_ref[...]
    la_r = jnp.minimum(f, 0.0) - jnp.log(1.0 + jnp.exp(-jnp.abs(f)))
    m_prev = m_ref[:, 0:1]
    cum_r, crow, e_r, m_new, s_old, s_new = _gate_rows(la_r, lw_r, tri_t, m_prev)
    ones = jnp.ones((t, dh), BF16)

    for h in range(heads):
        c0 = h * dh
        q = q_ref[:, c0:c0 + dh]
        k_t = k_ref[:, c0:c0 + dh].astype(F32).T
        v_aug = jnp.concatenate([v_ref[:, c0:c0 + dh], ones], axis=1)
        qk = _dot(q, k_t.astype(BF16))
        s0 = 2 * c0
        qs = _dot(q, st_ref[:, s0:s0 + 2 * dh].astype(BF16))
        p, cf, m_row = _head_probs(cum_r[h:h + 1], crow[h:h + 1], m_prev[h:h + 1], mask, qk)
        res = _dot(jnp.concatenate([p, (k_t * e_r[h:h + 1]).astype(BF16)], axis=0), v_aug)
        y = res[:t] + qs * cf
        o_ref[:, c0:c0 + dh] = y[:, :dh] / jnp.maximum(jnp.abs(y[:, dh:]), jnp.exp(-m_row))
        st_ref[:, s0:s0 + 2 * dh] = s_old[h:h + 1] * st_ref[:, s0:s0 + 2 * dh] + s_new[h:h + 1] * res[t:]
    m_ref[...] = jnp.broadcast_to(m_new, m_ref.shape)


def _ml_scan(qk, pml, gi_r, gf_r, bi_c, bf_c, heads, n_lat_ch, n_ch):
    nb, s, _ = qk.shape
    inner = heads * ML_HEADDIM
    cidx = _chunk_index(n_lat_ch, n_ch)
    gspec = pl.BlockSpec((None, None, heads, CHUNK), lambda b, d, j: (d, b, 0, cidx(d, j)))
    bspec = pl.BlockSpec((None, heads, 1), lambda b, d, j: (d, 0, 0))
    return pl.pallas_call(
        functools.partial(_ml_scan_kernel, heads=heads),
        out_shape=jax.ShapeDtypeStruct((2, nb, s, inner), F32),
        grid=(nb, 2, n_ch),
        in_specs=[pl.BlockSpec((None, CHUNK, inner), lambda b, d, j: (b, cidx(d, j), 0)),
                  pl.BlockSpec((None, CHUNK, inner), lambda b, d, j: (b, cidx(d, j), 1)),
                  pl.BlockSpec((None, CHUNK, inner), lambda b, d, j: (b, cidx(d, j), 2)),
                  gspec, gspec, bspec, bspec],
        out_specs=pl.BlockSpec((None, None, CHUNK, inner), lambda b, d, j: (d, b, cidx(d, j), 0)),
        scratch_shapes=[pltpu.VMEM((ML_HEADDIM, 2 * inner), F32), pltpu.VMEM((heads, 128), F32)],
        compiler_params=_params(("parallel", "parallel", "arbitrary")),
        name="ml_scan",
    )(qk, qk, pml, gi_r, gf_r, bi_c, bf_c)


def _ssd_out_kernel(yf_ref, yb_ref, xs_ref, z_ref, d_ref, nw_ref, o_ref):
    y = yf_ref[...] + yb_ref[...] + d_ref[...] * xs_ref[...].astype(F32)
    y = y * _silu(z_ref[...].astype(F32))
    y = y * lax.rsqrt(jnp.mean(y * y, axis=-1, keepdims=True) + EPS) * nw_ref[...]
    o_ref[...] = y.astype(o_ref.dtype)


def _ssd_out(y, xbc, pnat, d_full, norm_w, layer, n_tiles):
    _, nb, s, inner = y.shape
    row = pl.BlockSpec((None, TOKEN_TILE, inner), lambda b, i: (b, i, 0))
    return pl.pallas_call(
        _ssd_out_kernel,
        out_shape=jax.ShapeDtypeStruct((nb, s, inner), BF16),
        grid=(nb, n_tiles),
        in_specs=[pl.BlockSpec((None, None, TOKEN_TILE, inner), lambda b, i: (0, b, i, 0)),
                  pl.BlockSpec((None, None, TOKEN_TILE, inner), lambda b, i: (1, b, i, 0)),
                  row, row,
                  pl.BlockSpec((None, 1, inner), lambda b, i: (layer, 0, 0)),
                  pl.BlockSpec((None, 1, inner), lambda b, i: (layer, 0, 0))],
        out_specs=row,
        compiler_params=_params(("parallel", "parallel")),
        name="ssd_out",
    )(y, y, xbc, pnat, d_full, norm_w)


def _ml_out_kernel(cf_ref, cb_ref, o_in_ref, nw_ref, o_ref, *, heads):
    dh = ML_HEADDIM
    for h in range(heads):
        sl = slice(h * dh, (h + 1) * dh)
        c = cf_ref[:, sl] + cb_ref[:, sl]
        c = c * lax.rsqrt(jnp.mean(c * c, axis=-1, keepdims=True) + EPS) * nw_ref[:, sl]
        o_ref[:, sl] = (jax.nn.sigmoid(o_in_ref[:, sl].astype(F32)) * c).astype(o_ref.dtype)


def _ml_out(cell, pml, norm_w, layer, heads, n_tiles):
    _, nb, s, inner = cell.shape
    return pl.pallas_call(
        functools.partial(_ml_out_kernel, heads=heads),
        out_shape=jax.ShapeDtypeStruct((nb, s, inner), BF16),
        grid=(nb, n_tiles),
        in_specs=[pl.BlockSpec((None, None, TOKEN_TILE, inner), lambda b, i: (0, b, i, 0)),
                  pl.BlockSpec((None, None, TOKEN_TILE, inner), lambda b, i: (1, b, i, 0)),
                  pl.BlockSpec((None, TOKEN_TILE, inner), lambda b, i: (b, i, 3)),
                  pl.BlockSpec((None, 1, inner), lambda b, i: (layer, 0, 0))],
        out_specs=pl.BlockSpec((None, TOKEN_TILE, inner), lambda b, i: (b, i, 0)),
        compiler_params=_params(("parallel", "parallel")),
        name="ml_out",
    )(cell, cell, pml, norm_w)


def _dft_tables(length):
    m = length // 2
    k = np.arange(m, dtype=np.int64)
    ang = ((k[:, None] * k[None, :]) % (2 * m)).astype(np.float64) * (np.pi / m)
    alt8 = np.broadcast_to(np.where(k % 2 == 0, 1.0, -1.0)[None, :], (8, m))
    tw = k.astype(np.float64)[:, None] * (np.pi / length) * np.ones((1, LANE_TILE))
    return (jnp.asarray(np.cos(ang), BF16), jnp.asarray(np.sin(ang), BF16), jnp.asarray(alt8, BF16),
            jnp.asarray(np.cos(tw), F32), jnp.asarray(np.sin(tw), F32))


def _hy_feats(length):
    t = jnp.arange(length, dtype=F32)
    t_norm = t / length
    bands = jnp.linspace(1e-4, HY_BANDS - 1, HY_BANDS, dtype=F32)
    ang = (2.0 * math.pi / length) * t[:, None] * bands[None, :]
    feats = jnp.concatenate([t_norm[:, None], jnp.cos(ang), -jnp.sin(ang)], axis=-1)
    return feats[0::2], feats[1::2], t_norm[0::2, None], t_norm[1::2, None]


def _split_spectrum(ae, be, ao, bo, twc, tws):
    tr = twc * ao - tws * bo
    tm = twc * bo + tws * ao
    return (ae + tr, ae - tr), (be + tm, tm - be)


def _hy_filter_kernel(fe_ref, fo_ref, tne_ref, tno_ref, w1_ref, b1_ref, w2_ref, b2_ref, w3f_ref, w3b_ref,
                      df_ref, db_ref, cm_ref, sm_ref, alt_ref, twc_ref, tws_ref, ha_ref, hb_ref, hm_ref):
    m = cm_ref.shape[0]

    def taps(feats, tn):
        hid = jnp.sin(_hdot(feats, w1_ref[...]) + b1_ref[...])
        hid = jnp.sin(_hdot(hid, w2_ref[...]) + b2_ref[...])
        h_f = _hdot(hid, w3f_ref[...]) * jnp.exp(-tn * jnp.abs(df_ref[...]))
        h_b = _hdot(hid, w3b_ref[...]) * jnp.exp(-tn * jnp.abs(db_ref[...]))
        return (h_f + h_b).astype(BF16), (h_f - h_b).astype(BF16)

    sum_e, dif_e = taps(fe_ref[...], tne_ref[...])
    sum_o, dif_o = taps(fo_ref[...], tno_ref[...])
    twc, tws = twc_ref[...], tws_ref[...]
    cm, sm = cm_ref[...], sm_ref[...]
    (ha_lo, ha_hi), _ = _split_spectrum(_dot(cm, sum_e), 0.0, _dot(cm, sum_o), _dot(sm, sum_o), twc, tws)
    _, (hb_lo, hb_hi) = _split_spectrum(0.0, _dot(sm, dif_e), _dot(cm, dif_o), _dot(sm, dif_o), twc, tws)
    ha_ref[0:m, :] = ha_lo
    ha_ref[m:2 * m, :] = ha_hi
    hb_ref[0:m, :] = hb_lo
    hb_ref[m:2 * m, :] = hb_hi
    row = lax.broadcasted_iota(jnp.int32, hm_ref.shape, 0)
    hm_ref[...] = jnp.where(row == 0, _dot(alt_ref[...], sum_e), _dot(alt_ref[...], dif_o))


def _hy_filters(length, layer, w1, b1, w2, b2, w3, decay, tables):
    cm, sm, alt8, twc, tws = tables
    m = length // 2
    feats = _hy_feats(length)
    nfeat, nf = w1.shape[-2:]
    ch = decay.shape[-1]
    nct = ch // LANE_TILE
    const = lambda shape: pl.BlockSpec(shape, lambda n, c: (0,) * len(shape))
    w3spec = lambda dr: pl.BlockSpec((None, None, None, nf, LANE_TILE), lambda n, c: (layer, n, dr, 0, c))
    dspec = lambda dr: pl.BlockSpec((None, None, None, 1, LANE_TILE), lambda n, c: (layer, n, dr, 0, c))
    lay = lambda a, b: pl.BlockSpec((None, a, b), lambda n, c: (layer, 0, 0))
    out = lambda rows: pl.BlockSpec((None, rows, LANE_TILE), lambda n, c: (n, 0, c))
    return pl.pallas_call(
        _hy_filter_kernel,
        out_shape=(jax.ShapeDtypeStruct((HY_ORDER, length, ch), F32),
                   jax.ShapeDtypeStruct((HY_ORDER, length, ch), F32),
                   jax.ShapeDtypeStruct((HY_ORDER, 8, ch), F32)),
        grid=(HY_ORDER, nct),
        in_specs=[const((m, nfeat)), const((m, nfeat)), const((m, 1)), const((m, 1)),
                  lay(nfeat, nf), lay(1, nf), lay(nf, nf), lay(1, nf),
                  w3spec(0), w3spec(1), dspec(0), dspec(1),
                  const((m, m)), const((m, m)), const((8, m)),
                  const((m, LANE_TILE)), const((m, LANE_TILE))],
        out_specs=(out(length), out(length), out(8)),
        compiler_params=_params(("parallel", "parallel")),
        name="hyena_filters_%d" % length,
    )(*feats, w1, b1, w2, b2, w3, w3, decay, decay, cm, sm, alt8, twc, tws)


def _short_conv(u, w, b):
    length = u.shape[0]
    taps = w.shape[0]
    pad = taps // 2
    row = lax.broadcasted_iota(jnp.int32, u.shape, 0)
    acc = b + u * w[pad:pad + 1]
    for j in range(taps):
        off = j - pad
        if off != 0:
            moved = pltpu.roll(u, (-off) % length, 0)
            valid = (row + off >= 0) & (row + off < length)
            acc = acc + jnp.where(valid, moved, 0.0) * w[j:j + 1]
    return acc


def _hy_conv_kernel(z_ref, g_ref, wz_ref, bz_ref, wg_ref, bg_ref, ha_ref, hb_ref, hm_ref, skip_ref,
                    cm_ref, sm_ref, alt_ref, twc_ref, tws_ref, o_ref, tmp_ref, ze_ref, zo_ref):
    n = pl.program_id(2)
    m, width = ze_ref.shape
    lanes = tmp_ref.shape[-1]
    slabs = range(width // lanes)

    def put(val):
        for h in slabs:
            tmp_ref[h] = val[:, h * lanes:(h + 1) * lanes]

    def take(first):
        return jnp.concatenate([tmp_ref[h, pl.ds(first, m, stride=2), :] for h in slabs], axis=1)

    @pl.when(n == 0)
    def _():
        put(_short_conv(z_ref[...].astype(F32), wz_ref[...], bz_ref[...]))
        ze_ref[...] = take(0)
        zo_ref[...] = take(1)

    put(_short_conv(g_ref[...].astype(F32), wg_ref[...], bg_ref[...]))
    ge, go = take(0), take(1)
    ze, zo = ze_ref[...], zo_ref[...]
    zz = jnp.concatenate([ze.astype(BF16), zo.astype(BF16)], axis=1)
    a = _dot(cm_ref[...], zz)
    b = _dot(sm_ref[...], zz)
    mid = _dot(alt_ref[...], zz)[0:1]
    twc, tws = twc_ref[...], tws_ref[...]
    (a_lo, a_hi), (b_lo, b_hi) = _split_spectrum(a[:, :width], b[:, :width], a[:, width:], b[:, width:], twc, tws)

    ha_lo, ha_hi, hb_lo, hb_hi = ha_ref[0:m, :], ha_ref[m:2 * m, :], hb_ref[0:m, :], hb_ref[m:2 * m, :]
    yr_lo, ym_lo = a_lo * ha_lo - b_lo * hb_lo, a_lo * hb_lo + b_lo * ha_lo
    yr_hi, ym_hi = a_hi * ha_hi - b_hi * hb_hi, a_hi * hb_hi + b_hi * ha_hi
    ha_m, hb_m = hm_ref[0:1, :], hm_ref[1:2, :]
    yr_m = mid[:, :width] * ha_m - mid[:, width:] * hb_m
    ym_m = mid[:, :width] * hb_m + mid[:, width:] * ha_m

    row = lax.broadcasted_iota(jnp.int32, (m, width), 0)
    half0 = jnp.where(row == 0, 0.5, 1.0)
    qr, qm = yr_lo - yr_hi, ym_lo + ym_hi
    pr = jnp.concatenate([((yr_lo + yr_hi) * half0).astype(BF16),
                          ((qr * twc + qm * tws) * half0).astype(BF16)], axis=1)
    pm = jnp.concatenate([(ym_lo - ym_hi).astype(BF16), (qm * twc - qr * tws).astype(BF16)], axis=1)
    y = _dot(cm_ref[...], pr) + _dot(sm_ref[...], pm)
    alt = jnp.where((row & 1) == 0, 1.0, -1.0)
    scale = 1.0 / (2 * m)
    skip = skip_ref[...]
    ze_new = ge * ((y[:, :width] + alt * yr_m) * scale + skip * ze)
    zo_new = go * ((y[:, width:] + alt * ym_m) * scale + skip * zo)

    @pl.when(n < HY_ORDER - 1)
    def _():
        ze_ref[...] = ze_new
        zo_ref[...] = zo_new

    @pl.when(n == HY_ORDER - 1)
    def _():
        for h in slabs:
            tmp_ref[h, pl.ds(0, m, stride=2), :] = ze_new[:, h * lanes:(h + 1) * lanes]
            tmp_ref[h, pl.ds(1, m, stride=2), :] = zo_new[:, h * lanes:(h + 1) * lanes]
            o_ref[:, h * lanes:(h + 1) * lanes] = tmp_ref[h].astype(o_ref.dtype)


def _hy_conv(u, col0, ch, conv_w, conv_b, out_prev, row_block, length, layer, ha, hb, hm, skip, tables):
    cm, sm, alt8, twc, tws = tables
    nb, s, _ = u.shape
    m = length // 2
    nct = ch // LANE_TILE
    assert col0 % LANE_TILE == 0
    cb = col0 // LANE_TILE
    taps = conv_w.shape[0]
    const = lambda shape: pl.BlockSpec(shape, lambda c, b, n: (0,) * len(shape))
    hspec = lambda rows: pl.BlockSpec((None, rows, LANE_TILE), lambda c, b, n: (n, 0, c))
    zcol = lambda c, n: c
    gcol = lambda c, n: (1 + n) * nct + c
    wspec = lambda rows, col: pl.BlockSpec((rows, LANE_TILE), lambda c, b, n: (0, col(c, n)))
    in_specs = [pl.BlockSpec((None, length, LANE_TILE), lambda c, b, n: (b, row_block, cb + zcol(c, n))),
                pl.BlockSpec((None, length, LANE_TILE), lambda c, b, n: (b, row_block, cb + gcol(c, n))),
                wspec(taps, zcol), wspec(1, zcol), wspec(taps, gcol), wspec(1, gcol),
                hspec(length), hspec(length), hspec(8),
                pl.BlockSpec((None, None, 1, LANE_TILE), lambda c, b, n: (layer, n, 0, c)),
                const((m, m)), const((m, m)), const((8, m)), const((m, LANE_TILE)), const((m, LANE_TILE))]
    args = [u, u, conv_w, conv_b, conv_w, conv_b, ha, hb, hm, skip, cm, sm, alt8, twc, tws]
    n_in = len(args)
    aliases = {}
    if out_prev is not None:
        in_specs.append(pl.BlockSpec(memory_space=pl.ANY))
        args.append(out_prev)
        aliases = {len(args) - 1: 0}

    def body(*refs):
        _hy_conv_kernel(*refs[:n_in], *refs[-4:])

    return pl.pallas_call(
        body,
        out_shape=jax.ShapeDtypeStruct((nb, s, ch), BF16),
        grid=(nct, nb, HY_ORDER),
        in_specs=in_specs,
        out_specs=pl.BlockSpec((None, length, LANE_TILE), lambda c, b, n: (b, row_block, c)),
        scratch_shapes=[pltpu.VMEM((LANE_TILE // 128, length, 128), F32), pltpu.VMEM((m, LANE_TILE), F32),
                        pltpu.VMEM((m, LANE_TILE), F32)],
        input_output_aliases=aliases,
        compiler_params=_params(("parallel", "parallel", "arbitrary")),
        name="hyena_conv_%d" % length,
    )(*args)


def _merge_kernel(ys_ref, ym_ref, yh_ref, g0_ref, g1_ref, g2_ref, wb_ref, wo_ref, x_ref, mod_ref, o_ref):
    acc = None
    for n, (y_ref, g_ref) in enumerate(((ys_ref, g0_ref), (ym_ref, g1_ref), (yh_ref, g2_ref))):
        term = jax.nn.sigmoid(g_ref[...].astype(F32)) * _dot(y_ref[...], wb_ref[n])
        acc = term if acc is None else acc + term
    o = _dot(acc.astype(BF16), wo_ref[...])
    o_ref[...] = x_ref[...] + mod_ref[2:3] * o


def _merge(ys, ym, yh, pnat, wb, wo, x, mod, n_lat_tiles, n_tiles):
    nb, s, d = x.shape
    bw = ys.shape[-1]
    row = pl.BlockSpec((None, TOKEN_TILE, bw), lambda b, i: (b, i, 0))
    gate = lambda n: pl.BlockSpec((None, TOKEN_TILE, d), lambda b, i: (b, i, 1 + n))
    xrow = pl.BlockSpec((None, TOKEN_TILE, d), lambda b, i: (b, i, 0))
    single = dict(pipeline_mode=pl.Buffered(1))
    return pl.pallas_call(
        _merge_kernel,
        out_shape=jax.ShapeDtypeStruct((nb, s, d), F32),
        grid=(nb, n_tiles),
        in_specs=[row, row, row, gate(0), gate(1), gate(2),
                  pl.BlockSpec((N_BRANCH, bw, d), lambda b, i: (0, 0, 0), **single),
                  pl.BlockSpec((d, d), lambda b, i: (0, 0), **single),
                  xrow,
                  pl.BlockSpec((None, 6, d), _mod_row(nb, n_lat_tiles))],
        out_specs=xrow,
        input_output_aliases={8: 0},
        compiler_params=_params(("parallel", "parallel")),
        name="merge",
    )(ys, ym, yh, pnat, pnat, pnat, wb, wo, x, mod)


def _mlp_kernel(x_ref, nw_ref, mod_ref, w1_ref, w2_ref, o_ref):
    x = x_ref[...]
    m = mod_ref[...]
    h = x * lax.rsqrt(jnp.mean(x * x, axis=-1, keepdims=True) + EPS) * nw_ref[...]
    h = (h * (1.0 + m[4:5]) + m[3:4]).astype(BF16)
    a = jnp.maximum(_dot(h, w1_ref[...]), 0.0)
    o_ref[...] = x + m[5:6] * _dot((a * a).astype(BF16), w2_ref[...])


def _mlp(x, norm_w, layer, mod, w1, w2, n_lat_tiles, n_tiles):
    nb, s, d = x.shape
    hidden = w1.shape[1]
    xrow = pl.BlockSpec((None, TOKEN_TILE, d), lambda b, i: (b, i, 0))
    single = dict(pipeline_mode=pl.Buffered(1))
    return pl.pallas_call(
        _mlp_kernel,
        out_shape=jax.ShapeDtypeStruct((nb, s, d), F32),
        grid=(nb, n_tiles),
        in_specs=[xrow,
                  pl.BlockSpec((None, 1, d), lambda b, i: (layer, 0, 0)),
                  pl.BlockSpec((None, 6, d), _mod_row(nb, n_lat_tiles)),
                  pl.BlockSpec((d, hidden), lambda b, i: (0, 0), **single),
                  pl.BlockSpec((hidden, d), lambda b, i: (0, 0), **single)],
        out_specs=xrow,
        input_output_aliases={0: 0},
        compiler_params=_params(("parallel", "parallel")),
        name="mlp",
    )(x, norm_w, mod, w1, w2)


def _final_norm_kernel(x_ref, w_ref, o_ref):
    x = x_ref[...]
    o_ref[...] = x * lax.rsqrt(jnp.mean(x * x, axis=-1, keepdims=True) + EPS) * w_ref[...]


def _final_norm(x, w, seq):
    nb, _, d = x.shape
    return pl.pallas_call(
        _final_norm_kernel,
        out_shape=jax.ShapeDtypeStruct((nb, seq, d), F32),
        grid=(nb, seq // TOKEN_TILE),
        in_specs=[pl.BlockSpec((None, TOKEN_TILE, d), lambda b, i: (b, i, 0)),
                  pl.BlockSpec((1, d), lambda b, i: (0, 0))],
        out_specs=pl.BlockSpec((None, TOKEN_TILE, d), lambda b, i: (b, i, 0)),
        compiler_params=_params(("parallel", "parallel")),
        name="final_norm",
    )(x, w)


def _lat_colmajor(t, seq):
    nb, _, ch = t.shape
    rows = seq // GRID_W
    lat = t[:, :seq].reshape(nb, rows, GRID_W, ch).swapaxes(1, 2).reshape(nb, seq, ch)
    return jnp.concatenate([lat, t[:, seq:]], axis=1)


def _lat_rowmajor(t, seq):
    nb, _, ch = t.shape
    rows = seq // GRID_W
    lat = t[:, :seq].reshape(nb, GRID_W, rows, ch).swapaxes(1, 2).reshape(nb, seq, ch)
    return jnp.concatenate([lat, t[:, seq:]], axis=1)


def _dir_rows(g, nb, s, per_dir):
    g = g[:, :2 * per_dir].reshape(nb, s, 2, per_dir)
    return jnp.transpose(g, (2, 0, 3, 1))


def kernel(x, c, ctx, c_ctx, norm1_w, mod_w, mod_b, w_in, ssd_conv_w, ssd_conv_b, ssd_dt_bias, ssd_a_log,
           ssd_d, ssd_norm_w, ml_conv_w, ml_conv_b, ml_gate_b, ml_norm_w, hy_conv_w, hy_conv_b, hy_ffn_w1,
           hy_ffn_b1, hy_ffn_w2, hy_ffn_b2, hy_ffn_w3, hy_decay, hy_skip, w_branch, w_out, norm2_w,
           mlp_w1, mlp_w2, norm_f_w):
    nb, seq, d = x.shape
    ctx_len = ctx.shape[1]
    depth = w_in.shape[0]
    s = seq + ctx_len
    assert seq % ctx_len == 0 and ctx_len % TOKEN_TILE == 0 and seq % GRID_W == 0
    n_lat_tiles, n_tiles = seq // TOKEN_TILE, s // TOKEN_TILE
    n_lat_ch, n_ch = seq // CHUNK, s // CHUNK

    ssd_heads = ssd_d.shape[-1]
    ssd_inner = ssd_norm_w.shape[-1]
    ssd_hd = ssd_inner // ssd_heads
    ssd_conv_ch = ssd_conv_w.shape[-1]
    ssd_ds = (ssd_conv_ch - ssd_inner) // (2 * SSD_GROUPS)
    ml_heads = ml_gate_b.shape[-1]
    ml_inner = ml_heads * ML_HEADDIM
    hy_inner = hy_skip.shape[-1]
    ssd_cols = ssd_conv_ch + ssd_inner + 2 * ssd_heads
    ml_cols = 4 * ml_inner + 4 * ml_heads
    rec_cols = ssd_cols + ml_cols
    hy_cols = (HY_ORDER + 1) * hy_inner

    o_z = ssd_conv_ch
    o_dt = ssd_conv_ch + ssd_inner
    o_ml = ssd_cols
    o_mlg = ssd_cols + 4 * ml_inner
    o_hy = rec_cols
    o_g = rec_cols + hy_cols

    xa = jnp.concatenate([x, ctx], axis=1)
    rpad = (-(nb + 1)) % 8
    c_all = jnp.concatenate([c, c_ctx[None], jnp.zeros((rpad, d), F32)], axis=0)

    tab_lat = _dft_tables(seq)
    tab_ctx = _dft_tables(ctx_len)
    k_scale = jnp.concatenate([jnp.ones((1, ml_inner), F32),
                               jnp.full((1, ml_inner), ML_HEADDIM ** -0.5, F32)], axis=1)
    ones_row = lambda n: jnp.ones((1, n), F32)

    norm1 = norm1_w[:, None, :]
    norm2 = norm2_w[:, None, :]
    ssd_norm = ssd_norm_w[:, None, :]
    ml_norm = ml_norm_w[:, None, :]
    d_full = jnp.repeat(ssd_d, ssd_hd, axis=-1)[:, None, :]
    w3 = hy_ffn_w3.reshape(depth, hy_ffn_w3.shape[1], HY_ORDER, 2, hy_inner).transpose(0, 2, 3, 1, 4)
    decay = hy_decay[:, :, :, None, :]
    skip = hy_skip[:, :, None, :]
    hb1 = hy_ffn_b1[:, None, :]
    hb2 = hy_ffn_b2[:, None, :]

    for l in range(depth):
        need_ctx = l < depth - 1
        used_tiles = n_tiles if need_ctx else n_lat_tiles
        mod = _mod_vectors(c_all, mod_w, mod_b[:, None, :], l).reshape(-1, 6, d)

        wl = w_in[l]
        w_nat = jnp.concatenate([wl[:, o_z:o_dt], wl[:, o_g:], wl[:, o_hy:o_g]], axis=1).astype(BF16)
        w_xbc = wl[:, :o_z].astype(BF16)
        w_ml = wl[:, o_ml:o_mlg].astype(BF16)
        gpad = lambda w: jnp.pad(w, ((0, 0), (0, 128 - w.shape[1]))).astype(BF16)
        hn = _normmod(xa, norm1, l, mod, 0, n_lat_tiles, n_tiles)
        hn_cm = _lat_colmajor(hn, seq)
        hn2 = hn.reshape(nb * s, d)
        hn_cm2 = hn_cm.reshape(nb * s, d)
        pnat = _matmul(hn2, w_nat, BF16, "proj_nat").reshape(nb, s, -1)
        pml = _matmul(hn_cm2, w_ml, BF16, "proj_ml").reshape(nb, s, -1)
        p_dt = _matmul(hn2, gpad(wl[:, o_dt:o_ml]), F32, "proj_dt")
        p_mlg = _matmul(hn_cm2, gpad(wl[:, o_mlg:o_hy]), F32, "proj_mlg")
        pxbc = _matmul(hn2, w_xbc, BF16, "proj_xbc").reshape(nb, s, -1)
        c_g = ssd_inner
        c_hy = c_g + N_BRANCH * d

        xbc = _dwconv(pxbc, 0, ssd_conv_ch, ssd_conv_w[l], ssd_conv_b[l][None], ones_row(ssd_conv_ch),
                      True, n_lat_tiles, n_tiles, "ssd_conv")
        dt_r = _dir_rows(p_dt, nb, s, ssd_heads)
        y_ssd = _ssd_scan(xbc, dt_r, ssd_dt_bias[l][:, :, None], ssd_a_log[l][:, :, None],
                          ssd_heads, ssd_hd, ssd_ds, n_lat_ch, n_ch)
        ys = _ssd_out(y_ssd, xbc, pnat, d_full, ssd_norm, l, used_tiles)

        qk = _dwconv(pml, 0, 2 * ml_inner, ml_conv_w[l], ml_conv_b[l][None], k_scale,
                     True, n_lat_tiles, n_tiles, "ml_conv")
        g_r = _dir_rows(p_mlg, nb, s, 2 * ml_heads)
        gate_b = ml_gate_b[l]
        cell = _ml_scan(qk, pml, g_r[:, :, :ml_heads], g_r[:, :, ml_heads:],
                        gate_b[:, 0, :, None], gate_b[:, 1, :, None], ml_heads, n_lat_ch, n_ch)
        ym = _lat_rowmajor(_ml_out(cell, pml, ml_norm, l, ml_heads, used_tiles), seq)

        hcw, hcb = hy_conv_w[l], hy_conv_b[l][None]
        fl = _hy_filters(seq, l, hy_ffn_w1, hb1, hy_ffn_w2, hb2, w3, decay, tab_lat)
        yh = _hy_conv(pnat, c_hy, hy_inner, hcw, hcb, None, 0, seq, l, *fl, skip, tab_lat)
        if need_ctx:
            fc = _hy_filters(ctx_len, l, hy_ffn_w1, hb1, hy_ffn_w2, hb2, w3, decay, tab_ctx)
            yh = _hy_conv(pnat, c_hy, hy_inner, hcw, hcb, yh, seq // ctx_len, ctx_len, l, *fc, skip, tab_ctx)

        xa = _merge(ys, ym, yh, pnat, w_branch[l].astype(BF16), w_out[l].astype(BF16), xa, mod,
                    n_lat_tiles, used_tiles)
        xa = _mlp(xa, norm2, l, mod, mlp_w1[l].astype(BF16), mlp_w2[l].astype(BF16), n_lat_tiles, used_tiles)

    return _final_norm(xa, norm_f_w[None], seq)
```

```python
import functools
import math

import jax
import jax.numpy as jnp
import numpy as np
from jax import lax
from jax.experimental import pallas as pl
from jax.experimental.pallas import tpu as pltpu

F32 = jnp.float32
BF16 = jnp.bfloat16
HIGHEST = lax.Precision.HIGHEST

GRID_W = 64
CHUNK = 128
EPS = 1e-6
SSD_GROUPS = 2
SSD_CONV = 5
ML_HEADDIM = 128
ML_CONV = 5
HY_ORDER = 2
HY_SHORT = 3
HY_BANDS = 16
N_BRANCH = 3

TOKEN_TILE = 256
LANE_TILE = 256
HALO = 16
VMEM_LIMIT = 56 * 1024 * 1024

_hdot = functools.partial(jnp.dot, precision=HIGHEST, preferred_element_type=F32)
_dot = functools.partial(jnp.dot, preferred_element_type=F32)


def _params(sem, vmem=None):
    return pltpu.CompilerParams(dimension_semantics=sem, vmem_limit_bytes=vmem or VMEM_LIMIT)


def _softplus(x):
    return jnp.maximum(x, 0.0) + jnp.log(1.0 + jnp.exp(-jnp.abs(x)))


def _silu(x):
    return x * jax.nn.sigmoid(x)


def _mod_kernel(c_ref, w_ref, b_ref, o_ref):
    o_ref[...] = _hdot(_silu(c_ref[...]), w_ref[...]) + b_ref[...]


def _mod_vectors(c_all, mod_w, mod_b, layer):
    r, d = c_all.shape
    n = mod_w.shape[-1]
    tn = n // 6
    return pl.pallas_call(
        _mod_kernel,
        out_shape=jax.ShapeDtypeStruct((r, n), F32),
        grid=(n // tn,),
        in_specs=[pl.BlockSpec((r, d), lambda j: (0, 0)),
                  pl.BlockSpec((None, d, tn), lambda j: (layer, 0, j)),
                  pl.BlockSpec((None, 1, tn), lambda j: (layer, 0, j))],
        out_specs=pl.BlockSpec((r, tn), lambda j: (0, j)),
        compiler_params=_params(("parallel",)),
        name="mod_vectors",
    )(c_all, mod_w, mod_b)


def _normmod_kernel(x_ref, nw_ref, mod_ref, o_ref, *, si):
    x = x_ref[...]
    h = x * lax.rsqrt(jnp.mean(x * x, axis=-1, keepdims=True) + EPS) * nw_ref[...]
    m = mod_ref[...]
    o_ref[...] = (h * (1.0 + m[si + 1:si + 2]) + m[si:si + 1]).astype(o_ref.dtype)


def _mod_row(nb, n_lat_tiles):
    return lambda b, i: (jnp.where(i < n_lat_tiles, b, nb), 0, 0)


def _normmod(x, norm_w, layer, mod, si, n_lat_tiles, n_tiles):
    nb, s, d = x.shape
    return pl.pallas_call(
        functools.partial(_normmod_kernel, si=si),
        out_shape=jax.ShapeDtypeStruct((nb, s, d), BF16),
        grid=(nb, n_tiles),
        in_specs=[pl.BlockSpec((None, TOKEN_TILE, d), lambda b, i: (b, i, 0)),
                  pl.BlockSpec((None, 1, d), lambda b, i: (layer, 0, 0)),
                  pl.BlockSpec((None, 6, d), _mod_row(nb, n_lat_tiles))],
        out_specs=pl.BlockSpec((None, TOKEN_TILE, d), lambda b, i: (b, i, 0)),
        compiler_params=_params(("parallel", "parallel")),
        name="normmod",
    )(x, norm_w, mod)


def _mm_kernel(a_ref, w_ref, o_ref):
    o_ref[...] = _dot(a_ref[...], w_ref[...]).astype(o_ref.dtype)


def _pick(n, cands):
    for c in cands:
        if n % c == 0:
            return c
    return n


def _matmul(a, w, out_dtype, name):
    t, k = a.shape
    n = w.shape[1]
    tm = _pick(t, (1024, 768, 512, 256))
    tn = _pick(n, (1024, 512, 256, 128))
    return pl.pallas_call(
        _mm_kernel,
        out_shape=jax.ShapeDtypeStruct((t, n), out_dtype),
        grid=(t // tm, n // tn),
        in_specs=[pl.BlockSpec((tm, k), lambda i, j: (i, 0)),
                  pl.BlockSpec((k, tn), lambda i, j: (0, j))],
        out_specs=pl.BlockSpec((tm, tn), lambda i, j: (i, j)),
        compiler_params=_params(("parallel", "parallel")),
        name=name,
    )(a, w)


def _dwconv_kernel(u_ref, p_ref, n_ref, w_ref, b_ref, s_ref, o_ref, *, taps, act, bounds):
    i = pl.program_id(1)
    tm = u_ref.shape[0]
    pad = taps // 2
    lv, rv = jnp.float32(1.0), jnp.float32(1.0)
    for e in bounds:
        lv = jnp.where(i == e, 0.0, lv)
        rv = jnp.where(i == e - 1, 0.0, rv)
    ext = jnp.concatenate([p_ref[...].astype(F32) * lv, u_ref[...].astype(F32),
                           n_ref[...].astype(F32) * rv], axis=0)
    rows = ext.shape[0]
    w = w_ref[...]
    acc = jnp.zeros((tm, u_ref.shape[1]), F32) + b_ref[...]
    for j in range(taps):
        sh = (pad - j) % rows
        shifted = ext if sh == 0 else pltpu.roll(ext, sh, 0)
        acc = acc + shifted[HALO:HALO + tm] * w[j:j + 1]
    if act:
        acc = _silu(acc)
    o_ref[...] = (acc * s_ref[...]).astype(o_ref.dtype)


def _dwconv(u, col0, ncols, w, b, scale, act, n_lat_tiles, n_tiles, name):
    nb, s, _ = u.shape
    taps = w.shape[0]
    tc = _pick(ncols, (2048, 1536, 1024, 512, 256))
    assert col0 % tc == 0
    c0 = col0 // tc
    hb = TOKEN_TILE // HALO
    last = s // HALO - 1
    bounds = (0, n_lat_tiles, n_tiles)
    return pl.pallas_call(
        functools.partial(_dwconv_kernel, taps=taps, act=act, bounds=bounds),
        out_shape=jax.ShapeDtypeStruct((nb, s, ncols), BF16),
        grid=(nb, n_tiles, ncols // tc),
        in_specs=[pl.BlockSpec((None, TOKEN_TILE, tc), lambda bb, i, c: (bb, i, c0 + c)),
                  pl.BlockSpec((None, HALO, tc), lambda bb, i, c: (bb, jnp.maximum(i * hb - 1, 0), c0 + c)),
                  pl.BlockSpec((None, HALO, tc), lambda bb, i, c: (bb, jnp.minimum((i + 1) * hb, last), c0 + c)),
                  pl.BlockSpec((taps, tc), lambda bb, i, c: (0, c)),
                  pl.BlockSpec((1, tc), lambda bb, i, c: (0, c)),
                  pl.BlockSpec((1, tc), lambda bb, i, c: (0, c))],
        out_specs=pl.BlockSpec((None, TOKEN_TILE, tc), lambda bb, i, c: (bb, i, c)),
        compiler_params=_params(("parallel", "parallel", "parallel")),
        name=name,
    )(u, u, u, w, b, scale)


def _chunk_index(n_lat_ch, n_ch):
    return lambda d, j: (j + n_lat_ch) % n_ch if d == 0 else n_ch - 1 - j


def _scan_masks(d, t):
    ii = lax.broadcasted_iota(jnp.int32, (t, t), 0)
    jj = lax.broadcasted_iota(jnp.int32, (t, t), 1)
    mask = jj <= ii if d == 0 else jj >= ii
    tri_t = jnp.where(ii <= jj if d == 0 else ii >= jj, 1.0, 0.0).astype(F32)
    return mask, tri_t


def _gate_rows(la_r, lw_r, tri_t, m_prev):
    cum_r = _hdot(la_r, tri_t)
    last = jnp.sum(la_r, axis=1, keepdims=True)
    g_r = last - cum_r + lw_r
    m_loc = jnp.max(g_r, axis=1, keepdims=True)
    e_r = jnp.exp(g_r - m_loc)
    m_new = jnp.maximum(last + m_prev, m_loc)
    s_old = jnp.exp(last + m_prev - m_new)
    s_new = jnp.exp(m_loc - m_new)
    return cum_r, cum_r - lw_r, e_r, m_new, s_old, s_new


def _head_probs(cum_row, crow_row, m_prev_h, mask, qk):
    t = qk.shape[0]
    colb = jnp.broadcast_to(cum_row, (t, t)).T
    dlog = jnp.where(mask, colb - crow_row, -jnp.inf)
    inter = colb[:, 0:1] + m_prev_h
    m_row = jnp.maximum(inter, jnp.max(dlog, axis=1, keepdims=True))
    p = jnp.exp(dlog - m_row) * qk
    return p.astype(BF16), jnp.exp(inter - m_row), m_row


def _ssd_scan_kernel(*refs, heads, groups, hd, ds):
    j = pl.program_id(1)
    ins, outs, states = refs[:12], refs[12:14], refs[14:16]

    @pl.when(j == 0)
    def _():
        for st_ref in states:
            st_ref[...] = jnp.zeros_like(st_ref)

    for d in range(2):
        _ssd_chunk(d, *ins[6 * d:6 * d + 6], outs[d], states[d], heads=heads, groups=groups, hd=hd, ds=ds)


def _ssd_chunk(d, xs_ref, b_ref, c_ref, dt_ref, bias_ref, alog_ref, o_ref, st_ref, *, heads, groups, hd, ds):
    t = xs_ref.shape[0]
    hpg = heads // groups
    pw = 2 * hd
    mask, tri_t = _scan_masks(d, t)
    dt = _softplus(dt_ref[...] + bias_ref[...])
    la_r = -dt * jnp.exp(alog_ref[...])
    cum_r = _hdot(la_r, tri_t)
    last = jnp.sum(la_r, axis=1, keepdims=True)
    crow = cum_r - jnp.log(dt)
    e_r = jnp.exp(last - crow)
    e_last = jnp.exp(last)
    lo = lax.broadcasted_iota(jnp.int32, (1, pw), 1) < hd

    for g in range(groups):
        q = c_ref[:, g * ds:(g + 1) * ds]
        k_t = b_ref[:, g * ds:(g + 1) * ds].astype(F32).T
        qk = _dot(q, k_t.astype(BF16))
        w0 = g * hpg * hd
        qs = _dot(q, st_ref[:, w0:w0 + hpg * hd].astype(BF16))
        for i in range(hpg // 2):
            h0 = g * hpg + 2 * i
            c0 = h0 * hd
            vp = xs_ref[:, c0:c0 + pw]
            zero = jnp.zeros_like(vp)
            v_bd = jnp.concatenate([jnp.where(lo, vp, zero), jnp.where(lo, zero, vp)], axis=0)
            probs, carry, kte = [], [], []
            for h in (h0, h0 + 1):
                colb = jnp.broadcast_to(cum_r[h:h + 1], (t, t)).T
                dlog = jnp.where(mask, colb - crow[h:h + 1], -jnp.inf)
                probs.append((jnp.exp(dlog) * qk).astype(BF16))
                carry.append(jnp.exp(colb))
                kte.append((k_t * e_r[h:h + 1]).astype(BF16))
            lhs = jnp.concatenate([jnp.concatenate(probs, axis=1), jnp.concatenate(kte, axis=1)], axis=0)
            res = _dot(lhs, v_bd)
            o_ref[:, c0:c0 + pw] = res[:t] + qs[:, 2 * i * hd:2 * i * hd + pw] * jnp.where(lo, carry[0], carry[1])
            decay = jnp.where(lo, e_last[h0:h0 + 1], e_last[h0 + 1:h0 + 2])
            st_ref[:, c0:c0 + pw] = decay * st_ref[:, c0:c0 + pw] + res[t:]


def _ssd_scan(xbc, dt_r, bias_c, alog_c, heads, hd, ds, n_lat_ch, n_ch):
    nb, s, _ = xbc.shape
    inner = heads * hd
    gn = SSD_GROUPS * ds
    assert 2 * hd == CHUNK and (heads // SSD_GROUPS) % 2 == 0 and inner % gn == 0
    cidx = _chunk_index(n_lat_ch, n_ch)

    def specs(d):
        return [pl.BlockSpec((None, CHUNK, inner), lambda b, j: (b, cidx(d, j), 0)),
                pl.BlockSpec((None, CHUNK, gn), lambda b, j: (b, cidx(d, j), inner // gn)),
                pl.BlockSpec((None, CHUNK, gn), lambda b, j: (b, cidx(d, j), inner // gn + 1)),
                pl.BlockSpec((None, None, heads, CHUNK), lambda b, j: (d, b, 0, cidx(d, j))),
                pl.BlockSpec((None, heads, 1), lambda b, j: (d, 0, 0)),
                pl.BlockSpec((None, heads, 1), lambda b, j: (d, 0, 0))]

    out = lambda d: pl.BlockSpec((None, CHUNK, inner), lambda b, j: (b, cidx(d, j), 0))
    args = (xbc, xbc, xbc, dt_r, bias_c, alog_c)
    return pl.pallas_call(
        functools.partial(_ssd_scan_kernel, heads=heads, groups=SSD_GROUPS, hd=hd, ds=ds),
        out_shape=(jax.ShapeDtypeStruct((nb, s, inner), F32),) * 2,
        grid=(nb, n_ch),
        in_specs=specs(0) + specs(1),
        out_specs=(out(0), out(1)),
        scratch_shapes=[pltpu.VMEM((ds, inner), F32)] * 2,
        compiler_params=_params(("parallel", "arbitrary")),
        name="ssd_scan",
    )(*args, *args)


def _ml_scan_kernel(*refs, heads):
    j = pl.program_id(1)
    ins, outs, states = refs[:14], refs[14:16], refs[16:20]

    @pl.when(j == 0)
    def _():
        for ref in states:
            ref[...] = jnp.zeros_like(ref)

    for d in range(2):
        _ml_chunk(d, *ins[7 * d:7 * d + 7], outs[d], *states[2 * d:2 * d + 2], heads=heads)


def _ml_chunk(d, q_ref, k_ref, v_ref, gi_ref, gf_ref, bi_ref, bf_ref, o_ref, st_ref, m_ref, *, heads):
    t = q_ref.shape[0]
    dh = ML_HEADDIM
    mask, tri_t = _scan_masks(d, t)
    lw_r = gi_ref[...] + bi_ref[...]
    f = gf_ref[...] + bf_ref[...]
    la_r = jnp.minimum(f, 0.0) - jnp.log(1.0 + jnp.exp(-jnp.abs(f)))
    m_prev = m_ref[:, 0:1]
    cum_r, crow, e_r, m_new, s_old, s_new = _gate_rows(la_r, lw_r, tri_t, m_prev)
    ones = jnp.ones((t, dh), BF16)

    for h in range(heads):
        c0 = h * dh
        q = q_ref[:, c0:c0 + dh]
        k_t = k_ref[:, c0:c0 + dh].astype(F32).T
        v_aug = jnp.concatenate([v_ref[:, c0:c0 + dh], ones], axis=1)
        qk = _dot(q, k_t.astype(BF16))
        s0 = 2 * c0
        qs = _dot(q, st_ref[:, s0:s0 + 2 * dh].astype(BF16))
        p, cf, m_row = _head_probs(cum_r[h:h + 1], crow[h:h + 1], m_prev[h:h + 1], mask, qk)
        res = _dot(jnp.concatenate([p, (k_t * e_r[h:h + 1]).astype(BF16)], axis=0), v_aug)
        y = res[:t] + qs * cf
        o_ref[:, c0:c0 + dh] = y[:, :dh] / jnp.maximum(jnp.abs(y[:, dh:]), jnp.exp(-m_row))
        st_ref[:, s0:s0 + 2 * dh] = s_old[h:h + 1] * st_ref[:, s0:s0 + 2 * dh] + s_new[h:h + 1] * res[t:]
    m_ref[...] = jnp.broadcast_to(m_new, m_ref.shape)


def _ml_scan(qk, pml, gi_r, gf_r, bi_c, bf_c, heads, n_lat_ch, n_ch):
    nb, s, _ = qk.shape
    inner = heads * ML_HEADDIM
    cidx = _chunk_index(n_lat_ch, n_ch)

    def specs(d):
        gspec = pl.BlockSpec((None, None, heads, CHUNK), lambda b, j: (d, b, 0, cidx(d, j)))
        bspec = pl.BlockSpec((None, heads, 1), lambda b, j: (d, 0, 0))
        col = lambda c: pl.BlockSpec((None, CHUNK, inner), lambda b, j: (b, cidx(d, j), c))
        return [col(0), col(1), col(2), gspec, gspec, bspec, bspec]

    out = lambda d: pl.BlockSpec((None, CHUNK, inner), lambda b, j: (b, cidx(d, j), 0))
    args = (qk, qk, pml, gi_r, gf_r, bi_c, bf_c)
    return pl.pallas_call(
        functools.partial(_ml_scan_kernel, heads=heads),
        out_shape=(jax.ShapeDtypeStruct((nb, s, inner), F32),) * 2,
        grid=(nb, n_ch),
        in_specs=specs(0) + specs(1),
        out_specs=(out(0), out(1)),
        scratch_shapes=[pltpu.VMEM((ML_HEADDIM, 2 * inner), F32), pltpu.VMEM((heads, 128), F32)] * 2,
        compiler_params=_params(("parallel", "arbitrary")),
        name="ml_scan",
    )(*args, *args)


def _ssd_out_kernel(yf_ref, yb_ref, xs_ref, z_ref, d_ref, nw_ref, o_ref):
    y = yf_ref[...] + yb_ref[...] + d_ref[...] * xs_ref[...].astype(F32)
    y = y * _silu(z_ref[...].astype(F32))
    y = y * lax.rsqrt(jnp.mean(y * y, axis=-1, keepdims=True) + EPS) * nw_ref[...]
    o_ref[...] = y.astype(o_ref.dtype)


def _ssd_out(yf, yb, xbc, pnat, d_full, norm_w, layer, n_tiles):
    nb, _, inner = yf.shape
    row = pl.BlockSpec((None, TOKEN_TILE, inner), lambda b, i: (b, i, 0))
    return pl.pallas_call(
        _ssd_out_kernel,
        out_shape=jax.ShapeDtypeStruct((nb, n_tiles * TOKEN_TILE, inner), BF16),
        grid=(nb, n_tiles),
        in_specs=[row, row, row, row,
                  pl.BlockSpec((None, 1, inner), lambda b, i: (layer, 0, 0)),
                  pl.BlockSpec((None, 1, inner), lambda b, i: (layer, 0, 0))],
        out_specs=row,
        compiler_params=_params(("parallel", "parallel")),
        name="ssd_out",
    )(yf, yb, xbc, pnat, d_full, norm_w)


def _ml_out_kernel(cf_ref, cb_ref, o_in_ref, nw_ref, o_ref, *, heads):
    dh = ML_HEADDIM
    for h in range(heads):
        sl = slice(h * dh, (h + 1) * dh)
        c = cf_ref[:, sl] + cb_ref[:, sl]
        c = c * lax.rsqrt(jnp.mean(c * c, axis=-1, keepdims=True) + EPS) * nw_ref[:, sl]
        o_ref[:, sl] = (jax.nn.sigmoid(o_in_ref[:, sl].astype(F32)) * c).astype(o_ref.dtype)


def _ml_out(cf, cb, pml, norm_w, layer, heads, n_tiles):
    nb, _, inner = cf.shape
    row = pl.BlockSpec((None, TOKEN_TILE, inner), lambda b, i: (b, i, 0))
    return pl.pallas_call(
        functools.partial(_ml_out_kernel, heads=heads),
        out_shape=jax.ShapeDtypeStruct((nb, n_tiles * TOKEN_TILE, inner), BF16),
        grid=(nb, n_tiles),
        in_specs=[row, row,
                  pl.BlockSpec((None, TOKEN_TILE, inner), lambda b, i: (b, i, 3)),
                  pl.BlockSpec((None, 1, inner), lambda b, i: (layer, 0, 0))],
        out_specs=row,
        compiler_params=_params(("parallel", "parallel")),
        name="ml_out",
    )(cf, cb, pml, norm_w)


def _dft_tables(length):
    m = length // 2
    k = np.arange(m, dtype=np.int64)
    ang = ((k[:, None] * k[None, :]) % (2 * m)).astype(np.float64) * (np.pi / m)
    alt8 = np.broadcast_to(np.where(k % 2 == 0, 1.0, -1.0)[None, :], (8, m))
    tw = k.astype(np.float64)[:, None] * (np.pi / length) * np.ones((1, LANE_TILE))
    return (jnp.asarray(np.cos(ang), BF16), jnp.asarray(np.sin(ang), BF16), jnp.asarray(alt8, BF16),
            jnp.asarray(np.cos(tw), F32), jnp.asarray(np.sin(tw), F32))


def _hy_feats(length):
    t = jnp.arange(length, dtype=F32)
    t_norm = t / length
    bands = jnp.linspace(1e-4, HY_BANDS - 1, HY_BANDS, dtype=F32)
    ang = (2.0 * math.pi / length) * t[:, None] * bands[None, :]
    feats = jnp.concatenate([t_norm[:, None], jnp.cos(ang), -jnp.sin(ang)], axis=-1)
    return feats[0::2], feats[1::2], t_norm[0::2, None], t_norm[1::2, None]


def _split_spectrum(ae, be, ao, bo, twc, tws):
    tr = twc * ao - tws * bo
    tm = twc * bo + tws * ao
    return (ae + tr, ae - tr), (be + tm, tm - be)


def _hy_filter_kernel(fe_ref, fo_ref, tne_ref, tno_ref, w1_ref, b1_ref, w2_ref, b2_ref, w3f_ref, w3b_ref,
                      df_ref, db_ref, cm_ref, sm_ref, alt_ref, twc_ref, tws_ref, ha_ref, hb_ref, hm_ref):
    m = cm_ref.shape[0]

    def taps(feats, tn):
        hid = jnp.sin(_hdot(feats, w1_ref[...]) + b1_ref[...])
        hid = jnp.sin(_hdot(hid, w2_ref[...]) + b2_ref[...])
        h_f = _hdot(hid, w3f_ref[...]) * jnp.exp(-tn * jnp.abs(df_ref[...]))
        h_b = _hdot(hid, w3b_ref[...]) * jnp.exp(-tn * jnp.abs(db_ref[...]))
        return (h_f + h_b).astype(BF16), (h_f - h_b).astype(BF16)

    sum_e, dif_e = taps(fe_ref[...], tne_ref[...])
    sum_o, dif_o = taps(fo_ref[...], tno_ref[...])
    twc, tws = twc_ref[...], tws_ref[...]
    cm, sm = cm_ref[...], sm_ref[...]
    (ha_lo, ha_hi), _ = _split_spectrum(_dot(cm, sum_e), 0.0, _dot(cm, sum_o), _dot(sm, sum_o), twc, tws)
    _, (hb_lo, hb_hi) = _split_spectrum(0.0, _dot(sm, dif_e), _dot(cm, dif_o), _dot(sm, dif_o), twc, tws)
    ha_ref[0:m, :] = ha_lo
    ha_ref[m:2 * m, :] = ha_hi
    hb_ref[0:m, :] = hb_lo
    hb_ref[m:2 * m, :] = hb_hi
    row = lax.broadcasted_iota(jnp.int32, hm_ref.shape, 0)
    hm_ref[...] = jnp.where(row == 0, _dot(alt_ref[...], sum_e), _dot(alt_ref[...], dif_o))


def _hy_filters(length, layer, w1, b1, w2, b2, w3, decay, tables):
    cm, sm, alt8, twc, tws = tables
    m = length // 2
    feats = _hy_feats(length)
    nfeat, nf = w1.shape[-2:]
    ch = decay.shape[-1]
    nct = ch // LANE_TILE
    const = lambda shape: pl.BlockSpec(shape, lambda n, c: (0,) * len(shape))
    w3spec = lambda dr: pl.BlockSpec((None, None, None, nf, LANE_TILE), lambda n, c: (layer, n, dr, 0, c))
    dspec = lambda dr: pl.BlockSpec((None, None, None, 1, LANE_TILE), lambda n, c: (layer, n, dr, 0, c))
    lay = lambda a, b: pl.BlockSpec((None, a, b), lambda n, c: (layer, 0, 0))
    out = lambda rows: pl.BlockSpec((None, rows, LANE_TILE), lambda n, c: (n, 0, c))
    return pl.pallas_call(
        _hy_filter_kernel,
        out_shape=(jax.ShapeDtypeStruct((HY_ORDER, length, ch), F32),
                   jax.ShapeDtypeStruct((HY_ORDER, length, ch), F32),
                   jax.ShapeDtypeStruct((HY_ORDER, 8, ch), F32)),
        grid=(HY_ORDER, nct),
        in_specs=[const((m, nfeat)), const((m, nfeat)), const((m, 1)), const((m, 1)),
                  lay(nfeat, nf), lay(1, nf), lay(nf, nf), lay(1, nf),
                  w3spec(0), w3spec(1), dspec(0), dspec(1),
                  const((m, m)), const((m, m)), const((8, m)),
                  const((m, LANE_TILE)), const((m, LANE_TILE))],
        out_specs=(out(length), out(length), out(8)),
        compiler_params=_params(("parallel", "parallel")),
        name="hyena_filters_%d" % length,
    )(*feats, w1, b1, w2, b2, w3, w3, decay, decay, cm, sm, alt8, twc, tws)


def _short_conv(u, w, b):
    length = u.shape[0]
    taps = w.shape[0]
    pad = taps // 2
    row = lax.broadcasted_iota(jnp.int32, u.shape, 0)
    acc = b + u * w[pad:pad + 1]
    for j in range(taps):
        off = j - pad
        if off != 0:
            moved = pltpu.roll(u, (-off) % length, 0)
            valid = (row + off >= 0) & (row + off < length)
            acc = acc + jnp.where(valid, moved, 0.0) * w[j:j + 1]
    return acc


def _hy_conv_kernel(z_ref, g_ref, wz_ref, bz_ref, wg_ref, bg_ref, ha_ref, hb_ref, hm_ref, skip_ref,
                    cm_ref, sm_ref, alt_ref, twc_ref, tws_ref, o_ref, tmp_ref, ze_ref, zo_ref):
    n = pl.program_id(2)
    m, width = ze_ref.shape
    lanes = tmp_ref.shape[-1]
    slabs = range(width // lanes)

    def put(val):
        for h in slabs:
            tmp_ref[h] = val[:, h * lanes:(h + 1) * lanes]

    def take(first):
        return jnp.concatenate([tmp_ref[h, pl.ds(first, m, stride=2), :] for h in slabs], axis=1)

    @pl.when(n == 0)
    def _():
        put(_short_conv(z_ref[...].astype(F32), wz_ref[...], bz_ref[...]))
        ze_ref[...] = take(0)
        zo_ref[...] = take(1)

    put(_short_conv(g_ref[...].astype(F32), wg_ref[...], bg_ref[...]))
    ge, go = take(0), take(1)
    ze, zo = ze_ref[...], zo_ref[...]
    zz = jnp.concatenate([ze.astype(BF16), zo.astype(BF16)], axis=1)
    a = _dot(cm_ref[...], zz)
    b = _dot(sm_ref[...], zz)
    mid = _dot(alt_ref[...], zz)[0:1]
    twc, tws = twc_ref[...], tws_ref[...]
    (a_lo, a_hi), (b_lo, b_hi) = _split_spectrum(a[:, :width], b[:, :width], a[:, width:], b[:, width:], twc, tws)

    ha_lo, ha_hi, hb_lo, hb_hi = ha_ref[0:m, :], ha_ref[m:2 * m, :], hb_ref[0:m, :], hb_ref[m:2 * m, :]
    yr_lo, ym_lo = a_lo * ha_lo - b_lo * hb_lo, a_lo * hb_lo + b_lo * ha_lo
    yr_hi, ym_hi = a_hi * ha_hi - b_hi * hb_hi, a_hi * hb_hi + b_hi * ha_hi
    ha_m, hb_m = hm_ref[0:1, :], hm_ref[1:2, :]
    yr_m = mid[:, :width] * ha_m - mid[:, width:] * hb_m
    ym_m = mid[:, :width] * hb_m + mid[:, width:] * ha_m

    row = lax.broadcasted_iota(jnp.int32, (m, width), 0)
    half0 = jnp.where(row == 0, 0.5, 1.0)
    qr, qm = yr_lo - yr_hi, ym_lo + ym_hi
    pr = jnp.concatenate([((yr_lo + yr_hi) * half0).astype(BF16),
                          ((qr * twc + qm * tws) * half0).astype(BF16)], axis=1)
    pm = jnp.concatenate([(ym_lo - ym_hi).astype(BF16), (qm * twc - qr * tws).astype(BF16)], axis=1)
    y = _dot(cm_ref[...], pr) + _dot(sm_ref[...], pm)
    alt = jnp.where((row & 1) == 0, 1.0, -1.0)
    scale = 1.0 / (2 * m)
    skip = skip_ref[...]
    ze_new = ge * ((y[:, :width] + alt * yr_m) * scale + skip * ze)
    zo_new = go * ((y[:, width:] + alt * ym_m) * scale + skip * zo)

    @pl.when(n < HY_ORDER - 1)
    def _():
        ze_ref[...] = ze_new
        zo_ref[...] = zo_new

    @pl.when(n == HY_ORDER - 1)
    def _():
        for h in slabs:
            tmp_ref[h, pl.ds(0, m, stride=2), :] = ze_new[:, h * lanes:(h + 1) * lanes]
            tmp_ref[h, pl.ds(1, m, stride=2), :] = zo_new[:, h * lanes:(h + 1) * lanes]
            o_ref[:, h * lanes:(h + 1) * lanes] = tmp_ref[h].astype(o_ref.dtype)


def _hy_conv(u, col0, ch, conv_w, conv_b, out_prev, out_rows, row_block, length, layer, ha, hb, hm, skip,
             tables):
    cm, sm, alt8, twc, tws = tables
    nb = u.shape[0]
    s = out_rows
    m = length // 2
    nct = ch // LANE_TILE
    assert col0 % LANE_TILE == 0
    cb = col0 // LANE_TILE
    taps = conv_w.shape[0]
    const = lambda shape: pl.BlockSpec(shape, lambda c, b, n: (0,) * len(shape))
    hspec = lambda rows: pl.BlockSpec((None, rows, LANE_TILE), lambda c, b, n: (n, 0, c))
    zcol = lambda c, n: c
    gcol = lambda c, n: (1 + n) * nct + c
    wspec = lambda rows, col: pl.BlockSpec((rows, LANE_TILE), lambda c, b, n: (0, col(c, n)))
    in_specs = [pl.BlockSpec((None, length, LANE_TILE), lambda c, b, n: (b, row_block, cb + zcol(c, n))),
                pl.BlockSpec((None, length, LANE_TILE), lambda c, b, n: (b, row_block, cb + gcol(c, n))),
                wspec(taps, zcol), wspec(1, zcol), wspec(taps, gcol), wspec(1, gcol),
                hspec(length), hspec(length), hspec(8),
                pl.BlockSpec((None, None, 1, LANE_TILE), lambda c, b, n: (layer, n, 0, c)),
                const((m, m)), const((m, m)), const((8, m)), const((m, LANE_TILE)), const((m, LANE_TILE))]
    args = [u, u, conv_w, conv_b, conv_w, conv_b, ha, hb, hm, skip, cm, sm, alt8, twc, tws]
    n_in = len(args)
    aliases = {}
    if out_prev is not None:
        in_specs.append(pl.BlockSpec(memory_space=pl.ANY))
        args.append(out_prev)
        aliases = {len(args) - 1: 0}

    def body(*refs):
        _hy_conv_kernel(*refs[:n_in], *refs[-4:])

    return pl.pallas_call(
        body,
        out_shape=jax.ShapeDtypeStruct((nb, s, ch), BF16),
        grid=(nct, nb, HY_ORDER),
        in_specs=in_specs,
        out_specs=pl.BlockSpec((None, length, LANE_TILE), lambda c, b, n: (b, row_block, c)),
        scratch_shapes=[pltpu.VMEM((LANE_TILE // 128, length, 128), F32), pltpu.VMEM((m, LANE_TILE), F32),
                        pltpu.VMEM((m, LANE_TILE), F32)],
        input_output_aliases=aliases,
        compiler_params=_params(("parallel", "parallel", "arbitrary")),
        name="hyena_conv_%d" % length,
    )(*args)


def _merge_kernel(ys_ref, ym_ref, yh_ref, g0_ref, g1_ref, g2_ref, wb_ref, wo_ref, x_ref, mod_ref, o_ref):
    acc = None
    for n, (y_ref, g_ref) in enumerate(((ys_ref, g0_ref), (ym_ref, g1_ref), (yh_ref, g2_ref))):
        term = jax.nn.sigmoid(g_ref[...].astype(F32)) * _dot(y_ref[...], wb_ref[n])
        acc = term if acc is None else acc + term
    o = _dot(acc.astype(BF16), wo_ref[...])
    o_ref[...] = x_ref[...] + mod_ref[2:3] * o


def _merge(ys, ym, yh, pnat, wb, wo, x, mod, n_lat_tiles, n_tiles):
    nb, s, d = x.shape
    bw = ys.shape[-1]
    row = pl.BlockSpec((None, TOKEN_TILE, bw), lambda b, i: (b, i, 0))
    gate = lambda n: pl.BlockSpec((None, TOKEN_TILE, d), lambda b, i: (b, i, 1 + n))
    xrow = pl.BlockSpec((None, TOKEN_TILE, d), lambda b, i: (b, i, 0))
    single = dict(pipeline_mode=pl.Buffered(1))
    return pl.pallas_call(
        _merge_kernel,
        out_shape=jax.ShapeDtypeStruct((nb, s, d), F32),
        grid=(nb, n_tiles),
        in_specs=[row, row, row, gate(0), gate(1), gate(2),
                  pl.BlockSpec((N_BRANCH, bw, d), lambda b, i: (0, 0, 0), **single),
                  pl.BlockSpec((d, d), lambda b, i: (0, 0), **single),
                  xrow,
                  pl.BlockSpec((None, 6, d), _mod_row(nb, n_lat_tiles))],
        out_specs=xrow,
        input_output_aliases={8: 0},
        compiler_params=_params(("parallel", "parallel")),
        name="merge",
    )(ys, ym, yh, pnat, pnat, pnat, wb, wo, x, mod)


def _mlp_kernel(x_ref, nw_ref, mod_ref, w1_ref, w2_ref, o_ref):
    x = x_ref[...]
    m = mod_ref[...]
    h = x * lax.rsqrt(jnp.mean(x * x, axis=-1, keepdims=True) + EPS) * nw_ref[...]
    h = (h * (1.0 + m[4:5]) + m[3:4]).astype(BF16)
    a = jnp.maximum(_dot(h, w1_ref[...]), 0.0)
    o_ref[...] = x + m[5:6] * _dot((a * a).astype(BF16), w2_ref[...])


def _mlp(x, norm_w, layer, mod, w1, w2, n_lat_tiles, n_tiles):
    nb, s, d = x.shape
    hidden = w1.shape[1]
    xrow = pl.BlockSpec((None, TOKEN_TILE, d), lambda b, i: (b, i, 0))
    single = dict(pipeline_mode=pl.Buffered(1))
    return pl.pallas_call(
        _mlp_kernel,
        out_shape=jax.ShapeDtypeStruct((nb, s, d), F32),
        grid=(nb, n_tiles),
        in_specs=[xrow,
                  pl.BlockSpec((None, 1, d), lambda b, i: (layer, 0, 0)),
                  pl.BlockSpec((None, 6, d), _mod_row(nb, n_lat_tiles)),
                  pl.BlockSpec((d, hidden), lambda b, i: (0, 0), **single),
                  pl.BlockSpec((hidden, d), lambda b, i: (0, 0), **single)],
        out_specs=xrow,
        input_output_aliases={0: 0},
        compiler_params=_params(("parallel", "parallel")),
        name="mlp",
    )(x, norm_w, mod, w1, w2)


def _final_norm_kernel(x_ref, w_ref, o_ref):
    x = x_ref[...]
    o_ref[...] = x * lax.rsqrt(jnp.mean(x * x, axis=-1, keepdims=True) + EPS) * w_ref[...]


def _final_norm(x, w, seq):
    nb, _, d = x.shape
    return pl.pallas_call(
        _final_norm_kernel,
        out_shape=jax.ShapeDtypeStruct((nb, seq, d), F32),
        grid=(nb, seq // TOKEN_TILE),
        in_specs=[pl.BlockSpec((None, TOKEN_TILE, d), lambda b, i: (b, i, 0)),
                  pl.BlockSpec((1, d), lambda b, i: (0, 0))],
        out_specs=pl.BlockSpec((None, TOKEN_TILE, d), lambda b, i: (b, i, 0)),
        compiler_params=_params(("parallel", "parallel")),
        name="final_norm",
    )(x, w)


def _lat_colmajor(t, seq):
    nb, _, ch = t.shape
    rows = seq // GRID_W
    lat = t[:, :seq].reshape(nb, rows, GRID_W, ch).swapaxes(1, 2).reshape(nb, seq, ch)
    return jnp.concatenate([lat, t[:, seq:]], axis=1)


def _lat_rowmajor(t, seq):
    nb, _, ch = t.shape
    rows = seq // GRID_W
    lat = t[:, :seq].reshape(nb, GRID_W, rows, ch).swapaxes(1, 2).reshape(nb, seq, ch)
    return jnp.concatenate([lat, t[:, seq:]], axis=1)


def _dir_rows(g, nb, s, per_dir):
    g = g[:, :2 * per_dir].reshape(nb, s, 2, per_dir)
    return jnp.transpose(g, (2, 0, 3, 1))


def kernel(x, c, ctx, c_ctx, norm1_w, mod_w, mod_b, w_in, ssd_conv_w, ssd_conv_b, ssd_dt_bias, ssd_a_log,
           ssd_d, ssd_norm_w, ml_conv_w, ml_conv_b, ml_gate_b, ml_norm_w, hy_conv_w, hy_conv_b, hy_ffn_w1,
           hy_ffn_b1, hy_ffn_w2, hy_ffn_b2, hy_ffn_w3, hy_decay, hy_skip, w_branch, w_out, norm2_w,
           mlp_w1, mlp_w2, norm_f_w):
    nb, seq, d = x.shape
    ctx_len = ctx.shape[1]
    depth = w_in.shape[0]
    s = seq + ctx_len
    assert seq % ctx_len == 0 and ctx_len % TOKEN_TILE == 0 and seq % GRID_W == 0
    n_lat_tiles, n_tiles = seq // TOKEN_TILE, s // TOKEN_TILE
    n_lat_ch, n_ch = seq // CHUNK, s // CHUNK

    ssd_heads = ssd_d.shape[-1]
    ssd_inner = ssd_norm_w.shape[-1]
    ssd_hd = ssd_inner // ssd_heads
    ssd_conv_ch = ssd_conv_w.shape[-1]
    ssd_ds = (ssd_conv_ch - ssd_inner) // (2 * SSD_GROUPS)
    ml_heads = ml_gate_b.shape[-1]
    ml_inner = ml_heads * ML_HEADDIM
    hy_inner = hy_skip.shape[-1]
    ssd_cols = ssd_conv_ch + ssd_inner + 2 * ssd_heads
    ml_cols = 4 * ml_inner + 4 * ml_heads
    rec_cols = ssd_cols + ml_cols
    hy_cols = (HY_ORDER + 1) * hy_inner

    o_z = ssd_conv_ch
    o_dt = ssd_conv_ch + ssd_inner
    o_ml = ssd_cols
    o_mlg = ssd_cols + 4 * ml_inner
    o_hy = rec_cols
    o_g = rec_cols + hy_cols

    xa = jnp.concatenate([x, ctx], axis=1)
    rpad = (-(nb + 1)) % 8
    c_all = jnp.concatenate([c, c_ctx[None], jnp.zeros((rpad, d), F32)], axis=0)

    tab_lat = _dft_tables(seq)
    tab_ctx = _dft_tables(ctx_len)
    k_scale = jnp.concatenate([jnp.ones((1, ml_inner), F32),
                               jnp.full((1, ml_inner), ML_HEADDIM ** -0.5, F32)], axis=1)
    ones_row = lambda n: jnp.ones((1, n), F32)

    norm1 = norm1_w[:, None, :]
    norm2 = norm2_w[:, None, :]
    ssd_norm = ssd_norm_w[:, None, :]
    ml_norm = ml_norm_w[:, None, :]
    d_full = jnp.repeat(ssd_d, ssd_hd, axis=-1)[:, None, :]
    w3 = hy_ffn_w3.reshape(depth, hy_ffn_w3.shape[1], HY_ORDER, 2, hy_inner).transpose(0, 2, 3, 1, 4)
    decay = hy_decay[:, :, :, None, :]
    skip = hy_skip[:, :, None, :]
    hb1 = hy_ffn_b1[:, None, :]
    hb2 = hy_ffn_b2[:, None, :]

    for l in range(depth):
        need_ctx = l < depth - 1
        used_tiles = n_tiles if need_ctx else n_lat_tiles
        mod = _mod_vectors(c_all, mod_w, mod_b[:, None, :], l).reshape(-1, 6, d)

        wl = w_in[l].astype(BF16)
        w_nat = jnp.concatenate([wl[:, o_z:o_dt], wl[:, o_g:], wl[:, o_hy:o_g]], axis=1)
        w_xbc = wl[:, :o_z]
        w_ml = wl[:, o_ml:o_mlg]
        gpad = lambda w: jnp.pad(w, ((0, 0), (0, 128 - w.shape[1])))
        hn = _normmod(xa, norm1, l, mod, 0, n_lat_tiles, n_tiles)
        hn_cm = _lat_colmajor(hn, seq)
        hn2 = hn.reshape(nb * s, d)
        hn_cm2 = hn_cm.reshape(nb * s, d)
        pnat = _matmul(hn2, w_nat, BF16, "proj_nat").reshape(nb, s, -1)
        pml = _matmul(hn_cm2, w_ml, BF16, "proj_ml").reshape(nb, s, -1)
        p_dt = _matmul(hn2, gpad(wl[:, o_dt:o_ml]), F32, "proj_dt")
        p_mlg = _matmul(hn_cm2, gpad(wl[:, o_mlg:o_hy]), F32, "proj_mlg")
        pxbc = _matmul(hn2, w_xbc, BF16, "proj_xbc").reshape(nb, s, -1)
        c_g = ssd_inner
        c_hy = c_g + N_BRANCH * d

        xbc = _dwconv(pxbc, 0, ssd_conv_ch, ssd_conv_w[l], ssd_conv_b[l][None], ones_row(ssd_conv_ch),
                      True, n_lat_tiles, n_tiles, "ssd_conv")
        dt_r = _dir_rows(p_dt, nb, s, ssd_heads)
        y_f, y_b = _ssd_scan(xbc, dt_r, ssd_dt_bias[l][:, :, None], ssd_a_log[l][:, :, None],
                             ssd_heads, ssd_hd, ssd_ds, n_lat_ch, n_ch)
        ys = _ssd_out(y_f, y_b, xbc, pnat, d_full, ssd_norm, l, used_tiles)

        qk = _dwconv(pml, 0, 2 * ml_inner, ml_conv_w[l], ml_conv_b[l][None], k_scale,
                     True, n_lat_tiles, n_tiles, "ml_conv")
        g_r = _dir_rows(p_mlg, nb, s, 2 * ml_heads)
        gate_b = ml_gate_b[l]
        c_f, c_b = _ml_scan(qk, pml, g_r[:, :, :ml_heads], g_r[:, :, ml_heads:],
                            gate_b[:, 0, :, None], gate_b[:, 1, :, None], ml_heads, n_lat_ch, n_ch)
        ym = _lat_rowmajor(_ml_out(c_f, c_b, pml, ml_norm, l, ml_heads, used_tiles), seq)

        hcw, hcb = hy_conv_w[l], hy_conv_b[l][None]
        fl = _hy_filters(seq, l, hy_ffn_w1, hb1, hy_ffn_w2, hb2, w3, decay, tab_lat)
        rows = s if need_ctx else seq
        yh = _hy_conv(pnat, c_hy, hy_inner, hcw, hcb, None, rows, 0, seq, l, *fl, skip, tab_lat)
        if need_ctx:
            fc = _hy_filters(ctx_len, l, hy_ffn_w1, hb1, hy_ffn_w2, hb2, w3, decay, tab_ctx)
            yh = _hy_conv(pnat, c_hy, hy_inner, hcw, hcb, yh, rows, seq // ctx_len, ctx_len, l, *fc, skip,
                          tab_ctx)

        xa = _merge(ys, ym, yh, pnat, w_branch[l].astype(BF16), w_out[l].astype(BF16), xa, mod,
                    n_lat_tiles, used_tiles)
        xa = _mlp(xa, norm2, l, mod, mlp_w1[l].astype(BF16), mlp_w2[l].astype(BF16), n_lat_tiles, used_tiles)

    return _final_norm(xa, norm_f_w[None], seq)
```

```python
import functools
import math

import jax
import jax.numpy as jnp
import numpy as np
from jax import lax
from jax.experimental import pallas as pl
from jax.experimental.pallas import tpu as pltpu

F32 = jnp.float32
BF16 = jnp.bfloat16
HIGHEST = lax.Precision.HIGHEST

GRID_W = 64
CHUNK = 128
EPS = 1e-6
SSD_GROUPS = 2
SSD_CONV = 5
ML_HEADDIM = 128
ML_CONV = 5
HY_ORDER = 2
HY_SHORT = 3
HY_BANDS = 16
N_BRANCH = 3

TOKEN_TILE = 256
LANE_TILE = 256
HALO = 16
VMEM_LIMIT = 56 * 1024 * 1024

_hdot = functools.partial(jnp.dot, precision=HIGHEST, preferred_element_type=F32)
_dot = functools.partial(jnp.dot, preferred_element_type=F32)


def _params(sem, vmem=None):
    return pltpu.CompilerParams(dimension_semantics=sem, vmem_limit_bytes=vmem or VMEM_LIMIT)


def _softplus(x):
    return jnp.maximum(x, 0.0) + jnp.log(1.0 + jnp.exp(-jnp.abs(x)))


def _silu(x):
    return x * jax.nn.sigmoid(x)


def _mod_kernel(c_ref, w_ref, b_ref, o_ref):
    o_ref[...] = _hdot(_silu(c_ref[...]), w_ref[...]) + b_ref[...]


def _mod_vectors(c_all, mod_w, mod_b, layer):
    r, d = c_all.shape
    n = mod_w.shape[-1]
    tn = n // 6
    return pl.pallas_call(
        _mod_kernel,
        out_shape=jax.ShapeDtypeStruct((r, n), F32),
        grid=(n // tn,),
        in_specs=[pl.BlockSpec((r, d), lambda j: (0, 0)),
                  pl.BlockSpec((None, d, tn), lambda j: (layer, 0, j)),
                  pl.BlockSpec((None, 1, tn), lambda j: (layer, 0, j))],
        out_specs=pl.BlockSpec((r, tn), lambda j: (0, j)),
        compiler_params=_params(("parallel",)),
        name="mod_vectors",
    )(c_all, mod_w, mod_b)


def _grid_perm(rows_g):
    wpt = TOKEN_TILE // rows_g
    src = np.arange(TOKEN_TILE).reshape(rows_g, wpt).T.reshape(-1)
    p = np.zeros((TOKEN_TILE, TOKEN_TILE), np.float32)
    p[np.arange(TOKEN_TILE), src] = 1.0
    return p


def _normmod_kernel(x_ref, x4_ref, perm_ref, nw_ref, mod_ref, o_ref, ocm_ref, *, si, n_lat_tiles):
    i = pl.program_id(1)
    m = mod_ref[...]

    def normed(x):
        h = x * lax.rsqrt(jnp.mean(x * x, axis=-1, keepdims=True) + EPS) * nw_ref[...]
        return (h * (1.0 + m[si + 1:si + 2]) + m[si:si + 1]).astype(BF16)

    h = normed(x_ref[...])
    o_ref[...] = h

    @pl.when(i < n_lat_tiles)
    def _():
        x4 = x4_ref[...]
        ocm_ref[...] = _dot(perm_ref[...], normed(x4.reshape(TOKEN_TILE, x4.shape[-1]))).astype(BF16)

    @pl.when(i >= n_lat_tiles)
    def _():
        ocm_ref[...] = h


def _mod_row(nb, n_lat_tiles):
    return lambda b, i: (jnp.where(i < n_lat_tiles, b, nb), 0, 0)


def _normmod(x, norm_w, layer, mod, si, rows_g, n_lat_tiles, n_tiles):
    nb, s, d = x.shape
    wpt = TOKEN_TILE // rows_g
    assert wpt % 8 == 0 and GRID_W % wpt == 0 and s % GRID_W == 0
    x4 = x.reshape(nb, s // GRID_W, GRID_W, d)
    tile = pl.BlockSpec((None, TOKEN_TILE, d), lambda b, i: (b, i, 0))
    return pl.pallas_call(
        functools.partial(_normmod_kernel, si=si, n_lat_tiles=n_lat_tiles),
        out_shape=(jax.ShapeDtypeStruct((nb, s, d), BF16),) * 2,
        grid=(nb, n_tiles),
        in_specs=[tile,
                  pl.BlockSpec((None, rows_g, wpt, d), lambda b, i: (b, 0, jnp.minimum(i, n_lat_tiles - 1), 0)),
                  pl.BlockSpec((TOKEN_TILE, TOKEN_TILE), lambda b, i: (0, 0)),
                  pl.BlockSpec((None, 1, d), lambda b, i: (layer, 0, 0)),
                  pl.BlockSpec((None, 6, d), _mod_row(nb, n_lat_tiles))],
        out_specs=(tile, tile),
        compiler_params=_params(("parallel", "parallel")),
        name="normmod",
    )(x, x4, jnp.asarray(_grid_perm(rows_g), BF16), norm_w, mod)


def _mm_kernel(a_ref, w_ref, o_ref):
    o_ref[...] = _dot(a_ref[...], w_ref[...]).astype(o_ref.dtype)


def _pick(n, cands):
    for c in cands:
        if n % c == 0:
            return c
    return n


def _matmul(a, w, out_dtype, name):
    t, k = a.shape
    n = w.shape[1]
    tm = _pick(t, (1024, 768, 512, 256))
    tn = _pick(n, (1024, 512, 256, 128))
    return pl.pallas_call(
        _mm_kernel,
        out_shape=jax.ShapeDtypeStruct((t, n), out_dtype),
        grid=(t // tm, n // tn),
        in_specs=[pl.BlockSpec((tm, k), lambda i, j: (i, 0)),
                  pl.BlockSpec((k, tn), lambda i, j: (0, j))],
        out_specs=pl.BlockSpec((tm, tn), lambda i, j: (i, j)),
        compiler_params=_params(("parallel", "parallel")),
        name=name,
    )(a, w)


def _dwconv_kernel(u_ref, p_ref, n_ref, w_ref, b_ref, s_ref, o_ref, *, taps, act, bounds):
    i = pl.program_id(1)
    tm = u_ref.shape[0]
    pad = taps // 2
    lv, rv = jnp.float32(1.0), jnp.float32(1.0)
    for e in bounds:
        lv = jnp.where(i == e, 0.0, lv)
        rv = jnp.where(i == e - 1, 0.0, rv)
    ext = jnp.concatenate([p_ref[...].astype(F32) * lv, u_ref[...].astype(F32),
                           n_ref[...].astype(F32) * rv], axis=0)
    rows = ext.shape[0]
    w = w_ref[...]
    acc = jnp.zeros((tm, u_ref.shape[1]), F32) + b_ref[...]
    for j in range(taps):
        sh = (pad - j) % rows
        shifted = ext if sh == 0 else pltpu.roll(ext, sh, 0)
        acc = acc + shifted[HALO:HALO + tm] * w[j:j + 1]
    if act:
        acc = _silu(acc)
    o_ref[...] = (acc * s_ref[...]).astype(o_ref.dtype)


def _dwconv(u, col0, ncols, w, b, scale, act, n_lat_tiles, n_tiles, name):
    nb, s, _ = u.shape
    taps = w.shape[0]
    tc = _pick(ncols, (2048, 1536, 1024, 512, 256))
    assert col0 % tc == 0
    c0 = col0 // tc
    hb = TOKEN_TILE // HALO
    last = s // HALO - 1
    bounds = (0, n_lat_tiles, n_tiles)
    return pl.pallas_call(
        functools.partial(_dwconv_kernel, taps=taps, act=act, bounds=bounds),
        out_shape=jax.ShapeDtypeStruct((nb, s, ncols), BF16),
        grid=(nb, n_tiles, ncols // tc),
        in_specs=[pl.BlockSpec((None, TOKEN_TILE, tc), lambda bb, i, c: (bb, i, c0 + c)),
                  pl.BlockSpec((None, HALO, tc), lambda bb, i, c: (bb, jnp.maximum(i * hb - 1, 0), c0 + c)),
                  pl.BlockSpec((None, HALO, tc), lambda bb, i, c: (bb, jnp.minimum((i + 1) * hb, last), c0 + c)),
                  pl.BlockSpec((taps, tc), lambda bb, i, c: (0, c)),
                  pl.BlockSpec((1, tc), lambda bb, i, c: (0, c)),
                  pl.BlockSpec((1, tc), lambda bb, i, c: (0, c))],
        out_specs=pl.BlockSpec((None, TOKEN_TILE, tc), lambda bb, i, c: (bb, i, c)),
        compiler_params=_params(("parallel", "parallel", "parallel")),
        name=name,
    )(u, u, u, w, b, scale)


def _chunk_index(n_lat_ch, n_ch):
    return lambda d, j: (j + n_lat_ch) % n_ch if d == 0 else n_ch - 1 - j


def _scan_masks(d, t):
    ii = lax.broadcasted_iota(jnp.int32, (t, t), 0)
    jj = lax.broadcasted_iota(jnp.int32, (t, t), 1)
    mask = jj <= ii if d == 0 else jj >= ii
    tri_t = jnp.where(ii <= jj if d == 0 else ii >= jj, 1.0, 0.0).astype(F32)
    return mask, tri_t


def _gate_rows(la_r, lw_r, tri_t, m_prev):
    cum_r = _hdot(la_r, tri_t)
    last = jnp.sum(la_r, axis=1, keepdims=True)
    g_r = last - cum_r + lw_r
    m_loc = jnp.max(g_r, axis=1, keepdims=True)
    e_r = jnp.exp(g_r - m_loc)
    m_new = jnp.maximum(last + m_prev, m_loc)
    s_old = jnp.exp(last + m_prev - m_new)
    s_new = jnp.exp(m_loc - m_new)
    return cum_r, cum_r - lw_r, e_r, m_new, s_old, s_new


def _head_probs(cum_row, crow_row, m_prev_h, mask, qk):
    t = qk.shape[0]
    colb = jnp.broadcast_to(cum_row, (t, t)).T
    dlog = jnp.where(mask, colb - crow_row, -jnp.inf)
    inter = colb[:, 0:1] + m_prev_h
    m_row = jnp.maximum(inter, jnp.max(dlog, axis=1, keepdims=True))
    p = jnp.exp(dlog - m_row) * qk
    return p.astype(BF16), jnp.exp(inter - m_row), m_row


def _ssd_scan_kernel(*refs, heads, groups, hd, ds):
    j = pl.program_id(1)
    ins, outs, states = refs[:12], refs[12:14], refs[14:16]

    @pl.when(j == 0)
    def _():
        for st_ref in states:
            st_ref[...] = jnp.zeros_like(st_ref)

    for d in range(2):
        _ssd_chunk(d, *ins[6 * d:6 * d + 6], outs[d], states[d], heads=heads, groups=groups, hd=hd, ds=ds)


def _ssd_chunk(d, xs_ref, b_ref, c_ref, dt_ref, bias_ref, alog_ref, o_ref, st_ref, *, heads, groups, hd, ds):
    t = xs_ref.shape[0]
    hpg = heads // groups
    pw = 2 * hd
    mask, tri_t = _scan_masks(d, t)
    dt = _softplus(dt_ref[...] + bias_ref[...])
    la_r = -dt * jnp.exp(alog_ref[...])
    cum_r = _hdot(la_r, tri_t)
    last = jnp.sum(la_r, axis=1, keepdims=True)
    crow = cum_r - jnp.log(dt)
    e_r = jnp.exp(last - crow)
    e_last = jnp.exp(last)
    lo = lax.broadcasted_iota(jnp.int32, (1, pw), 1) < hd

    for g in range(groups):
        q = c_ref[:, g * ds:(g + 1) * ds]
        k_t = b_ref[:, g * ds:(g + 1) * ds].astype(F32).T
        qk = _dot(q, k_t.astype(BF16))
        w0 = g * hpg * hd
        qs = _dot(q, st_ref[:, w0:w0 + hpg * hd].astype(BF16))
        for i in range(hpg // 2):
            h0 = g * hpg + 2 * i
            c0 = h0 * hd
            vp = xs_ref[:, c0:c0 + pw]
            zero = jnp.zeros_like(vp)
            v_bd = jnp.concatenate([jnp.where(lo, vp, zero), jnp.where(lo, zero, vp)], axis=0)
            probs, carry, kte = [], [], []
            for h in (h0, h0 + 1):
                colb = jnp.broadcast_to(cum_r[h:h + 1], (t, t)).T
                dlog = jnp.where(mask, colb - crow[h:h + 1], -jnp.inf)
                probs.append((jnp.exp(dlog) * qk).astype(BF16))
                carry.append(jnp.exp(colb))
                kte.append((k_t * e_r[h:h + 1]).astype(BF16))
            lhs = jnp.concatenate([jnp.concatenate(probs, axis=1), jnp.concatenate(kte, axis=1)], axis=0)
            res = _dot(lhs, v_bd)
            o_ref[:, c0:c0 + pw] = res[:t] + qs[:, 2 * i * hd:2 * i * hd + pw] * jnp.where(lo, carry[0], carry[1])
            decay = jnp.where(lo, e_last[h0:h0 + 1], e_last[h0 + 1:h0 + 2])
            st_ref[:, c0:c0 + pw] = decay * st_ref[:, c0:c0 + pw] + res[t:]


def _ssd_scan(xbc, dt_r, bias_c, alog_c, heads, hd, ds, n_lat_ch, n_ch):
    nb, s, _ = xbc.shape
    inner = heads * hd
    gn = SSD_GROUPS * ds
    assert 2 * hd == CHUNK and (heads // SSD_GROUPS) % 2 == 0 and inner % gn == 0
    cidx = _chunk_index(n_lat_ch, n_ch)

    def specs(d):
        return [pl.BlockSpec((None, CHUNK, inner), lambda b, j: (b, cidx(d, j), 0)),
                pl.BlockSpec((None, CHUNK, gn), lambda b, j: (b, cidx(d, j), inner // gn)),
                pl.BlockSpec((None, CHUNK, gn), lambda b, j: (b, cidx(d, j), inner // gn + 1)),
                pl.BlockSpec((None, None, heads, CHUNK), lambda b, j: (d, b, 0, cidx(d, j))),
                pl.BlockSpec((None, heads, 1), lambda b, j: (d, 0, 0)),
                pl.BlockSpec((None, heads, 1), lambda b, j: (d, 0, 0))]

    out = lambda d: pl.BlockSpec((None, CHUNK, inner), lambda b, j: (b, cidx(d, j), 0))
    args = (xbc, xbc, xbc, dt_r, bias_c, alog_c)
    return pl.pallas_call(
        functools.partial(_ssd_scan_kernel, heads=heads, groups=SSD_GROUPS, hd=hd, ds=ds),
        out_shape=(jax.ShapeDtypeStruct((nb, s, inner), F32),) * 2,
        grid=(nb, n_ch),
        in_specs=specs(0) + specs(1),
        out_specs=(out(0), out(1)),
        scratch_shapes=[pltpu.VMEM((ds, inner), F32)] * 2,
        compiler_params=_params(("parallel", "arbitrary")),
        name="ssd_scan",
    )(*args, *args)


def _ml_scan_kernel(*refs, heads):
    j = pl.program_id(1)
    ins, outs, states = refs[:14], refs[14:16], refs[16:20]

    @pl.when(j == 0)
    def _():
        for ref in states:
            ref[...] = jnp.zeros_like(ref)

    for d in range(2):
        _ml_chunk(d, *ins[7 * d:7 * d + 7], outs[d], *states[2 * d:2 * d + 2], heads=heads)


def _ml_chunk(d, q_ref, k_ref, v_ref, gi_ref, gf_ref, bi_ref, bf_ref, o_ref, st_ref, m_ref, *, heads):
    t = q_ref.shape[0]
    dh = ML_HEADDIM
    mask, tri_t = _scan_masks(d, t)
    lw_r = gi_ref[...] + bi_ref[...]
    f = gf_ref[...] + bf_ref[...]
    la_r = jnp.minimum(f, 0.0) - jnp.log(1.0 + jnp.exp(-jnp.abs(f)))
    m_prev = m_ref[:, 0:1]
    cum_r, crow, e_r, m_new, s_old, s_new = _gate_rows(la_r, lw_r, tri_t, m_prev)
    ones = jnp.ones((t, dh), BF16)

    for h in range(heads):
        c0 = h * dh
        q = q_ref[:, c0:c0 + dh]
        k_t = k_ref[:, c0:c0 + dh].astype(F32).T
        v_aug = jnp.concatenate([v_ref[:, c0:c0 + dh], ones], axis=1)
        qk = _dot(q, k_t.astype(BF16))
        s0 = 2 * c0
        qs = _dot(q, st_ref[:, s0:s0 + 2 * dh].astype(BF16))
        p, cf, m_row = _head_probs(cum_r[h:h + 1], crow[h:h + 1], m_prev[h:h + 1], mask, qk)
        res = _dot(jnp.concatenate([p, (k_t * e_r[h:h + 1]).astype(BF16)], axis=0), v_aug)
        y = res[:t] + qs * cf
        o_ref[:, c0:c0 + dh] = y[:, :dh] / jnp.maximum(jnp.abs(y[:, dh:]), jnp.exp(-m_row))
        st_ref[:, s0:s0 + 2 * dh] = s_old[h:h + 1] * st_ref[:, s0:s0 + 2 * dh] + s_new[h:h + 1] * res[t:]
    m_ref[...] = jnp.broadcast_to(m_new, m_ref.shape)


def _ml_scan(qk, pml, gi_r, gf_r, bi_c, bf_c, heads, n_lat_ch, n_ch):
    nb, s, _ = qk.shape
    inner = heads * ML_HEADDIM
    cidx = _chunk_index(n_lat_ch, n_ch)

    def specs(d):
        gspec = pl.BlockSpec((None, None, heads, CHUNK), lambda b, j: (d, b, 0, cidx(d, j)))
        bspec = pl.BlockSpec((None, heads, 1), lambda b, j: (d, 0, 0))
        col = lambda c: pl.BlockSpec((None, CHUNK, inner), lambda b, j: (b, cidx(d, j), c))
        return [col(0), col(1), col(2), gspec, gspec, bspec, bspec]

    out = lambda d: pl.BlockSpec((None, CHUNK, inner), lambda b, j: (b, cidx(d, j), 0))
    args = (qk, qk, pml, gi_r, gf_r, bi_c, bf_c)
    return pl.pallas_call(
        functools.partial(_ml_scan_kernel, heads=heads),
        out_shape=(jax.ShapeDtypeStruct((nb, s, inner), F32),) * 2,
        grid=(nb, n_ch),
        in_specs=specs(0) + specs(1),
        out_specs=(out(0), out(1)),
        scratch_shapes=[pltpu.VMEM((ML_HEADDIM, 2 * inner), F32), pltpu.VMEM((heads, 128), F32)] * 2,
        compiler_params=_params(("parallel", "arbitrary")),
        name="ml_scan",
    )(*args, *args)


def _ssd_out_kernel(yf_ref, yb_ref, xs_ref, z_ref, d_ref, nw_ref, o_ref):
    y = yf_ref[...] + yb_ref[...] + d_ref[...] * xs_ref[...].astype(F32)
    y = y * _silu(z_ref[...].astype(F32))
    y = y * lax.rsqrt(jnp.mean(y * y, axis=-1, keepdims=True) + EPS) * nw_ref[...]
    o_ref[...] = y.astype(o_ref.dtype)


def _ssd_out(yf, yb, xbc, pnat, d_full, norm_w, layer, n_tiles):
    nb, _, inner = yf.shape
    row = pl.BlockSpec((None, TOKEN_TILE, inner), lambda b, i: (b, i, 0))
    return pl.pallas_call(
        _ssd_out_kernel,
        out_shape=jax.ShapeDtypeStruct((nb, n_tiles * TOKEN_TILE, inner), BF16),
        grid=(nb, n_tiles),
        in_specs=[row, row, row, row,
                  pl.BlockSpec((None, 1, inner), lambda b, i: (layer, 0, 0)),
                  pl.BlockSpec((None, 1, inner), lambda b, i: (layer, 0, 0))],
        out_specs=row,
        compiler_params=_params(("parallel", "parallel")),
        name="ssd_out",
    )(yf, yb, xbc, pnat, d_full, norm_w)


def _ml_out_kernel(cf_ref, cb_ref, o_in_ref, nw_ref, perm_ref, *rest, heads, permute):
    o_ref = rest[-1]
    dh = ML_HEADDIM
    parts = []
    for h in range(heads):
        sl = slice(h * dh, (h + 1) * dh)
        c = cf_ref[:, sl] + cb_ref[:, sl]
        c = c * lax.rsqrt(jnp.mean(c * c, axis=-1, keepdims=True) + EPS) * nw_ref[:, sl]
        parts.append((jax.nn.sigmoid(o_in_ref[:, sl].astype(F32)) * c).astype(BF16))
    y = jnp.concatenate(parts, axis=1)
    y = _dot(perm_ref[...], y) if permute else y.astype(F32)
    o_ref[...] = y.reshape(o_ref.shape)


def _ml_out(cf, cb, pml, norm_w, layer, heads, rows_g, n_lat_tiles, n_tiles):
    nb, _, inner = cf.shape
    wpt = TOKEN_TILE // rows_g
    rpt = TOKEN_TILE // GRID_W
    perm_t = jnp.asarray(_grid_perm(rows_g).T, BF16)
    out_shape = jax.ShapeDtypeStruct((nb, n_tiles * rpt, GRID_W, inner), F32)

    def call(tile0, tiles, out_spec, permute, prev):
        row = pl.BlockSpec((None, TOKEN_TILE, inner), lambda b, i: (b, tile0 + i, 0))
        in_specs = [row, row,
                    pl.BlockSpec((None, TOKEN_TILE, inner), lambda b, i: (b, tile0 + i, 3)),
                    pl.BlockSpec((None, 1, inner), lambda b, i: (layer, 0, 0)),
                    pl.BlockSpec((TOKEN_TILE, TOKEN_TILE), lambda b, i: (0, 0))]
        args = [cf, cb, pml, norm_w, perm_t]
        if prev is not None:
            in_specs.append(pl.BlockSpec(memory_space=pl.ANY))
            args.append(prev)
        return pl.pallas_call(
            functools.partial(_ml_out_kernel, heads=heads, permute=permute),
            out_shape=out_shape,
            grid=(nb, tiles),
            in_specs=in_specs,
            out_specs=out_spec,
            input_output_aliases={} if prev is None else {len(args) - 1: 0},
            compiler_params=_params(("parallel", "parallel")),
            name="ml_out_lat" if permute else "ml_out_ctx",
        )(*args)

    y = call(0, n_lat_tiles, pl.BlockSpec((None, rows_g, wpt, inner), lambda b, i: (b, 0, i, 0)), True, None)
    if n_tiles > n_lat_tiles:
        y = call(n_lat_tiles, n_tiles - n_lat_tiles,
                 pl.BlockSpec((None, rpt, GRID_W, inner), lambda b, i: (b, n_lat_tiles + i, 0, 0)), False, y)
    return y.reshape(nb, n_tiles * TOKEN_TILE, inner)


def _dft_tables(length):
    m = length // 2
    k = np.arange(m, dtype=np.int64)
    ang = ((k[:, None] * k[None, :]) % (2 * m)).astype(np.float64) * (np.pi / m)
    alt8 = np.broadcast_to(np.where(k % 2 == 0, 1.0, -1.0)[None, :], (8, m))
    tw = k.astype(np.float64)[:, None] * (np.pi / length) * np.ones((1, LANE_TILE))
    return (jnp.asarray(np.cos(ang), BF16), jnp.asarray(np.sin(ang), BF16), jnp.asarray(alt8, BF16),
            jnp.asarray(np.cos(tw), F32), jnp.asarray(np.sin(tw), F32))


def _hy_feats(length):
    t = jnp.arange(length, dtype=F32)
    t_norm = t / length
    bands = jnp.linspace(1e-4, HY_BANDS - 1, HY_BANDS, dtype=F32)
    ang = (2.0 * math.pi / length) * t[:, None] * bands[None, :]
    feats = jnp.concatenate([t_norm[:, None], jnp.cos(ang), -jnp.sin(ang)], axis=-1)
    return feats[0::2], feats[1::2], t_norm[0::2, None], t_norm[1::2, None]


def _split_spectrum(ae, be, ao, bo, twc, tws):
    tr = twc * ao - tws * bo
    tm = twc * bo + tws * ao
    return (ae + tr, ae - tr), (be + tm, tm - be)


def _hy_filter_kernel(fe_ref, fo_ref, tne_ref, tno_ref, w1_ref, b1_ref, w2_ref, b2_ref, w3f_ref, w3b_ref,
                      df_ref, db_ref, cm_ref, sm_ref, alt_ref, twc_ref, tws_ref, ha_ref, hb_ref, hm_ref,
                      hide_ref, hido_ref):
    m = cm_ref.shape[0]

    @pl.when((pl.program_id(0) == 0) & (pl.program_id(1) == 0))
    def _():
        for f_ref, h_ref in ((fe_ref, hide_ref), (fo_ref, hido_ref)):
            hid = jnp.sin(_hdot(f_ref[...], w1_ref[...]) + b1_ref[...])
            h_ref[...] = jnp.sin(_hdot(hid, w2_ref[...]) + b2_ref[...])

    def taps(hid, tn):
        h_f = _hdot(hid, w3f_ref[...]) * jnp.exp(-tn * jnp.abs(df_ref[...]))
        h_b = _hdot(hid, w3b_ref[...]) * jnp.exp(-tn * jnp.abs(db_ref[...]))
        return (h_f + h_b).astype(BF16), (h_f - h_b).astype(BF16)

    sum_e, dif_e = taps(hide_ref[...], tne_ref[...])
    sum_o, dif_o = taps(hido_ref[...], tno_ref[...])
    twc, tws = twc_ref[...], tws_ref[...]
    cm, sm = cm_ref[...], sm_ref[...]
    (ha_lo, ha_hi), _ = _split_spectrum(_dot(cm, sum_e), 0.0, _dot(cm, sum_o), _dot(sm, sum_o), twc, tws)
    _, (hb_lo, hb_hi) = _split_spectrum(0.0, _dot(sm, dif_e), _dot(cm, dif_o), _dot(sm, dif_o), twc, tws)
    ha_ref[0:m, :] = ha_lo
    ha_ref[m:2 * m, :] = ha_hi
    hb_ref[0:m, :] = hb_lo
    hb_ref[m:2 * m, :] = hb_hi
    row = lax.broadcasted_iota(jnp.int32, hm_ref.shape, 0)
    hm_ref[...] = jnp.where(row == 0, _dot(alt_ref[...], sum_e), _dot(alt_ref[...], dif_o))


def _hy_filters(length, layer, w1, b1, w2, b2, w3, decay, tables):
    cm, sm, alt8, twc, tws = tables
    m = length // 2
    feats = _hy_feats(length)
    nfeat, nf = w1.shape[-2:]
    ch = decay.shape[-1]
    nct = ch // LANE_TILE
    const = lambda shape: pl.BlockSpec(shape, lambda n, c: (0,) * len(shape))
    w3spec = lambda dr: pl.BlockSpec((None, None, None, nf, LANE_TILE), lambda n, c: (layer, n, dr, 0, c))
    dspec = lambda dr: pl.BlockSpec((None, None, None, 1, LANE_TILE), lambda n, c: (layer, n, dr, 0, c))
    lay = lambda a, b: pl.BlockSpec((None, a, b), lambda n, c: (layer, 0, 0))
    out = lambda rows: pl.BlockSpec((None, rows, LANE_TILE), lambda n, c: (n, 0, c))
    return pl.pallas_call(
        _hy_filter_kernel,
        out_shape=(jax.ShapeDtypeStruct((HY_ORDER, length, ch), F32),
                   jax.ShapeDtypeStruct((HY_ORDER, length, ch), F32),
                   jax.ShapeDtypeStruct((HY_ORDER, 8, ch), F32)),
        grid=(HY_ORDER, nct),
        in_specs=[const((m, nfeat)), const((m, nfeat)), const((m, 1)), const((m, 1)),
                  lay(nfeat, nf), lay(1, nf), lay(nf, nf), lay(1, nf),
                  w3spec(0), w3spec(1), dspec(0), dspec(1),
                  const((m, m)), const((m, m)), const((8, m)),
                  const((m, LANE_TILE)), const((m, LANE_TILE))],
        out_specs=(out(length), out(length), out(8)),
        scratch_shapes=[pltpu.VMEM((m, nf), F32)] * 2,
        compiler_params=_params(("arbitrary", "arbitrary")),
        name="hyena_filters_%d" % length,
    )(*feats, w1, b1, w2, b2, w3, w3, decay, decay, cm, sm, alt8, twc, tws)


def _short_conv(u, w, b):
    length = u.shape[0]
    taps = w.shape[0]
    pad = taps // 2
    row = lax.broadcasted_iota(jnp.int32, u.shape, 0)
    acc = b + u * w[pad:pad + 1]
    for j in range(taps):
        off = j - pad
        if off != 0:
            moved = pltpu.roll(u, (-off) % length, 0)
            valid = (row + off >= 0) & (row + off < length)
            acc = acc + jnp.where(valid, moved, 0.0) * w[j:j + 1]
    return acc


def _hy_conv_kernel(z_ref, g_ref, wz_ref, bz_ref, wg_ref, bg_ref, ha_ref, hb_ref, hm_ref, skip_ref,
                    cm_ref, sm_ref, alt_ref, twc_ref, tws_ref, o_ref, tmp_ref, ze_ref, zo_ref):
    n = pl.program_id(2)
    m, width = ze_ref.shape
    lanes = tmp_ref.shape[-1]
    slabs = range(width // lanes)

    def put(val):
        for h in slabs:
            tmp_ref[h] = val[:, h * lanes:(h + 1) * lanes]

    def take(first):
        return jnp.concatenate([tmp_ref[h, pl.ds(first, m, stride=2), :] for h in slabs], axis=1)

    @pl.when(n == 0)
    def _():
        put(_short_conv(z_ref[...].astype(F32), wz_ref[...], bz_ref[...]))
        ze_ref[...] = take(0)
        zo_ref[...] = take(1)

    put(_short_conv(g_ref[...].astype(F32), wg_ref[...], bg_ref[...]))
    ge, go = take(0), take(1)
    ze, zo = ze_ref[...], zo_ref[...]
    zz = jnp.concatenate([ze.astype(BF16), zo.astype(BF16)], axis=1)
    a = _dot(cm_ref[...], zz)
    b = _dot(sm_ref[...], zz)
    mid = _dot(alt_ref[...], zz)[0:1]
    twc, tws = twc_ref[...], tws_ref[...]
    (a_lo, a_hi), (b_lo, b_hi) = _split_spectrum(a[:, :width], b[:, :width], a[:, width:], b[:, width:], twc, tws)

    ha_lo, ha_hi, hb_lo, hb_hi = ha_ref[0:m, :], ha_ref[m:2 * m, :], hb_ref[0:m, :], hb_ref[m:2 * m, :]
    yr_lo, ym_lo = a_lo * ha_lo - b_lo * hb_lo, a_lo * hb_lo + b_lo * ha_lo
    yr_hi, ym_hi = a_hi * ha_hi - b_hi * hb_hi, a_hi * hb_hi + b_hi * ha_hi
    ha_m, hb_m = hm_ref[0:1, :], hm_ref[1:2, :]
    yr_m = mid[:, :width] * ha_m - mid[:, width:] * hb_m
    ym_m = mid[:, :width] * hb_m + mid[:, width:] * ha_m

    row = lax.broadcasted_iota(jnp.int32, (m, width), 0)
    half0 = jnp.where(row == 0, 0.5, 1.0)
    qr, qm = yr_lo - yr_hi, ym_lo + ym_hi
    pr = jnp.concatenate([((yr_lo + yr_hi) * half0).astype(BF16),
                          ((qr * twc + qm * tws) * half0).astype(BF16)], axis=1)
    pm = jnp.concatenate([(ym_lo - ym_hi).astype(BF16), (qm * twc - qr * tws).astype(BF16)], axis=1)
    y = _dot(cm_ref[...], pr) + _dot(sm_ref[...], pm)
    alt = jnp.where((row & 1) == 0, 1.0, -1.0)
    scale = 1.0 / (2 * m)
    skip = skip_ref[...]
    ze_new = ge * ((y[:, :width] + alt * yr_m) * scale + skip * ze)
    zo_new = go * ((y[:, width:] + alt * ym_m) * scale + skip * zo)

    @pl.when(n < HY_ORDER - 1)
    def _():
        ze_ref[...] = ze_new
        zo_ref[...] = zo_new

    @pl.when(n == HY_ORDER - 1)
    def _():
        for h in slabs:
            tmp_ref[h, pl.ds(0, m, stride=2), :] = ze_new[:, h * lanes:(h + 1) * lanes]
            tmp_ref[h, pl.ds(1, m, stride=2), :] = zo_new[:, h * lanes:(h + 1) * lanes]
            o_ref[:, h * lanes:(h + 1) * lanes] = tmp_ref[h].astype(o_ref.dtype)


def _hy_conv(u, col0, ch, conv_w, conv_b, out_prev, out_rows, row_block, length, layer, ha, hb, hm, skip,
             tables):
    cm, sm, alt8, twc, tws = tables
    nb = u.shape[0]
    s = out_rows
    m = length // 2
    nct = ch // LANE_TILE
    assert col0 % LANE_TILE == 0
    cb = col0 // LANE_TILE
    taps = conv_w.shape[0]
    const = lambda shape: pl.BlockSpec(shape, lambda c, b, n: (0,) * len(shape))
    hspec = lambda rows: pl.BlockSpec((None, rows, LANE_TILE), lambda c, b, n: (n, 0, c))
    zcol = lambda c, n: c
    gcol = lambda c, n: (1 + n) * nct + c
    wspec = lambda rows, col: pl.BlockSpec((rows, LANE_TILE), lambda c, b, n: (0, col(c, n)))
    in_specs = [pl.BlockSpec((None, length, LANE_TILE), lambda c, b, n: (b, row_block, cb + zcol(c, n))),
                pl.BlockSpec((None, length, LANE_TILE), lambda c, b, n: (b, row_block, cb + gcol(c, n))),
                wspec(taps, zcol), wspec(1, zcol), wspec(taps, gcol), wspec(1, gcol),
                hspec(length), hspec(length), hspec(8),
                pl.BlockSpec((None, None, 1, LANE_TILE), lambda c, b, n: (layer, n, 0, c)),
                const((m, m)), const((m, m)), const((8, m)), const((m, LANE_TILE)), const((m, LANE_TILE))]
    args = [u, u, conv_w, conv_b, conv_w, conv_b, ha, hb, hm, skip, cm, sm, alt8, twc, tws]
    n_in = len(args)
    aliases = {}
    if out_prev is not None:
        in_specs.append(pl.BlockSpec(memory_space=pl.ANY))
        args.append(out_prev)
        aliases = {len(args) - 1: 0}

    def body(*refs):
        _hy_conv_kernel(*refs[:n_in], *refs[-4:])

    return pl.pallas_call(
        body,
        out_shape=jax.ShapeDtypeStruct((nb, s, ch), BF16),
        grid=(nct, nb, HY_ORDER),
        in_specs=in_specs,
        out_specs=pl.BlockSpec((None, length, LANE_TILE), lambda c, b, n: (b, row_block, c)),
        scratch_shapes=[pltpu.VMEM((LANE_TILE // 128, length, 128), F32), pltpu.VMEM((m, LANE_TILE), F32),
                        pltpu.VMEM((m, LANE_TILE), F32)],
        input_output_aliases=aliases,
        compiler_params=_params(("parallel", "parallel", "arbitrary")),
        name="hyena_conv_%d" % length,
    )(*args)


def _rms(x, w):
    return x * lax.rsqrt(jnp.mean(x * x, axis=-1, keepdims=True) + EPS) * w


def _merge_mlp_kernel(ys_ref, ym_ref, yh_ref, g0_ref, g1_ref, g2_ref, wb_ref, wo_ref, x_ref, mod_ref,
                      nw_ref, w1_ref, w2_ref, nf_ref, o_ref, *, final):
    m = mod_ref[...]
    acc = None
    for n, (y_ref, g_ref) in enumerate(((ys_ref, g0_ref), (ym_ref, g1_ref), (yh_ref, g2_ref))):
        term = jax.nn.sigmoid(g_ref[...].astype(F32)) * _dot(y_ref[...].astype(BF16), wb_ref[n])
        acc = term if acc is None else acc + term
    x = x_ref[...] + m[2:3] * _dot(acc.astype(BF16), wo_ref[...])
    h = (_rms(x, nw_ref[...]) * (1.0 + m[4:5]) + m[3:4]).astype(BF16)
    a = jnp.maximum(_dot(h, w1_ref[...]), 0.0)
    x = x + m[5:6] * _dot((a * a).astype(BF16), w2_ref[...])
    o_ref[...] = _rms(x, nf_ref[...]) if final else x


def _merge_mlp(ys, ym, yh, pnat, wb, wo, x, mod, norm_w, layer, w1, w2, norm_f, final, n_lat_tiles, n_tiles):
    nb, s, d = x.shape
    bw = ys.shape[-1]
    hidden = w1.shape[1]
    row = pl.BlockSpec((None, TOKEN_TILE, bw), lambda b, i: (b, i, 0))
    gate = lambda n: pl.BlockSpec((None, TOKEN_TILE, d), lambda b, i: (b, i, 1 + n))
    xrow = pl.BlockSpec((None, TOKEN_TILE, d), lambda b, i: (b, i, 0))
    single = dict(pipeline_mode=pl.Buffered(1))
    const = lambda shape: pl.BlockSpec(shape, lambda b, i: (0,) * len(shape), **single)
    return pl.pallas_call(
        functools.partial(_merge_mlp_kernel, final=final),
        out_shape=jax.ShapeDtypeStruct((nb, n_tiles * TOKEN_TILE if final else s, d), F32),
        grid=(nb, n_tiles),
        in_specs=[row, row, row, gate(0), gate(1), gate(2),
                  const((N_BRANCH, bw, d)), const((d, d)),
                  xrow,
                  pl.BlockSpec((None, 6, d), _mod_row(nb, n_lat_tiles)),
                  pl.BlockSpec((None, 1, d), lambda b, i: (layer, 0, 0)),
                  const((d, hidden)), const((hidden, d)),
                  pl.BlockSpec((1, d), lambda b, i: (0, 0))],
        out_specs=xrow,
        input_output_aliases={} if final else {8: 0},
        compiler_params=_params(("parallel", "parallel")),
        name="merge_mlp",
    )(ys, ym, yh, pnat, pnat, pnat, wb, wo, x, mod, norm_w, w1, w2, norm_f)


def _dir_rows(g, nb, s, per_dir):
    g = g[:, :2 * per_dir].reshape(nb, s, 2, per_dir)
    return jnp.transpose(g, (2, 0, 3, 1))


def kernel(x, c, ctx, c_ctx, norm1_w, mod_w, mod_b, w_in, ssd_conv_w, ssd_conv_b, ssd_dt_bias, ssd_a_log,
           ssd_d, ssd_norm_w, ml_conv_w, ml_conv_b, ml_gate_b, ml_norm_w, hy_conv_w, hy_conv_b, hy_ffn_w1,
           hy_ffn_b1, hy_ffn_w2, hy_ffn_b2, hy_ffn_w3, hy_decay, hy_skip, w_branch, w_out, norm2_w,
           mlp_w1, mlp_w2, norm_f_w):
    nb, seq, d = x.shape
    ctx_len = ctx.shape[1]
    depth = w_in.shape[0]
    s = seq + ctx_len
    assert seq % ctx_len == 0 and ctx_len % TOKEN_TILE == 0 and seq % GRID_W == 0
    n_lat_tiles, n_tiles = seq // TOKEN_TILE, s // TOKEN_TILE
    n_lat_ch, n_ch = seq // CHUNK, s // CHUNK
    rows_g = seq // GRID_W

    ssd_heads = ssd_d.shape[-1]
    ssd_inner = ssd_norm_w.shape[-1]
    ssd_hd = ssd_inner // ssd_heads
    ssd_conv_ch = ssd_conv_w.shape[-1]
    ssd_ds = (ssd_conv_ch - ssd_inner) // (2 * SSD_GROUPS)
    ml_heads = ml_gate_b.shape[-1]
    ml_inner = ml_heads * ML_HEADDIM
    hy_inner = hy_skip.shape[-1]
    ssd_cols = ssd_conv_ch + ssd_inner + 2 * ssd_heads
    ml_cols = 4 * ml_inner + 4 * ml_heads
    rec_cols = ssd_cols + ml_cols
    hy_cols = (HY_ORDER + 1) * hy_inner

    o_z = ssd_conv_ch
    o_dt = ssd_conv_ch + ssd_inner
    o_ml = ssd_cols
    o_mlg = ssd_cols + 4 * ml_inner
    o_hy = rec_cols
    o_g = rec_cols + hy_cols

    xa = jnp.concatenate([x, ctx], axis=1)
    rpad = (-(nb + 1)) % 8
    c_all = jnp.concatenate([c, c_ctx[None], jnp.zeros((rpad, d), F32)], axis=0)

    tab_lat = _dft_tables(seq)
    tab_ctx = _dft_tables(ctx_len)
    k_scale = jnp.concatenate([jnp.ones((1, ml_inner), F32),
                               jnp.full((1, ml_inner), ML_HEADDIM ** -0.5, F32)], axis=1)
    ones_row = lambda n: jnp.ones((1, n), F32)

    norm1 = norm1_w[:, None, :]
    norm2 = norm2_w[:, None, :]
    ssd_norm = ssd_norm_w[:, None, :]
    ml_norm = ml_norm_w[:, None, :]
    d_full = jnp.repeat(ssd_d, ssd_hd, axis=-1)[:, None, :]
    w3 = hy_ffn_w3.reshape(depth, hy_ffn_w3.shape[1], HY_ORDER, 2, hy_inner).transpose(0, 2, 3, 1, 4)
    decay = hy_decay[:, :, :, None, :]
    skip = hy_skip[:, :, None, :]
    hb1 = hy_ffn_b1[:, None, :]
    hb2 = hy_ffn_b2[:, None, :]

    for l in range(depth):
        need_ctx = l < depth - 1
        used_tiles = n_tiles if need_ctx else n_lat_tiles
        mod = _mod_vectors(c_all, mod_w, mod_b[:, None, :], l).reshape(-1, 6, d)

        wl = w_in[l].astype(BF16)
        w_nat = jnp.concatenate([wl[:, o_z:o_dt], wl[:, o_g:], wl[:, o_hy:o_g]], axis=1)
        w_xbc = wl[:, :o_z]
        w_ml = wl[:, o_ml:o_mlg]
        gpad = lambda w: jnp.pad(w, ((0, 0), (0, 128 - w.shape[1])))
        hn, hn_cm = _normmod(xa, norm1, l, mod, 0, rows_g, n_lat_tiles, n_tiles)
        hn2 = hn.reshape(nb * s, d)
        hn_cm2 = hn_cm.reshape(nb * s, d)
        pnat = _matmul(hn2, w_nat, BF16, "proj_nat").reshape(nb, s, -1)
        pml = _matmul(hn_cm2, w_ml, BF16, "proj_ml").reshape(nb, s, -1)
        p_dt = _matmul(hn2, gpad(wl[:, o_dt:o_ml]), F32, "proj_dt")
        p_mlg = _matmul(hn_cm2, gpad(wl[:, o_mlg:o_hy]), F32, "proj_mlg")
        pxbc = _matmul(hn2, w_xbc, BF16, "proj_xbc").reshape(nb, s, -1)
        c_g = ssd_inner
        c_hy = c_g + N_BRANCH * d

        xbc = _dwconv(pxbc, 0, ssd_conv_ch, ssd_conv_w[l], ssd_conv_b[l][None], ones_row(ssd_conv_ch),
                      True, n_lat_tiles, n_tiles, "ssd_conv")
        dt_r = _dir_rows(p_dt, nb, s, ssd_heads)
        y_f, y_b = _ssd_scan(xbc, dt_r, ssd_dt_bias[l][:, :, None], ssd_a_log[l][:, :, None],
                             ssd_heads, ssd_hd, ssd_ds, n_lat_ch, n_ch)
        ys = _ssd_out(y_f, y_b, xbc, pnat, d_full, ssd_norm, l, used_tiles)

        qk = _dwconv(pml, 0, 2 * ml_inner, ml_conv_w[l], ml_conv_b[l][None], k_scale,
                     True, n_lat_tiles, n_tiles, "ml_conv")
        g_r = _dir_rows(p_mlg, nb, s, 2 * ml_heads)
        gate_b = ml_gate_b[l]
        c_f, c_b = _ml_scan(qk, pml, g_r[:, :, :ml_heads], g_r[:, :, ml_heads:],
                            gate_b[:, 0, :, None], gate_b[:, 1, :, None], ml_heads, n_lat_ch, n_ch)
        ym = _ml_out(c_f, c_b, pml, ml_norm, l, ml_heads, rows_g, n_lat_tiles, used_tiles)

        hcw, hcb = hy_conv_w[l], hy_conv_b[l][None]
        fl = _hy_filters(seq, l, hy_ffn_w1, hb1, hy_ffn_w2, hb2, w3, decay, tab_lat)
        rows = s if need_ctx else seq
        yh = _hy_conv(pnat, c_hy, hy_inner, hcw, hcb, None, rows, 0, seq, l, *fl, skip, tab_lat)
        if need_ctx:
            fc = _hy_filters(ctx_len, l, hy_ffn_w1, hb1, hy_ffn_w2, hb2, w3, decay, tab_ctx)
            yh = _hy_conv(pnat, c_hy, hy_inner, hcw, hcb, yh, rows, seq // ctx_len, ctx_len, l, *fc, skip,
                          tab_ctx)

        xa = _merge_mlp(ys, ym, yh, pnat, w_branch[l].astype(BF16), w_out[l].astype(BF16), xa, mod, norm2, l,
                        mlp_w1[l].astype(BF16), mlp_w2[l].astype(BF16), norm_f_w[None], not need_ctx,
                        n_lat_tiles, used_tiles)

    return xa
```

```python
import functools
import math

import jax
import jax.numpy as jnp
import numpy as np
from jax import lax
from jax.experimental import pallas as pl
from jax.experimental.pallas import tpu as pltpu

F32 = jnp.float32
BF16 = jnp.bfloat16
HIGHEST = lax.Precision.HIGHEST

GRID_W = 64
CHUNK = 128
EPS = 1e-6
SSD_GROUPS = 2
SSD_CONV = 5
ML_HEADDIM = 128
ML_CONV = 5
HY_ORDER = 2
HY_SHORT = 3
HY_BANDS = 16
N_BRANCH = 3

TOKEN_TILE = 256
LANE_TILE = 256
HALO = 16
VMEM_LIMIT = 56 * 1024 * 1024

_hdot = functools.partial(jnp.dot, precision=HIGHEST, preferred_element_type=F32)
_dot = functools.partial(jnp.dot, preferred_element_type=F32)


def _params(sem, vmem=None):
    return pltpu.CompilerParams(dimension_semantics=sem, vmem_limit_bytes=vmem or VMEM_LIMIT)


def _softplus(x):
    return jnp.maximum(x, 0.0) + jnp.log(1.0 + jnp.exp(-jnp.abs(x)))


def _silu(x):
    return x * jax.nn.sigmoid(x)


def _mod_kernel(c_ref, w_ref, b_ref, o_ref):
    o_ref[...] = _hdot(_silu(c_ref[...]), w_ref[...]) + b_ref[...]


def _mod_vectors(c_all, mod_w, mod_b, layer):
    r, d = c_all.shape
    n = mod_w.shape[-1]
    tn = n // 6
    return pl.pallas_call(
        _mod_kernel,
        out_shape=jax.ShapeDtypeStruct((r, n), F32),
        grid=(n // tn,),
        in_specs=[pl.BlockSpec((r, d), lambda j: (0, 0)),
                  pl.BlockSpec((None, d, tn), lambda j: (layer, 0, j)),
                  pl.BlockSpec((None, 1, tn), lambda j: (layer, 0, j))],
        out_specs=pl.BlockSpec((r, tn), lambda j: (0, j)),
        compiler_params=_params(("parallel",)),
        name="mod_vectors",
    )(c_all, mod_w, mod_b)


def _grid_perm(rows_g):
    wpt = TOKEN_TILE // rows_g
    src = np.arange(TOKEN_TILE).reshape(rows_g, wpt).T.reshape(-1)
    p = np.zeros((TOKEN_TILE, TOKEN_TILE), np.float32)
    p[np.arange(TOKEN_TILE), src] = 1.0
    return p


def _normmod_kernel(x_ref, xc_ref, x4_ref, perm_ref, nw_ref, mod_ref, o_ref, ocm_ref, *, si, n_lat_tiles):
    i = pl.program_id(1)
    m = mod_ref[...]

    def normed(x):
        h = x * lax.rsqrt(jnp.mean(x * x, axis=-1, keepdims=True) + EPS) * nw_ref[...]
        return (h * (1.0 + m[si + 1:si + 2]) + m[si:si + 1]).astype(BF16)

    h = normed(x_ref[...] if xc_ref is None else jnp.where(i < n_lat_tiles, x_ref[...], xc_ref[...]))
    o_ref[...] = h

    @pl.when(i < n_lat_tiles)
    def _():
        x4 = x4_ref[...]
        ocm_ref[...] = _dot(perm_ref[...], normed(x4.reshape(TOKEN_TILE, x4.shape[-1]))).astype(BF16)

    @pl.when(i >= n_lat_tiles)
    def _():
        ocm_ref[...] = h


def _mod_row(nb, n_lat_tiles):
    return lambda b, i: (jnp.where(i < n_lat_tiles, b, nb), 0, 0)


def _normmod(x, x_ctx, norm_w, layer, mod, si, rows_g, n_lat_tiles, n_tiles):
    nb, rows, d = x.shape
    s = n_tiles * TOKEN_TILE
    wpt = TOKEN_TILE // rows_g
    assert wpt % 8 == 0 and GRID_W % wpt == 0 and rows % GRID_W == 0
    x4 = x.reshape(nb, rows // GRID_W, GRID_W, d)
    tile = pl.BlockSpec((None, TOKEN_TILE, d), lambda b, i: (b, i, 0))
    if x_ctx is None:
        srcs, src_specs = [x], [tile]
    else:
        srcs = [x, x_ctx]
        src_specs = [pl.BlockSpec((None, TOKEN_TILE, d), lambda b, i: (b, jnp.minimum(i, n_lat_tiles - 1), 0)),
                     pl.BlockSpec((None, TOKEN_TILE, d), lambda b, i: (b, jnp.maximum(i - n_lat_tiles, 0), 0))]

    def body(*refs):
        xc = None if x_ctx is None else refs[1]
        _normmod_kernel(refs[0], xc, *refs[len(srcs):], si=si, n_lat_tiles=n_lat_tiles)

    return pl.pallas_call(
        body,
        out_shape=(jax.ShapeDtypeStruct((nb, s, d), BF16),) * 2,
        grid=(nb, n_tiles),
        in_specs=src_specs + [
            pl.BlockSpec((None, rows_g, wpt, d), lambda b, i: (b, 0, jnp.minimum(i, n_lat_tiles - 1), 0)),
            pl.BlockSpec((TOKEN_TILE, TOKEN_TILE), lambda b, i: (0, 0)),
            pl.BlockSpec((None, 1, d), lambda b, i: (layer, 0, 0)),
            pl.BlockSpec((None, 6, d), _mod_row(nb, n_lat_tiles))],
        out_specs=(tile, tile),
        compiler_params=_params(("parallel", "parallel")),
        name="normmod",
    )(*srcs, x4, jnp.asarray(_grid_perm(rows_g), BF16), norm_w, mod)


def _mm_kernel(a_ref, w_ref, o_ref):
    o_ref[...] = _dot(a_ref[...], w_ref[...]).astype(o_ref.dtype)


def _pick(n, cands):
    for c in cands:
        if n % c == 0:
            return c
    return n


def _mm2_kernel(a_ref, w_ref, wg_ref, o_ref, og_ref):
    o_ref[...] = _dot(a_ref[...], w_ref[...]).astype(o_ref.dtype)

    @pl.when(pl.program_id(1) == 0)
    def _():
        og_ref[...] = _dot(a_ref[...], wg_ref[...])


def _matmul(a, w, out_dtype, name, w_gate=None):
    t, k = a.shape
    n = w.shape[1]
    tm = _pick(t, (1024, 768, 512, 256))
    tn = _pick(n, (1024, 512, 256, 128))
    in_specs = [pl.BlockSpec((tm, k), lambda i, j: (i, 0)),
                pl.BlockSpec((k, tn), lambda i, j: (0, j))]
    out_spec = pl.BlockSpec((tm, tn), lambda i, j: (i, j))
    out_shape = jax.ShapeDtypeStruct((t, n), out_dtype)
    if w_gate is None:
        return pl.pallas_call(
            _mm_kernel, out_shape=out_shape, grid=(t // tm, n // tn), in_specs=in_specs, out_specs=out_spec,
            compiler_params=_params(("parallel", "parallel")), name=name,
        )(a, w)
    ng = w_gate.shape[1]
    return pl.pallas_call(
        _mm2_kernel,
        out_shape=(out_shape, jax.ShapeDtypeStruct((t, ng), F32)),
        grid=(t // tm, n // tn),
        in_specs=in_specs + [pl.BlockSpec((k, ng), lambda i, j: (0, 0))],
        out_specs=(out_spec, pl.BlockSpec((tm, ng), lambda i, j: (i, 0))),
        compiler_params=_params(("parallel", "arbitrary")),
        name=name,
    )(a, w, w_gate)


def _dwconv_kernel(u_ref, p_ref, n_ref, w_ref, b_ref, s_ref, o_ref, *, taps, act, bounds):
    i = pl.program_id(1)
    tm = u_ref.shape[0]
    pad = taps // 2
    lv, rv = jnp.float32(1.0), jnp.float32(1.0)
    for e in bounds:
        lv = jnp.where(i == e, 0.0, lv)
        rv = jnp.where(i == e - 1, 0.0, rv)
    ext = jnp.concatenate([p_ref[...].astype(F32) * lv, u_ref[...].astype(F32),
                           n_ref[...].astype(F32) * rv], axis=0)
    rows = ext.shape[0]
    w = w_ref[...]
    acc = jnp.zeros((tm, u_ref.shape[1]), F32) + b_ref[...]
    for j in range(taps):
        sh = (pad - j) % rows
        shifted = ext if sh == 0 else pltpu.roll(ext, sh, 0)
        acc = acc + shifted[HALO:HALO + tm] * w[j:j + 1]
    if act:
        acc = _silu(acc)
    o_ref[...] = (acc * s_ref[...]).astype(o_ref.dtype)


def _dwconv(u, col0, ncols, w, b, scale, act, n_lat_tiles, n_tiles, name):
    nb, s, _ = u.shape
    taps = w.shape[0]
    tc = _pick(ncols, (2048, 1536, 1024, 512, 256))
    assert col0 % tc == 0
    c0 = col0 // tc
    hb = TOKEN_TILE // HALO
    last = s // HALO - 1
    bounds = (0, n_lat_tiles, n_tiles)
    return pl.pallas_call(
        functools.partial(_dwconv_kernel, taps=taps, act=act, bounds=bounds),
        out_shape=jax.ShapeDtypeStruct((nb, s, ncols), BF16),
        grid=(nb, n_tiles, ncols // tc),
        in_specs=[pl.BlockSpec((None, TOKEN_TILE, tc), lambda bb, i, c: (bb, i, c0 + c)),
                  pl.BlockSpec((None, HALO, tc), lambda bb, i, c: (bb, jnp.maximum(i * hb - 1, 0), c0 + c)),
                  pl.BlockSpec((None, HALO, tc), lambda bb, i, c: (bb, jnp.minimum((i + 1) * hb, last), c0 + c)),
                  pl.BlockSpec((taps, tc), lambda bb, i, c: (0, c)),
                  pl.BlockSpec((1, tc), lambda bb, i, c: (0, c)),
                  pl.BlockSpec((1, tc), lambda bb, i, c: (0, c))],
        out_specs=pl.BlockSpec((None, TOKEN_TILE, tc), lambda bb, i, c: (bb, i, c)),
        compiler_params=_params(("parallel", "parallel", "parallel")),
        name=name,
    )(u, u, u, w, b, scale)


def _chunk_index(n_lat_ch, n_ch):
    return lambda d, j: (j + n_lat_ch) % n_ch if d == 0 else n_ch - 1 - j


def _scan_masks(d, t):
    ii = lax.broadcasted_iota(jnp.int32, (t, t), 0)
    jj = lax.broadcasted_iota(jnp.int32, (t, t), 1)
    mask = jj <= ii if d == 0 else jj >= ii
    tri_t = jnp.where(ii <= jj if d == 0 else ii >= jj, 1.0, 0.0).astype(F32)
    return mask, tri_t


def _gate_rows(la_r, lw_r, tri_t, m_prev):
    cum_r = _hdot(la_r, tri_t)
    last = jnp.sum(la_r, axis=1, keepdims=True)
    g_r = last - cum_r + lw_r
    m_loc = jnp.max(g_r, axis=1, keepdims=True)
    e_r = jnp.exp(g_r - m_loc)
    m_new = jnp.maximum(last + m_prev, m_loc)
    s_old = jnp.exp(last + m_prev - m_new)
    s_new = jnp.exp(m_loc - m_new)
    return cum_r, cum_r - lw_r, e_r, m_new, s_old, s_new


def _head_probs(cum_row, crow_row, m_prev_h, mask, qk):
    t = qk.shape[0]
    colb = jnp.broadcast_to(cum_row, (t, t)).T
    dlog = jnp.where(mask, colb - crow_row, -jnp.inf)
    inter = colb[:, 0:1] + m_prev_h
    m_row = jnp.maximum(inter, jnp.max(dlog, axis=1, keepdims=True))
    p = jnp.exp(dlog - m_row) * qk
    return p.astype(BF16), jnp.exp(inter - m_row), m_row


def _ssd_scan_kernel(*refs, heads, groups, hd, ds):
    j = pl.program_id(1)
    ins, outs, states = refs[:12], refs[12:14], refs[14:16]

    @pl.when(j == 0)
    def _():
        for st_ref in states:
            st_ref[...] = jnp.zeros_like(st_ref)

    for d in range(2):
        _ssd_chunk(d, *ins[6 * d:6 * d + 6], outs[d], states[d], heads=heads, groups=groups, hd=hd, ds=ds)


def _ssd_chunk(d, xs_ref, b_ref, c_ref, dt_ref, bias_ref, alog_ref, o_ref, st_ref, *, heads, groups, hd, ds):
    t = xs_ref.shape[0]
    hpg = heads // groups
    pw = 2 * hd
    mask, tri_t = _scan_masks(d, t)
    dt = _softplus(dt_ref[...] + bias_ref[...])
    la_r = -dt * jnp.exp(alog_ref[...])
    cum_r = _hdot(la_r, tri_t)
    last = jnp.sum(la_r, axis=1, keepdims=True)
    crow = cum_r - jnp.log(dt)
    e_r = jnp.exp(last - crow)
    e_last = jnp.exp(last)
    lo = lax.broadcasted_iota(jnp.int32, (1, pw), 1) < hd

    for g in range(groups):
        q = c_ref[:, g * ds:(g + 1) * ds]
        k_t = b_ref[:, g * ds:(g + 1) * ds].astype(F32).T
        qk = _dot(q, k_t.astype(BF16))
        w0 = g * hpg * hd
        qs = _dot(q, st_ref[:, w0:w0 + hpg * hd].astype(BF16))
        for i in range(hpg // 2):
            h0 = g * hpg + 2 * i
            c0 = h0 * hd
            vp = xs_ref[:, c0:c0 + pw]
            zero = jnp.zeros_like(vp)
            v_bd = jnp.concatenate([jnp.where(lo, vp, zero), jnp.where(lo, zero, vp)], axis=0)
            probs, carry, kte = [], [], []
            for h in (h0, h0 + 1):
                colb = jnp.broadcast_to(cum_r[h:h + 1], (t, t)).T
                dlog = jnp.where(mask, colb - crow[h:h + 1], -jnp.inf)
                probs.append((jnp.exp(dlog) * qk).astype(BF16))
                carry.append(jnp.exp(colb))
                kte.append((k_t * e_r[h:h + 1]).astype(BF16))
            lhs = jnp.concatenate([jnp.concatenate(probs, axis=1), jnp.concatenate(kte, axis=1)], axis=0)
            res = _dot(lhs, v_bd)
            o_ref[:, c0:c0 + pw] = res[:t] + qs[:, 2 * i * hd:2 * i * hd + pw] * jnp.where(lo, carry[0], carry[1])
            decay = jnp.where(lo, e_last[h0:h0 + 1], e_last[h0 + 1:h0 + 2])
            st_ref[:, c0:c0 + pw] = decay * st_ref[:, c0:c0 + pw] + res[t:]


def _ssd_scan(xbc, dt_r, bias_c, alog_c, heads, hd, ds, n_lat_ch, n_ch):
    nb, s, _ = xbc.shape
    inner = heads * hd
    gn = SSD_GROUPS * ds
    assert 2 * hd == CHUNK and (heads // SSD_GROUPS) % 2 == 0 and inner % gn == 0
    cidx = _chunk_index(n_lat_ch, n_ch)

    def specs(d):
        return [pl.BlockSpec((None, CHUNK, inner), lambda b, j: (b, cidx(d, j), 0)),
                pl.BlockSpec((None, CHUNK, gn), lambda b, j: (b, cidx(d, j), inner // gn)),
                pl.BlockSpec((None, CHUNK, gn), lambda b, j: (b, cidx(d, j), inner // gn + 1)),
                pl.BlockSpec((None, None, heads, CHUNK), lambda b, j: (d, b, 0, cidx(d, j))),
                pl.BlockSpec((None, heads, 1), lambda b, j: (d, 0, 0)),
                pl.BlockSpec((None, heads, 1), lambda b, j: (d, 0, 0))]

    out = lambda d: pl.BlockSpec((None, CHUNK, inner), lambda b, j: (b, cidx(d, j), 0))
    args = (xbc, xbc, xbc, dt_r, bias_c, alog_c)
    return pl.pallas_call(
        functools.partial(_ssd_scan_kernel, heads=heads, groups=SSD_GROUPS, hd=hd, ds=ds),
        out_shape=(jax.ShapeDtypeStruct((nb, s, inner), F32),) * 2,
        grid=(nb, n_ch),
        in_specs=specs(0) + specs(1),
        out_specs=(out(0), out(1)),
        scratch_shapes=[pltpu.VMEM((ds, inner), F32)] * 2,
        compiler_params=_params(("parallel", "arbitrary")),
        name="ssd_scan",
    )(*args, *args)


def _ml_scan_kernel(*refs, heads):
    j = pl.program_id(1)
    ins, outs, states = refs[:14], refs[14:16], refs[16:20]

    @pl.when(j == 0)
    def _():
        for ref in states:
            ref[...] = jnp.zeros_like(ref)

    for d in range(2):
        _ml_chunk(d, *ins[7 * d:7 * d + 7], outs[d], *states[2 * d:2 * d + 2], heads=heads)


def _ml_chunk(d, q_ref, k_ref, v_ref, gi_ref, gf_ref, bi_ref, bf_ref, o_ref, st_ref, m_ref, *, heads):
    t = q_ref.shape[0]
    dh = ML_HEADDIM
    mask, tri_t = _scan_masks(d, t)
    lw_r = gi_ref[...] + bi_ref[...]
    f = gf_ref[...] + bf_ref[...]
    la_r = jnp.minimum(f, 0.0) - jnp.log(1.0 + jnp.exp(-jnp.abs(f)))
    m_prev = m_ref[:, 0:1]
    cum_r, crow, e_r, m_new, s_old, s_new = _gate_rows(la_r, lw_r, tri_t, m_prev)
    ones = jnp.ones((t, dh), BF16)

    for h in range(heads):
        c0 = h * dh
        q = q_ref[:, c0:c0 + dh]
        k_t = k_ref[:, c0:c0 + dh].astype(F32).T
        v_aug = jnp.concatenate([v_ref[:, c0:c0 + dh], ones], axis=1)
        qk = _dot(q, k_t.astype(BF16))
        s0 = 2 * c0
        qs = _dot(q, st_ref[:, s0:s0 + 2 * dh].astype(BF16))
        p, cf, m_row = _head_probs(cum_r[h:h + 1], crow[h:h + 1], m_prev[h:h + 1], mask, qk)
        res = _dot(jnp.concatenate([p, (k_t * e_r[h:h + 1]).astype(BF16)], axis=0), v_aug)
        y = res[:t] + qs * cf
        o_ref[:, c0:c0 + dh] = y[:, :dh] / jnp.maximum(jnp.abs(y[:, dh:]), jnp.exp(-m_row))
        st_ref[:, s0:s0 + 2 * dh] = s_old[h:h + 1] * st_ref[:, s0:s0 + 2 * dh] + s_new[h:h + 1] * res[t:]
    m_ref[...] = jnp.broadcast_to(m_new, m_ref.shape)


def _ml_scan(qk, pml, gi_r, gf_r, bi_c, bf_c, heads, n_lat_ch, n_ch):
    nb, s, _ = qk.shape
    inner = heads * ML_HEADDIM
    cidx = _chunk_index(n_lat_ch, n_ch)

    def specs(d):
        gspec = pl.BlockSpec((None, None, heads, CHUNK), lambda b, j: (d, b, 0, cidx(d, j)))
        bspec = pl.BlockSpec((None, heads, 1), lambda b, j: (d, 0, 0))
        col = lambda c: pl.BlockSpec((None, CHUNK, inner), lambda b, j: (b, cidx(d, j), c))
        return [col(0), col(1), col(2), gspec, gspec, bspec, bspec]

    out = lambda d: pl.BlockSpec((None, CHUNK, inner), lambda b, j: (b, cidx(d, j), 0))
    args = (qk, qk, pml, gi_r, gf_r, bi_c, bf_c)
    return pl.pallas_call(
        functools.partial(_ml_scan_kernel, heads=heads),
        out_shape=(jax.ShapeDtypeStruct((nb, s, inner), F32),) * 2,
        grid=(nb, n_ch),
        in_specs=specs(0) + specs(1),
        out_specs=(out(0), out(1)),
        scratch_shapes=[pltpu.VMEM((ML_HEADDIM, 2 * inner), F32), pltpu.VMEM((heads, 128), F32)] * 2,
        compiler_params=_params(("parallel", "arbitrary")),
        name="ml_scan",
    )(*args, *args)


def _ml_out_kernel(cf_ref, cb_ref, o_in_ref, nw_ref, perm_ref, *rest, heads, permute):
    o_ref = rest[-1]
    dh = ML_HEADDIM
    parts = []
    for h in range(heads):
        sl = slice(h * dh, (h + 1) * dh)
        c = cf_ref[:, sl] + cb_ref[:, sl]
        c = c * lax.rsqrt(jnp.mean(c * c, axis=-1, keepdims=True) + EPS) * nw_ref[:, sl]
        parts.append((jax.nn.sigmoid(o_in_ref[:, sl].astype(F32)) * c).astype(BF16))
    y = jnp.concatenate(parts, axis=1)
    y = _dot(perm_ref[...], y) if permute else y.astype(F32)
    o_ref[...] = y.reshape(o_ref.shape)


def _ml_out(cf, cb, pml, norm_w, layer, heads, rows_g, n_lat_tiles, n_tiles):
    nb, _, inner = cf.shape
    wpt = TOKEN_TILE // rows_g
    rpt = TOKEN_TILE // GRID_W
    perm_t = jnp.asarray(_grid_perm(rows_g).T, BF16)
    out_shape = jax.ShapeDtypeStruct((nb, n_tiles * rpt, GRID_W, inner), F32)

    def call(tile0, tiles, out_spec, permute, prev):
        row = pl.BlockSpec((None, TOKEN_TILE, inner), lambda b, i: (b, tile0 + i, 0))
        in_specs = [row, row,
                    pl.BlockSpec((None, TOKEN_TILE, inner), lambda b, i: (b, tile0 + i, 3)),
                    pl.BlockSpec((None, 1, inner), lambda b, i: (layer, 0, 0)),
                    pl.BlockSpec((TOKEN_TILE, TOKEN_TILE), lambda b, i: (0, 0))]
        args = [cf, cb, pml, norm_w, perm_t]
        if prev is not None:
            in_specs.append(pl.BlockSpec(memory_space=pl.ANY))
            args.append(prev)
        return pl.pallas_call(
            functools.partial(_ml_out_kernel, heads=heads, permute=permute),
            out_shape=out_shape,
            grid=(nb, tiles),
            in_specs=in_specs,
            out_specs=out_spec,
            input_output_aliases={} if prev is None else {len(args) - 1: 0},
            compiler_params=_params(("parallel", "parallel")),
            name="ml_out_lat" if permute else "ml_out_ctx",
        )(*args)

    y = call(0, n_lat_tiles, pl.BlockSpec((None, rows_g, wpt, inner), lambda b, i: (b, 0, i, 0)), True, None)
    if n_tiles > n_lat_tiles:
        y = call(n_lat_tiles, n_tiles - n_lat_tiles,
                 pl.BlockSpec((None, rpt, GRID_W, inner), lambda b, i: (b, n_lat_tiles + i, 0, 0)), False, y)
    return y.reshape(nb, n_tiles * TOKEN_TILE, inner)


def _dft_tables(length):
    m = length // 2
    k = np.arange(m, dtype=np.int64)
    ang = ((k[:, None] * k[None, :]) % (2 * m)).astype(np.float64) * (np.pi / m)
    alt8 = np.broadcast_to(np.where(k % 2 == 0, 1.0, -1.0)[None, :], (8, m))
    tw = k.astype(np.float64)[:, None] * (np.pi / length) * np.ones((1, LANE_TILE))
    return (jnp.asarray(np.cos(ang), BF16), jnp.asarray(np.sin(ang), BF16), jnp.asarray(alt8, BF16),
            jnp.asarray(np.cos(tw), F32), jnp.asarray(np.sin(tw), F32))


def _hy_feats(length):
    t = jnp.arange(length, dtype=F32)
    t_norm = t / length
    bands = jnp.linspace(1e-4, HY_BANDS - 1, HY_BANDS, dtype=F32)
    ang = (2.0 * math.pi / length) * t[:, None] * bands[None, :]
    feats = jnp.concatenate([t_norm[:, None], jnp.cos(ang), -jnp.sin(ang)], axis=-1)
    return feats[0::2], feats[1::2], t_norm[0::2, None], t_norm[1::2, None]


def _split_spectrum(ae, be, ao, bo, twc, tws):
    tr = twc * ao - tws * bo
    tm = twc * bo + tws * ao
    return (ae + tr, ae - tr), (be + tm, tm - be)


def _hy_filter_kernel(fe_ref, fo_ref, tne_ref, tno_ref, w1_ref, b1_ref, w2_ref, b2_ref, w3f_ref, w3b_ref,
                      df_ref, db_ref, cm_ref, sm_ref, alt_ref, twc_ref, tws_ref, ha_ref, hb_ref, hm_ref,
                      hide_ref, hido_ref):
    m = cm_ref.shape[0]

    @pl.when((pl.program_id(0) == 0) & (pl.program_id(1) == 0))
    def _():
        for f_ref, h_ref in ((fe_ref, hide_ref), (fo_ref, hido_ref)):
            hid = jnp.sin(_hdot(f_ref[...], w1_ref[...]) + b1_ref[...])
            h_ref[...] = jnp.sin(_hdot(hid, w2_ref[...]) + b2_ref[...])

    def taps(hid, tn):
        h_f = _hdot(hid, w3f_ref[...]) * jnp.exp(-tn * jnp.abs(df_ref[...]))
        h_b = _hdot(hid, w3b_ref[...]) * jnp.exp(-tn * jnp.abs(db_ref[...]))
        return (h_f + h_b).astype(BF16), (h_f - h_b).astype(BF16)

    sum_e, dif_e = taps(hide_ref[...], tne_ref[...])
    sum_o, dif_o = taps(hido_ref[...], tno_ref[...])
    twc, tws = twc_ref[...], tws_ref[...]
    cm, sm = cm_ref[...], sm_ref[...]
    (ha_lo, ha_hi), _ = _split_spectrum(_dot(cm, sum_e), 0.0, _dot(cm, sum_o), _dot(sm, sum_o), twc, tws)
    _, (hb_lo, hb_hi) = _split_spectrum(0.0, _dot(sm, dif_e), _dot(cm, dif_o), _dot(sm, dif_o), twc, tws)
    ha_ref[0:m, :] = ha_lo
    ha_ref[m:2 * m, :] = ha_hi
    hb_ref[0:m, :] = hb_lo
    hb_ref[m:2 * m, :] = hb_hi
    row = lax.broadcasted_iota(jnp.int32, hm_ref.shape, 0)
    hm_ref[...] = jnp.where(row == 0, _dot(alt_ref[...], sum_e), _dot(alt_ref[...], dif_o))


def _hy_filters(length, layer, w1, b1, w2, b2, w3, decay, tables):
    cm, sm, alt8, twc, tws = tables
    m = length // 2
    feats = _hy_feats(length)
    nfeat, nf = w1.shape[-2:]
    ch = decay.shape[-1]
    nct = ch // LANE_TILE
    const = lambda shape: pl.BlockSpec(shape, lambda n, c: (0,) * len(shape))
    w3spec = lambda dr: pl.BlockSpec((None, None, None, nf, LANE_TILE), lambda n, c: (layer, n, dr, 0, c))
    dspec = lambda dr: pl.BlockSpec((None, None, None, 1, LANE_TILE), lambda n, c: (layer, n, dr, 0, c))
    lay = lambda a, b: pl.BlockSpec((None, a, b), lambda n, c: (layer, 0, 0))
    out = lambda rows: pl.BlockSpec((None, rows, LANE_TILE), lambda n, c: (n, 0, c))
    return pl.pallas_call(
        _hy_filter_kernel,
        out_shape=(jax.ShapeDtypeStruct((HY_ORDER, length, ch), F32),
                   jax.ShapeDtypeStruct((HY_ORDER, length, ch), F32),
                   jax.ShapeDtypeStruct((HY_ORDER, 8, ch), F32)),
        grid=(HY_ORDER, nct),
        in_specs=[const((m, nfeat)), const((m, nfeat)), const((m, 1)), const((m, 1)),
                  lay(nfeat, nf), lay(1, nf), lay(nf, nf), lay(1, nf),
                  w3spec(0), w3spec(1), dspec(0), dspec(1),
                  const((m, m)), const((m, m)), const((8, m)),
                  const((m, LANE_TILE)), const((m, LANE_TILE))],
        out_specs=(out(length), out(length), out(8)),
        scratch_shapes=[pltpu.VMEM((m, nf), F32)] * 2,
        compiler_params=_params(("arbitrary", "arbitrary")),
        name="hyena_filters_%d" % length,
    )(*feats, w1, b1, w2, b2, w3, w3, decay, decay, cm, sm, alt8, twc, tws)


def _short_conv(u, w, b):
    length = u.shape[0]
    taps = w.shape[0]
    pad = taps // 2
    row = lax.broadcasted_iota(jnp.int32, u.shape, 0)
    acc = b + u * w[pad:pad + 1]
    for j in range(taps):
        off = j - pad
        if off != 0:
            moved = pltpu.roll(u, (-off) % length, 0)
            valid = (row + off >= 0) & (row + off < length)
            acc = acc + jnp.where(valid, moved, 0.0) * w[j:j + 1]
    return acc


def _hy_conv_kernel(z_ref, g_ref, wz_ref, bz_ref, wg_ref, bg_ref, ha_ref, hb_ref, hm_ref, skip_ref,
                    cm_ref, sm_ref, alt_ref, twc_ref, tws_ref, o_ref, tmp_ref, ze_ref, zo_ref):
    n = pl.program_id(2)
    m, width = ze_ref.shape
    lanes = tmp_ref.shape[-1]
    slabs = range(width // lanes)

    def put(val):
        for h in slabs:
            tmp_ref[h] = val[:, h * lanes:(h + 1) * lanes]

    def take(first):
        return jnp.concatenate([tmp_ref[h, pl.ds(first, m, stride=2), :] for h in slabs], axis=1)

    @pl.when(n == 0)
    def _():
        put(_short_conv(z_ref[...].astype(F32), wz_ref[...], bz_ref[...]))
        ze_ref[...] = take(0)
        zo_ref[...] = take(1)

    put(_short_conv(g_ref[...].astype(F32), wg_ref[...], bg_ref[...]))
    ge, go = take(0), take(1)
    ze, zo = ze_ref[...], zo_ref[...]
    zz = jnp.concatenate([ze.astype(BF16), zo.astype(BF16)], axis=1)
    a = _dot(cm_ref[...], zz)
    b = _dot(sm_ref[...], zz)
    mid = _dot(alt_ref[...], zz)[0:1]
    twc, tws = twc_ref[...], tws_ref[...]
    (a_lo, a_hi), (b_lo, b_hi) = _split_spectrum(a[:, :width], b[:, :width], a[:, width:], b[:, width:], twc, tws)

    ha_lo, ha_hi, hb_lo, hb_hi = ha_ref[0:m, :], ha_ref[m:2 * m, :], hb_ref[0:m, :], hb_ref[m:2 * m, :]
    yr_lo, ym_lo = a_lo * ha_lo - b_lo * hb_lo, a_lo * hb_lo + b_lo * ha_lo
    yr_hi, ym_hi = a_hi * ha_hi - b_hi * hb_hi, a_hi * hb_hi + b_hi * ha_hi
    ha_m, hb_m = hm_ref[0:1, :], hm_ref[1:2, :]
    yr_m = mid[:, :width] * ha_m - mid[:, width:] * hb_m
    ym_m = mid[:, :width] * hb_m + mid[:, width:] * ha_m

    row = lax.broadcasted_iota(jnp.int32, (m, width), 0)
    half0 = jnp.where(row == 0, 0.5, 1.0)
    qr, qm = yr_lo - yr_hi, ym_lo + ym_hi
    pr = jnp.concatenate([((yr_lo + yr_hi) * half0).astype(BF16),
                          ((qr * twc + qm * tws) * half0).astype(BF16)], axis=1)
    pm = jnp.concatenate([(ym_lo - ym_hi).astype(BF16), (qm * twc - qr * tws).astype(BF16)], axis=1)
    y = _dot(cm_ref[...], pr) + _dot(sm_ref[...], pm)
    alt = jnp.where((row & 1) == 0, 1.0, -1.0)
    scale = 1.0 / (2 * m)
    skip = skip_ref[...]
    ze_new = ge * ((y[:, :width] + alt * yr_m) * scale + skip * ze)
    zo_new = go * ((y[:, width:] + alt * ym_m) * scale + skip * zo)

    @pl.when(n < HY_ORDER - 1)
    def _():
        ze_ref[...] = ze_new
        zo_ref[...] = zo_new

    @pl.when(n == HY_ORDER - 1)
    def _():
        for h in slabs:
            tmp_ref[h, pl.ds(0, m, stride=2), :] = ze_new[:, h * lanes:(h + 1) * lanes]
            tmp_ref[h, pl.ds(1, m, stride=2), :] = zo_new[:, h * lanes:(h + 1) * lanes]
            o_ref[:, h * lanes:(h + 1) * lanes] = tmp_ref[h].astype(o_ref.dtype)


def _hy_conv(u, col0, ch, conv_w, conv_b, out_prev, out_rows, row_block, length, layer, ha, hb, hm, skip,
             tables):
    cm, sm, alt8, twc, tws = tables
    nb = u.shape[0]
    s = out_rows
    m = length // 2
    nct = ch // LANE_TILE
    assert col0 % LANE_TILE == 0
    cb = col0 // LANE_TILE
    taps = conv_w.shape[0]
    const = lambda shape: pl.BlockSpec(shape, lambda c, b, n: (0,) * len(shape))
    hspec = lambda rows: pl.BlockSpec((None, rows, LANE_TILE), lambda c, b, n: (n, 0, c))
    zcol = lambda c, n: c
    gcol = lambda c, n: (1 + n) * nct + c
    wspec = lambda rows, col: pl.BlockSpec((rows, LANE_TILE), lambda c, b, n: (0, col(c, n)))
    in_specs = [pl.BlockSpec((None, length, LANE_TILE), lambda c, b, n: (b, row_block, cb + zcol(c, n))),
                pl.BlockSpec((None, length, LANE_TILE), lambda c, b, n: (b, row_block, cb + gcol(c, n))),
                wspec(taps, zcol), wspec(1, zcol), wspec(taps, gcol), wspec(1, gcol),
                hspec(length), hspec(length), hspec(8),
                pl.BlockSpec((None, None, 1, LANE_TILE), lambda c, b, n: (layer, n, 0, c)),
                const((m, m)), const((m, m)), const((8, m)), const((m, LANE_TILE)), const((m, LANE_TILE))]
    args = [u, u, conv_w, conv_b, conv_w, conv_b, ha, hb, hm, skip, cm, sm, alt8, twc, tws]
    n_in = len(args)
    aliases = {}
    if out_prev is not None:
        in_specs.append(pl.BlockSpec(memory_space=pl.ANY))
        args.append(out_prev)
        aliases = {len(args) - 1: 0}

    def body(*refs):
        _hy_conv_kernel(*refs[:n_in], *refs[-4:])

    return pl.pallas_call(
        body,
        out_shape=jax.ShapeDtypeStruct((nb, s, ch), BF16),
        grid=(nct, nb, HY_ORDER),
        in_specs=in_specs,
        out_specs=pl.BlockSpec((None, length, LANE_TILE), lambda c, b, n: (b, row_block, c)),
        scratch_shapes=[pltpu.VMEM((LANE_TILE // 128, length, 128), F32), pltpu.VMEM((m, LANE_TILE), F32),
                        pltpu.VMEM((m, LANE_TILE), F32)],
        input_output_aliases=aliases,
        compiler_params=_params(("parallel", "parallel", "arbitrary")),
        name="hyena_conv_%d" % length,
    )(*args)


def _rms(x, w):
    return x * lax.rsqrt(jnp.mean(x * x, axis=-1, keepdims=True) + EPS) * w


def _merge_mlp_kernel(yf_ref, yb_ref, xs_ref, z_ref, dsk_ref, sn_ref, ym_ref, yh_ref, g0_ref, g1_ref, g2_ref,
                      wb_ref, wo_ref, x_ref, xc_ref, mod_ref, nw_ref, w1_ref, w2_ref, nf_ref, o_ref,
                      *, final, n_lat_tiles):
    m = mod_ref[...]
    ys = yf_ref[...] + yb_ref[...] + dsk_ref[...] * xs_ref[...].astype(F32)
    ys = _rms(ys * _silu(z_ref[...].astype(F32)), sn_ref[...]).astype(BF16)
    acc = None
    for n, (y, g_ref) in enumerate(((ys, g0_ref), (ym_ref[...].astype(BF16), g1_ref), (yh_ref[...], g2_ref))):
        term = jax.nn.sigmoid(g_ref[...].astype(F32)) * _dot(y, wb_ref[n])
        acc = term if acc is None else acc + term
    x_in = x_ref[...] if xc_ref is None else jnp.where(pl.program_id(1) < n_lat_tiles, x_ref[...], xc_ref[...])
    x = x_in + m[2:3] * _dot(acc.astype(BF16), wo_ref[...])
    h = (_rms(x, nw_ref[...]) * (1.0 + m[4:5]) + m[3:4]).astype(BF16)
    a = jnp.maximum(_dot(h, w1_ref[...]), 0.0)
    x = x + m[5:6] * _dot((a * a).astype(BF16), w2_ref[...])
    o_ref[...] = _rms(x, nf_ref[...]) if final else x


def _merge_mlp(yf, yb, xbc, pnat, d_full, ssd_norm, ym, yh, wb, wo, x, x_ctx, mod, norm_w, layer, w1, w2,
               norm_f, final, n_lat_tiles, n_tiles):
    nb, _, d = x.shape
    s = yf.shape[1]
    bw = yf.shape[-1]
    hidden = w1.shape[1]
    row = pl.BlockSpec((None, TOKEN_TILE, bw), lambda b, i: (b, i, 0))
    gate = lambda n: pl.BlockSpec((None, TOKEN_TILE, d), lambda b, i: (b, i, 1 + n))
    xrow = pl.BlockSpec((None, TOKEN_TILE, d), lambda b, i: (b, i, 0))
    lay = lambda n: pl.BlockSpec((None, 1, n), lambda b, i: (layer, 0, 0))
    single = dict(pipeline_mode=pl.Buffered(1))
    const = lambda shape: pl.BlockSpec(shape, lambda b, i: (0,) * len(shape), **single)
    in_specs = [row, row, row, row, lay(bw), lay(bw), row, row, gate(0), gate(1), gate(2),
                const((N_BRANCH, bw, d)), const((d, d))]
    args = [yf, yb, xbc, pnat, d_full, ssd_norm, ym, yh, pnat, pnat, pnat, wb, wo]
    aliases = {}
    if x_ctx is None:
        in_specs.append(xrow)
        args.append(x)
        if not final:
            aliases = {len(args) - 1: 0}
    else:
        in_specs += [pl.BlockSpec((None, TOKEN_TILE, d), lambda b, i: (b, jnp.minimum(i, n_lat_tiles - 1), 0)),
                     pl.BlockSpec((None, TOKEN_TILE, d), lambda b, i: (b, jnp.maximum(i - n_lat_tiles, 0), 0))]
        args += [x, x_ctx]
    in_specs += [pl.BlockSpec((None, 6, d), _mod_row(nb, n_lat_tiles)), lay(d),
                 const((d, hidden)), const((hidden, d)), pl.BlockSpec((1, d), lambda b, i: (0, 0))]
    args += [mod, norm_w, w1, w2, norm_f]

    def body(*refs):
        head, tail = refs[:14], refs[14:]
        xc = None if x_ctx is None else tail[0]
        _merge_mlp_kernel(*head, xc, *tail[0 if x_ctx is None else 1:], final=final, n_lat_tiles=n_lat_tiles)

    return pl.pallas_call(
        body,
        out_shape=jax.ShapeDtypeStruct((nb, n_tiles * TOKEN_TILE if final else s, d), F32),
        grid=(nb, n_tiles),
        in_specs=in_specs,
        out_specs=xrow,
        input_output_aliases=aliases,
        compiler_params=_params(("parallel", "parallel")),
        name="merge_mlp",
    )(*args)


def _dir_rows(g, nb, s, per_dir):
    g = g[:, :2 * per_dir].reshape(nb, s, 2, per_dir)
    return jnp.transpose(g, (2, 0, 3, 1))


def kernel(x, c, ctx, c_ctx, norm1_w, mod_w, mod_b, w_in, ssd_conv_w, ssd_conv_b, ssd_dt_bias, ssd_a_log,
           ssd_d, ssd_norm_w, ml_conv_w, ml_conv_b, ml_gate_b, ml_norm_w, hy_conv_w, hy_conv_b, hy_ffn_w1,
           hy_ffn_b1, hy_ffn_w2, hy_ffn_b2, hy_ffn_w3, hy_decay, hy_skip, w_branch, w_out, norm2_w,
           mlp_w1, mlp_w2, norm_f_w):
    nb, seq, d = x.shape
    ctx_len = ctx.shape[1]
    depth = w_in.shape[0]
    s = seq + ctx_len
    assert seq % ctx_len == 0 and ctx_len % TOKEN_TILE == 0 and seq % GRID_W == 0
    n_lat_tiles, n_tiles = seq // TOKEN_TILE, s // TOKEN_TILE
    n_lat_ch, n_ch = seq // CHUNK, s // CHUNK
    rows_g = seq // GRID_W

    ssd_heads = ssd_d.shape[-1]
    ssd_inner = ssd_norm_w.shape[-1]
    ssd_hd = ssd_inner // ssd_heads
    ssd_conv_ch = ssd_conv_w.shape[-1]
    ssd_ds = (ssd_conv_ch - ssd_inner) // (2 * SSD_GROUPS)
    ml_heads = ml_gate_b.shape[-1]
    ml_inner = ml_heads * ML_HEADDIM
    hy_inner = hy_skip.shape[-1]
    ssd_cols = ssd_conv_ch + ssd_inner + 2 * ssd_heads
    ml_cols = 4 * ml_inner + 4 * ml_heads
    rec_cols = ssd_cols + ml_cols
    hy_cols = (HY_ORDER + 1) * hy_inner

    o_z = ssd_conv_ch
    o_dt = ssd_conv_ch + ssd_inner
    o_ml = ssd_cols
    o_mlg = ssd_cols + 4 * ml_inner
    o_hy = rec_cols
    o_g = rec_cols + hy_cols

    xa, xa_ctx = x, ctx
    rpad = (-(nb + 1)) % 8
    c_all = jnp.concatenate([c, c_ctx[None], jnp.zeros((rpad, d), F32)], axis=0)

    tab_lat = _dft_tables(seq)
    tab_ctx = _dft_tables(ctx_len)
    k_scale = jnp.concatenate([jnp.ones((1, ml_inner), F32),
                               jnp.full((1, ml_inner), ML_HEADDIM ** -0.5, F32)], axis=1)
    ones_row = lambda n: jnp.ones((1, n), F32)

    norm1 = norm1_w[:, None, :]
    norm2 = norm2_w[:, None, :]
    ssd_norm = ssd_norm_w[:, None, :]
    ml_norm = ml_norm_w[:, None, :]
    d_full = jnp.repeat(ssd_d, ssd_hd, axis=-1)[:, None, :]
    w3 = hy_ffn_w3.reshape(depth, hy_ffn_w3.shape[1], HY_ORDER, 2, hy_inner).transpose(0, 2, 3, 1, 4)
    decay = hy_decay[:, :, :, None, :]
    skip = hy_skip[:, :, None, :]
    hb1 = hy_ffn_b1[:, None, :]
    hb2 = hy_ffn_b2[:, None, :]

    for l in range(depth):
        need_ctx = l < depth - 1
        used_tiles = n_tiles if need_ctx else n_lat_tiles
        mod = _mod_vectors(c_all, mod_w, mod_b[:, None, :], l).reshape(-1, 6, d)

        wl = w_in[l].astype(BF16)
        w_nat = jnp.concatenate([wl[:, o_z:o_dt], wl[:, o_g:], wl[:, o_hy:o_g]], axis=1)
        w_xbc = wl[:, :o_z]
        w_ml = wl[:, o_ml:o_mlg]
        gpad = lambda w: jnp.pad(w, ((0, 0), (0, 128 - w.shape[1])))
        hn, hn_cm = _normmod(xa, xa_ctx, norm1, l, mod, 0, rows_g, n_lat_tiles, n_tiles)
        hn2 = hn.reshape(nb * s, d)
        hn_cm2 = hn_cm.reshape(nb * s, d)
        pnat = _matmul(hn2, w_nat, BF16, "proj_nat").reshape(nb, s, -1)
        pml, p_mlg = _matmul(hn_cm2, w_ml, BF16, "proj_ml", gpad(wl[:, o_mlg:o_hy]))
        pxbc, p_dt = _matmul(hn2, w_xbc, BF16, "proj_xbc", gpad(wl[:, o_dt:o_ml]))
        pml = pml.reshape(nb, s, -1)
        pxbc = pxbc.reshape(nb, s, -1)
        c_g = ssd_inner
        c_hy = c_g + N_BRANCH * d

        xbc = _dwconv(pxbc, 0, ssd_conv_ch, ssd_conv_w[l], ssd_conv_b[l][None], ones_row(ssd_conv_ch),
                      True, n_lat_tiles, n_tiles, "ssd_conv")
        dt_r = _dir_rows(p_dt, nb, s, ssd_heads)
        y_f, y_b = _ssd_scan(xbc, dt_r, ssd_dt_bias[l][:, :, None], ssd_a_log[l][:, :, None],
                             ssd_heads, ssd_hd, ssd_ds, n_lat_ch, n_ch)

        qk = _dwconv(pml, 0, 2 * ml_inner, ml_conv_w[l], ml_conv_b[l][None], k_scale,
                     True, n_lat_tiles, n_tiles, "ml_conv")
        g_r = _dir_rows(p_mlg, nb, s, 2 * ml_heads)
        gate_b = ml_gate_b[l]
        c_f, c_b = _ml_scan(qk, pml, g_r[:, :, :ml_heads], g_r[:, :, ml_heads:],
                            gate_b[:, 0, :, None], gate_b[:, 1, :, None], ml_heads, n_lat_ch, n_ch)
        ym = _ml_out(c_f, c_b, pml, ml_norm, l, ml_heads, rows_g, n_lat_tiles, used_tiles)

        hcw, hcb = hy_conv_w[l], hy_conv_b[l][None]
        fl = _hy_filters(seq, l, hy_ffn_w1, hb1, hy_ffn_w2, hb2, w3, decay, tab_lat)
        rows = s if need_ctx else seq
        yh = _hy_conv(pnat, c_hy, hy_inner, hcw, hcb, None, rows, 0, seq, l, *fl, skip, tab_lat)
        if need_ctx:
            fc = _hy_filters(ctx_len, l, hy_ffn_w1, hb1, hy_ffn_w2, hb2, w3, decay, tab_ctx)
            yh = _hy_conv(pnat, c_hy, hy_inner, hcw, hcb, yh, rows, seq // ctx_len, ctx_len, l, *fc, skip,
                          tab_ctx)

        xa = _merge_mlp(y_f, y_b, xbc, pnat, d_full, ssd_norm, ym, yh, w_branch[l].astype(BF16),
                        w_out[l].astype(BF16), xa, xa_ctx, mod, norm2, l, mlp_w1[l].astype(BF16),
                        mlp_w2[l].astype(BF16), norm_f_w[None], not need_ctx, n_lat_tiles, used_tiles)
        xa_ctx = None

    return xa
```

```python
import functools
import math

import jax
import jax.numpy as jnp
import numpy as np
from jax import lax
from jax.experimental import pallas as pl
from jax.experimental.pallas import tpu as pltpu

F32 = jnp.float32
BF16 = jnp.bfloat16
HIGHEST = lax.Precision.HIGHEST

GRID_W = 64
CHUNK = 128
EPS = 1e-6
SSD_GROUPS = 2
SSD_CONV = 5
ML_HEADDIM = 128
ML_CONV = 5
HY_ORDER = 2
HY_SHORT = 3
HY_BANDS = 16
N_BRANCH = 3

TOKEN_TILE = 256
LANE_TILE = 256
HALO = 16
VMEM_LIMIT = 56 * 1024 * 1024

_hdot = functools.partial(jnp.dot, precision=HIGHEST, preferred_element_type=F32)
_dot = functools.partial(jnp.dot, preferred_element_type=F32)


def _params(sem, vmem=None):
    return pltpu.CompilerParams(dimension_semantics=sem, vmem_limit_bytes=vmem or VMEM_LIMIT)


def _softplus(x):
    return jnp.maximum(x, 0.0) + jnp.log(1.0 + jnp.exp(-jnp.abs(x)))


def _silu(x):
    return x * jax.nn.sigmoid(x)


def _mod_kernel(c_ref, w_ref, b_ref, o_ref):
    o_ref[...] = _hdot(_silu(c_ref[...]), w_ref[...]) + b_ref[...]


def _mod_vectors(c_all, mod_w, mod_b, layer):
    r, d = c_all.shape
    n = mod_w.shape[-1]
    tn = n // 6
    return pl.pallas_call(
        _mod_kernel,
        out_shape=jax.ShapeDtypeStruct((r, n), F32),
        grid=(n // tn,),
        in_specs=[pl.BlockSpec((r, d), lambda j: (0, 0)),
                  pl.BlockSpec((None, d, tn), lambda j: (layer, 0, j)),
                  pl.BlockSpec((None, 1, tn), lambda j: (layer, 0, j))],
        out_specs=pl.BlockSpec((r, tn), lambda j: (0, j)),
        compiler_params=_params(("parallel",)),
        name="mod_vectors",
    )(c_all, mod_w, mod_b)


def _grid_perm(rows_g):
    wpt = TOKEN_TILE // rows_g
    src = np.arange(TOKEN_TILE).reshape(rows_g, wpt).T.reshape(-1)
    p = np.zeros((TOKEN_TILE, TOKEN_TILE), np.float32)
    p[np.arange(TOKEN_TILE), src] = 1.0
    return p


def _normmod_kernel(x_ref, xc_ref, x4_ref, perm_ref, nw_ref, mod_ref, o_ref, ocm_ref, *, si, n_lat_tiles):
    i = pl.program_id(1)
    m = mod_ref[...]

    def normed(x):
        h = x * lax.rsqrt(jnp.mean(x * x, axis=-1, keepdims=True) + EPS) * nw_ref[...]
        return (h * (1.0 + m[si + 1:si + 2]) + m[si:si + 1]).astype(BF16)

    h = normed(x_ref[...] if xc_ref is None else jnp.where(i < n_lat_tiles, x_ref[...], xc_ref[...]))
    o_ref[...] = h

    @pl.when(i < n_lat_tiles)
    def _():
        x4 = x4_ref[...]
        ocm_ref[...] = _dot(perm_ref[...], normed(x4.reshape(TOKEN_TILE, x4.shape[-1]))).astype(BF16)

    @pl.when(i >= n_lat_tiles)
    def _():
        ocm_ref[...] = h


def _mod_row(nb, n_lat_tiles):
    return lambda b, i: (jnp.where(i < n_lat_tiles, b, nb), 0, 0)


def _normmod(x, x_ctx, norm_w, layer, mod, si, rows_g, n_lat_tiles, n_tiles):
    nb, rows, d = x.shape
    s = n_tiles * TOKEN_TILE
    wpt = TOKEN_TILE // rows_g
    assert wpt % 8 == 0 and GRID_W % wpt == 0 and rows % GRID_W == 0
    x4 = x.reshape(nb, rows // GRID_W, GRID_W, d)
    tile = pl.BlockSpec((None, TOKEN_TILE, d), lambda b, i: (b, i, 0))
    if x_ctx is None:
        srcs, src_specs = [x], [tile]
    else:
        srcs = [x, x_ctx]
        src_specs = [pl.BlockSpec((None, TOKEN_TILE, d), lambda b, i: (b, jnp.minimum(i, n_lat_tiles - 1), 0)),
                     pl.BlockSpec((None, TOKEN_TILE, d), lambda b, i: (b, jnp.maximum(i - n_lat_tiles, 0), 0))]

    def body(*refs):
        xc = None if x_ctx is None else refs[1]
        _normmod_kernel(refs[0], xc, *refs[len(srcs):], si=si, n_lat_tiles=n_lat_tiles)

    return pl.pallas_call(
        body,
        out_shape=(jax.ShapeDtypeStruct((nb, s, d), BF16),) * 2,
        grid=(nb, n_tiles),
        in_specs=src_specs + [
            pl.BlockSpec((None, rows_g, wpt, d), lambda b, i: (b, 0, jnp.minimum(i, n_lat_tiles - 1), 0)),
            pl.BlockSpec((TOKEN_TILE, TOKEN_TILE), lambda b, i: (0, 0)),
            pl.BlockSpec((None, 1, d), lambda b, i: (layer, 0, 0)),
            pl.BlockSpec((None, 6, d), _mod_row(nb, n_lat_tiles))],
        out_specs=(tile, tile),
        compiler_params=_params(("parallel", "parallel")),
        name="normmod",
    )(*srcs, x4, jnp.asarray(_grid_perm(rows_g), BF16), norm_w, mod)


def _mm_kernel(a_ref, w_ref, o_ref):
    o_ref[...] = _dot(a_ref[...], w_ref[...]).astype(o_ref.dtype)


def _pick(n, cands):
    for c in cands:
        if n % c == 0:
            return c
    return n


def _mm2_kernel(a_ref, w_ref, wg_ref, o_ref, og_ref):
    o_ref[...] = _dot(a_ref[...], w_ref[...]).astype(o_ref.dtype)

    @pl.when(pl.program_id(1) == 0)
    def _():
        og_ref[...] = _dot(a_ref[...], wg_ref[...])


def _matmul(a, w, out_dtype, name, w_gate=None):
    t, k = a.shape
    n = w.shape[1]
    tm = _pick(t, (2048, 1024, 768, 512, 256))
    tn = _pick(n, (1024, 512, 256, 128))
    in_specs = [pl.BlockSpec((tm, k), lambda i, j: (i, 0)),
                pl.BlockSpec((k, tn), lambda i, j: (0, j))]
    out_spec = pl.BlockSpec((tm, tn), lambda i, j: (i, j))
    out_shape = jax.ShapeDtypeStruct((t, n), out_dtype)
    if w_gate is None:
        return pl.pallas_call(
            _mm_kernel, out_shape=out_shape, grid=(t // tm, n // tn), in_specs=in_specs, out_specs=out_spec,
            compiler_params=_params(("parallel", "parallel")), name=name,
        )(a, w)
    ng = w_gate.shape[1]
    return pl.pallas_call(
        _mm2_kernel,
        out_shape=(out_shape, jax.ShapeDtypeStruct((t, ng), F32)),
        grid=(t // tm, n // tn),
        in_specs=in_specs + [pl.BlockSpec((k, ng), lambda i, j: (0, 0))],
        out_specs=(out_spec, pl.BlockSpec((tm, ng), lambda i, j: (i, 0))),
        compiler_params=_params(("parallel", "arbitrary")),
        name=name,
    )(a, w, w_gate)


def _shift_stack(taps):
    pad = taps // 2
    return np.concatenate([np.eye(TOKEN_TILE, k=j - pad, dtype=np.float32) for j in range(taps) if j != pad])


def _dwconv_kernel(u_ref, p_ref, n_ref, sh_ref, w_ref, b_ref, s_ref, o_ref, *, taps, act, bounds):
    i = pl.program_id(1)
    tm = u_ref.shape[0]
    pad = taps // 2
    lv, rv = jnp.float32(1.0), jnp.float32(1.0)
    for e in bounds:
        lv = jnp.where(i == e, 0.0, lv)
        rv = jnp.where(i == e - 1, 0.0, rv)
    w = w_ref[...]

    def finish(acc):
        if act:
            acc = _silu(acc)
        return (acc * s_ref[...]).astype(o_ref.dtype)

    def edge(window, first):
        rows = window.shape[0]
        acc = b_ref[...] + window[first:first + HALO] * w[pad:pad + 1]
        for j in range(taps):
            if j != pad:
                acc = acc + pltpu.roll(window, (pad - j) % rows, 0)[first:first + HALO] * w[j:j + 1]
        return finish(acc)

    u = u_ref[...]
    shifted = _dot(sh_ref[...], u)
    acc = b_ref[...] + u.astype(F32) * w[pad:pad + 1]
    blk = 0
    for j in range(taps):
        if j != pad:
            acc = acc + shifted[blk * tm:(blk + 1) * tm] * w[j:j + 1]
            blk += 1
    o_ref[...] = finish(acc)
    head = u_ref[0:2 * HALO, :].astype(F32)
    tail = u_ref[tm - 2 * HALO:tm, :].astype(F32)
    o_ref[0:HALO, :] = edge(jnp.concatenate([p_ref[...].astype(F32) * lv, head], axis=0), HALO)
    o_ref[tm - HALO:tm, :] = edge(jnp.concatenate([tail, n_ref[...].astype(F32) * rv], axis=0), HALO)


def _dwconv(u, col0, ncols, w, b, scale, act, n_lat_tiles, n_tiles, name):
    nb, s, _ = u.shape
    taps = w.shape[0]
    tc = _pick(ncols, (2048, 1536, 1024, 512, 256))
    assert col0 % tc == 0
    c0 = col0 // tc
    hb = TOKEN_TILE // HALO
    last = s // HALO - 1
    bounds = (0, n_lat_tiles, n_tiles)
    return pl.pallas_call(
        functools.partial(_dwconv_kernel, taps=taps, act=act, bounds=bounds),
        out_shape=jax.ShapeDtypeStruct((nb, s, ncols), BF16),
        grid=(nb, n_tiles, ncols // tc),
        in_specs=[pl.BlockSpec((None, TOKEN_TILE, tc), lambda bb, i, c: (bb, i, c0 + c)),
                  pl.BlockSpec((None, HALO, tc), lambda bb, i, c: (bb, jnp.maximum(i * hb - 1, 0), c0 + c)),
                  pl.BlockSpec((None, HALO, tc), lambda bb, i, c: (bb, jnp.minimum((i + 1) * hb, last), c0 + c)),
                  pl.BlockSpec(((taps - 1) * TOKEN_TILE, TOKEN_TILE), lambda bb, i, c: (0, 0)),
                  pl.BlockSpec((taps, tc), lambda bb, i, c: (0, c)),
                  pl.BlockSpec((1, tc), lambda bb, i, c: (0, c)),
                  pl.BlockSpec((1, tc), lambda bb, i, c: (0, c))],
        out_specs=pl.BlockSpec((None, TOKEN_TILE, tc), lambda bb, i, c: (bb, i, c)),
        compiler_params=_params(("parallel", "parallel", "parallel")),
        name=name,
    )(u, u, u, jnp.asarray(_shift_stack(taps), BF16), w, b, scale)


def _chunk_index(n_lat_ch, n_ch):
    return lambda d, j: (j + n_lat_ch) % n_ch if d == 0 else n_ch - 1 - j


def _scan_masks(d, t):
    ii = lax.broadcasted_iota(jnp.int32, (t, t), 0)
    jj = lax.broadcasted_iota(jnp.int32, (t, t), 1)
    mask = jj <= ii if d == 0 else jj >= ii
    tri_t = jnp.where(ii <= jj if d == 0 else ii >= jj, 1.0, 0.0).astype(F32)
    return mask, tri_t


def _gate_rows(la_r, lw_r, tri_t, m_prev):
    cum_r = _hdot(la_r, tri_t)
    last = jnp.sum(la_r, axis=1, keepdims=True)
    g_r = last - cum_r + lw_r
    m_loc = jnp.max(g_r, axis=1, keepdims=True)
    e_r = jnp.exp(g_r - m_loc)
    m_new = jnp.maximum(last + m_prev, m_loc)
    s_old = jnp.exp(last + m_prev - m_new)
    s_new = jnp.exp(m_loc - m_new)
    return cum_r, cum_r - lw_r, e_r, m_new, s_old, s_new


def _head_probs(cum_row, crow_row, m_prev_h, mask, qk):
    t = qk.shape[0]
    colb = jnp.broadcast_to(cum_row, (t, t)).T
    dlog = jnp.where(mask, colb - crow_row, -jnp.inf)
    inter = colb[:, 0:1] + m_prev_h
    m_row = jnp.maximum(inter, jnp.max(dlog, axis=1, keepdims=True))
    p = jnp.exp(dlog - m_row) * qk
    return p.astype(BF16), jnp.exp(inter - m_row), m_row


def _ssd_scan_kernel(*refs, heads, groups, hd, ds):
    j = pl.program_id(1)
    ins, outs, states = refs[:12], refs[12:14], refs[14:16]

    @pl.when(j == 0)
    def _():
        for st_ref in states:
            st_ref[...] = jnp.zeros_like(st_ref)

    for d in range(2):
        _ssd_chunk(d, *ins[6 * d:6 * d + 6], outs[d], states[d], heads=heads, groups=groups, hd=hd, ds=ds)


def _ssd_chunk(d, xs_ref, b_ref, c_ref, dt_ref, bias_ref, alog_ref, o_ref, st_ref, *, heads, groups, hd, ds):
    t = xs_ref.shape[0]
    hpg = heads // groups
    pw = 2 * hd
    mask, tri_t = _scan_masks(d, t)
    dt = _softplus(dt_ref[...] + bias_ref[...])
    la_r = -dt * jnp.exp(alog_ref[...])
    cum_r = _hdot(la_r, tri_t)
    last = jnp.sum(la_r, axis=1, keepdims=True)
    crow = cum_r - jnp.log(dt)
    e_r = jnp.exp(last - crow)
    e_last = jnp.exp(last)
    lo = lax.broadcasted_iota(jnp.int32, (1, pw), 1) < hd

    for g in range(groups):
        q = c_ref[:, g * ds:(g + 1) * ds]
        k_t = b_ref[:, g * ds:(g + 1) * ds].astype(F32).T
        qk = _dot(q, k_t.astype(BF16))
        w0 = g * hpg * hd
        qs = _dot(q, st_ref[:, w0:w0 + hpg * hd].astype(BF16))
        for i in range(hpg // 2):
            h0 = g * hpg + 2 * i
            c0 = h0 * hd
            vp = xs_ref[:, c0:c0 + pw]
            zero = jnp.zeros_like(vp)
            v_bd = jnp.concatenate([jnp.where(lo, vp, zero), jnp.where(lo, zero, vp)], axis=0)
            probs, carry, kte = [], [], []
            for h in (h0, h0 + 1):
                colb = jnp.broadcast_to(cum_r[h:h + 1], (t, t)).T
                dlog = jnp.where(mask, colb - crow[h:h + 1], -jnp.inf)
                probs.append((jnp.exp(dlog) * qk).astype(BF16))
                carry.append(jnp.exp(colb))
                kte.append((k_t * e_r[h:h + 1]).astype(BF16))
            lhs = jnp.concatenate([jnp.concatenate(probs, axis=1), jnp.concatenate(kte, axis=1)], axis=0)
            res = _dot(lhs, v_bd)
            y = res[:t] + qs[:, 2 * i * hd:2 * i * hd + pw] * jnp.where(lo, carry[0], carry[1])
            o_ref[:, c0:c0 + pw] = y.astype(o_ref.dtype)
            decay = jnp.where(lo, e_last[h0:h0 + 1], e_last[h0 + 1:h0 + 2])
            st_ref[:, c0:c0 + pw] = decay * st_ref[:, c0:c0 + pw] + res[t:]


def _ssd_scan(xbc, dt_r, bias_c, alog_c, heads, hd, ds, n_lat_ch, n_ch):
    nb, s, _ = xbc.shape
    inner = heads * hd
    gn = SSD_GROUPS * ds
    assert 2 * hd == CHUNK and (heads // SSD_GROUPS) % 2 == 0 and inner % gn == 0
    cidx = _chunk_index(n_lat_ch, n_ch)

    def specs(d):
        return [pl.BlockSpec((None, CHUNK, inner), lambda b, j: (b, cidx(d, j), 0)),
                pl.BlockSpec((None, CHUNK, gn), lambda b, j: (b, cidx(d, j), inner // gn)),
                pl.BlockSpec((None, CHUNK, gn), lambda b, j: (b, cidx(d, j), inner // gn + 1)),
                pl.BlockSpec((None, None, heads, CHUNK), lambda b, j: (d, b, 0, cidx(d, j))),
                pl.BlockSpec((None, heads, 1), lambda b, j: (d, 0, 0)),
                pl.BlockSpec((None, heads, 1), lambda b, j: (d, 0, 0))]

    out = lambda d: pl.BlockSpec((None, CHUNK, inner), lambda b, j: (b, cidx(d, j), 0))
    args = (xbc, xbc, xbc, dt_r, bias_c, alog_c)
    return pl.pallas_call(
        functools.partial(_ssd_scan_kernel, heads=heads, groups=SSD_GROUPS, hd=hd, ds=ds),
        out_shape=(jax.ShapeDtypeStruct((nb, s, inner), BF16),) * 2,
        grid=(nb, n_ch),
        in_specs=specs(0) + specs(1),
        out_specs=(out(0), out(1)),
        scratch_shapes=[pltpu.VMEM((ds, inner), F32)] * 2,
        compiler_params=_params(("parallel", "arbitrary")),
        name="ssd_scan",
    )(*args, *args)


def _ml_scan_kernel(*refs, heads):
    j = pl.program_id(1)
    ins, outs, states = refs[:14], refs[14:16], refs[16:20]

    @pl.when(j == 0)
    def _():
        for ref in states:
            ref[...] = jnp.zeros_like(ref)

    for d in range(2):
        _ml_chunk(d, *ins[7 * d:7 * d + 7], outs[d], *states[2 * d:2 * d + 2], heads=heads)


def _ml_chunk(d, q_ref, k_ref, v_ref, gi_ref, gf_ref, bi_ref, bf_ref, o_ref, st_ref, m_ref, *, heads):
    t = q_ref.shape[0]
    dh = ML_HEADDIM
    mask, tri_t = _scan_masks(d, t)
    lw_r = gi_ref[...] + bi_ref[...]
    f = gf_ref[...] + bf_ref[...]
    la_r = jnp.minimum(f, 0.0) - jnp.log(1.0 + jnp.exp(-jnp.abs(f)))
    m_prev = m_ref[:, 0:1]
    cum_r, crow, e_r, m_new, s_old, s_new = _gate_rows(la_r, lw_r, tri_t, m_prev)
    ones = jnp.ones((t, dh), BF16)

    for h in range(heads):
        c0 = h * dh
        q = q_ref[:, c0:c0 + dh]
        k_t = k_ref[:, c0:c0 + dh].astype(F32).T
        v_aug = jnp.concatenate([v_ref[:, c0:c0 + dh], ones], axis=1)
        qk = _dot(q, k_t.astype(BF16))
        s0 = 2 * c0
        qs = _dot(q, st_ref[:, s0:s0 + 2 * dh].astype(BF16))
        p, cf, m_row = _head_probs(cum_r[h:h + 1], crow[h:h + 1], m_prev[h:h + 1], mask, qk)
        res = _dot(jnp.concatenate([p, (k_t * e_r[h:h + 1]).astype(BF16)], axis=0), v_aug)
        y = res[:t] + qs * cf
        cell = y[:, :dh] / jnp.maximum(jnp.abs(y[:, dh:]), jnp.exp(-m_row))
        o_ref[:, c0:c0 + dh] = cell.astype(o_ref.dtype)
        st_ref[:, s0:s0 + 2 * dh] = s_old[h:h + 1] * st_ref[:, s0:s0 + 2 * dh] + s_new[h:h + 1] * res[t:]
    m_ref[...] = jnp.broadcast_to(m_new, m_ref.shape)


def _ml_scan(qk, pml, gi_r, gf_r, bi_c, bf_c, heads, n_lat_ch, n_ch):
    nb, s, _ = qk.shape
    inner = heads * ML_HEADDIM
    cidx = _chunk_index(n_lat_ch, n_ch)

    def specs(d):
        gspec = pl.BlockSpec((None, None, heads, CHUNK), lambda b, j: (d, b, 0, cidx(d, j)))
        bspec = pl.BlockSpec((None, heads, 1), lambda b, j: (d, 0, 0))
        col = lambda c: pl.BlockSpec((None, CHUNK, inner), lambda b, j: (b, cidx(d, j), c))
        return [col(0), col(1), col(2), gspec, gspec, bspec, bspec]

    out = lambda d: pl.BlockSpec((None, CHUNK, inner), lambda b, j: (b, cidx(d, j), 0))
    args = (qk, qk, pml, gi_r, gf_r, bi_c, bf_c)
    return pl.pallas_call(
        functools.partial(_ml_scan_kernel, heads=heads),
        out_shape=(jax.ShapeDtypeStruct((nb, s, inner), BF16),) * 2,
        grid=(nb, n_ch),
        in_specs=specs(0) + specs(1),
        out_specs=(out(0), out(1)),
        scratch_shapes=[pltpu.VMEM((ML_HEADDIM, 2 * inner), F32), pltpu.VMEM((heads, 128), F32)] * 2,
        compiler_params=_params(("parallel", "arbitrary")),
        name="ml_scan",
    )(*args, *args)


def _ml_out_kernel(cf_ref, cb_ref, o_in_ref, nw_ref, perm_ref, *rest, heads, permute):
    o_ref = rest[-1]
    dh = ML_HEADDIM
    parts = []
    for h in range(heads):
        sl = slice(h * dh, (h + 1) * dh)
        c = cf_ref[:, sl].astype(F32) + cb_ref[:, sl].astype(F32)
        c = c * lax.rsqrt(jnp.mean(c * c, axis=-1, keepdims=True) + EPS) * nw_ref[:, sl]
        parts.append((jax.nn.sigmoid(o_in_ref[:, sl].astype(F32)) * c).astype(BF16))
    y = jnp.concatenate(parts, axis=1)
    y = _dot(perm_ref[...], y) if permute else y.astype(F32)
    o_ref[...] = y.reshape(o_ref.shape)


def _ml_out(cf, cb, pml, norm_w, layer, heads, rows_g, n_lat_tiles, n_tiles):
    nb, _, inner = cf.shape
    wpt = TOKEN_TILE // rows_g
    rpt = TOKEN_TILE // GRID_W
    perm_t = jnp.asarray(_grid_perm(rows_g).T, BF16)
    out_shape = jax.ShapeDtypeStruct((nb, n_tiles * rpt, GRID_W, inner), F32)

    def call(tile0, tiles, out_spec, permute, prev):
        row = pl.BlockSpec((None, TOKEN_TILE, inner), lambda b, i: (b, tile0 + i, 0))
        in_specs = [row, row,
                    pl.BlockSpec((None, TOKEN_TILE, inner), lambda b, i: (b, tile0 + i, 3)),
                    pl.BlockSpec((None, 1, inner), lambda b, i: (layer, 0, 0)),
                    pl.BlockSpec((TOKEN_TILE, TOKEN_TILE), lambda b, i: (0, 0))]
        args = [cf, cb, pml, norm_w, perm_t]
        if prev is not None:
            in_specs.append(pl.BlockSpec(memory_space=pl.ANY))
            args.append(prev)
        return pl.pallas_call(
            functools.partial(_ml_out_kernel, heads=heads, permute=permute),
            out_shape=out_shape,
            grid=(nb, tiles),
            in_specs=in_specs,
            out_specs=out_spec,
            input_output_aliases={} if prev is None else {len(args) - 1: 0},
            compiler_params=_params(("parallel", "parallel")),
            name="ml_out_lat" if permute else "ml_out_ctx",
        )(*args)

    y = call(0, n_lat_tiles, pl.BlockSpec((None, rows_g, wpt, inner), lambda b, i: (b, 0, i, 0)), True, None)
    if n_tiles > n_lat_tiles:
        y = call(n_lat_tiles, n_tiles - n_lat_tiles,
                 pl.BlockSpec((None, rpt, GRID_W, inner), lambda b, i: (b, n_lat_tiles + i, 0, 0)), False, y)
    return y.reshape(nb, n_tiles * TOKEN_TILE, inner)


def _dft_tables(length):
    m = length // 2
    k = np.arange(m, dtype=np.int64)
    ang = ((k[:, None] * k[None, :]) % (2 * m)).astype(np.float64) * (np.pi / m)
    alt8 = np.broadcast_to(np.where(k % 2 == 0, 1.0, -1.0)[None, :], (8, m))
    tw = k.astype(np.float64)[:, None] * (np.pi / length) * np.ones((1, LANE_TILE))
    return (jnp.asarray(np.cos(ang), BF16), jnp.asarray(np.sin(ang), BF16), jnp.asarray(alt8, BF16),
            jnp.asarray(np.cos(tw), F32), jnp.asarray(np.sin(tw), F32))


def _hy_feats(length):
    t = jnp.arange(length, dtype=F32)
    t_norm = t / length
    bands = jnp.linspace(1e-4, HY_BANDS - 1, HY_BANDS, dtype=F32)
    ang = (2.0 * math.pi / length) * t[:, None] * bands[None, :]
    feats = jnp.concatenate([t_norm[:, None], jnp.cos(ang), -jnp.sin(ang)], axis=-1)
    return feats[0::2], feats[1::2], t_norm[0::2, None], t_norm[1::2, None]


def _split_spectrum(ae, be, ao, bo, twc, tws):
    tr = twc * ao - tws * bo
    tm = twc * bo + tws * ao
    return (ae + tr, ae - tr), (be + tm, tm - be)


def _hy_filter_kernel(fe_ref, fo_ref, tne_ref, tno_ref, w1_ref, b1_ref, w2_ref, b2_ref, w3f_ref, w3b_ref,
                      df_ref, db_ref, cm_ref, sm_ref, alt_ref, twc_ref, tws_ref, ha_ref, hb_ref, hm_ref,
                      hide_ref, hido_ref):
    m = cm_ref.shape[0]

    @pl.when((pl.program_id(0) == 0) & (pl.program_id(1) == 0))
    def _():
        for f_ref, h_ref in ((fe_ref, hide_ref), (fo_ref, hido_ref)):
            hid = jnp.sin(_hdot(f_ref[...], w1_ref[...]) + b1_ref[...])
            h_ref[...] = jnp.sin(_hdot(hid, w2_ref[...]) + b2_ref[...])

    def taps(hid, tn):
        h_f = _hdot(hid, w3f_ref[...]) * jnp.exp(-tn * jnp.abs(df_ref[...]))
        h_b = _hdot(hid, w3b_ref[...]) * jnp.exp(-tn * jnp.abs(db_ref[...]))
        return (h_f + h_b).astype(BF16), (h_f - h_b).astype(BF16)

    sum_e, dif_e = taps(hide_ref[...], tne_ref[...])
    sum_o, dif_o = taps(hido_ref[...], tno_ref[...])
    twc, tws = twc_ref[...], tws_ref[...]
    cm, sm = cm_ref[...], sm_ref[...]
    (ha_lo, ha_hi), _ = _split_spectrum(_dot(cm, sum_e), 0.0, _dot(cm, sum_o), _dot(sm, sum_o), twc, tws)
    _, (hb_lo, hb_hi) = _split_spectrum(0.0, _dot(sm, dif_e), _dot(cm, dif_o), _dot(sm, dif_o), twc, tws)
    ha_ref[0:m, :] = ha_lo
    ha_ref[m:2 * m, :] = ha_hi
    hb_ref[0:m, :] = hb_lo
    hb_ref[m:2 * m, :] = hb_hi
    row = lax.broadcasted_iota(jnp.int32, hm_ref.shape, 0)
    hm_ref[...] = jnp.where(row == 0, _dot(alt_ref[...], sum_e), _dot(alt_ref[...], dif_o))


def _hy_filters(length, layer, w1, b1, w2, b2, w3, decay, tables):
    cm, sm, alt8, twc, tws = tables
    m = length // 2
    feats = _hy_feats(length)
    nfeat, nf = w1.shape[-2:]
    ch = decay.shape[-1]
    nct = ch // LANE_TILE
    const = lambda shape: pl.BlockSpec(shape, lambda n, c: (0,) * len(shape))
    w3spec = lambda dr: pl.BlockSpec((None, None, None, nf, LANE_TILE), lambda n, c: (layer, n, dr, 0, c))
    dspec = lambda dr: pl.BlockSpec((None, None, None, 1, LANE_TILE), lambda n, c: (layer, n, dr, 0, c))
    lay = lambda a, b: pl.BlockSpec((None, a, b), lambda n, c: (layer, 0, 0))
    out = lambda rows: pl.BlockSpec((None, rows, LANE_TILE), lambda n, c: (n, 0, c))
    return pl.pallas_call(
        _hy_filter_kernel,
        out_shape=(jax.ShapeDtypeStruct((HY_ORDER, length, ch), F32),
                   jax.ShapeDtypeStruct((HY_ORDER, length, ch), F32),
                   jax.ShapeDtypeStruct((HY_ORDER, 8, ch), F32)),
        grid=(HY_ORDER, nct),
        in_specs=[const((m, nfeat)), const((m, nfeat)), const((m, 1)), const((m, 1)),
                  lay(nfeat, nf), lay(1, nf), lay(nf, nf), lay(1, nf),
                  w3spec(0), w3spec(1), dspec(0), dspec(1),
                  const((m, m)), const((m, m)), const((8, m)),
                  const((m, LANE_TILE)), const((m, LANE_TILE))],
        out_specs=(out(length), out(length), out(8)),
        scratch_shapes=[pltpu.VMEM((m, nf), F32)] * 2,
        compiler_params=_params(("arbitrary", "arbitrary")),
        name="hyena_filters_%d" % length,
    )(*feats, w1, b1, w2, b2, w3, w3, decay, decay, cm, sm, alt8, twc, tws)


def _short_conv(u, w, b):
    length = u.shape[0]
    taps = w.shape[0]
    pad = taps // 2
    row = lax.broadcasted_iota(jnp.int32, u.shape, 0)
    acc = b + u * w[pad:pad + 1]
    for j in range(taps):
        off = j - pad
        if off != 0:
            moved = pltpu.roll(u, (-off) % length, 0)
            valid = (row + off >= 0) & (row + off < length)
            acc = acc + jnp.where(valid, moved, 0.0) * w[j:j + 1]
    return acc


def _hy_conv_kernel(z_ref, g_ref, wz_ref, bz_ref, wg_ref, bg_ref, ha_ref, hb_ref, hm_ref, skip_ref,
                    cm_ref, sm_ref, alt_ref, twc_ref, tws_ref, o_ref, tmp_ref, ze_ref, zo_ref):
    n = pl.program_id(2)
    m, width = ze_ref.shape
    lanes = tmp_ref.shape[-1]
    slabs = range(width // lanes)

    def put(val):
        for h in slabs:
            tmp_ref[h] = val[:, h * lanes:(h + 1) * lanes]

    def take(first):
        return jnp.concatenate([tmp_ref[h, pl.ds(first, m, stride=2), :] for h in slabs], axis=1)

    @pl.when(n == 0)
    def _():
        put(_short_conv(z_ref[...].astype(F32), wz_ref[...], bz_ref[...]))
        ze_ref[...] = take(0)
        zo_ref[...] = take(1)

    put(_short_conv(g_ref[...].astype(F32), wg_ref[...], bg_ref[...]))
    ge, go = take(0), take(1)
    ze, zo = ze_ref[...], zo_ref[...]
    zz = jnp.concatenate([ze.astype(BF16), zo.astype(BF16)], axis=1)
    a = _dot(cm_ref[...], zz)
    b = _dot(sm_ref[...], zz)
    mid = _dot(alt_ref[...], zz)[0:1]
    twc, tws = twc_ref[...], tws_ref[...]
    (a_lo, a_hi), (b_lo, b_hi) = _split_spectrum(a[:, :width], b[:, :width], a[:, width:], b[:, width:], twc, tws)

    ha_lo, ha_hi, hb_lo, hb_hi = ha_ref[0:m, :], ha_ref[m:2 * m, :], hb_ref[0:m, :], hb_ref[m:2 * m, :]
    yr_lo, ym_lo = a_lo * ha_lo - b_lo * hb_lo, a_lo * hb_lo + b_lo * ha_lo
    yr_hi, ym_hi = a_hi * ha_hi - b_hi * hb_hi, a_hi * hb_hi + b_hi * ha_hi
    ha_m, hb_m = hm_ref[0:1, :], hm_ref[1:2, :]
    yr_m = mid[:, :width] * ha_m - mid[:, width:] * hb_m
    ym_m = mid[:, :width] * hb_m + mid[:, width:] * ha_m

    row = lax.broadcasted_iota(jnp.int32, (m, width), 0)
    half0 = jnp.where(row == 0, 0.5, 1.0)
    qr, qm = yr_lo - yr_hi, ym_lo + ym_hi
    pr = jnp.concatenate([((yr_lo + yr_hi) * half0).astype(BF16),
                          ((qr * twc + qm * tws) * half0).astype(BF16)], axis=1)
    pm = jnp.concatenate([(ym_lo - ym_hi).astype(BF16), (qm * twc - qr * tws).astype(BF16)], axis=1)
    y = _dot(cm_ref[...], pr) + _dot(sm_ref[...], pm)
    alt = jnp.where((row & 1) == 0, 1.0, -1.0)
    scale = 1.0 / (2 * m)
    skip = skip_ref[...]
    ze_new = ge * ((y[:, :width] + alt * yr_m) * scale + skip * ze)
    zo_new = go * ((y[:, width:] + alt * ym_m) * scale + skip * zo)

    @pl.when(n < HY_ORDER - 1)
    def _():
        ze_ref[...] = ze_new
        zo_ref[...] = zo_new

    @pl.when(n == HY_ORDER - 1)
    def _():
        for h in slabs:
            tmp_ref[h, pl.ds(0, m, stride=2), :] = ze_new[:, h * lanes:(h + 1) * lanes]
            tmp_ref[h, pl.ds(1, m, stride=2), :] = zo_new[:, h * lanes:(h + 1) * lanes]
            o_ref[:, h * lanes:(h + 1) * lanes] = tmp_ref[h].astype(o_ref.dtype)


def _hy_conv(u, col0, ch, conv_w, conv_b, out_prev, out_rows, row_block, length, layer, ha, hb, hm, skip,
             tables):
    cm, sm, alt8, twc, tws = tables
    nb = u.shape[0]
    s = out_rows
    m = length // 2
    nct = ch // LANE_TILE
    assert col0 % LANE_TILE == 0
    cb = col0 // LANE_TILE
    taps = conv_w.shape[0]
    const = lambda shape: pl.BlockSpec(shape, lambda c, b, n: (0,) * len(shape))
    hspec = lambda rows: pl.BlockSpec((None, rows, LANE_TILE), lambda c, b, n: (n, 0, c))
    zcol = lambda c, n: c
    gcol = lambda c, n: (1 + n) * nct + c
    wspec = lambda rows, col: pl.BlockSpec((rows, LANE_TILE), lambda c, b, n: (0, col(c, n)))
    in_specs = [pl.BlockSpec((None, length, LANE_TILE), lambda c, b, n: (b, row_block, cb + zcol(c, n))),
                pl.BlockSpec((None, length, LANE_TILE), lambda c, b, n: (b, row_block, cb + gcol(c, n))),
                wspec(taps, zcol), wspec(1, zcol), wspec(taps, gcol), wspec(1, gcol),
                hspec(length), hspec(length), hspec(8),
                pl.BlockSpec((None, None, 1, LANE_TILE), lambda c, b, n: (layer, n, 0, c)),
                const((m, m)), const((m, m)), const((8, m)), const((m, LANE_TILE)), const((m, LANE_TILE))]
    args = [u, u, conv_w, conv_b, conv_w, conv_b, ha, hb, hm, skip, cm, sm, alt8, twc, tws]
    n_in = len(args)
    aliases = {}
    if out_prev is not None:
        in_specs.append(pl.BlockSpec(memory_space=pl.ANY))
        args.append(out_prev)
        aliases = {len(args) - 1: 0}

    def body(*refs):
        _hy_conv_kernel(*refs[:n_in], *refs[-4:])

    return pl.pallas_call(
        body,
        out_shape=jax.ShapeDtypeStruct((nb, s, ch), BF16),
        grid=(nct, nb, HY_ORDER),
        in_specs=in_specs,
        out_specs=pl.BlockSpec((None, length, LANE_TILE), lambda c, b, n: (b, row_block, c)),
        scratch_shapes=[pltpu.VMEM((LANE_TILE // 128, length, 128), F32), pltpu.VMEM((m, LANE_TILE), F32),
                        pltpu.VMEM((m, LANE_TILE), F32)],
        input_output_aliases=aliases,
        compiler_params=_params(("parallel", "parallel", "arbitrary")),
        name="hyena_conv_%d" % length,
    )(*args)


def _rms(x, w):
    return x * lax.rsqrt(jnp.mean(x * x, axis=-1, keepdims=True) + EPS) * w


def _merge_mlp_kernel(yf_ref, yb_ref, xs_ref, z_ref, dsk_ref, sn_ref, ym_ref, yh_ref, g0_ref, g1_ref, g2_ref,
                      wb_ref, wo_ref, x_ref, xc_ref, mod_ref, nw_ref, w1_ref, w2_ref, nf_ref, o_ref,
                      *, final, n_lat_tiles):
    m = mod_ref[...]
    ys = yf_ref[...].astype(F32) + yb_ref[...].astype(F32) + dsk_ref[...] * xs_ref[...].astype(F32)
    ys = _rms(ys * _silu(z_ref[...].astype(F32)), sn_ref[...]).astype(BF16)
    acc = None
    for n, (y, g_ref) in enumerate(((ys, g0_ref), (ym_ref[...].astype(BF16), g1_ref), (yh_ref[...], g2_ref))):
        term = jax.nn.sigmoid(g_ref[...].astype(F32)) * _dot(y, wb_ref[n])
        acc = term if acc is None else acc + term
    x_in = x_ref[...] if xc_ref is None else jnp.where(pl.program_id(1) < n_lat_tiles, x_ref[...], xc_ref[...])
    x = x_in + m[2:3] * _dot(acc.astype(BF16), wo_ref[...])
    h = (_rms(x, nw_ref[...]) * (1.0 + m[4:5]) + m[3:4]).astype(BF16)
    a = jnp.maximum(_dot(h, w1_ref[...]), 0.0)
    x = x + m[5:6] * _dot((a * a).astype(BF16), w2_ref[...])
    o_ref[...] = _rms(x, nf_ref[...]) if final else x


def _merge_mlp(yf, yb, xbc, pnat, d_full, ssd_norm, ym, yh, wb, wo, x, x_ctx, mod, norm_w, layer, w1, w2,
               norm_f, final, n_lat_tiles, n_tiles):
    nb, _, d = x.shape
    s = yf.shape[1]
    bw = yf.shape[-1]
    hidden = w1.shape[1]
    row = pl.BlockSpec((None, TOKEN_TILE, bw), lambda b, i: (b, i, 0))
    gate = lambda n: pl.BlockSpec((None, TOKEN_TILE, d), lambda b, i: (b, i, 1 + n))
    xrow = pl.BlockSpec((None, TOKEN_TILE, d), lambda b, i: (b, i, 0))
    lay = lambda n: pl.BlockSpec((None, 1, n), lambda b, i: (layer, 0, 0))
    single = dict(pipeline_mode=pl.Buffered(1))
    const = lambda shape: pl.BlockSpec(shape, lambda b, i: (0,) * len(shape), **single)
    in_specs = [row, row, row, row, lay(bw), lay(bw), row, row, gate(0), gate(1), gate(2),
                const((N_BRANCH, bw, d)), const((d, d))]
    args = [yf, yb, xbc, pnat, d_full, ssd_norm, ym, yh, pnat, pnat, pnat, wb, wo]
    aliases = {}
    if x_ctx is None:
        in_specs.append(xrow)
        args.append(x)
        if not final:
            aliases = {len(args) - 1: 0}
    else:
        in_specs += [pl.BlockSpec((None, TOKEN_TILE, d), lambda b, i: (b, jnp.minimum(i, n_lat_tiles - 1), 0)),
                     pl.BlockSpec((None, TOKEN_TILE, d), lambda b, i: (b, jnp.maximum(i - n_lat_tiles, 0), 0))]
        args += [x, x_ctx]
    in_specs += [pl.BlockSpec((None, 6, d), _mod_row(nb, n_lat_tiles)), lay(d),
                 const((d, hidden)), const((hidden, d)), pl.BlockSpec((1, d), lambda b, i: (0, 0))]
    args += [mod, norm_w, w1, w2, norm_f]

    def body(*refs):
        head, tail = refs[:14], refs[14:]
        xc = None if x_ctx is None else tail[0]
        _merge_mlp_kernel(*head, xc, *tail[0 if x_ctx is None else 1:], final=final, n_lat_tiles=n_lat_tiles)

    return pl.pallas_call(
        body,
        out_shape=jax.ShapeDtypeStruct((nb, n_tiles * TOKEN_TILE if final else s, d), F32),
        grid=(nb, n_tiles),
        in_specs=in_specs,
        out_specs=xrow,
        input_output_aliases=aliases,
        compiler_params=_params(("parallel", "parallel")),
        name="merge_mlp",
    )(*args)


def _dir_rows(g, nb, s, per_dir):
    g = g[:, :2 * per_dir].reshape(nb, s, 2, per_dir)
    return jnp.transpose(g, (2, 0, 3, 1))


def kernel(x, c, ctx, c_ctx, norm1_w, mod_w, mod_b, w_in, ssd_conv_w, ssd_conv_b, ssd_dt_bias, ssd_a_log,
           ssd_d, ssd_norm_w, ml_conv_w, ml_conv_b, ml_gate_b, ml_norm_w, hy_conv_w, hy_conv_b, hy_ffn_w1,
           hy_ffn_b1, hy_ffn_w2, hy_ffn_b2, hy_ffn_w3, hy_decay, hy_skip, w_branch, w_out, norm2_w,
           mlp_w1, mlp_w2, norm_f_w):
    nb, seq, d = x.shape
    ctx_len = ctx.shape[1]
    depth = w_in.shape[0]
    s = seq + ctx_len
    assert seq % ctx_len == 0 and ctx_len % TOKEN_TILE == 0 and seq % GRID_W == 0
    n_lat_tiles, n_tiles = seq // TOKEN_TILE, s // TOKEN_TILE
    n_lat_ch, n_ch = seq // CHUNK, s // CHUNK
    rows_g = seq // GRID_W

    ssd_heads = ssd_d.shape[-1]
    ssd_inner = ssd_norm_w.shape[-1]
    ssd_hd = ssd_inner // ssd_heads
    ssd_conv_ch = ssd_conv_w.shape[-1]
    ssd_ds = (ssd_conv_ch - ssd_inner) // (2 * SSD_GROUPS)
    ml_heads = ml_gate_b.shape[-1]
    ml_inner = ml_heads * ML_HEADDIM
    hy_inner = hy_skip.shape[-1]
    ssd_cols = ssd_conv_ch + ssd_inner + 2 * ssd_heads
    ml_cols = 4 * ml_inner + 4 * ml_heads
    rec_cols = ssd_cols + ml_cols
    hy_cols = (HY_ORDER + 1) * hy_inner

    o_z = ssd_conv_ch
    o_dt = ssd_conv_ch + ssd_inner
    o_ml = ssd_cols
    o_mlg = ssd_cols + 4 * ml_inner
    o_hy = rec_cols
    o_g = rec_cols + hy_cols

    xa, xa_ctx = x, ctx
    rpad = (-(nb + 1)) % 8
    c_all = jnp.concatenate([c, c_ctx[None], jnp.zeros((rpad, d), F32)], axis=0)

    tab_lat = _dft_tables(seq)
    tab_ctx = _dft_tables(ctx_len)
    k_scale = jnp.concatenate([jnp.ones((1, ml_inner), F32),
                               jnp.full((1, ml_inner), ML_HEADDIM ** -0.5, F32)], axis=1)
    ones_row = lambda n: jnp.ones((1, n), F32)

    norm1 = norm1_w[:, None, :]
    norm2 = norm2_w[:, None, :]
    ssd_norm = ssd_norm_w[:, None, :]
    ml_norm = ml_norm_w[:, None, :]
    d_full = jnp.repeat(ssd_d, ssd_hd, axis=-1)[:, None, :]
    w3 = hy_ffn_w3.reshape(depth, hy_ffn_w3.shape[1], HY_ORDER, 2, hy_inner).transpose(0, 2, 3, 1, 4)
    decay = hy_decay[:, :, :, None, :]
    skip = hy_skip[:, :, None, :]
    hb1 = hy_ffn_b1[:, None, :]
    hb2 = hy_ffn_b2[:, None, :]

    for l in range(depth):
        need_ctx = l < depth - 1
        used_tiles = n_tiles if need_ctx else n_lat_tiles
        mod = _mod_vectors(c_all, mod_w, mod_b[:, None, :], l).reshape(-1, 6, d)

        wl = w_in[l].astype(BF16)
        w_nat = jnp.concatenate([wl[:, o_z:o_dt], wl[:, o_g:], wl[:, o_hy:o_g]], axis=1)
        w_xbc = wl[:, :o_z]
        w_ml = wl[:, o_ml:o_mlg]
        gpad = lambda w: jnp.pad(w, ((0, 0), (0, 128 - w.shape[1])))
        hn, hn_cm = _normmod(xa, xa_ctx, norm1, l, mod, 0, rows_g, n_lat_tiles, n_tiles)
        hn2 = hn.reshape(nb * s, d)
        hn_cm2 = hn_cm.reshape(nb * s, d)
        pnat = _matmul(hn2, w_nat, BF16, "proj_nat").reshape(nb, s, -1)
        pml, p_mlg = _matmul(hn_cm2, w_ml, BF16, "proj_ml", gpad(wl[:, o_mlg:o_hy]))
        pxbc, p_dt = _matmul(hn2, w_xbc, BF16, "proj_xbc", gpad(wl[:, o_dt:o_ml]))
        pml = pml.reshape(nb, s, -1)
        pxbc = pxbc.reshape(nb, s, -1)
        c_g = ssd_inner
        c_hy = c_g + N_BRANCH * d

        xbc = _dwconv(pxbc, 0, ssd_conv_ch, ssd_conv_w[l], ssd_conv_b[l][None], ones_row(ssd_conv_ch),
                      True, n_lat_tiles, n_tiles, "ssd_conv")
        dt_r = _dir_rows(p_dt, nb, s, ssd_heads)
        y_f, y_b = _ssd_scan(xbc, dt_r, ssd_dt_bias[l][:, :, None], ssd_a_log[l][:, :, None],
                             ssd_heads, ssd_hd, ssd_ds, n_lat_ch, n_ch)

        qk = _dwconv(pml, 0, 2 * ml_inner, ml_conv_w[l], ml_conv_b[l][None], k_scale,
                     True, n_lat_tiles, n_tiles, "ml_conv")
        g_r = _dir_rows(p_mlg, nb, s, 2 * ml_heads)
        gate_b = ml_gate_b[l]
        c_f, c_b = _ml_scan(qk, pml, g_r[:, :, :ml_heads], g_r[:, :, ml_heads:],
                            gate_b[:, 0, :, None], gate_b[:, 1, :, None], ml_heads, n_lat_ch, n_ch)
        ym = _ml_out(c_f, c_b, pml, ml_norm, l, ml_heads, rows_g, n_lat_tiles, used_tiles)

        hcw, hcb = hy_conv_w[l], hy_conv_b[l][None]
        fl = _hy_filters(seq, l, hy_ffn_w1, hb1, hy_ffn_w2, hb2, w3, decay, tab_lat)
        rows = s if need_ctx else seq
        yh = _hy_conv(pnat, c_hy, hy_inner, hcw, hcb, None, rows, 0, seq, l, *fl, skip, tab_lat)
        if need_ctx:
            fc = _hy_filters(ctx_len, l, hy_ffn_w1, hb1, hy_ffn_w2, hb2, w3, decay, tab_ctx)
            yh = _hy_conv(pnat, c_hy, hy_inner, hcw, hcb, yh, rows, seq // ctx_len, ctx_len, l, *fc, skip,
                          tab_ctx)

        xa = _merge_mlp(y_f, y_b, xbc, pnat, d_full, ssd_norm, ym, yh, w_branch[l].astype(BF16),
                        w_out[l].astype(BF16), xa, xa_ctx, mod, norm2, l, mlp_w1[l].astype(BF16),
                        mlp_w2[l].astype(BF16), norm_f_w[None], not need_ctx, n_lat_tiles, used_tiles)
        xa_ctx = None

    return xa
```

```python
import functools
import math

import jax
import jax.numpy as jnp
import numpy as np
from jax import lax
from jax.experimental import pallas as pl
from jax.experimental.pallas import tpu as pltpu

F32 = jnp.float32
BF16 = jnp.bfloat16
HIGHEST = lax.Precision.HIGHEST

GRID_W = 64
CHUNK = 128
EPS = 1e-6
SSD_GROUPS = 2
SSD_CONV = 5
ML_HEADDIM = 128
ML_CONV = 5
HY_ORDER = 2
HY_SHORT = 3
HY_BANDS = 16
N_BRANCH = 3

TOKEN_TILE = 256
LANE_TILE = 256
HALO = 16
SCAN_CHUNKS = 2
VMEM_LIMIT = 56 * 1024 * 1024

_hdot = functools.partial(jnp.dot, precision=HIGHEST, preferred_element_type=F32)
_dot = functools.partial(jnp.dot, preferred_element_type=F32)


def _params(sem, vmem=None):
    return pltpu.CompilerParams(dimension_semantics=sem, vmem_limit_bytes=vmem or VMEM_LIMIT)


def _softplus(x):
    return jnp.maximum(x, 0.0) + jnp.log(1.0 + jnp.exp(-jnp.abs(x)))


def _silu(x):
    return x * jax.nn.sigmoid(x)


def _mod_kernel(c_ref, w_ref, b_ref, o_ref):
    o_ref[...] = _hdot(_silu(c_ref[...]), w_ref[...]) + b_ref[...]


def _mod_vectors(c_all, mod_w, mod_b, layer):
    r, d = c_all.shape
    n = mod_w.shape[-1]
    tn = n // 6
    return pl.pallas_call(
        _mod_kernel,
        out_shape=jax.ShapeDtypeStruct((r, n), F32),
        grid=(n // tn,),
        in_specs=[pl.BlockSpec((r, d), lambda j: (0, 0)),
                  pl.BlockSpec((None, d, tn), lambda j: (layer, 0, j)),
                  pl.BlockSpec((None, 1, tn), lambda j: (layer, 0, j))],
        out_specs=pl.BlockSpec((r, tn), lambda j: (0, j)),
        compiler_params=_params(("parallel",)),
        name="mod_vectors",
    )(c_all, mod_w, mod_b)


def _grid_perm(rows_g):
    wpt = TOKEN_TILE // rows_g
    src = np.arange(TOKEN_TILE).reshape(rows_g, wpt).T.reshape(-1)
    p = np.zeros((TOKEN_TILE, TOKEN_TILE), np.float32)
    p[np.arange(TOKEN_TILE), src] = 1.0
    return p


def _normmod_kernel(x_ref, xc_ref, x4_ref, perm_ref, nw_ref, mod_ref, o_ref, ocm_ref, *, si, n_lat_tiles):
    i = pl.program_id(1)
    m = mod_ref[...]

    def normed(x):
        h = x * lax.rsqrt(jnp.mean(x * x, axis=-1, keepdims=True) + EPS) * nw_ref[...]
        return (h * (1.0 + m[si + 1:si + 2]) + m[si:si + 1]).astype(BF16)

    h = normed(x_ref[...] if xc_ref is None else jnp.where(i < n_lat_tiles, x_ref[...], xc_ref[...]))
    o_ref[...] = h

    @pl.when(i < n_lat_tiles)
    def _():
        x4 = x4_ref[...]
        ocm_ref[...] = _dot(perm_ref[...], normed(x4.reshape(TOKEN_TILE, x4.shape[-1]))).astype(BF16)

    @pl.when(i >= n_lat_tiles)
    def _():
        ocm_ref[...] = h


def _mod_row(nb, n_lat_tiles):
    return lambda b, i: (jnp.where(i < n_lat_tiles, b, nb), 0, 0)


def _normmod(x, x_ctx, norm_w, layer, mod, si, rows_g, n_lat_tiles, n_tiles):
    nb, rows, d = x.shape
    s = n_tiles * TOKEN_TILE
    wpt = TOKEN_TILE // rows_g
    assert wpt % 8 == 0 and GRID_W % wpt == 0 and rows % GRID_W == 0
    x4 = x.reshape(nb, rows // GRID_W, GRID_W, d)
    tile = pl.BlockSpec((None, TOKEN_TILE, d), lambda b, i: (b, i, 0))
    if x_ctx is None:
        srcs, src_specs = [x], [tile]
    else:
        srcs = [x, x_ctx]
        src_specs = [pl.BlockSpec((None, TOKEN_TILE, d), lambda b, i: (b, jnp.minimum(i, n_lat_tiles - 1), 0)),
                     pl.BlockSpec((None, TOKEN_TILE, d), lambda b, i: (b, jnp.maximum(i - n_lat_tiles, 0), 0))]

    def body(*refs):
        xc = None if x_ctx is None else refs[1]
        _normmod_kernel(refs[0], xc, *refs[len(srcs):], si=si, n_lat_tiles=n_lat_tiles)

    return pl.pallas_call(
        body,
        out_shape=(jax.ShapeDtypeStruct((nb, s, d), BF16),) * 2,
        grid=(nb, n_tiles),
        in_specs=src_specs + [
            pl.BlockSpec((None, rows_g, wpt, d), lambda b, i: (b, 0, jnp.minimum(i, n_lat_tiles - 1), 0)),
            pl.BlockSpec((TOKEN_TILE, TOKEN_TILE), lambda b, i: (0, 0)),
            pl.BlockSpec((None, 1, d), lambda b, i: (layer, 0, 0)),
            pl.BlockSpec((None, 6, d), _mod_row(nb, n_lat_tiles))],
        out_specs=(tile, tile),
        compiler_params=_params(("parallel", "parallel")),
        name="normmod",
    )(*srcs, x4, jnp.asarray(_grid_perm(rows_g), BF16), norm_w, mod)


def _mm_kernel(a_ref, w_ref, o_ref):
    o_ref[...] = _dot(a_ref[...], w_ref[...]).astype(o_ref.dtype)


def _pick(n, cands):
    for c in cands:
        if n % c == 0:
            return c
    return n


def _mm2_kernel(a_ref, w_ref, wg_ref, o_ref, og_ref):
    o_ref[...] = _dot(a_ref[...], w_ref[...]).astype(o_ref.dtype)

    @pl.when(pl.program_id(1) == 0)
    def _():
        og_ref[...] = _dot(a_ref[...], wg_ref[...])


def _matmul(a, w, out_dtype, name, w_gate=None):
    t, k = a.shape
    n = w.shape[1]
    tm = _pick(t, (2048, 1024, 768, 512, 256))
    tn = _pick(n, (1024, 512, 256, 128))
    in_specs = [pl.BlockSpec((tm, k), lambda i, j: (i, 0)),
                pl.BlockSpec((k, tn), lambda i, j: (0, j))]
    out_spec = pl.BlockSpec((tm, tn), lambda i, j: (i, j))
    out_shape = jax.ShapeDtypeStruct((t, n), out_dtype)
    if w_gate is None:
        return pl.pallas_call(
            _mm_kernel, out_shape=out_shape, grid=(t // tm, n // tn), in_specs=in_specs, out_specs=out_spec,
            compiler_params=_params(("parallel", "parallel")), name=name,
        )(a, w)
    ng = w_gate.shape[1]
    return pl.pallas_call(
        _mm2_kernel,
        out_shape=(out_shape, jax.ShapeDtypeStruct((t, ng), F32)),
        grid=(t // tm, n // tn),
        in_specs=in_specs + [pl.BlockSpec((k, ng), lambda i, j: (0, 0))],
        out_specs=(out_spec, pl.BlockSpec((tm, ng), lambda i, j: (i, 0))),
        compiler_params=_params(("parallel", "arbitrary")),
        name=name,
    )(a, w, w_gate)


def _shift_stack(taps):
    pad = taps // 2
    return np.concatenate([np.eye(TOKEN_TILE, k=j - pad, dtype=np.float32) for j in range(taps) if j != pad])


def _dwconv_kernel(u_ref, p_ref, n_ref, sh_ref, w_ref, b_ref, s_ref, o_ref, *, taps, act, bounds):
    i = pl.program_id(1)
    tm = u_ref.shape[0]
    pad = taps // 2
    lv, rv = jnp.float32(1.0), jnp.float32(1.0)
    for e in bounds:
        lv = jnp.where(i == e, 0.0, lv)
        rv = jnp.where(i == e - 1, 0.0, rv)
    w = w_ref[...]

    def finish(acc):
        if act:
            acc = _silu(acc)
        return (acc * s_ref[...]).astype(o_ref.dtype)

    def edge(window, first):
        rows = window.shape[0]
        acc = b_ref[...] + window[first:first + HALO] * w[pad:pad + 1]
        for j in range(taps):
            if j != pad:
                acc = acc + pltpu.roll(window, (pad - j) % rows, 0)[first:first + HALO] * w[j:j + 1]
        return finish(acc)

    u = u_ref[...]
    shifted = _dot(sh_ref[...], u)
    acc = b_ref[...] + u.astype(F32) * w[pad:pad + 1]
    blk = 0
    for j in range(taps):
        if j != pad:
            acc = acc + shifted[blk * tm:(blk + 1) * tm] * w[j:j + 1]
            blk += 1
    o_ref[...] = finish(acc)
    head = u_ref[0:2 * HALO, :].astype(F32)
    tail = u_ref[tm - 2 * HALO:tm, :].astype(F32)
    o_ref[0:HALO, :] = edge(jnp.concatenate([p_ref[...].astype(F32) * lv, head], axis=0), HALO)
    o_ref[tm - HALO:tm, :] = edge(jnp.concatenate([tail, n_ref[...].astype(F32) * rv], axis=0), HALO)


def _dwconv(u, col0, ncols, w, b, scale, act, n_lat_tiles, n_tiles, name):
    nb, s, _ = u.shape
    taps = w.shape[0]
    tc = _pick(ncols, (2048, 1536, 1024, 512, 256))
    assert col0 % tc == 0
    c0 = col0 // tc
    hb = TOKEN_TILE // HALO
    last = s // HALO - 1
    bounds = (0, n_lat_tiles, n_tiles)
    return pl.pallas_call(
        functools.partial(_dwconv_kernel, taps=taps, act=act, bounds=bounds),
        out_shape=jax.ShapeDtypeStruct((nb, s, ncols), BF16),
        grid=(nb, n_tiles, ncols // tc),
        in_specs=[pl.BlockSpec((None, TOKEN_TILE, tc), lambda bb, i, c: (bb, i, c0 + c)),
                  pl.BlockSpec((None, HALO, tc), lambda bb, i, c: (bb, jnp.maximum(i * hb - 1, 0), c0 + c)),
                  pl.BlockSpec((None, HALO, tc), lambda bb, i, c: (bb, jnp.minimum((i + 1) * hb, last), c0 + c)),
                  pl.BlockSpec(((taps - 1) * TOKEN_TILE, TOKEN_TILE), lambda bb, i, c: (0, 0)),
                  pl.BlockSpec((taps, tc), lambda bb, i, c: (0, c)),
                  pl.BlockSpec((1, tc), lambda bb, i, c: (0, c)),
                  pl.BlockSpec((1, tc), lambda bb, i, c: (0, c))],
        out_specs=pl.BlockSpec((None, TOKEN_TILE, tc), lambda bb, i, c: (bb, i, c)),
        compiler_params=_params(("parallel", "parallel", "parallel")),
        name=name,
    )(u, u, u, jnp.asarray(_shift_stack(taps), BF16), w, b, scale)


def _chunk_index(n_lat_ch, n_ch):
    return lambda d, j: (j + n_lat_ch) % n_ch if d == 0 else n_ch - 1 - j


def _scan_masks(d, t):
    ii = lax.broadcasted_iota(jnp.int32, (t, t), 0)
    jj = lax.broadcasted_iota(jnp.int32, (t, t), 1)
    mask = jj <= ii if d == 0 else jj >= ii
    tri_t = jnp.where(ii <= jj if d == 0 else ii >= jj, 1.0, 0.0).astype(F32)
    return mask, tri_t


def _gate_rows(la_r, lw_r, tri_t, m_prev):
    cum_r = _hdot(la_r, tri_t)
    last = jnp.sum(la_r, axis=1, keepdims=True)
    g_r = last - cum_r + lw_r
    m_loc = jnp.max(g_r, axis=1, keepdims=True)
    e_r = jnp.exp(g_r - m_loc)
    m_new = jnp.maximum(last + m_prev, m_loc)
    s_old = jnp.exp(last + m_prev - m_new)
    s_new = jnp.exp(m_loc - m_new)
    return cum_r, cum_r - lw_r, e_r, m_new, s_old, s_new


def _head_probs(cum_row, crow_row, m_prev_h, mask, qk):
    t = qk.shape[0]
    colb = jnp.broadcast_to(cum_row, (t, t)).T
    dlog = jnp.where(mask, colb - crow_row, -jnp.inf)
    inter = colb[:, 0:1] + m_prev_h
    m_row = jnp.maximum(inter, jnp.max(dlog, axis=1, keepdims=True))
    p = jnp.exp(dlog - m_row) * qk
    return p.astype(BF16), jnp.exp(inter - m_row), m_row


def _ssd_scan_kernel(*refs, heads, groups, hd, ds):
    j = pl.program_id(1)
    ins, outs, states = refs[:12], refs[12:14], refs[14:16]

    @pl.when(j == 0)
    def _():
        for st_ref in states:
            st_ref[...] = jnp.zeros_like(st_ref)

    for sub in range(SCAN_CHUNKS):
        for d in range(2):
            _ssd_chunk(d, _scan_rows(d, sub), *ins[6 * d:6 * d + 6], outs[d], states[d],
                       heads=heads, groups=groups, hd=hd, ds=ds)


def _scan_rows(d, sub):
    first = (sub if d == 0 else SCAN_CHUNKS - 1 - sub) * CHUNK
    return slice(first, first + CHUNK)


def _ssd_chunk(d, rs, xs_ref, b_ref, c_ref, dt_ref, bias_ref, alog_ref, o_ref, st_ref, *, heads, groups, hd, ds):
    t = CHUNK
    hpg = heads // groups
    pw = 2 * hd
    mask, tri_t = _scan_masks(d, t)
    dt = _softplus(dt_ref[:, rs] + bias_ref[...])
    la_r = -dt * jnp.exp(alog_ref[...])
    cum_r = _hdot(la_r, tri_t)
    last = jnp.sum(la_r, axis=1, keepdims=True)
    crow = cum_r - jnp.log(dt)
    e_r = jnp.exp(last - crow)
    e_last = jnp.exp(last)
    lo = lax.broadcasted_iota(jnp.int32, (1, pw), 1) < hd

    for g in range(groups):
        q = c_ref[rs, g * ds:(g + 1) * ds]
        k_t = b_ref[rs, g * ds:(g + 1) * ds].astype(F32).T
        qk = _dot(q, k_t.astype(BF16))
        w0 = g * hpg * hd
        qs = _dot(q, st_ref[:, w0:w0 + hpg * hd].astype(BF16))
        for i in range(hpg // 2):
            h0 = g * hpg + 2 * i
            c0 = h0 * hd
            vp = xs_ref[rs, c0:c0 + pw]
            zero = jnp.zeros_like(vp)
            v_bd = jnp.concatenate([jnp.where(lo, vp, zero), jnp.where(lo, zero, vp)], axis=0)
            probs, carry, kte = [], [], []
            for h in (h0, h0 + 1):
                colb = jnp.broadcast_to(cum_r[h:h + 1], (t, t)).T
                dlog = jnp.where(mask, colb - crow[h:h + 1], -jnp.inf)
                probs.append((jnp.exp(dlog) * qk).astype(BF16))
                carry.append(jnp.exp(colb))
                kte.append((k_t * e_r[h:h + 1]).astype(BF16))
            lhs = jnp.concatenate([jnp.concatenate(probs, axis=1), jnp.concatenate(kte, axis=1)], axis=0)
            res = _dot(lhs, v_bd)
            y = res[:t] + qs[:, 2 * i * hd:2 * i * hd + pw] * jnp.where(lo, carry[0], carry[1])
            o_ref[rs, c0:c0 + pw] = y.astype(o_ref.dtype)
            decay = jnp.where(lo, e_last[h0:h0 + 1], e_last[h0 + 1:h0 + 2])
            st_ref[:, c0:c0 + pw] = decay * st_ref[:, c0:c0 + pw] + res[t:]


def _ssd_scan(xbc, dt_r, bias_c, alog_c, heads, hd, ds, n_lat_ch, n_ch):
    nb, s, _ = xbc.shape
    inner = heads * hd
    gn = SSD_GROUPS * ds
    assert 2 * hd == CHUNK and (heads // SSD_GROUPS) % 2 == 0 and inner % gn == 0
    rows = SCAN_CHUNKS * CHUNK
    n_blk = n_ch // SCAN_CHUNKS
    cidx = _chunk_index(n_lat_ch // SCAN_CHUNKS, n_blk)

    def specs(d):
        return [pl.BlockSpec((None, rows, inner), lambda b, j: (b, cidx(d, j), 0)),
                pl.BlockSpec((None, rows, gn), lambda b, j: (b, cidx(d, j), inner // gn)),
                pl.BlockSpec((None, rows, gn), lambda b, j: (b, cidx(d, j), inner // gn + 1)),
                pl.BlockSpec((None, None, heads, rows), lambda b, j: (d, b, 0, cidx(d, j))),
                pl.BlockSpec((None, heads, 1), lambda b, j: (d, 0, 0)),
                pl.BlockSpec((None, heads, 1), lambda b, j: (d, 0, 0))]

    out = lambda d: pl.BlockSpec((None, rows, inner), lambda b, j: (b, cidx(d, j), 0))
    args = (xbc, xbc, xbc, dt_r, bias_c, alog_c)
    return pl.pallas_call(
        functools.partial(_ssd_scan_kernel, heads=heads, groups=SSD_GROUPS, hd=hd, ds=ds),
        out_shape=(jax.ShapeDtypeStruct((nb, s, inner), BF16),) * 2,
        grid=(nb, n_blk),
        in_specs=specs(0) + specs(1),
        out_specs=(out(0), out(1)),
        scratch_shapes=[pltpu.VMEM((ds, inner), F32)] * 2,
        compiler_params=_params(("parallel", "arbitrary")),
        name="ssd_scan",
    )(*args, *args)


def _ml_scan_kernel(*refs, heads):
    j = pl.program_id(1)
    ins, outs, states = refs[:14], refs[14:16], refs[16:20]

    @pl.when(j == 0)
    def _():
        for ref in states:
            ref[...] = jnp.zeros_like(ref)

    for sub in range(SCAN_CHUNKS):
        for d in range(2):
            _ml_chunk(d, _scan_rows(d, sub), *ins[7 * d:7 * d + 7], outs[d], *states[2 * d:2 * d + 2],
                      heads=heads)


def _ml_chunk(d, rs, q_ref, k_ref, v_ref, gi_ref, gf_ref, bi_ref, bf_ref, o_ref, st_ref, m_ref, *, heads):
    t = CHUNK
    dh = ML_HEADDIM
    mask, tri_t = _scan_masks(d, t)
    lw_r = gi_ref[:, rs] + bi_ref[...]
    f = gf_ref[:, rs] + bf_ref[...]
    la_r = jnp.minimum(f, 0.0) - jnp.log(1.0 + jnp.exp(-jnp.abs(f)))
    m_prev = m_ref[:, 0:1]
    cum_r, crow, e_r, m_new, s_old, s_new = _gate_rows(la_r, lw_r, tri_t, m_prev)
    ones = jnp.ones((t, dh), BF16)

    for h in range(heads):
        c0 = h * dh
        q = q_ref[rs, c0:c0 + dh]
        k_t = k_ref[rs, c0:c0 + dh].astype(F32).T
        v_aug = jnp.concatenate([v_ref[rs, c0:c0 + dh], ones], axis=1)
        qk = _dot(q, k_t.astype(BF16))
        s0 = 2 * c0
        qs = _dot(q, st_ref[:, s0:s0 + 2 * dh].astype(BF16))
        p, cf, m_row = _head_probs(cum_r[h:h + 1], crow[h:h + 1], m_prev[h:h + 1], mask, qk)
        res = _dot(jnp.concatenate([p, (k_t * e_r[h:h + 1]).astype(BF16)], axis=0), v_aug)
        y = res[:t] + qs * cf
        cell = y[:, :dh] / jnp.maximum(jnp.abs(y[:, dh:]), jnp.exp(-m_row))
        o_ref[rs, c0:c0 + dh] = cell.astype(o_ref.dtype)
        st_ref[:, s0:s0 + 2 * dh] = s_old[h:h + 1] * st_ref[:, s0:s0 + 2 * dh] + s_new[h:h + 1] * res[t:]
    m_ref[...] = jnp.broadcast_to(m_new, m_ref.shape)


def _ml_scan(qk, pml, gi_r, gf_r, bi_c, bf_c, heads, n_lat_ch, n_ch):
    nb, s, _ = qk.shape
    inner = heads * ML_HEADDIM
    rows = SCAN_CHUNKS * CHUNK
    n_blk = n_ch // SCAN_CHUNKS
    cidx = _chunk_index(n_lat_ch // SCAN_CHUNKS, n_blk)

    def specs(d):
        gspec = pl.BlockSpec((None, None, heads, rows), lambda b, j: (d, b, 0, cidx(d, j)))
        bspec = pl.BlockSpec((None, heads, 1), lambda b, j: (d, 0, 0))
        col = lambda c: pl.BlockSpec((None, rows, inner), lambda b, j: (b, cidx(d, j), c))
        return [col(0), col(1), col(2), gspec, gspec, bspec, bspec]

    out = lambda d: pl.BlockSpec((None, rows, inner), lambda b, j: (b, cidx(d, j), 0))
    args = (qk, qk, pml, gi_r, gf_r, bi_c, bf_c)
    return pl.pallas_call(
        functools.partial(_ml_scan_kernel, heads=heads),
        out_shape=(jax.ShapeDtypeStruct((nb, s, inner), BF16),) * 2,
        grid=(nb, n_blk),
        in_specs=specs(0) + specs(1),
        out_specs=(out(0), out(1)),
        scratch_shapes=[pltpu.VMEM((ML_HEADDIM, 2 * inner), F32), pltpu.VMEM((heads, 128), F32)] * 2,
        compiler_params=_params(("parallel", "arbitrary")),
        name="ml_scan",
    )(*args, *args)


def _ml_out_kernel(cf_ref, cb_ref, o_in_ref, nw_ref, perm_ref, *rest, heads, permute):
    o_ref = rest[-1]
    dh = ML_HEADDIM
    parts = []
    for h in range(heads):
        sl = slice(h * dh, (h + 1) * dh)
        c = cf_ref[:, sl].astype(F32) + cb_ref[:, sl].astype(F32)
        c = c * lax.rsqrt(jnp.mean(c * c, axis=-1, keepdims=True) + EPS) * nw_ref[:, sl]
        parts.append((jax.nn.sigmoid(o_in_ref[:, sl].astype(F32)) * c).astype(BF16))
    y = jnp.concatenate(parts, axis=1)
    y = _dot(perm_ref[...], y) if permute else y.astype(F32)
    o_ref[...] = y.reshape(o_ref.shape)


def _ml_out(cf, cb, pml, norm_w, layer, heads, rows_g, n_lat_tiles, n_tiles):
    nb, _, inner = cf.shape
    wpt = TOKEN_TILE // rows_g
    rpt = TOKEN_TILE // GRID_W
    perm_t = jnp.asarray(_grid_perm(rows_g).T, BF16)
    out_shape = jax.ShapeDtypeStruct((nb, n_tiles * rpt, GRID_W, inner), F32)

    def call(tile0, tiles, out_spec, permute, prev):
        row = pl.BlockSpec((None, TOKEN_TILE, inner), lambda b, i: (b, tile0 + i, 0))
        in_specs = [row, row,
                    pl.BlockSpec((None, TOKEN_TILE, inner), lambda b, i: (b, tile0 + i, 3)),
                    pl.BlockSpec((None, 1, inner), lambda b, i: (layer, 0, 0)),
                    pl.BlockSpec((TOKEN_TILE, TOKEN_TILE), lambda b, i: (0, 0))]
        args = [cf, cb, pml, norm_w, perm_t]
        if prev is not None:
            in_specs.append(pl.BlockSpec(memory_space=pl.ANY))
            args.append(prev)
        return pl.pallas_call(
            functools.partial(_ml_out_kernel, heads=heads, permute=permute),
            out_shape=out_shape,
            grid=(nb, tiles),
            in_specs=in_specs,
            out_specs=out_spec,
            input_output_aliases={} if prev is None else {len(args) - 1: 0},
            compiler_params=_params(("parallel", "parallel")),
            name="ml_out_lat" if permute else "ml_out_ctx",
        )(*args)

    y = call(0, n_lat_tiles, pl.BlockSpec((None, rows_g, wpt, inner), lambda b, i: (b, 0, i, 0)), True, None)
    if n_tiles > n_lat_tiles:
        y = call(n_lat_tiles, n_tiles - n_lat_tiles,
                 pl.BlockSpec((None, rpt, GRID_W, inner), lambda b, i: (b, n_lat_tiles + i, 0, 0)), False, y)
    return y.reshape(nb, n_tiles * TOKEN_TILE, inner)


def _dft_tables(length):
    m = length // 2
    k = np.arange(m, dtype=np.int64)
    ang = ((k[:, None] * k[None, :]) % (2 * m)).astype(np.float64) * (np.pi / m)
    alt8 = np.broadcast_to(np.where(k % 2 == 0, 1.0, -1.0)[None, :], (8, m))
    tw = k.astype(np.float64)[:, None] * (np.pi / length) * np.ones((1, LANE_TILE))
    return (jnp.asarray(np.cos(ang), BF16), jnp.asarray(np.sin(ang), BF16), jnp.asarray(alt8, BF16),
            jnp.asarray(np.cos(tw), F32), jnp.asarray(np.sin(tw), F32))


def _hy_feats(length):
    t = jnp.arange(length, dtype=F32)
    t_norm = t / length
    bands = jnp.linspace(1e-4, HY_BANDS - 1, HY_BANDS, dtype=F32)
    ang = (2.0 * math.pi / length) * t[:, None] * bands[None, :]
    feats = jnp.concatenate([t_norm[:, None], jnp.cos(ang), -jnp.sin(ang)], axis=-1)
    return feats[0::2], feats[1::2], t_norm[0::2, None], t_norm[1::2, None]


def _split_spectrum(ae, be, ao, bo, twc, tws):
    tr = twc * ao - tws * bo
    tm = twc * bo + tws * ao
    return (ae + tr, ae - tr), (be + tm, tm - be)


def _hy_filter_kernel(fe_ref, fo_ref, tne_ref, tno_ref, w1_ref, b1_ref, w2_ref, b2_ref, w3f_ref, w3b_ref,
                      df_ref, db_ref, cm_ref, sm_ref, alt_ref, twc_ref, tws_ref, ha_ref, hb_ref, hm_ref,
                      hide_ref, hido_ref):
    m = cm_ref.shape[0]

    @pl.when((pl.program_id(0) == 0) & (pl.program_id(1) == 0))
    def _():
        for f_ref, h_ref in ((fe_ref, hide_ref), (fo_ref, hido_ref)):
            hid = jnp.sin(_hdot(f_ref[...], w1_ref[...]) + b1_ref[...])
            h_ref[...] = jnp.sin(_hdot(hid, w2_ref[...]) + b2_ref[...])

    def taps(hid, tn):
        h_f = _hdot(hid, w3f_ref[...]) * jnp.exp(-tn * jnp.abs(df_ref[...]))
        h_b = _hdot(hid, w3b_ref[...]) * jnp.exp(-tn * jnp.abs(db_ref[...]))
        return (h_f + h_b).astype(BF16), (h_f - h_b).astype(BF16)

    sum_e, dif_e = taps(hide_ref[...], tne_ref[...])
    sum_o, dif_o = taps(hido_ref[...], tno_ref[...])
    twc, tws = twc_ref[...], tws_ref[...]
    cm, sm = cm_ref[...], sm_ref[...]
    (ha_lo, ha_hi), _ = _split_spectrum(_dot(cm, sum_e), 0.0, _dot(cm, sum_o), _dot(sm, sum_o), twc, tws)
    _, (hb_lo, hb_hi) = _split_spectrum(0.0, _dot(sm, dif_e), _dot(cm, dif_o), _dot(sm, dif_o), twc, tws)
    ha_ref[0:m, :] = ha_lo
    ha_ref[m:2 * m, :] = ha_hi
    hb_ref[0:m, :] = hb_lo
    hb_ref[m:2 * m, :] = hb_hi
    row = lax.broadcasted_iota(jnp.int32, hm_ref.shape, 0)
    hm_ref[...] = jnp.where(row == 0, _dot(alt_ref[...], sum_e), _dot(alt_ref[...], dif_o))


def _hy_filters(length, layer, w1, b1, w2, b2, w3, decay, tables):
    cm, sm, alt8, twc, tws = tables
    m = length // 2
    feats = _hy_feats(length)
    nfeat, nf = w1.shape[-2:]
    ch = decay.shape[-1]
    nct = ch // LANE_TILE
    const = lambda shape: pl.BlockSpec(shape, lambda n, c: (0,) * len(shape))
    w3spec = lambda dr: pl.BlockSpec((None, None, None, nf, LANE_TILE), lambda n, c: (layer, n, dr, 0, c))
    dspec = lambda dr: pl.BlockSpec((None, None, None, 1, LANE_TILE), lambda n, c: (layer, n, dr, 0, c))
    lay = lambda a, b: pl.BlockSpec((None, a, b), lambda n, c: (layer, 0, 0))
    out = lambda rows: pl.BlockSpec((None, rows, LANE_TILE), lambda n, c: (n, 0, c))
    return pl.pallas_call(
        _hy_filter_kernel,
        out_shape=(jax.ShapeDtypeStruct((HY_ORDER, length, ch), F32),
                   jax.ShapeDtypeStruct((HY_ORDER, length, ch), F32),
                   jax.ShapeDtypeStruct((HY_ORDER, 8, ch), F32)),
        grid=(HY_ORDER, nct),
        in_specs=[const((m, nfeat)), const((m, nfeat)), const((m, 1)), const((m, 1)),
                  lay(nfeat, nf), lay(1, nf), lay(nf, nf), lay(1, nf),
                  w3spec(0), w3spec(1), dspec(0), dspec(1),
                  const((m, m)), const((m, m)), const((8, m)),
                  const((m, LANE_TILE)), const((m, LANE_TILE))],
        out_specs=(out(length), out(length), out(8)),
        scratch_shapes=[pltpu.VMEM((m, nf), F32)] * 2,
        compiler_params=_params(("arbitrary", "arbitrary")),
        name="hyena_filters_%d" % length,
    )(*feats, w1, b1, w2, b2, w3, w3, decay, decay, cm, sm, alt8, twc, tws)


def _short_conv(ue, uo, w, b):
    m = ue.shape[0]
    row = lax.broadcasted_iota(jnp.int32, ue.shape, 0)
    prev_odd = jnp.where(row == 0, 0.0, pltpu.roll(uo, 1, 0))
    next_even = jnp.where(row == m - 1, 0.0, pltpu.roll(ue, m - 1, 0))
    w0, w1, w2 = w[0:1], w[1:2], w[2:3]
    return b + w0 * prev_odd + w1 * ue + w2 * uo, b + w0 * ue + w1 * uo + w2 * next_even


def _hy_conv_kernel(z_ref, g_refs, wz_ref, bz_ref, wg_refs, bg_refs, ha_ref, hb_ref, hm_ref, skip_ref,
                    cm_ref, sm_ref, alt_ref, twc_ref, tws_ref, o_ref, tmp_ref):
    width = z_ref.shape[1]
    m = cm_ref.shape[0]
    lanes = tmp_ref.shape[-1]
    slabs = range(width // lanes)

    def split(ref):
        for h in slabs:
            tmp_ref[h] = ref[:, h * lanes:(h + 1) * lanes].astype(F32)
        return [jnp.concatenate([tmp_ref[h, pl.ds(first, m, stride=2), :] for h in slabs], axis=1)
                for first in (0, 1)]

    ze, zo = _short_conv(*split(z_ref), wz_ref[...], bz_ref[...])
    for n in range(HY_ORDER):
        ge, go = _short_conv(*split(g_refs[n]), wg_refs[n][...], bg_refs[n][...])
        ze, zo = _hy_order(ze, zo, ge, go, ha_ref[n], hb_ref[n], hm_ref[n], skip_ref[n],
                           cm_ref, sm_ref, alt_ref, twc_ref[...], tws_ref[...])
    for h in slabs:
        tmp_ref[h, pl.ds(0, m, stride=2), :] = ze[:, h * lanes:(h + 1) * lanes]
        tmp_ref[h, pl.ds(1, m, stride=2), :] = zo[:, h * lanes:(h + 1) * lanes]
        o_ref[:, h * lanes:(h + 1) * lanes] = tmp_ref[h].astype(o_ref.dtype)


def _hy_order(ze, zo, ge, go, ha, hb, hm, skip, cm_ref, sm_ref, alt_ref, twc, tws):
    m, width = ze.shape
    zz = jnp.concatenate([ze.astype(BF16), zo.astype(BF16)], axis=1)
    a = _dot(cm_ref[...], zz)
    b = _dot(sm_ref[...], zz)
    mid = _dot(alt_ref[...], zz)[0:1]
    (a_lo, a_hi), (b_lo, b_hi) = _split_spectrum(a[:, :width], b[:, :width], a[:, width:], b[:, width:], twc, tws)

    ha_lo, ha_hi, hb_lo, hb_hi = ha[0:m, :], ha[m:2 * m, :], hb[0:m, :], hb[m:2 * m, :]
    yr_lo, ym_lo = a_lo * ha_lo - b_lo * hb_lo, a_lo * hb_lo + b_lo * ha_lo
    yr_hi, ym_hi = a_hi * ha_hi - b_hi * hb_hi, a_hi * hb_hi + b_hi * ha_hi
    ha_m, hb_m = hm[0:1, :], hm[1:2, :]
    yr_m = mid[:, :width] * ha_m - mid[:, width:] * hb_m
    ym_m = mid[:, :width] * hb_m + mid[:, width:] * ha_m

    row = lax.broadcasted_iota(jnp.int32, (m, width), 0)
    half0 = jnp.where(row == 0, 0.5, 1.0)
    qr, qm = yr_lo - yr_hi, ym_lo + ym_hi
    pr = jnp.concatenate([((yr_lo + yr_hi) * half0).astype(BF16),
                          ((qr * twc + qm * tws) * half0).astype(BF16)], axis=1)
    pm = jnp.concatenate([(ym_lo - ym_hi).astype(BF16), (qm * twc - qr * tws).astype(BF16)], axis=1)
    y = _dot(cm_ref[...], pr) + _dot(sm_ref[...], pm)
    alt = jnp.where((row & 1) == 0, 1.0, -1.0)
    scale = 1.0 / (2 * m)
    return (ge * ((y[:, :width] + alt * yr_m) * scale + skip * ze),
            go * ((y[:, width:] + alt * ym_m) * scale + skip * zo))


def _hy_conv(u, col0, ch, conv_w, conv_b, out_prev, out_rows, row_block, length, layer, ha, hb, hm, skip,
             tables):
    cm, sm, alt8, twc, tws = tables
    nb = u.shape[0]
    s = out_rows
    m = length // 2
    nct = ch // LANE_TILE
    assert col0 % LANE_TILE == 0
    cb = col0 // LANE_TILE
    taps = conv_w.shape[0]
    assert taps == HY_SHORT == 3
    single = dict(pipeline_mode=pl.Buffered(1))
    const = lambda shape: pl.BlockSpec(shape, lambda c, b: (0,) * len(shape), **single)
    hspec = lambda rows: pl.BlockSpec((HY_ORDER, rows, LANE_TILE), lambda c, b: (0, 0, c), **single)
    ucol = lambda k: pl.BlockSpec((None, length, LANE_TILE), lambda c, b: (b, row_block, cb + k * nct + c))
    wcol = lambda rows, k: pl.BlockSpec((rows, LANE_TILE), lambda c, b: (0, k * nct + c))
    orders = range(1, HY_ORDER + 1)
    in_specs = ([ucol(0)] + [ucol(k) for k in orders] + [wcol(taps, 0), wcol(1, 0)]
                + [wcol(taps, k) for k in orders] + [wcol(1, k) for k in orders]
                + [hspec(length), hspec(length), hspec(8),
                   pl.BlockSpec((None, HY_ORDER, 1, LANE_TILE), lambda c, b: (layer, 0, 0, c)),
                   const((m, m)), const((m, m)), const((8, m)), const((m, LANE_TILE)), const((m, LANE_TILE))])
    args = ([u] * (1 + HY_ORDER) + [conv_w, conv_b] + [conv_w] * HY_ORDER + [conv_b] * HY_ORDER
            + [ha, hb, hm, skip, cm, sm, alt8, twc, tws])
    n_in = len(args)
    aliases = {}
    if out_prev is not None:
        in_specs.append(pl.BlockSpec(memory_space=pl.ANY))
        args.append(out_prev)
        aliases = {len(args) - 1: 0}

    def body(*refs):
        k = HY_ORDER
        z_ref, g_refs = refs[0], refs[1:1 + k]
        wz_ref, bz_ref = refs[1 + k], refs[2 + k]
        wg_refs, bg_refs = refs[3 + k:3 + 2 * k], refs[3 + 2 * k:3 + 3 * k]
        _hy_conv_kernel(z_ref, g_refs, wz_ref, bz_ref, wg_refs, bg_refs, *refs[3 + 3 * k:n_in], *refs[-2:])

    return pl.pallas_call(
        body,
        out_shape=jax.ShapeDtypeStruct((nb, s, ch), BF16),
        grid=(nct, nb),
        in_specs=in_specs,
        out_specs=pl.BlockSpec((None, length, LANE_TILE), lambda c, b: (b, row_block, c)),
        scratch_shapes=[pltpu.VMEM((LANE_TILE // 128, length, 128), F32)],
        input_output_aliases=aliases,
        compiler_params=_params(("parallel", "parallel")),
        name="hyena_conv_%d" % length,
    )(*args)


def _rms(x, w):
    return x * lax.rsqrt(jnp.mean(x * x, axis=-1, keepdims=True) + EPS) * w


def _merge_mlp_kernel(yf_ref, yb_ref, xs_ref, z_ref, dsk_ref, sn_ref, ym_ref, yh_ref, g0_ref, g1_ref, g2_ref,
                      wb_ref, wo_ref, x_ref, xc_ref, mod_ref, nw_ref, w1_ref, w2_ref, nf_ref, o_ref,
                      *, final, n_lat_tiles):
    m = mod_ref[...]
    ys = yf_ref[...].astype(F32) + yb_ref[...].astype(F32) + dsk_ref[...] * xs_ref[...].astype(F32)
    ys = _rms(ys * _silu(z_ref[...].astype(F32)), sn_ref[...]).astype(BF16)
    acc = None
    for n, (y, g_ref) in enumerate(((ys, g0_ref), (ym_ref[...].astype(BF16), g1_ref), (yh_ref[...], g2_ref))):
        term = jax.nn.sigmoid(g_ref[...].astype(F32)) * _dot(y, wb_ref[n])
        acc = term if acc is None else acc + term
    x_in = x_ref[...] if xc_ref is None else jnp.where(pl.program_id(1) < n_lat_tiles, x_ref[...], xc_ref[...])
    x = x_in + m[2:3] * _dot(acc.astype(BF16), wo_ref[...])
    h = (_rms(x, nw_ref[...]) * (1.0 + m[4:5]) + m[3:4]).astype(BF16)
    a = jnp.maximum(_dot(h, w1_ref[...]), 0.0)
    x = x + m[5:6] * _dot((a * a).astype(BF16), w2_ref[...])
    o_ref[...] = _rms(x, nf_ref[...]) if final else x


def _merge_mlp(yf, yb, xbc, pnat, d_full, ssd_norm, ym, yh, wb, wo, x, x_ctx, mod, norm_w, layer, w1, w2,
               norm_f, final, n_lat_tiles, n_tiles):
    nb, _, d = x.shape
    s = yf.shape[1]
    bw = yf.shape[-1]
    hidden = w1.shape[1]
    row = pl.BlockSpec((None, TOKEN_TILE, bw), lambda b, i: (b, i, 0))
    gate = lambda n: pl.BlockSpec((None, TOKEN_TILE, d), lambda b, i: (b, i, 1 + n))
    xrow = pl.BlockSpec((None, TOKEN_TILE, d), lambda b, i: (b, i, 0))
    lay = lambda n: pl.BlockSpec((None, 1, n), lambda b, i: (layer, 0, 0))
    single = dict(pipeline_mode=pl.Buffered(1))
    const = lambda shape: pl.BlockSpec(shape, lambda b, i: (0,) * len(shape), **single)
    in_specs = [row, row, row, row, lay(bw), lay(bw), row, row, gate(0), gate(1), gate(2),
                const((N_BRANCH, bw, d)), const((d, d))]
    args = [yf, yb, xbc, pnat, d_full, ssd_norm, ym, yh, pnat, pnat, pnat, wb, wo]
    aliases = {}
    if x_ctx is None:
        in_specs.append(xrow)
        args.append(x)
        if not final:
            aliases = {len(args) - 1: 0}
    else:
        in_specs += [pl.BlockSpec((None, TOKEN_TILE, d), lambda b, i: (b, jnp.minimum(i, n_lat_tiles - 1), 0)),
                     pl.BlockSpec((None, TOKEN_TILE, d), lambda b, i: (b, jnp.maximum(i - n_lat_tiles, 0), 0))]
        args += [x, x_ctx]
    in_specs += [pl.BlockSpec((None, 6, d), _mod_row(nb, n_lat_tiles)), lay(d),
                 const((d, hidden)), const((hidden, d)), pl.BlockSpec((1, d), lambda b, i: (0, 0))]
    args += [mod, norm_w, w1, w2, norm_f]

    def body(*refs):
        head, tail = refs[:14], refs[14:]
        xc = None if x_ctx is None else tail[0]
        _merge_mlp_kernel(*head, xc, *tail[0 if x_ctx is None else 1:], final=final, n_lat_tiles=n_lat_tiles)

    return pl.pallas_call(
        body,
        out_shape=jax.ShapeDtypeStruct((nb, n_tiles * TOKEN_TILE if final else s, d), F32),
        grid=(nb, n_tiles),
        in_specs=in_specs,
        out_specs=xrow,
        input_output_aliases=aliases,
        compiler_params=_params(("parallel", "parallel")),
        name="merge_mlp",
    )(*args)


def _dir_rows(g, nb, s, per_dir):
    g = g[:, :2 * per_dir].reshape(nb, s, 2, per_dir)
    return jnp.transpose(g, (2, 0, 3, 1))


def kernel(x, c, ctx, c_ctx, norm1_w, mod_w, mod_b, w_in, ssd_conv_w, ssd_conv_b, ssd_dt_bias, ssd_a_log,
           ssd_d, ssd_norm_w, ml_conv_w, ml_conv_b, ml_gate_b, ml_norm_w, hy_conv_w, hy_conv_b, hy_ffn_w1,
           hy_ffn_b1, hy_ffn_w2, hy_ffn_b2, hy_ffn_w3, hy_decay, hy_skip, w_branch, w_out, norm2_w,
           mlp_w1, mlp_w2, norm_f_w):
    nb, seq, d = x.shape
    ctx_len = ctx.shape[1]
    depth = w_in.shape[0]
    s = seq + ctx_len
    assert seq % ctx_len == 0 and ctx_len % TOKEN_TILE == 0 and seq % GRID_W == 0
    n_lat_tiles, n_tiles = seq // TOKEN_TILE, s // TOKEN_TILE
    n_lat_ch, n_ch = seq // CHUNK, s // CHUNK
    assert n_lat_ch % SCAN_CHUNKS == 0 and n_ch % SCAN_CHUNKS == 0
    rows_g = seq // GRID_W

    ssd_heads = ssd_d.shape[-1]
    ssd_inner = ssd_norm_w.shape[-1]
    ssd_hd = ssd_inner // ssd_heads
    ssd_conv_ch = ssd_conv_w.shape[-1]
    ssd_ds = (ssd_conv_ch - ssd_inner) // (2 * SSD_GROUPS)
    ml_heads = ml_gate_b.shape[-1]
    ml_inner = ml_heads * ML_HEADDIM
    hy_inner = hy_skip.shape[-1]
    ssd_cols = ssd_conv_ch + ssd_inner + 2 * ssd_heads
    ml_cols = 4 * ml_inner + 4 * ml_heads
    rec_cols = ssd_cols + ml_cols
    hy_cols = (HY_ORDER + 1) * hy_inner

    o_z = ssd_conv_ch
    o_dt = ssd_conv_ch + ssd_inner
    o_ml = ssd_cols
    o_mlg = ssd_cols + 4 * ml_inner
    o_hy = rec_cols
    o_g = rec_cols + hy_cols

    xa, xa_ctx = x, ctx
    rpad = (-(nb + 1)) % 8
    c_all = jnp.concatenate([c, c_ctx[None], jnp.zeros((rpad, d), F32)], axis=0)

    tab_lat = _dft_tables(seq)
    tab_ctx = _dft_tables(ctx_len)
    k_scale = jnp.concatenate([jnp.ones((1, ml_inner), F32),
                               jnp.full((1, ml_inner), ML_HEADDIM ** -0.5, F32)], axis=1)
    ones_row = lambda n: jnp.ones((1, n), F32)

    norm1 = norm1_w[:, None, :]
    norm2 = norm2_w[:, None, :]
    ssd_norm = ssd_norm_w[:, None, :]
    ml_norm = ml_norm_w[:, None, :]
    d_full = jnp.repeat(ssd_d, ssd_hd, axis=-1)[:, None, :]
    w3 = hy_ffn_w3.reshape(depth, hy_ffn_w3.shape[1], HY_ORDER, 2, hy_inner).transpose(0, 2, 3, 1, 4)
    decay = hy_decay[:, :, :, None, :]
    skip = hy_skip[:, :, None, :]
    hb1 = hy_ffn_b1[:, None, :]
    hb2 = hy_ffn_b2[:, None, :]

    for l in range(depth):
        need_ctx = l < depth - 1
        used_tiles = n_tiles if need_ctx else n_lat_tiles
        mod = _mod_vectors(c_all, mod_w, mod_b[:, None, :], l).reshape(-1, 6, d)

        wl = w_in[l].astype(BF16)
        w_nat = jnp.concatenate([wl[:, o_z:o_dt], wl[:, o_g:], wl[:, o_hy:o_g]], axis=1)
        w_xbc = wl[:, :o_z]
        w_ml = wl[:, o_ml:o_mlg]
        gpad = lambda w: jnp.pad(w, ((0, 0), (0, 128 - w.shape[1])))
        hn, hn_cm = _normmod(xa, xa_ctx, norm1, l, mod, 0, rows_g, n_lat_tiles, n_tiles)
        hn2 = hn.reshape(nb * s, d)
        hn_cm2 = hn_cm.reshape(nb * s, d)
        pnat = _matmul(hn2, w_nat, BF16, "proj_nat").reshape(nb, s, -1)
        pml, p_mlg = _matmul(hn_cm2, w_ml, BF16, "proj_ml", gpad(wl[:, o_mlg:o_hy]))
        pxbc, p_dt = _matmul(hn2, w_xbc, BF16, "proj_xbc", gpad(wl[:, o_dt:o_ml]))
        pml = pml.reshape(nb, s, -1)
        pxbc = pxbc.reshape(nb, s, -1)
        c_g = ssd_inner
        c_hy = c_g + N_BRANCH * d

        xbc = _dwconv(pxbc, 0, ssd_conv_ch, ssd_conv_w[l], ssd_conv_b[l][None], ones_row(ssd_conv_ch),
                      True, n_lat_tiles, n_tiles, "ssd_conv")
        dt_r = _dir_rows(p_dt, nb, s, ssd_heads)
        y_f, y_b = _ssd_scan(xbc, dt_r, ssd_dt_bias[l][:, :, None], ssd_a_log[l][:, :, None],
                             ssd_heads, ssd_hd, ssd_ds, n_lat_ch, n_ch)

        qk = _dwconv(pml, 0, 2 * ml_inner, ml_conv_w[l], ml_conv_b[l][None], k_scale,
                     True, n_lat_tiles, n_tiles, "ml_conv")
        g_r = _dir_rows(p_mlg, nb, s, 2 * ml_heads)
        gate_b = ml_gate_b[l]
        c_f, c_b = _ml_scan(qk, pml, g_r[:, :, :ml_heads], g_r[:, :, ml_heads:],
                            gate_b[:, 0, :, None], gate_b[:, 1, :, None], ml_heads, n_lat_ch, n_ch)
        ym = _ml_out(c_f, c_b, pml, ml_norm, l, ml_heads, rows_g, n_lat_tiles, used_tiles)

        hcw, hcb = hy_conv_w[l], hy_conv_b[l][None]
        fl = _hy_filters(seq, l, hy_ffn_w1, hb1, hy_ffn_w2, hb2, w3, decay, tab_lat)
        rows = s if need_ctx else seq
        yh = _hy_conv(pnat, c_hy, hy_inner, hcw, hcb, None, rows, 0, seq, l, *fl, skip, tab_lat)
        if need_ctx:
            fc = _hy_filters(ctx_len, l, hy_ffn_w1, hb1, hy_ffn_w2, hb2, w3, decay, tab_ctx)
            yh = _hy_conv(pnat, c_hy, hy_inner, hcw, hcb, yh, rows, seq // ctx_len, ctx_len, l, *fc, skip,
                          tab_ctx)

        xa = _merge_mlp(y_f, y_b, xbc, pnat, d_full, ssd_norm, ym, yh, w_branch[l].astype(BF16),
                        w_out[l].astype(BF16), xa, xa_ctx, mod, norm2, l, mlp_w1[l].astype(BF16),
                        mlp_w2[l].astype(BF16), norm_f_w[None], not need_ctx, n_lat_tiles, used_tiles)
        xa_ctx = None

    return xa
```

```python
import functools
import math

import jax
import jax.numpy as jnp
import numpy as np
from jax import lax
from jax.experimental import pallas as pl
from jax.experimental.pallas import tpu as pltpu

F32 = jnp.float32
BF16 = jnp.bfloat16
HIGHEST = lax.Precision.HIGHEST

GRID_W = 64
CHUNK = 128
EPS = 1e-6
SSD_GROUPS = 2
SSD_CONV = 5
ML_HEADDIM = 128
ML_CONV = 5
HY_ORDER = 2
HY_SHORT = 3
HY_BANDS = 16
N_BRANCH = 3

TOKEN_TILE = 256
LANE_TILE = 256
HALO = 16
SCAN_CHUNKS = 2
VMEM_LIMIT = 56 * 1024 * 1024

_hdot = functools.partial(jnp.dot, precision=HIGHEST, preferred_element_type=F32)
_dot = functools.partial(jnp.dot, preferred_element_type=F32)


def _params(sem, vmem=None):
    return pltpu.CompilerParams(dimension_semantics=sem, vmem_limit_bytes=vmem or VMEM_LIMIT)


def _softplus(x):
    return jnp.maximum(x, 0.0) + jnp.log(1.0 + jnp.exp(-jnp.abs(x)))


def _silu(x):
    return x * jax.nn.sigmoid(x)


def _mod_kernel(c_ref, w_ref, b_ref, o_ref):
    o_ref[...] = _hdot(_silu(c_ref[...]), w_ref[...]) + b_ref[...]


def _mod_vectors(c_all, mod_w, mod_b, layer):
    r, d = c_all.shape
    n = mod_w.shape[-1]
    tn = n // 6
    return pl.pallas_call(
        _mod_kernel,
        out_shape=jax.ShapeDtypeStruct((r, n), F32),
        grid=(n // tn,),
        in_specs=[pl.BlockSpec((r, d), lambda j: (0, 0)),
                  pl.BlockSpec((None, d, tn), lambda j: (layer, 0, j)),
                  pl.BlockSpec((None, 1, tn), lambda j: (layer, 0, j))],
        out_specs=pl.BlockSpec((r, tn), lambda j: (0, j)),
        compiler_params=_params(("parallel",)),
        name="mod_vectors",
    )(c_all, mod_w, mod_b)


def _grid_perm(rows_g):
    wpt = TOKEN_TILE // rows_g
    src = np.arange(TOKEN_TILE).reshape(rows_g, wpt).T.reshape(-1)
    p = np.zeros((TOKEN_TILE, TOKEN_TILE), np.float32)
    p[np.arange(TOKEN_TILE), src] = 1.0
    return p


def _normmod_kernel(x_ref, xc_ref, x4_ref, perm_ref, nw_ref, mod_ref, o_ref, ocm_ref, *, si, n_lat_tiles):
    i = pl.program_id(1)
    m = mod_ref[...]

    def normed(x):
        h = x * lax.rsqrt(jnp.mean(x * x, axis=-1, keepdims=True) + EPS) * nw_ref[...]
        return (h * (1.0 + m[si + 1:si + 2]) + m[si:si + 1]).astype(BF16)

    h = normed(x_ref[...] if xc_ref is None else jnp.where(i < n_lat_tiles, x_ref[...], xc_ref[...]))
    o_ref[...] = h

    @pl.when(i < n_lat_tiles)
    def _():
        x4 = x4_ref[...]
        ocm_ref[...] = _dot(perm_ref[...], normed(x4.reshape(TOKEN_TILE, x4.shape[-1]))).astype(BF16)

    @pl.when(i >= n_lat_tiles)
    def _():
        ocm_ref[...] = h


def _mod_row(nb, n_lat_tiles):
    return lambda b, i: (jnp.where(i < n_lat_tiles, b, nb), 0, 0)


def _normmod(x, x_ctx, norm_w, layer, mod, si, rows_g, n_lat_tiles, n_tiles):
    nb, rows, d = x.shape
    s = n_tiles * TOKEN_TILE
    wpt = TOKEN_TILE // rows_g
    assert wpt % 8 == 0 and GRID_W % wpt == 0 and rows % GRID_W == 0
    x4 = x.reshape(nb, rows // GRID_W, GRID_W, d)
    tile = pl.BlockSpec((None, TOKEN_TILE, d), lambda b, i: (b, i, 0))
    if x_ctx is None:
        srcs, src_specs = [x], [tile]
    else:
        srcs = [x, x_ctx]
        src_specs = [pl.BlockSpec((None, TOKEN_TILE, d), lambda b, i: (b, jnp.minimum(i, n_lat_tiles - 1), 0)),
                     pl.BlockSpec((None, TOKEN_TILE, d), lambda b, i: (b, jnp.maximum(i - n_lat_tiles, 0), 0))]

    def body(*refs):
        xc = None if x_ctx is None else refs[1]
        _normmod_kernel(refs[0], xc, *refs[len(srcs):], si=si, n_lat_tiles=n_lat_tiles)

    return pl.pallas_call(
        body,
        out_shape=(jax.ShapeDtypeStruct((nb, s, d), BF16),) * 2,
        grid=(nb, n_tiles),
        in_specs=src_specs + [
            pl.BlockSpec((None, rows_g, wpt, d), lambda b, i: (b, 0, jnp.minimum(i, n_lat_tiles - 1), 0)),
            pl.BlockSpec((TOKEN_TILE, TOKEN_TILE), lambda b, i: (0, 0)),
            pl.BlockSpec((None, 1, d), lambda b, i: (layer, 0, 0)),
            pl.BlockSpec((None, 6, d), _mod_row(nb, n_lat_tiles))],
        out_specs=(tile, tile),
        compiler_params=_params(("parallel", "parallel")),
        name="normmod",
    )(*srcs, x4, jnp.asarray(_grid_perm(rows_g), BF16), norm_w, mod)


def _mm_kernel(a_ref, w_ref, o_ref):
    o_ref[...] = _dot(a_ref[...], w_ref[...]).astype(o_ref.dtype)


def _pick(n, cands):
    for c in cands:
        if n % c == 0:
            return c
    return n


def _mm2_kernel(a_ref, w_ref, wg_ref, o_ref, og_ref):
    o_ref[...] = _dot(a_ref[...], w_ref[...]).astype(o_ref.dtype)

    @pl.when(pl.program_id(1) == 0)
    def _():
        og_ref[...] = _dot(a_ref[...], wg_ref[...])


def _matmul(a, w, out_dtype, name, w_gate=None):
    t, k = a.shape
    n = w.shape[1]
    tm = _pick(t, (2048, 1024, 768, 512, 256))
    tn = _pick(n, (1024, 512, 256, 128))
    in_specs = [pl.BlockSpec((tm, k), lambda i, j: (i, 0)),
                pl.BlockSpec((k, tn), lambda i, j: (0, j))]
    out_spec = pl.BlockSpec((tm, tn), lambda i, j: (i, j))
    out_shape = jax.ShapeDtypeStruct((t, n), out_dtype)
    if w_gate is None:
        return pl.pallas_call(
            _mm_kernel, out_shape=out_shape, grid=(t // tm, n // tn), in_specs=in_specs, out_specs=out_spec,
            compiler_params=_params(("parallel", "parallel")), name=name,
        )(a, w)
    ng = w_gate.shape[1]
    return pl.pallas_call(
        _mm2_kernel,
        out_shape=(out_shape, jax.ShapeDtypeStruct((t, ng), F32)),
        grid=(t // tm, n // tn),
        in_specs=in_specs + [pl.BlockSpec((k, ng), lambda i, j: (0, 0))],
        out_specs=(out_spec, pl.BlockSpec((tm, ng), lambda i, j: (i, 0))),
        compiler_params=_params(("parallel", "arbitrary")),
        name=name,
    )(a, w, w_gate)


def _shift_stack(taps):
    pad = taps // 2
    return np.concatenate([np.eye(TOKEN_TILE, k=j - pad, dtype=np.float32) for j in range(taps) if j != pad])


def _dwconv_kernel(u_ref, p_ref, n_ref, sh_ref, w_ref, b_ref, s_ref, o_ref, *, taps, act, bounds):
    i = pl.program_id(1)
    tm = u_ref.shape[0]
    pad = taps // 2
    lv, rv = jnp.float32(1.0), jnp.float32(1.0)
    for e in bounds:
        lv = jnp.where(i == e, 0.0, lv)
        rv = jnp.where(i == e - 1, 0.0, rv)
    w = w_ref[...]

    def finish(acc):
        if act:
            acc = _silu(acc)
        return (acc * s_ref[...]).astype(o_ref.dtype)

    def edge(window, first):
        rows = window.shape[0]
        acc = b_ref[...] + window[first:first + HALO] * w[pad:pad + 1]
        for j in range(taps):
            if j != pad:
                acc = acc + pltpu.roll(window, (pad - j) % rows, 0)[first:first + HALO] * w[j:j + 1]
        return finish(acc)

    u = u_ref[...]
    shifted = _dot(sh_ref[...], u)
    acc = b_ref[...] + u.astype(F32) * w[pad:pad + 1]
    blk = 0
    for j in range(taps):
        if j != pad:
            acc = acc + shifted[blk * tm:(blk + 1) * tm] * w[j:j + 1]
            blk += 1
    o_ref[...] = finish(acc)
    head = u_ref[0:2 * HALO, :].astype(F32)
    tail = u_ref[tm - 2 * HALO:tm, :].astype(F32)
    o_ref[0:HALO, :] = edge(jnp.concatenate([p_ref[...].astype(F32) * lv, head], axis=0), HALO)
    o_ref[tm - HALO:tm, :] = edge(jnp.concatenate([tail, n_ref[...].astype(F32) * rv], axis=0), HALO)


def _dwconv(u, col0, ncols, w, b, scale, act, n_lat_tiles, n_tiles, name):
    nb, s, _ = u.shape
    taps = w.shape[0]
    tc = _pick(ncols, (2048, 1536, 1024, 512, 256))
    assert col0 % tc == 0
    c0 = col0 // tc
    hb = TOKEN_TILE // HALO
    last = s // HALO - 1
    bounds = (0, n_lat_tiles, n_tiles)
    return pl.pallas_call(
        functools.partial(_dwconv_kernel, taps=taps, act=act, bounds=bounds),
        out_shape=jax.ShapeDtypeStruct((nb, s, ncols), BF16),
        grid=(nb, n_tiles, ncols // tc),
        in_specs=[pl.BlockSpec((None, TOKEN_TILE, tc), lambda bb, i, c: (bb, i, c0 + c)),
                  pl.BlockSpec((None, HALO, tc), lambda bb, i, c: (bb, jnp.maximum(i * hb - 1, 0), c0 + c)),
                  pl.BlockSpec((None, HALO, tc), lambda bb, i, c: (bb, jnp.minimum((i + 1) * hb, last), c0 + c)),
                  pl.BlockSpec(((taps - 1) * TOKEN_TILE, TOKEN_TILE), lambda bb, i, c: (0, 0)),
                  pl.BlockSpec((taps, tc), lambda bb, i, c: (0, c)),
                  pl.BlockSpec((1, tc), lambda bb, i, c: (0, c)),
                  pl.BlockSpec((1, tc), lambda bb, i, c: (0, c))],
        out_specs=pl.BlockSpec((None, TOKEN_TILE, tc), lambda bb, i, c: (bb, i, c)),
        compiler_params=_params(("parallel", "parallel", "parallel")),
        name=name,
    )(u, u, u, jnp.asarray(_shift_stack(taps), BF16), w, b, scale)


def _chunk_index(n_lat_ch, n_ch):
    return lambda d, j: (j + n_lat_ch) % n_ch if d == 0 else n_ch - 1 - j


def _scan_masks(d, t):
    ii = lax.broadcasted_iota(jnp.int32, (t, t), 0)
    jj = lax.broadcasted_iota(jnp.int32, (t, t), 1)
    mask = jj <= ii if d == 0 else jj >= ii
    tri_t = jnp.where(ii <= jj if d == 0 else ii >= jj, 1.0, 0.0).astype(F32)
    return mask, tri_t


def _gate_rows(la_r, lw_r, tri_t, m_prev):
    cum_r = _hdot(la_r, tri_t)
    last = jnp.sum(la_r, axis=1, keepdims=True)
    g_r = last - cum_r + lw_r
    m_loc = jnp.max(g_r, axis=1, keepdims=True)
    e_r = jnp.exp(g_r - m_loc)
    m_new = jnp.maximum(last + m_prev, m_loc)
    s_old = jnp.exp(last + m_prev - m_new)
    s_new = jnp.exp(m_loc - m_new)
    return cum_r, cum_r - lw_r, e_r, m_new, s_old, s_new


def _head_probs(cum_row, crow_row, m_prev_h, mask, qk):
    t = qk.shape[0]
    colb = jnp.broadcast_to(cum_row, (t, t)).T
    dlog = jnp.where(mask, colb - crow_row, -jnp.inf)
    inter = colb[:, 0:1] + m_prev_h
    m_row = jnp.maximum(inter, jnp.max(dlog, axis=1, keepdims=True))
    p = jnp.exp(dlog - m_row) * qk
    return p.astype(BF16), jnp.exp(inter - m_row), m_row


def _ssd_scan_kernel(*refs, heads, groups, hd, ds):
    j = pl.program_id(1)
    ins, outs, states = refs[:12], refs[12:14], refs[14:16]

    @pl.when(j == 0)
    def _():
        for st_ref in states:
            st_ref[...] = jnp.zeros_like(st_ref)

    for sub in range(SCAN_CHUNKS):
        for d in range(2):
            _ssd_chunk(d, _scan_rows(d, sub), *ins[6 * d:6 * d + 6], outs[d], states[d],
                       heads=heads, groups=groups, hd=hd, ds=ds)


def _scan_rows(d, sub):
    first = (sub if d == 0 else SCAN_CHUNKS - 1 - sub) * CHUNK
    return slice(first, first + CHUNK)


def _ssd_chunk(d, rs, xs_ref, b_ref, c_ref, dt_ref, bias_ref, alog_ref, o_ref, st_ref, *, heads, groups, hd, ds):
    t = CHUNK
    hpg = heads // groups
    pw = 2 * hd
    mask, tri_t = _scan_masks(d, t)
    dt = _softplus(dt_ref[:, rs] + bias_ref[...])
    la_r = -dt * jnp.exp(alog_ref[...])
    cum_r = _hdot(la_r, tri_t)
    last = jnp.sum(la_r, axis=1, keepdims=True)
    crow = cum_r - jnp.log(dt)
    e_r = jnp.exp(last - crow)
    e_last = jnp.exp(last)
    lo = lax.broadcasted_iota(jnp.int32, (1, pw), 1) < hd

    for g in range(groups):
        q = c_ref[rs, g * ds:(g + 1) * ds]
        k_t = b_ref[rs, g * ds:(g + 1) * ds].astype(F32).T
        qk = _dot(q, k_t.astype(BF16))
        w0 = g * hpg * hd
        qs = _dot(q, st_ref[:, w0:w0 + hpg * hd].astype(BF16))
        for i in range(hpg // 2):
            h0 = g * hpg + 2 * i
            c0 = h0 * hd
            vp = xs_ref[rs, c0:c0 + pw]
            zero = jnp.zeros_like(vp)
            v_bd = jnp.concatenate([jnp.where(lo, vp, zero), jnp.where(lo, zero, vp)], axis=0)
            probs, carry, kte = [], [], []
            for h in (h0, h0 + 1):
                colb = jnp.broadcast_to(cum_r[h:h + 1], (t, t)).T
                dlog = jnp.where(mask, colb - crow[h:h + 1], -jnp.inf)
                probs.append((jnp.exp(dlog) * qk).astype(BF16))
                carry.append(jnp.exp(colb))
                kte.append((k_t * e_r[h:h + 1]).astype(BF16))
            lhs = jnp.concatenate([jnp.concatenate(probs, axis=1), jnp.concatenate(kte, axis=1)], axis=0)
            res = _dot(lhs, v_bd)
            y = res[:t] + qs[:, 2 * i * hd:2 * i * hd + pw] * jnp.where(lo, carry[0], carry[1])
            o_ref[rs, c0:c0 + pw] = y.astype(o_ref.dtype)
            decay = jnp.where(lo, e_last[h0:h0 + 1], e_last[h0 + 1:h0 + 2])
            st_ref[:, c0:c0 + pw] = decay * st_ref[:, c0:c0 + pw] + res[t:]


def _ssd_scan(xbc, dt_r, bias_c, alog_c, heads, hd, ds, n_lat_ch, n_ch):
    nb, s, _ = xbc.shape
    inner = heads * hd
    gn = SSD_GROUPS * ds
    assert 2 * hd == CHUNK and (heads // SSD_GROUPS) % 2 == 0 and inner % gn == 0
    rows = SCAN_CHUNKS * CHUNK
    n_blk = n_ch // SCAN_CHUNKS
    cidx = _chunk_index(n_lat_ch // SCAN_CHUNKS, n_blk)

    def specs(d):
        return [pl.BlockSpec((None, rows, inner), lambda b, j: (b, cidx(d, j), 0)),
                pl.BlockSpec((None, rows, gn), lambda b, j: (b, cidx(d, j), inner // gn)),
                pl.BlockSpec((None, rows, gn), lambda b, j: (b, cidx(d, j), inner // gn + 1)),
                pl.BlockSpec((None, None, heads, rows), lambda b, j: (d, b, 0, cidx(d, j))),
                pl.BlockSpec((None, heads, 1), lambda b, j: (d, 0, 0)),
                pl.BlockSpec((None, heads, 1), lambda b, j: (d, 0, 0))]

    out = lambda d: pl.BlockSpec((None, rows, inner), lambda b, j: (b, cidx(d, j), 0))
    args = (xbc, xbc, xbc, dt_r, bias_c, alog_c)
    return pl.pallas_call(
        functools.partial(_ssd_scan_kernel, heads=heads, groups=SSD_GROUPS, hd=hd, ds=ds),
        out_shape=(jax.ShapeDtypeStruct((nb, s, inner), BF16),) * 2,
        grid=(nb, n_blk),
        in_specs=specs(0) + specs(1),
        out_specs=(out(0), out(1)),
        scratch_shapes=[pltpu.VMEM((ds, inner), F32)] * 2,
        compiler_params=_params(("parallel", "arbitrary")),
        name="ssd_scan",
    )(*args, *args)


def _ml_scan_kernel(*refs, heads):
    j = pl.program_id(1)
    ins, outs, states = refs[:14], refs[14:16], refs[16:20]

    @pl.when(j == 0)
    def _():
        for ref in states:
            ref[...] = jnp.zeros_like(ref)

    for sub in range(SCAN_CHUNKS):
        for d in range(2):
            _ml_chunk(d, _scan_rows(d, sub), *ins[7 * d:7 * d + 7], outs[d], *states[2 * d:2 * d + 2],
                      heads=heads)


def _ml_chunk(d, rs, q_ref, k_ref, v_ref, gi_ref, gf_ref, bi_ref, bf_ref, o_ref, st_ref, m_ref, *, heads):
    t = CHUNK
    dh = ML_HEADDIM
    mask, tri_t = _scan_masks(d, t)
    lw_r = gi_ref[:, rs] + bi_ref[...]
    f = gf_ref[:, rs] + bf_ref[...]
    la_r = jnp.minimum(f, 0.0) - jnp.log(1.0 + jnp.exp(-jnp.abs(f)))
    m_prev = m_ref[:, 0:1]
    cum_r, crow, e_r, m_new, s_old, s_new = _gate_rows(la_r, lw_r, tri_t, m_prev)
    ones = jnp.ones((t, dh), BF16)

    for h in range(heads):
        c0 = h * dh
        q = q_ref[rs, c0:c0 + dh]
        k_t = k_ref[rs, c0:c0 + dh].astype(F32).T
        v_aug = jnp.concatenate([v_ref[rs, c0:c0 + dh], ones], axis=1)
        qk = _dot(q, k_t.astype(BF16))
        s0 = 2 * c0
        qs = _dot(q, st_ref[:, s0:s0 + 2 * dh].astype(BF16))
        p, cf, m_row = _head_probs(cum_r[h:h + 1], crow[h:h + 1], m_prev[h:h + 1], mask, qk)
        res = _dot(jnp.concatenate([p, (k_t * e_r[h:h + 1]).astype(BF16)], axis=0), v_aug)
        y = res[:t] + qs * cf
        cell = y[:, :dh] / jnp.maximum(jnp.abs(y[:, dh:]), jnp.exp(-m_row))
        o_ref[rs, c0:c0 + dh] = cell.astype(o_ref.dtype)
        st_ref[:, s0:s0 + 2 * dh] = s_old[h:h + 1] * st_ref[:, s0:s0 + 2 * dh] + s_new[h:h + 1] * res[t:]
    m_ref[...] = jnp.broadcast_to(m_new, m_ref.shape)


def _ml_scan(qk, pml, gi_r, gf_r, bi_c, bf_c, heads, n_lat_ch, n_ch):
    nb, s, _ = qk.shape
    inner = heads * ML_HEADDIM
    rows = SCAN_CHUNKS * CHUNK
    n_blk = n_ch // SCAN_CHUNKS
    cidx = _chunk_index(n_lat_ch // SCAN_CHUNKS, n_blk)

    def specs(d):
        gspec = pl.BlockSpec((None, None, heads, rows), lambda b, j: (d, b, 0, cidx(d, j)))
        bspec = pl.BlockSpec((None, heads, 1), lambda b, j: (d, 0, 0))
        col = lambda c: pl.BlockSpec((None, rows, inner), lambda b, j: (b, cidx(d, j), c))
        return [col(0), col(1), col(2), gspec, gspec, bspec, bspec]

    out = lambda d: pl.BlockSpec((None, rows, inner), lambda b, j: (b, cidx(d, j), 0))
    args = (qk, qk, pml, gi_r, gf_r, bi_c, bf_c)
    return pl.pallas_call(
        functools.partial(_ml_scan_kernel, heads=heads),
        out_shape=(jax.ShapeDtypeStruct((nb, s, inner), BF16),) * 2,
        grid=(nb, n_blk),
        in_specs=specs(0) + specs(1),
        out_specs=(out(0), out(1)),
        scratch_shapes=[pltpu.VMEM((ML_HEADDIM, 2 * inner), F32), pltpu.VMEM((heads, 128), F32)] * 2,
        compiler_params=_params(("parallel", "arbitrary")),
        name="ml_scan",
    )(*args, *args)


def _ml_out_kernel(cf_ref, cb_ref, o_in_ref, nw_ref, perm_ref, *rest, heads, permute):
    o_ref = rest[-1]
    dh = ML_HEADDIM
    parts = []
    for h in range(heads):
        sl = slice(h * dh, (h + 1) * dh)
        c = cf_ref[:, sl].astype(F32) + cb_ref[:, sl].astype(F32)
        c = c * lax.rsqrt(jnp.mean(c * c, axis=-1, keepdims=True) + EPS) * nw_ref[:, sl]
        parts.append((jax.nn.sigmoid(o_in_ref[:, sl].astype(F32)) * c).astype(BF16))
    y = jnp.concatenate(parts, axis=1)
    y = _dot(perm_ref[...], y) if permute else y.astype(F32)
    o_ref[...] = y.reshape(o_ref.shape)


def _ml_out(cf, cb, pml, norm_w, layer, heads, rows_g, n_lat_tiles, n_tiles):
    nb, _, inner = cf.shape
    wpt = TOKEN_TILE // rows_g
    rpt = TOKEN_TILE // GRID_W
    perm_t = jnp.asarray(_grid_perm(rows_g).T, BF16)
    out_shape = jax.ShapeDtypeStruct((nb, n_tiles * rpt, GRID_W, inner), F32)

    def call(tile0, tiles, out_spec, permute, prev):
        row = pl.BlockSpec((None, TOKEN_TILE, inner), lambda b, i: (b, tile0 + i, 0))
        in_specs = [row, row,
                    pl.BlockSpec((None, TOKEN_TILE, inner), lambda b, i: (b, tile0 + i, 3)),
                    pl.BlockSpec((None, 1, inner), lambda b, i: (layer, 0, 0)),
                    pl.BlockSpec((TOKEN_TILE, TOKEN_TILE), lambda b, i: (0, 0))]
        args = [cf, cb, pml, norm_w, perm_t]
        if prev is not None:
            in_specs.append(pl.BlockSpec(memory_space=pl.ANY))
            args.append(prev)
        return pl.pallas_call(
            functools.partial(_ml_out_kernel, heads=heads, permute=permute),
            out_shape=out_shape,
            grid=(nb, tiles),
            in_specs=in_specs,
            out_specs=out_spec,
            input_output_aliases={} if prev is None else {len(args) - 1: 0},
            compiler_params=_params(("parallel", "parallel")),
            name="ml_out_lat" if permute else "ml_out_ctx",
        )(*args)

    y = call(0, n_lat_tiles, pl.BlockSpec((None, rows_g, wpt, inner), lambda b, i: (b, 0, i, 0)), True, None)
    if n_tiles > n_lat_tiles:
        y = call(n_lat_tiles, n_tiles - n_lat_tiles,
                 pl.BlockSpec((None, rpt, GRID_W, inner), lambda b, i: (b, n_lat_tiles + i, 0, 0)), False, y)
    return y.reshape(nb, n_tiles * TOKEN_TILE, inner)


def _dft_tables(length):
    m = length // 2
    k = np.arange(m, dtype=np.int64)
    ang = ((k[:, None] * k[None, :]) % (2 * m)).astype(np.float64) * (np.pi / m)
    alt8 = np.broadcast_to(np.where(k % 2 == 0, 1.0, -1.0)[None, :], (8, m))
    tw = k.astype(np.float64)[:, None] * (np.pi / length) * np.ones((1, LANE_TILE))
    return (jnp.asarray(np.cos(ang), BF16), jnp.asarray(np.sin(ang), BF16), jnp.asarray(alt8, BF16),
            jnp.asarray(np.cos(tw), F32), jnp.asarray(np.sin(tw), F32))


def _hy_feats(length):
    t = jnp.arange(length, dtype=F32)
    t_norm = t / length
    bands = jnp.linspace(1e-4, HY_BANDS - 1, HY_BANDS, dtype=F32)
    ang = (2.0 * math.pi / length) * t[:, None] * bands[None, :]
    feats = jnp.concatenate([t_norm[:, None], jnp.cos(ang), -jnp.sin(ang)], axis=-1)
    return feats[0::2], feats[1::2], t_norm[0::2, None], t_norm[1::2, None]


def _split_spectrum(ae, be, ao, bo, twc, tws):
    tr = twc * ao - tws * bo
    tm = twc * bo + tws * ao
    return (ae + tr, ae - tr), (be + tm, tm - be)


def _hy_filter_kernel(fe_ref, fo_ref, tne_ref, tno_ref, w1_ref, b1_ref, w2_ref, b2_ref, w3f_ref, w3b_ref,
                      df_ref, db_ref, cm_ref, sm_ref, alt_ref, twc_ref, tws_ref, ha_ref, hb_ref, hm_ref,
                      hide_ref, hido_ref):
    m = cm_ref.shape[0]

    @pl.when((pl.program_id(0) == 0) & (pl.program_id(1) == 0))
    def _():
        for f_ref, h_ref in ((fe_ref, hide_ref), (fo_ref, hido_ref)):
            hid = jnp.sin(_hdot(f_ref[...], w1_ref[...]) + b1_ref[...])
            h_ref[...] = jnp.sin(_hdot(hid, w2_ref[...]) + b2_ref[...])

    def taps(hid, tn):
        h_f = _hdot(hid, w3f_ref[...]) * jnp.exp(-tn * jnp.abs(df_ref[...]))
        h_b = _hdot(hid, w3b_ref[...]) * jnp.exp(-tn * jnp.abs(db_ref[...]))
        return (h_f + h_b).astype(BF16), (h_f - h_b).astype(BF16)

    sum_e, dif_e = taps(hide_ref[...], tne_ref[...])
    sum_o, dif_o = taps(hido_ref[...], tno_ref[...])
    twc, tws = twc_ref[...], tws_ref[...]
    cm, sm = cm_ref[...], sm_ref[...]
    (ha_lo, ha_hi), _ = _split_spectrum(_dot(cm, sum_e), 0.0, _dot(cm, sum_o), _dot(sm, sum_o), twc, tws)
    _, (hb_lo, hb_hi) = _split_spectrum(0.0, _dot(sm, dif_e), _dot(cm, dif_o), _dot(sm, dif_o), twc, tws)
    ha_ref[0:m, :] = ha_lo
    ha_ref[m:2 * m, :] = ha_hi
    hb_ref[0:m, :] = hb_lo
    hb_ref[m:2 * m, :] = hb_hi
    row = lax.broadcasted_iota(jnp.int32, hm_ref.shape, 0)
    hm_ref[...] = jnp.where(row == 0, _dot(alt_ref[...], sum_e), _dot(alt_ref[...], dif_o))


def _hy_filters(length, layer, w1, b1, w2, b2, w3, decay, tables):
    cm, sm, alt8, twc, tws = tables
    m = length // 2
    feats = _hy_feats(length)
    nfeat, nf = w1.shape[-2:]
    ch = decay.shape[-1]
    nct = ch // LANE_TILE
    const = lambda shape: pl.BlockSpec(shape, lambda n, c: (0,) * len(shape))
    w3spec = lambda dr: pl.BlockSpec((None, None, None, nf, LANE_TILE), lambda n, c: (layer, n, dr, 0, c))
    dspec = lambda dr: pl.BlockSpec((None, None, None, 1, LANE_TILE), lambda n, c: (layer, n, dr, 0, c))
    lay = lambda a, b: pl.BlockSpec((None, a, b), lambda n, c: (layer, 0, 0))
    out = lambda rows: pl.BlockSpec((None, rows, LANE_TILE), lambda n, c: (n, 0, c))
    return pl.pallas_call(
        _hy_filter_kernel,
        out_shape=(jax.ShapeDtypeStruct((HY_ORDER, length, ch), F32),
                   jax.ShapeDtypeStruct((HY_ORDER, length, ch), F32),
                   jax.ShapeDtypeStruct((HY_ORDER, 8, ch), F32)),
        grid=(HY_ORDER, nct),
        in_specs=[const((m, nfeat)), const((m, nfeat)), const((m, 1)), const((m, 1)),
                  lay(nfeat, nf), lay(1, nf), lay(nf, nf), lay(1, nf),
                  w3spec(0), w3spec(1), dspec(0), dspec(1),
                  const((m, m)), const((m, m)), const((8, m)),
                  const((m, LANE_TILE)), const((m, LANE_TILE))],
        out_specs=(out(length), out(length), out(8)),
        scratch_shapes=[pltpu.VMEM((m, nf), F32)] * 2,
        compiler_params=_params(("arbitrary", "arbitrary")),
        name="hyena_filters_%d" % length,
    )(*feats, w1, b1, w2, b2, w3, w3, decay, decay, cm, sm, alt8, twc, tws)


def _short_conv(ue, uo, w, b):
    m = ue.shape[0]
    row = lax.broadcasted_iota(jnp.int32, ue.shape, 0)
    prev_odd = jnp.where(row == 0, 0.0, pltpu.roll(uo, 1, 0))
    next_even = jnp.where(row == m - 1, 0.0, pltpu.roll(ue, m - 1, 0))
    w0, w1, w2 = w[0:1], w[1:2], w[2:3]
    return b + w0 * prev_odd + w1 * ue + w2 * uo, b + w0 * ue + w1 * uo + w2 * next_even


def _hy_conv_kernel(u_refs, wz_ref, bz_ref, wg_refs, bg_refs, ha_ref, hb_ref, hm_ref, skip_ref,
                    cm_ref, sm_ref, alt_ref, twc_ref, tws_ref, oe_ref, oo_ref):
    f32 = lambda ref: ref[...].astype(F32)
    ze, zo = _short_conv(f32(u_refs[0]), f32(u_refs[1]), wz_ref[...], bz_ref[...])
    for n in range(HY_ORDER):
        ge, go = _short_conv(f32(u_refs[2 * n + 2]), f32(u_refs[2 * n + 3]), wg_refs[n][...], bg_refs[n][...])
        ze, zo = _hy_order(ze, zo, ge, go, ha_ref[n], hb_ref[n], hm_ref[n], skip_ref[n],
                           cm_ref, sm_ref, alt_ref, twc_ref[...], tws_ref[...])
    oe_ref[...] = ze.astype(oe_ref.dtype)
    oo_ref[...] = zo.astype(oo_ref.dtype)


def _hy_order(ze, zo, ge, go, ha, hb, hm, skip, cm_ref, sm_ref, alt_ref, twc, tws):
    m, width = ze.shape
    zz = jnp.concatenate([ze.astype(BF16), zo.astype(BF16)], axis=1)
    a = _dot(cm_ref[...], zz)
    b = _dot(sm_ref[...], zz)
    mid = _dot(alt_ref[...], zz)[0:1]
    (a_lo, a_hi), (b_lo, b_hi) = _split_spectrum(a[:, :width], b[:, :width], a[:, width:], b[:, width:], twc, tws)

    ha_lo, ha_hi, hb_lo, hb_hi = ha[0:m, :], ha[m:2 * m, :], hb[0:m, :], hb[m:2 * m, :]
    yr_lo, ym_lo = a_lo * ha_lo - b_lo * hb_lo, a_lo * hb_lo + b_lo * ha_lo
    yr_hi, ym_hi = a_hi * ha_hi - b_hi * hb_hi, a_hi * hb_hi + b_hi * ha_hi
    ha_m, hb_m = hm[0:1, :], hm[1:2, :]
    yr_m = mid[:, :width] * ha_m - mid[:, width:] * hb_m
    ym_m = mid[:, :width] * hb_m + mid[:, width:] * ha_m

    row = lax.broadcasted_iota(jnp.int32, (m, width), 0)
    half0 = jnp.where(row == 0, 0.5, 1.0)
    qr, qm = yr_lo - yr_hi, ym_lo + ym_hi
    pr = jnp.concatenate([((yr_lo + yr_hi) * half0).astype(BF16),
                          ((qr * twc + qm * tws) * half0).astype(BF16)], axis=1)
    pm = jnp.concatenate([(ym_lo - ym_hi).astype(BF16), (qm * twc - qr * tws).astype(BF16)], axis=1)
    y = _dot(cm_ref[...], pr) + _dot(sm_ref[...], pm)
    alt = jnp.where((row & 1) == 0, 1.0, -1.0)
    scale = 1.0 / (2 * m)
    return (ge * ((y[:, :width] + alt * yr_m) * scale + skip * ze),
            go * ((y[:, width:] + alt * ym_m) * scale + skip * zo))


def _hy_conv(u, col0, ch, conv_w, conv_b, out_prev, out_rows, row_block, length, layer, ha, hb, hm, skip,
             tables):
    cm, sm, alt8, twc, tws = tables
    nb, s_in, ctot = u.shape
    u2 = u.reshape(nb, s_in // 2, 2 * ctot)
    m = length // 2
    nct = ch // LANE_TILE
    assert col0 % LANE_TILE == 0 and ctot % LANE_TILE == 0
    cb = col0 // LANE_TILE
    taps = conv_w.shape[0]
    assert taps == HY_SHORT == 3
    single = dict(pipeline_mode=pl.Buffered(1))
    const = lambda shape: pl.BlockSpec(shape, lambda c, b: (0,) * len(shape), **single)
    hspec = lambda rows: pl.BlockSpec((HY_ORDER, rows, LANE_TILE), lambda c, b: (0, 0, c), **single)

    def ucol(k, parity):
        first = parity * (ctot // LANE_TILE) + cb + k * nct
        return pl.BlockSpec((None, m, LANE_TILE), lambda c, b: (b, row_block, first + c))

    wcol = lambda rows, k: pl.BlockSpec((rows, LANE_TILE), lambda c, b: (0, k * nct + c))
    orders = range(1, HY_ORDER + 1)
    n_u = 2 * (1 + HY_ORDER)
    in_specs = ([ucol(k, p) for k in range(1 + HY_ORDER) for p in (0, 1)] + [wcol(taps, 0), wcol(1, 0)]
                + [wcol(taps, k) for k in orders] + [wcol(1, k) for k in orders]
                + [hspec(length), hspec(length), hspec(8),
                   pl.BlockSpec((None, HY_ORDER, 1, LANE_TILE), lambda c, b: (layer, 0, 0, c)),
                   const((m, m)), const((m, m)), const((8, m)), const((m, LANE_TILE)), const((m, LANE_TILE))])
    args = ([u2] * n_u + [conv_w, conv_b] + [conv_w] * HY_ORDER + [conv_b] * HY_ORDER
            + [ha, hb, hm, skip, cm, sm, alt8, twc, tws])
    n_in = len(args)
    aliases = {}
    if out_prev is not None:
        in_specs += [pl.BlockSpec(memory_space=pl.ANY)] * 2
        args += list(out_prev)
        aliases = {n_in: 0, n_in + 1: 1}

    def body(*refs):
        k = HY_ORDER
        rest = refs[n_u:]
        _hy_conv_kernel(refs[:n_u], rest[0], rest[1], rest[2:2 + k], rest[2 + k:2 + 2 * k],
                        *refs[n_u + 2 + 2 * k:n_in], *refs[-2:])

    out_spec = pl.BlockSpec((None, m, LANE_TILE), lambda c, b: (b, row_block, c))
    return pl.pallas_call(
        body,
        out_shape=(jax.ShapeDtypeStruct((nb, out_rows // 2, ch), BF16),) * 2,
        grid=(nct, nb),
        in_specs=in_specs,
        out_specs=(out_spec, out_spec),
        input_output_aliases=aliases,
        compiler_params=_params(("parallel", "parallel")),
        name="hyena_conv_%d" % length,
    )(*args)


def _rms(x, w):
    return x * lax.rsqrt(jnp.mean(x * x, axis=-1, keepdims=True) + EPS) * w


def _merge_mlp_kernel(yf_ref, yb_ref, xs_ref, z_ref, dsk_ref, sn_ref, ym_ref, yhe_ref, yho_ref, il_ref,
                      g0_ref, g1_ref, g2_ref, wb_ref, wo_ref, x_ref, xc_ref, mod_ref, nw_ref, w1_ref, w2_ref,
                      nf_ref, o_ref, *, final, n_lat_tiles):
    m = mod_ref[...]
    ys = yf_ref[...].astype(F32) + yb_ref[...].astype(F32) + dsk_ref[...] * xs_ref[...].astype(F32)
    ys = _rms(ys * _silu(z_ref[...].astype(F32)), sn_ref[...]).astype(BF16)
    yh = _dot(il_ref[...], jnp.concatenate([yhe_ref[...], yho_ref[...]], axis=0)).astype(BF16)
    acc = None
    for n, (y, g_ref) in enumerate(((ys, g0_ref), (ym_ref[...].astype(BF16), g1_ref), (yh, g2_ref))):
        term = jax.nn.sigmoid(g_ref[...].astype(F32)) * _dot(y, wb_ref[n])
        acc = term if acc is None else acc + term
    x_in = x_ref[...] if xc_ref is None else jnp.where(pl.program_id(1) < n_lat_tiles, x_ref[...], xc_ref[...])
    x = x_in + m[2:3] * _dot(acc.astype(BF16), wo_ref[...])
    h = (_rms(x, nw_ref[...]) * (1.0 + m[4:5]) + m[3:4]).astype(BF16)
    a = jnp.maximum(_dot(h, w1_ref[...]), 0.0)
    x = x + m[5:6] * _dot((a * a).astype(BF16), w2_ref[...])
    o_ref[...] = _rms(x, nf_ref[...]) if final else x


def _merge_mlp(yf, yb, xbc, pnat, d_full, ssd_norm, ym, yh, wb, wo, x, x_ctx, mod, norm_w, layer, w1, w2,
               norm_f, final, n_lat_tiles, n_tiles):
    nb, _, d = x.shape
    s = yf.shape[1]
    bw = yf.shape[-1]
    hidden = w1.shape[1]
    row = pl.BlockSpec((None, TOKEN_TILE, bw), lambda b, i: (b, i, 0))
    gate = lambda n: pl.BlockSpec((None, TOKEN_TILE, d), lambda b, i: (b, i, 1 + n))
    xrow = pl.BlockSpec((None, TOKEN_TILE, d), lambda b, i: (b, i, 0))
    lay = lambda n: pl.BlockSpec((None, 1, n), lambda b, i: (layer, 0, 0))
    single = dict(pipeline_mode=pl.Buffered(1))
    const = lambda shape: pl.BlockSpec(shape, lambda b, i: (0,) * len(shape), **single)
    half = pl.BlockSpec((None, TOKEN_TILE // 2, bw), lambda b, i: (b, i, 0))
    il = np.zeros((TOKEN_TILE, TOKEN_TILE), np.float32)
    il[np.arange(TOKEN_TILE), np.arange(TOKEN_TILE) // 2 + (np.arange(TOKEN_TILE) % 2) * (TOKEN_TILE // 2)] = 1.0
    in_specs = [row, row, row, row, lay(bw), lay(bw), row, half, half, const((TOKEN_TILE, TOKEN_TILE)),
                gate(0), gate(1), gate(2), const((N_BRANCH, bw, d)), const((d, d))]
    args = [yf, yb, xbc, pnat, d_full, ssd_norm, ym, yh[0], yh[1], jnp.asarray(il, BF16), pnat, pnat, pnat,
            wb, wo]
    aliases = {}
    if x_ctx is None:
        in_specs.append(xrow)
        args.append(x)
        if not final:
            aliases = {len(args) - 1: 0}
    else:
        in_specs += [pl.BlockSpec((None, TOKEN_TILE, d), lambda b, i: (b, jnp.minimum(i, n_lat_tiles - 1), 0)),
                     pl.BlockSpec((None, TOKEN_TILE, d), lambda b, i: (b, jnp.maximum(i - n_lat_tiles, 0), 0))]
        args += [x, x_ctx]
    in_specs += [pl.BlockSpec((None, 6, d), _mod_row(nb, n_lat_tiles)), lay(d),
                 const((d, hidden)), const((hidden, d)), pl.BlockSpec((1, d), lambda b, i: (0, 0))]
    args += [mod, norm_w, w1, w2, norm_f]

    def body(*refs):
        head, tail = refs[:16], refs[16:]
        xc = None if x_ctx is None else tail[0]
        _merge_mlp_kernel(*head, xc, *tail[0 if x_ctx is None else 1:], final=final, n_lat_tiles=n_lat_tiles)

    return pl.pallas_call(
        body,
        out_shape=jax.ShapeDtypeStruct((nb, n_tiles * TOKEN_TILE if final else s, d), F32),
        grid=(nb, n_tiles),
        in_specs=in_specs,
        out_specs=xrow,
        input_output_aliases=aliases,
        compiler_params=_params(("parallel", "parallel")),
        name="merge_mlp",
    )(*args)


def _dir_rows(g, nb, s, per_dir):
    g = g[:, :2 * per_dir].reshape(nb, s, 2, per_dir)
    return jnp.transpose(g, (2, 0, 3, 1))


def kernel(x, c, ctx, c_ctx, norm1_w, mod_w, mod_b, w_in, ssd_conv_w, ssd_conv_b, ssd_dt_bias, ssd_a_log,
           ssd_d, ssd_norm_w, ml_conv_w, ml_conv_b, ml_gate_b, ml_norm_w, hy_conv_w, hy_conv_b, hy_ffn_w1,
           hy_ffn_b1, hy_ffn_w2, hy_ffn_b2, hy_ffn_w3, hy_decay, hy_skip, w_branch, w_out, norm2_w,
           mlp_w1, mlp_w2, norm_f_w):
    nb, seq, d = x.shape
    ctx_len = ctx.shape[1]
    depth = w_in.shape[0]
    s = seq + ctx_len
    assert seq % ctx_len == 0 and ctx_len % TOKEN_TILE == 0 and seq % GRID_W == 0
    n_lat_tiles, n_tiles = seq // TOKEN_TILE, s // TOKEN_TILE
    n_lat_ch, n_ch = seq // CHUNK, s // CHUNK
    assert n_lat_ch % SCAN_CHUNKS == 0 and n_ch % SCAN_CHUNKS == 0
    rows_g = seq // GRID_W

    ssd_heads = ssd_d.shape[-1]
    ssd_inner = ssd_norm_w.shape[-1]
    ssd_hd = ssd_inner // ssd_heads
    ssd_conv_ch = ssd_conv_w.shape[-1]
    ssd_ds = (ssd_conv_ch - ssd_inner) // (2 * SSD_GROUPS)
    ml_heads = ml_gate_b.shape[-1]
    ml_inner = ml_heads * ML_HEADDIM
    hy_inner = hy_skip.shape[-1]
    ssd_cols = ssd_conv_ch + ssd_inner + 2 * ssd_heads
    ml_cols = 4 * ml_inner + 4 * ml_heads
    rec_cols = ssd_cols + ml_cols
    hy_cols = (HY_ORDER + 1) * hy_inner

    o_z = ssd_conv_ch
    o_dt = ssd_conv_ch + ssd_inner
    o_ml = ssd_cols
    o_mlg = ssd_cols + 4 * ml_inner
    o_hy = rec_cols
    o_g = rec_cols + hy_cols

    xa, xa_ctx = x, ctx
    rpad = (-(nb + 1)) % 8
    c_all = jnp.concatenate([c, c_ctx[None], jnp.zeros((rpad, d), F32)], axis=0)

    tab_lat = _dft_tables(seq)
    tab_ctx = _dft_tables(ctx_len)
    k_scale = jnp.concatenate([jnp.ones((1, ml_inner), F32),
                               jnp.full((1, ml_inner), ML_HEADDIM ** -0.5, F32)], axis=1)
    ones_row = lambda n: jnp.ones((1, n), F32)

    norm1 = norm1_w[:, None, :]
    norm2 = norm2_w[:, None, :]
    ssd_norm = ssd_norm_w[:, None, :]
    ml_norm = ml_norm_w[:, None, :]
    d_full = jnp.repeat(ssd_d, ssd_hd, axis=-1)[:, None, :]
    w3 = hy_ffn_w3.reshape(depth, hy_ffn_w3.shape[1], HY_ORDER, 2, hy_inner).transpose(0, 2, 3, 1, 4)
    decay = hy_decay[:, :, :, None, :]
    skip = hy_skip[:, :, None, :]
    hb1 = hy_ffn_b1[:, None, :]
    hb2 = hy_ffn_b2[:, None, :]

    for l in range(depth):
        need_ctx = l < depth - 1
        used_tiles = n_tiles if need_ctx else n_lat_tiles
        mod = _mod_vectors(c_all, mod_w, mod_b[:, None, :], l).reshape(-1, 6, d)

        wl = w_in[l].astype(BF16)
        w_nat = jnp.concatenate([wl[:, o_z:o_dt], wl[:, o_g:], wl[:, o_hy:o_g]], axis=1)
        w_xbc = wl[:, :o_z]
        w_ml = wl[:, o_ml:o_mlg]
        gpad = lambda w: jnp.pad(w, ((0, 0), (0, 128 - w.shape[1])))
        hn, hn_cm = _normmod(xa, xa_ctx, norm1, l, mod, 0, rows_g, n_lat_tiles, n_tiles)
        hn2 = hn.reshape(nb * s, d)
        hn_cm2 = hn_cm.reshape(nb * s, d)
        pnat = _matmul(hn2, w_nat, BF16, "proj_nat").reshape(nb, s, -1)
        pml, p_mlg = _matmul(hn_cm2, w_ml, BF16, "proj_ml", gpad(wl[:, o_mlg:o_hy]))
        pxbc, p_dt = _matmul(hn2, w_xbc, BF16, "proj_xbc", gpad(wl[:, o_dt:o_ml]))
        pml = pml.reshape(nb, s, -1)
        pxbc = pxbc.reshape(nb, s, -1)
        c_g = ssd_inner
        c_hy = c_g + N_BRANCH * d

        xbc = _dwconv(pxbc, 0, ssd_conv_ch, ssd_conv_w[l], ssd_conv_b[l][None], ones_row(ssd_conv_ch),
                      True, n_lat_tiles, n_tiles, "ssd_conv")
        dt_r = _dir_rows(p_dt, nb, s, ssd_heads)
        y_f, y_b = _ssd_scan(xbc, dt_r, ssd_dt_bias[l][:, :, None], ssd_a_log[l][:, :, None],
                             ssd_heads, ssd_hd, ssd_ds, n_lat_ch, n_ch)

        qk = _dwconv(pml, 0, 2 * ml_inner, ml_conv_w[l], ml_conv_b[l][None], k_scale,
                     True, n_lat_tiles, n_tiles, "ml_conv")
        g_r = _dir_rows(p_mlg, nb, s, 2 * ml_heads)
        gate_b = ml_gate_b[l]
        c_f, c_b = _ml_scan(qk, pml, g_r[:, :, :ml_heads], g_r[:, :, ml_heads:],
                            gate_b[:, 0, :, None], gate_b[:, 1, :, None], ml_heads, n_lat_ch, n_ch)
        ym = _ml_out(c_f, c_b, pml, ml_norm, l, ml_heads, rows_g, n_lat_tiles, used_tiles)

        hcw, hcb = hy_conv_w[l], hy_conv_b[l][None]
        fl = _hy_filters(seq, l, hy_ffn_w1, hb1, hy_ffn_w2, hb2, w3, decay, tab_lat)
        rows = s if need_ctx else seq
        yh = _hy_conv(pnat, c_hy, hy_inner, hcw, hcb, None, rows, 0, seq, l, *fl, skip, tab_lat)
        if need_ctx:
            fc = _hy_filters(ctx_len, l, hy_ffn_w1, hb1, hy_ffn_w2, hb2, w3, decay, tab_ctx)
            yh = _hy_conv(pnat, c_hy, hy_inner, hcw, hcb, yh, rows, seq // ctx_len, ctx_len, l, *fc, skip,
                          tab_ctx)

        xa = _merge_mlp(y_f, y_b, xbc, pnat, d_full, ssd_norm, ym, yh, w_branch[l].astype(BF16),
                        w_out[l].astype(BF16), xa, xa_ctx, mod, norm2, l, mlp_w1[l].astype(BF16),
                        mlp_w2[l].astype(BF16), norm_f_w[None], not need_ctx, n_lat_tiles, used_tiles)
        xa_ctx = None

    return xa
```

```python
import functools
import math

import jax
import jax.numpy as jnp
import numpy as np
from jax import lax
from jax.experimental import pallas as pl
from jax.experimental.pallas import tpu as pltpu

F32 = jnp.float32
BF16 = jnp.bfloat16
HIGHEST = lax.Precision.HIGHEST

GRID_W = 64
CHUNK = 128
EPS = 1e-6
SSD_GROUPS = 2
SSD_CONV = 5
ML_HEADDIM = 128
ML_CONV = 5
HY_ORDER = 2
HY_SHORT = 3
HY_BANDS = 16
N_BRANCH = 3

TOKEN_TILE = 256
LANE_TILE = 256
HALO = 16
SCAN_CHUNKS = 2
VMEM_LIMIT = 56 * 1024 * 1024

_hdot = functools.partial(jnp.dot, precision=HIGHEST, preferred_element_type=F32)
_dot = functools.partial(jnp.dot, preferred_element_type=F32)


def _params(sem, vmem=None):
    return pltpu.CompilerParams(dimension_semantics=sem, vmem_limit_bytes=vmem or VMEM_LIMIT)


def _softplus(x):
    return jnp.maximum(x, 0.0) + jnp.log(1.0 + jnp.exp(-jnp.abs(x)))


def _silu(x):
    return x * jax.nn.sigmoid(x)


def _mod_kernel(c_ref, w_ref, b_ref, o_ref):
    o_ref[...] = _hdot(_silu(c_ref[...]), w_ref[...]) + b_ref[...]


def _mod_vectors(c_all, mod_w, mod_b, layer):
    r, d = c_all.shape
    n = mod_w.shape[-1]
    tn = n // 6
    return pl.pallas_call(
        _mod_kernel,
        out_shape=jax.ShapeDtypeStruct((r, n), F32),
        grid=(n // tn,),
        in_specs=[pl.BlockSpec((r, d), lambda j: (0, 0)),
                  pl.BlockSpec((None, d, tn), lambda j: (layer, 0, j)),
                  pl.BlockSpec((None, 1, tn), lambda j: (layer, 0, j))],
        out_specs=pl.BlockSpec((r, tn), lambda j: (0, j)),
        compiler_params=_params(("parallel",)),
        name="mod_vectors",
    )(c_all, mod_w, mod_b)


def _grid_perm(rows_g):
    wpt = TOKEN_TILE // rows_g
    src = np.arange(TOKEN_TILE).reshape(rows_g, wpt).T.reshape(-1)
    p = np.zeros((TOKEN_TILE, TOKEN_TILE), np.float32)
    p[np.arange(TOKEN_TILE), src] = 1.0
    return p


def _normmod_kernel(x_ref, xc_ref, x4_ref, perm_ref, nw_ref, mod_ref, o_ref, ocm_ref, *, si, n_lat_tiles):
    i = pl.program_id(1)
    m = mod_ref[...]

    def normed(x):
        h = x * lax.rsqrt(jnp.mean(x * x, axis=-1, keepdims=True) + EPS) * nw_ref[...]
        return (h * (1.0 + m[si + 1:si + 2]) + m[si:si + 1]).astype(BF16)

    h = normed(x_ref[...] if xc_ref is None else jnp.where(i < n_lat_tiles, x_ref[...], xc_ref[...]))
    o_ref[...] = h

    @pl.when(i < n_lat_tiles)
    def _():
        x4 = x4_ref[...]
        ocm_ref[...] = _dot(perm_ref[...], normed(x4.reshape(TOKEN_TILE, x4.shape[-1]))).astype(BF16)

    @pl.when(i >= n_lat_tiles)
    def _():
        ocm_ref[...] = h


def _mod_row(nb, n_lat_tiles):
    return lambda b, i: (jnp.where(i < n_lat_tiles, b, nb), 0, 0)


def _normmod(x, x_ctx, norm_w, layer, mod, si, rows_g, n_lat_tiles, n_tiles):
    nb, rows, d = x.shape
    s = n_tiles * TOKEN_TILE
    wpt = TOKEN_TILE // rows_g
    assert wpt % 8 == 0 and GRID_W % wpt == 0 and rows % GRID_W == 0
    x4 = x.reshape(nb, rows // GRID_W, GRID_W, d)
    tile = pl.BlockSpec((None, TOKEN_TILE, d), lambda b, i: (b, i, 0))
    if x_ctx is None:
        srcs, src_specs = [x], [tile]
    else:
        srcs = [x, x_ctx]
        src_specs = [pl.BlockSpec((None, TOKEN_TILE, d), lambda b, i: (b, jnp.minimum(i, n_lat_tiles - 1), 0)),
                     pl.BlockSpec((None, TOKEN_TILE, d), lambda b, i: (b, jnp.maximum(i - n_lat_tiles, 0), 0))]

    def body(*refs):
        xc = None if x_ctx is None else refs[1]
        _normmod_kernel(refs[0], xc, *refs[len(srcs):], si=si, n_lat_tiles=n_lat_tiles)

    return pl.pallas_call(
        body,
        out_shape=(jax.ShapeDtypeStruct((nb, s, d), BF16),) * 2,
        grid=(nb, n_tiles),
        in_specs=src_specs + [
            pl.BlockSpec((None, rows_g, wpt, d), lambda b, i: (b, 0, jnp.minimum(i, n_lat_tiles - 1), 0)),
            pl.BlockSpec((TOKEN_TILE, TOKEN_TILE), lambda b, i: (0, 0)),
            pl.BlockSpec((None, 1, d), lambda b, i: (layer, 0, 0)),
            pl.BlockSpec((None, 6, d), _mod_row(nb, n_lat_tiles))],
        out_specs=(tile, tile),
        compiler_params=_params(("parallel", "parallel")),
        name="normmod",
    )(*srcs, x4, jnp.asarray(_grid_perm(rows_g), BF16), norm_w, mod)


def _mm_kernel(a_ref, w_ref, o_ref):
    o_ref[...] = _dot(a_ref[...], w_ref[...]).astype(o_ref.dtype)


def _pick(n, cands):
    for c in cands:
        if n % c == 0:
            return c
    return n


def _mm2_kernel(a_ref, w_ref, wg_ref, o_ref, og_ref):
    o_ref[...] = _dot(a_ref[...], w_ref[...]).astype(o_ref.dtype)

    @pl.when(pl.program_id(1) == 0)
    def _():
        og_ref[...] = _dot(a_ref[...], wg_ref[...])


def _matmul(a, w, out_dtype, name, w_gate=None):
    t, k = a.shape
    n = w.shape[1]
    tm = _pick(t, (2048, 1024, 768, 512, 256))
    tn = _pick(n, (1024, 512, 256, 128))
    in_specs = [pl.BlockSpec((tm, k), lambda i, j: (i, 0)),
                pl.BlockSpec((k, tn), lambda i, j: (0, j))]
    out_spec = pl.BlockSpec((tm, tn), lambda i, j: (i, j))
    out_shape = jax.ShapeDtypeStruct((t, n), out_dtype)
    if w_gate is None:
        return pl.pallas_call(
            _mm_kernel, out_shape=out_shape, grid=(t // tm, n // tn), in_specs=in_specs, out_specs=out_spec,
            compiler_params=_params(("parallel", "parallel")), name=name,
        )(a, w)
    ng = w_gate.shape[1]
    return pl.pallas_call(
        _mm2_kernel,
        out_shape=(out_shape, jax.ShapeDtypeStruct((t, ng), F32)),
        grid=(t // tm, n // tn),
        in_specs=in_specs + [pl.BlockSpec((k, ng), lambda i, j: (0, 0))],
        out_specs=(out_spec, pl.BlockSpec((tm, ng), lambda i, j: (i, 0))),
        compiler_params=_params(("parallel", "arbitrary")),
        name=name,
    )(a, w, w_gate)


def _shift_stack(taps):
    pad = taps // 2
    return np.concatenate([np.eye(TOKEN_TILE, k=j - pad, dtype=np.float32) for j in range(taps) if j != pad])


def _dwconv_kernel(u_ref, p_ref, n_ref, sh_ref, w_ref, b_ref, *rest, taps, act, bounds):
    o_ref = rest[-1]
    s_ref = rest[0] if len(rest) == 2 else None
    i = pl.program_id(1)
    tm = u_ref.shape[0]
    pad = taps // 2
    lv, rv = jnp.float32(1.0), jnp.float32(1.0)
    for e in bounds:
        lv = jnp.where(i == e, 0.0, lv)
        rv = jnp.where(i == e - 1, 0.0, rv)
    w = w_ref[...]

    def finish(acc):
        if act:
            acc = _silu(acc)
        return (acc if s_ref is None else acc * s_ref[...]).astype(o_ref.dtype)

    def edge(window, first):
        rows = window.shape[0]
        acc = b_ref[...] + window[first:first + HALO] * w[pad:pad + 1]
        for j in range(taps):
            if j != pad:
                acc = acc + pltpu.roll(window, (pad - j) % rows, 0)[first:first + HALO] * w[j:j + 1]
        return finish(acc)

    u = u_ref[...]
    shifted = _dot(sh_ref[...], u)
    acc = b_ref[...] + u.astype(F32) * w[pad:pad + 1]
    blk = 0
    for j in range(taps):
        if j != pad:
            acc = acc + shifted[blk * tm:(blk + 1) * tm] * w[j:j + 1]
            blk += 1
    o_ref[...] = finish(acc)
    head = u_ref[0:2 * HALO, :].astype(F32)
    tail = u_ref[tm - 2 * HALO:tm, :].astype(F32)
    o_ref[0:HALO, :] = edge(jnp.concatenate([p_ref[...].astype(F32) * lv, head], axis=0), HALO)
    o_ref[tm - HALO:tm, :] = edge(jnp.concatenate([tail, n_ref[...].astype(F32) * rv], axis=0), HALO)


def _dwconv(u, col0, ncols, w, b, scale, act, n_lat_tiles, n_tiles, name):
    nb, s, _ = u.shape
    taps = w.shape[0]
    tc = _pick(ncols, (2048, 1536, 1024, 512, 256))
    assert col0 % tc == 0
    c0 = col0 // tc
    hb = TOKEN_TILE // HALO
    last = s // HALO - 1
    bounds = (0, n_lat_tiles, n_tiles)
    return pl.pallas_call(
        functools.partial(_dwconv_kernel, taps=taps, act=act, bounds=bounds),
        out_shape=jax.ShapeDtypeStruct((nb, s, ncols), BF16),
        grid=(nb, n_tiles, ncols // tc),
        in_specs=[pl.BlockSpec((None, TOKEN_TILE, tc), lambda bb, i, c: (bb, i, c0 + c)),
                  pl.BlockSpec((None, HALO, tc), lambda bb, i, c: (bb, jnp.maximum(i * hb - 1, 0), c0 + c)),
                  pl.BlockSpec((None, HALO, tc), lambda bb, i, c: (bb, jnp.minimum((i + 1) * hb, last), c0 + c)),
                  pl.BlockSpec(((taps - 1) * TOKEN_TILE, TOKEN_TILE), lambda bb, i, c: (0, 0)),
                  pl.BlockSpec((taps, tc), lambda bb, i, c: (0, c)),
                  pl.BlockSpec((1, tc), lambda bb, i, c: (0, c))]
                 + ([] if scale is None else [pl.BlockSpec((1, tc), lambda bb, i, c: (0, c))]),
        out_specs=pl.BlockSpec((None, TOKEN_TILE, tc), lambda bb, i, c: (bb, i, c)),
        compiler_params=_params(("parallel", "parallel", "parallel")),
        name=name,
    )(u, u, u, jnp.asarray(_shift_stack(taps), BF16), w, b, *([] if scale is None else [scale]))


def _chunk_index(n_lat_ch, n_ch):
    return lambda d, j: (j + n_lat_ch) % n_ch if d == 0 else n_ch - 1 - j


def _scan_masks(d, t):
    ii = lax.broadcasted_iota(jnp.int32, (t, t), 0)
    jj = lax.broadcasted_iota(jnp.int32, (t, t), 1)
    mask = jj <= ii if d == 0 else jj >= ii
    tri_t = jnp.where(ii <= jj if d == 0 else ii >= jj, 1.0, 0.0).astype(F32)
    return mask, tri_t


def _gate_rows(la_r, lw_r, tri_t, m_prev):
    cum_r = _hdot(la_r, tri_t)
    last = jnp.sum(la_r, axis=1, keepdims=True)
    g_r = last - cum_r + lw_r
    m_loc = jnp.max(g_r, axis=1, keepdims=True)
    e_r = jnp.exp(g_r - m_loc)
    m_new = jnp.maximum(last + m_prev, m_loc)
    s_old = jnp.exp(last + m_prev - m_new)
    s_new = jnp.exp(m_loc - m_new)
    return cum_r, cum_r - lw_r, e_r, m_new, s_old, s_new


def _head_probs(cum_row, crow_row, m_prev_h, mask, qk):
    t = qk.shape[0]
    colb = jnp.broadcast_to(cum_row, (t, t)).T
    dlog = jnp.where(mask, colb - crow_row, -jnp.inf)
    inter = colb[:, 0:1] + m_prev_h
    m_row = jnp.maximum(inter, jnp.max(dlog, axis=1, keepdims=True))
    p = jnp.exp(dlog - m_row) * qk
    return p.astype(BF16), jnp.exp(inter - m_row), m_row


def _ssd_scan_kernel(*refs, heads, groups, hd, ds):
    j = pl.program_id(1)
    ins, outs, states = refs[:12], refs[12:14], refs[14:16]

    @pl.when(j == 0)
    def _():
        for st_ref in states:
            st_ref[...] = jnp.zeros_like(st_ref)

    for sub in range(SCAN_CHUNKS):
        for d in range(2):
            _ssd_chunk(d, _scan_rows(d, sub), *ins[6 * d:6 * d + 6], outs[d], states[d],
                       heads=heads, groups=groups, hd=hd, ds=ds)


def _scan_rows(d, sub):
    first = (sub if d == 0 else SCAN_CHUNKS - 1 - sub) * CHUNK
    return slice(first, first + CHUNK)


def _ssd_chunk(d, rs, xs_ref, b_ref, c_ref, dt_ref, bias_ref, alog_ref, o_ref, st_ref, *, heads, groups, hd, ds):
    t = CHUNK
    hpg = heads // groups
    pw = 2 * hd
    mask, tri_t = _scan_masks(d, t)
    dt = _softplus(dt_ref[:, rs] + bias_ref[...])
    la_r = -dt * jnp.exp(alog_ref[...])
    cum_r = _hdot(la_r, tri_t)
    last = jnp.sum(la_r, axis=1, keepdims=True)
    crow = cum_r - jnp.log(dt)
    e_r = jnp.exp(last - crow)
    e_last = jnp.exp(last)
    lo = lax.broadcasted_iota(jnp.int32, (1, pw), 1) < hd

    for g in range(groups):
        q = c_ref[rs, g * ds:(g + 1) * ds]
        k_t = b_ref[rs, g * ds:(g + 1) * ds].astype(F32).T
        qk = _dot(q, k_t.astype(BF16))
        w0 = g * hpg * hd
        qs = _dot(q, st_ref[:, w0:w0 + hpg * hd].astype(BF16))
        for i in range(hpg // 2):
            h0 = g * hpg + 2 * i
            c0 = h0 * hd
            vp = xs_ref[rs, c0:c0 + pw]
            zero = jnp.zeros_like(vp)
            v_bd = jnp.concatenate([jnp.where(lo, vp, zero), jnp.where(lo, zero, vp)], axis=0)
            probs, carry, kte = [], [], []
            for h in (h0, h0 + 1):
                colb = jnp.broadcast_to(cum_r[h:h + 1], (t, t)).T
                dlog = jnp.where(mask, colb - crow[h:h + 1], -jnp.inf)
                probs.append((jnp.exp(dlog) * qk).astype(BF16))
                carry.append(jnp.exp(colb))
                kte.append((k_t * e_r[h:h + 1]).astype(BF16))
            lhs = jnp.concatenate([jnp.concatenate(probs, axis=1), jnp.concatenate(kte, axis=1)], axis=0)
            res = _dot(lhs, v_bd)
            y = res[:t] + qs[:, 2 * i * hd:2 * i * hd + pw] * jnp.where(lo, carry[0], carry[1])
            o_ref[rs, c0:c0 + pw] = y.astype(o_ref.dtype)
            decay = jnp.where(lo, e_last[h0:h0 + 1], e_last[h0 + 1:h0 + 2])
            st_ref[:, c0:c0 + pw] = decay * st_ref[:, c0:c0 + pw] + res[t:]


def _ssd_scan(xbc, dt_r, bias_c, alog_c, heads, hd, ds, n_lat_ch, n_ch):
    nb, s, _ = xbc.shape
    inner = heads * hd
    gn = SSD_GROUPS * ds
    assert 2 * hd == CHUNK and (heads // SSD_GROUPS) % 2 == 0 and inner % gn == 0
    rows = SCAN_CHUNKS * CHUNK
    n_blk = n_ch // SCAN_CHUNKS
    cidx = _chunk_index(n_lat_ch // SCAN_CHUNKS, n_blk)

    def specs(d):
        return [pl.BlockSpec((None, rows, inner), lambda b, j: (b, cidx(d, j), 0)),
                pl.BlockSpec((None, rows, gn), lambda b, j: (b, cidx(d, j), inner // gn)),
                pl.BlockSpec((None, rows, gn), lambda b, j: (b, cidx(d, j), inner // gn + 1)),
                pl.BlockSpec((None, None, heads, rows), lambda b, j: (d, b, 0, cidx(d, j))),
                pl.BlockSpec((None, heads, 1), lambda b, j: (d, 0, 0)),
                pl.BlockSpec((None, heads, 1), lambda b, j: (d, 0, 0))]

    out = lambda d: pl.BlockSpec((None, rows, inner), lambda b, j: (b, cidx(d, j), 0))
    args = (xbc, xbc, xbc, dt_r, bias_c, alog_c)
    return pl.pallas_call(
        functools.partial(_ssd_scan_kernel, heads=heads, groups=SSD_GROUPS, hd=hd, ds=ds),
        out_shape=(jax.ShapeDtypeStruct((nb, s, inner), BF16),) * 2,
        grid=(nb, n_blk),
        in_specs=specs(0) + specs(1),
        out_specs=(out(0), out(1)),
        scratch_shapes=[pltpu.VMEM((ds, inner), F32)] * 2,
        compiler_params=_params(("parallel", "arbitrary")),
        name="ssd_scan",
    )(*args, *args)


def _ml_scan_kernel(*refs, heads):
    j = pl.program_id(1)
    ins, outs, states = refs[:14], refs[14:16], refs[16:20]

    @pl.when(j == 0)
    def _():
        for ref in states:
            ref[...] = jnp.zeros_like(ref)

    for sub in range(SCAN_CHUNKS):
        for d in range(2):
            _ml_chunk(d, _scan_rows(d, sub), *ins[7 * d:7 * d + 7], outs[d], *states[2 * d:2 * d + 2],
                      heads=heads)


def _ml_chunk(d, rs, q_ref, k_ref, v_ref, gi_ref, gf_ref, bi_ref, bf_ref, o_ref, st_ref, m_ref, *, heads):
    t = CHUNK
    dh = ML_HEADDIM
    mask, tri_t = _scan_masks(d, t)
    lw_r = gi_ref[:, rs] + bi_ref[...]
    f = gf_ref[:, rs] + bf_ref[...]
    la_r = jnp.minimum(f, 0.0) - jnp.log(1.0 + jnp.exp(-jnp.abs(f)))
    m_prev = m_ref[:, 0:1]
    cum_r, crow, e_r, m_new, s_old, s_new = _gate_rows(la_r, lw_r, tri_t, m_prev)
    ones = jnp.ones((t, dh), BF16)

    for h in range(heads):
        c0 = h * dh
        q = q_ref[rs, c0:c0 + dh]
        k_t = k_ref[rs, c0:c0 + dh].astype(F32).T
        v_aug = jnp.concatenate([v_ref[rs, c0:c0 + dh], ones], axis=1)
        qk = _dot(q, k_t.astype(BF16))
        s0 = 2 * c0
        qs = _dot(q, st_ref[:, s0:s0 + 2 * dh].astype(BF16))
        p, cf, m_row = _head_probs(cum_r[h:h + 1], crow[h:h + 1], m_prev[h:h + 1], mask, qk)
        res = _dot(jnp.concatenate([p, (k_t * e_r[h:h + 1]).astype(BF16)], axis=0), v_aug)
        y = res[:t] + qs * cf
        cell = y[:, :dh] / jnp.maximum(jnp.abs(y[:, dh:]), jnp.exp(-m_row))
        o_ref[rs, c0:c0 + dh] = cell.astype(o_ref.dtype)
        st_ref[:, s0:s0 + 2 * dh] = s_old[h:h + 1] * st_ref[:, s0:s0 + 2 * dh] + s_new[h:h + 1] * res[t:]
    m_ref[...] = jnp.broadcast_to(m_new, m_ref.shape)


def _ml_scan(qk, pml, gi_r, gf_r, bi_c, bf_c, heads, n_lat_ch, n_ch):
    nb, s, _ = qk.shape
    inner = heads * ML_HEADDIM
    rows = SCAN_CHUNKS * CHUNK
    n_blk = n_ch // SCAN_CHUNKS
    cidx = _chunk_index(n_lat_ch // SCAN_CHUNKS, n_blk)

    def specs(d):
        gspec = pl.BlockSpec((None, None, heads, rows), lambda b, j: (d, b, 0, cidx(d, j)))
        bspec = pl.BlockSpec((None, heads, 1), lambda b, j: (d, 0, 0))
        col = lambda c: pl.BlockSpec((None, rows, inner), lambda b, j: (b, cidx(d, j), c))
        return [col(0), col(1), col(2), gspec, gspec, bspec, bspec]

    out = lambda d: pl.BlockSpec((None, rows, inner), lambda b, j: (b, cidx(d, j), 0))
    args = (qk, qk, pml, gi_r, gf_r, bi_c, bf_c)
    return pl.pallas_call(
        functools.partial(_ml_scan_kernel, heads=heads),
        out_shape=(jax.ShapeDtypeStruct((nb, s, inner), BF16),) * 2,
        grid=(nb, n_blk),
        in_specs=specs(0) + specs(1),
        out_specs=(out(0), out(1)),
        scratch_shapes=[pltpu.VMEM((ML_HEADDIM, 2 * inner), F32), pltpu.VMEM((heads, 128), F32)] * 2,
        compiler_params=_params(("parallel", "arbitrary")),
        name="ml_scan",
    )(*args, *args)


def _ml_out_kernel(cf_ref, cb_ref, o_in_ref, nw_ref, perm_ref, *rest, heads, permute):
    o_ref = rest[-1]
    dh = ML_HEADDIM
    parts = []
    for h in range(heads):
        sl = slice(h * dh, (h + 1) * dh)
        c = cf_ref[:, sl].astype(F32) + cb_ref[:, sl].astype(F32)
        c = c * lax.rsqrt(jnp.mean(c * c, axis=-1, keepdims=True) + EPS) * nw_ref[:, sl]
        parts.append((jax.nn.sigmoid(o_in_ref[:, sl].astype(F32)) * c).astype(BF16))
    y = jnp.concatenate(parts, axis=1)
    y = _dot(perm_ref[...], y) if permute else y.astype(F32)
    o_ref[...] = y.reshape(o_ref.shape)


def _ml_out(cf, cb, pml, norm_w, layer, heads, rows_g, n_lat_tiles, n_tiles):
    nb, _, inner = cf.shape
    wpt = TOKEN_TILE // rows_g
    rpt = TOKEN_TILE // GRID_W
    perm_t = jnp.asarray(_grid_perm(rows_g).T, BF16)
    out_shape = jax.ShapeDtypeStruct((nb, n_tiles * rpt, GRID_W, inner), F32)

    def call(tile0, tiles, out_spec, permute, prev):
        row = pl.BlockSpec((None, TOKEN_TILE, inner), lambda b, i: (b, tile0 + i, 0))
        in_specs = [row, row,
                    pl.BlockSpec((None, TOKEN_TILE, inner), lambda b, i: (b, tile0 + i, 3)),
                    pl.BlockSpec((None, 1, inner), lambda b, i: (layer, 0, 0)),
                    pl.BlockSpec((TOKEN_TILE, TOKEN_TILE), lambda b, i: (0, 0))]
        args = [cf, cb, pml, norm_w, perm_t]
        if prev is not None:
            in_specs.append(pl.BlockSpec(memory_space=pl.ANY))
            args.append(prev)
        return pl.pallas_call(
            functools.partial(_ml_out_kernel, heads=heads, permute=permute),
            out_shape=out_shape,
            grid=(nb, tiles),
            in_specs=in_specs,
            out_specs=out_spec,
            input_output_aliases={} if prev is None else {len(args) - 1: 0},
            compiler_params=_params(("parallel", "parallel")),
            name="ml_out_lat" if permute else "ml_out_ctx",
        )(*args)

    y = call(0, n_lat_tiles, pl.BlockSpec((None, rows_g, wpt, inner), lambda b, i: (b, 0, i, 0)), True, None)
    if n_tiles > n_lat_tiles:
        y = call(n_lat_tiles, n_tiles - n_lat_tiles,
                 pl.BlockSpec((None, rpt, GRID_W, inner), lambda b, i: (b, n_lat_tiles + i, 0, 0)), False, y)
    return y.reshape(nb, n_tiles * TOKEN_TILE, inner)


def _dft_tables(length):
    m = length // 2
    k = np.arange(m, dtype=np.int64)
    ang = ((k[:, None] * k[None, :]) % (2 * m)).astype(np.float64) * (np.pi / m)
    alt8 = np.broadcast_to(np.where(k % 2 == 0, 1.0, -1.0)[None, :], (8, m))
    tw = k.astype(np.float64)[:, None] * (np.pi / length) * np.ones((1, LANE_TILE))
    return (jnp.asarray(np.cos(ang), BF16), jnp.asarray(np.sin(ang), BF16), jnp.asarray(alt8, BF16),
            jnp.asarray(np.cos(tw), F32), jnp.asarray(np.sin(tw), F32))


def _hy_feats(length):
    t = jnp.arange(length, dtype=F32)
    t_norm = t / length
    bands = jnp.linspace(1e-4, HY_BANDS - 1, HY_BANDS, dtype=F32)
    ang = (2.0 * math.pi / length) * t[:, None] * bands[None, :]
    feats = jnp.concatenate([t_norm[:, None], jnp.cos(ang), -jnp.sin(ang)], axis=-1)
    return feats[0::2], feats[1::2], t_norm[0::2, None], t_norm[1::2, None]


def _split_spectrum(ae, be, ao, bo, twc, tws):
    tr = twc * ao - tws * bo
    tm = twc * bo + tws * ao
    return (ae + tr, ae - tr), (be + tm, tm - be)


def _hy_filter_kernel(fe_ref, fo_ref, tne_ref, tno_ref, w1_ref, b1_ref, w2_ref, b2_ref, w3f_ref, w3b_ref,
                      df_ref, db_ref, cm_ref, sm_ref, alt_ref, twc_ref, tws_ref, ha_ref, hb_ref, hm_ref,
                      hide_ref, hido_ref):
    m = cm_ref.shape[0]

    @pl.when((pl.program_id(0) == 0) & (pl.program_id(1) == 0))
    def _():
        for f_ref, h_ref in ((fe_ref, hide_ref), (fo_ref, hido_ref)):
            hid = jnp.sin(_hdot(f_ref[...], w1_ref[...]) + b1_ref[...])
            h_ref[...] = jnp.sin(_hdot(hid, w2_ref[...]) + b2_ref[...])

    def taps(hid, tn):
        h_f = _hdot(hid, w3f_ref[...]) * jnp.exp(-tn * jnp.abs(df_ref[...]))
        h_b = _hdot(hid, w3b_ref[...]) * jnp.exp(-tn * jnp.abs(db_ref[...]))
        return (h_f + h_b).astype(BF16), (h_f - h_b).astype(BF16)

    sum_e, dif_e = taps(hide_ref[...], tne_ref[...])
    sum_o, dif_o = taps(hido_ref[...], tno_ref[...])
    twc, tws = twc_ref[...], tws_ref[...]
    cm, sm = cm_ref[...], sm_ref[...]
    (ha_lo, ha_hi), _ = _split_spectrum(_dot(cm, sum_e), 0.0, _dot(cm, sum_o), _dot(sm, sum_o), twc, tws)
    _, (hb_lo, hb_hi) = _split_spectrum(0.0, _dot(sm, dif_e), _dot(cm, dif_o), _dot(sm, dif_o), twc, tws)
    ha_ref[0:m, :] = ha_lo
    ha_ref[m:2 * m, :] = ha_hi
    hb_ref[0:m, :] = hb_lo
    hb_ref[m:2 * m, :] = hb_hi
    row = lax.broadcasted_iota(jnp.int32, hm_ref.shape, 0)
    hm_ref[...] = jnp.where(row == 0, _dot(alt_ref[...], sum_e), _dot(alt_ref[...], dif_o))


def _hy_filters(length, layer, w1, b1, w2, b2, w3, decay, tables):
    cm, sm, alt8, twc, tws = tables
    m = length // 2
    feats = _hy_feats(length)
    nfeat, nf = w1.shape[-2:]
    ch = decay.shape[-1]
    nct = ch // LANE_TILE
    const = lambda shape: pl.BlockSpec(shape, lambda n, c: (0,) * len(shape))
    w3spec = lambda dr: pl.BlockSpec((None, None, None, nf, LANE_TILE), lambda n, c: (layer, n, dr, 0, c))
    dspec = lambda dr: pl.BlockSpec((None, None, None, 1, LANE_TILE), lambda n, c: (layer, n, dr, 0, c))
    lay = lambda a, b: pl.BlockSpec((None, a, b), lambda n, c: (layer, 0, 0))
    out = lambda rows: pl.BlockSpec((None, rows, LANE_TILE), lambda n, c: (n, 0, c))
    return pl.pallas_call(
        _hy_filter_kernel,
        out_shape=(jax.ShapeDtypeStruct((HY_ORDER, length, ch), F32),
                   jax.ShapeDtypeStruct((HY_ORDER, length, ch), F32),
                   jax.ShapeDtypeStruct((HY_ORDER, 8, ch), F32)),
        grid=(HY_ORDER, nct),
        in_specs=[const((m, nfeat)), const((m, nfeat)), const((m, 1)), const((m, 1)),
                  lay(nfeat, nf), lay(1, nf), lay(nf, nf), lay(1, nf),
                  w3spec(0), w3spec(1), dspec(0), dspec(1),
                  const((m, m)), const((m, m)), const((8, m)),
                  const((m, LANE_TILE)), const((m, LANE_TILE))],
        out_specs=(out(length), out(length), out(8)),
        scratch_shapes=[pltpu.VMEM((m, nf), F32)] * 2,
        compiler_params=_params(("arbitrary", "arbitrary")),
        name="hyena_filters_%d" % length,
    )(*feats, w1, b1, w2, b2, w3, w3, decay, decay, cm, sm, alt8, twc, tws)


def _short_conv(ue, uo, w, b):
    m = ue.shape[0]
    row = lax.broadcasted_iota(jnp.int32, ue.shape, 0)
    prev_odd = jnp.where(row == 0, 0.0, pltpu.roll(uo, 1, 0))
    next_even = jnp.where(row == m - 1, 0.0, pltpu.roll(ue, m - 1, 0))
    w0, w1, w2 = w[0:1], w[1:2], w[2:3]
    return b + w0 * prev_odd + w1 * ue + w2 * uo, b + w0 * ue + w1 * uo + w2 * next_even


def _hy_conv_kernel(z_ref, g_refs, wz_ref, bz_ref, wg_refs, bg_refs, ha_ref, hb_ref, hm_ref, skip_ref,
                    cm_ref, sm_ref, alt_ref, twc_ref, tws_ref, o_ref, tmp_ref):
    width = z_ref.shape[1]
    m = cm_ref.shape[0]
    lanes = tmp_ref.shape[-1]
    slabs = range(width // lanes)

    def split(ref):
        for h in slabs:
            tmp_ref[h] = ref[:, h * lanes:(h + 1) * lanes].astype(F32)
        return [jnp.concatenate([tmp_ref[h, pl.ds(first, m, stride=2), :] for h in slabs], axis=1)
                for first in (0, 1)]

    ze, zo = _short_conv(*split(z_ref), wz_ref[...], bz_ref[...])
    for n in range(HY_ORDER):
        ge, go = _short_conv(*split(g_refs[n]), wg_refs[n][...], bg_refs[n][...])
        ze, zo = _hy_order(ze, zo, ge, go, ha_ref[n], hb_ref[n], hm_ref[n], skip_ref[n],
                           cm_ref, sm_ref, alt_ref, twc_ref[...], tws_ref[...])
    for h in slabs:
        tmp_ref[h, pl.ds(0, m, stride=2), :] = ze[:, h * lanes:(h + 1) * lanes]
        tmp_ref[h, pl.ds(1, m, stride=2), :] = zo[:, h * lanes:(h + 1) * lanes]
        o_ref[:, h * lanes:(h + 1) * lanes] = tmp_ref[h].astype(o_ref.dtype)


def _hy_order(ze, zo, ge, go, ha, hb, hm, skip, cm_ref, sm_ref, alt_ref, twc, tws):
    m, width = ze.shape
    zz = jnp.concatenate([ze.astype(BF16), zo.astype(BF16)], axis=1)
    a = _dot(cm_ref[...], zz)
    b = _dot(sm_ref[...], zz)
    mid = _dot(alt_ref[...], zz)[0:1]
    (a_lo, a_hi), (b_lo, b_hi) = _split_spectrum(a[:, :width], b[:, :width], a[:, width:], b[:, width:], twc, tws)

    ha_lo, ha_hi, hb_lo, hb_hi = ha[0:m, :], ha[m:2 * m, :], hb[0:m, :], hb[m:2 * m, :]
    yr_lo, ym_lo = a_lo * ha_lo - b_lo * hb_lo, a_lo * hb_lo + b_lo * ha_lo
    yr_hi, ym_hi = a_hi * ha_hi - b_hi * hb_hi, a_hi * hb_hi + b_hi * ha_hi
    ha_m, hb_m = hm[0:1, :], hm[1:2, :]
    yr_m = mid[:, :width] * ha_m - mid[:, width:] * hb_m
    ym_m = mid[:, :width] * hb_m + mid[:, width:] * ha_m

    row = lax.broadcasted_iota(jnp.int32, (m, width), 0)
    half0 = jnp.where(row == 0, 0.5, 1.0)
    qr, qm = yr_lo - yr_hi, ym_lo + ym_hi
    pr = jnp.concatenate([((yr_lo + yr_hi) * half0).astype(BF16),
                          ((qr * twc + qm * tws) * half0).astype(BF16)], axis=1)
    pm = jnp.concatenate([(ym_lo - ym_hi).astype(BF16), (qm * twc - qr * tws).astype(BF16)], axis=1)
    y = _dot(cm_ref[...], pr) + _dot(sm_ref[...], pm)
    alt = jnp.where((row & 1) == 0, 1.0, -1.0)
    scale = 1.0 / (2 * m)
    return (ge * ((y[:, :width] + alt * yr_m) * scale + skip * ze),
            go * ((y[:, width:] + alt * ym_m) * scale + skip * zo))


def _hy_conv(u, col0, ch, conv_w, conv_b, out_prev, out_rows, row_block, length, layer, ha, hb, hm, skip,
             tables):
    cm, sm, alt8, twc, tws = tables
    nb = u.shape[0]
    s = out_rows
    m = length // 2
    nct = ch // LANE_TILE
    assert col0 % LANE_TILE == 0
    cb = col0 // LANE_TILE
    taps = conv_w.shape[0]
    assert taps == HY_SHORT == 3
    single = dict(pipeline_mode=pl.Buffered(1))
    const = lambda shape: pl.BlockSpec(shape, lambda c, b: (0,) * len(shape), **single)
    hspec = lambda rows: pl.BlockSpec((HY_ORDER, rows, LANE_TILE), lambda c, b: (0, 0, c), **single)
    ucol = lambda k: pl.BlockSpec((None, length, LANE_TILE), lambda c, b: (b, row_block, cb + k * nct + c))
    wcol = lambda rows, k: pl.BlockSpec((rows, LANE_TILE), lambda c, b: (0, k * nct + c))
    orders = range(1, HY_ORDER + 1)
    in_specs = ([ucol(0)] + [ucol(k) for k in orders] + [wcol(taps, 0), wcol(1, 0)]
                + [wcol(taps, k) for k in orders] + [wcol(1, k) for k in orders]
                + [hspec(length), hspec(length), hspec(8),
                   pl.BlockSpec((None, HY_ORDER, 1, LANE_TILE), lambda c, b: (layer, 0, 0, c)),
                   const((m, m)), const((m, m)), const((8, m)), const((m, LANE_TILE)), const((m, LANE_TILE))])
    args = ([u] * (1 + HY_ORDER) + [conv_w, conv_b] + [conv_w] * HY_ORDER + [conv_b] * HY_ORDER
            + [ha, hb, hm, skip, cm, sm, alt8, twc, tws])
    n_in = len(args)
    aliases = {}
    if out_prev is not None:
        in_specs.append(pl.BlockSpec(memory_space=pl.ANY))
        args.append(out_prev)
        aliases = {len(args) - 1: 0}

    def body(*refs):
        k = HY_ORDER
        z_ref, g_refs = refs[0], refs[1:1 + k]
        wz_ref, bz_ref = refs[1 + k], refs[2 + k]
        wg_refs, bg_refs = refs[3 + k:3 + 2 * k], refs[3 + 2 * k:3 + 3 * k]
        _hy_conv_kernel(z_ref, g_refs, wz_ref, bz_ref, wg_refs, bg_refs, *refs[3 + 3 * k:n_in], *refs[-2:])

    return pl.pallas_call(
        body,
        out_shape=jax.ShapeDtypeStruct((nb, s, ch), BF16),
        grid=(nct, nb),
        in_specs=in_specs,
        out_specs=pl.BlockSpec((None, length, LANE_TILE), lambda c, b: (b, row_block, c)),
        scratch_shapes=[pltpu.VMEM((LANE_TILE // 128, length, 128), F32)],
        input_output_aliases=aliases,
        compiler_params=_params(("parallel", "parallel")),
        name="hyena_conv_%d" % length,
    )(*args)


def _rms(x, w):
    return x * lax.rsqrt(jnp.mean(x * x, axis=-1, keepdims=True) + EPS) * w


def _merge_mlp_kernel(yf_ref, yb_ref, xs_ref, zg_ref, dsk_ref, sn_ref, ym_ref, yh_ref, wb_ref, wo_ref,
                      x_ref, xc_ref, mod_ref, nw_ref, w1_ref, w2_ref, nf_ref, o_ref, *, final, n_lat_tiles):
    m = mod_ref[...]
    bw = yf_ref.shape[1]
    ys = yf_ref[...].astype(F32) + yb_ref[...].astype(F32) + dsk_ref[...] * xs_ref[...].astype(F32)
    ys = _rms(ys * _silu(zg_ref[:, 0:bw].astype(F32)), sn_ref[...]).astype(BF16)
    acc = None
    for n, y in enumerate((ys, ym_ref[...].astype(BF16), yh_ref[...])):
        gate = zg_ref[:, (1 + n) * bw:(2 + n) * bw].astype(F32)
        term = jax.nn.sigmoid(gate) * _dot(y, wb_ref[n])
        acc = term if acc is None else acc + term
    x_in = x_ref[...] if xc_ref is None else jnp.where(pl.program_id(1) < n_lat_tiles, x_ref[...], xc_ref[...])
    x = x_in + m[2:3] * _dot(acc.astype(BF16), wo_ref[...])
    h = (_rms(x, nw_ref[...]) * (1.0 + m[4:5]) + m[3:4]).astype(BF16)
    a = jnp.maximum(_dot(h, w1_ref[...]), 0.0)
    x = x + m[5:6] * _dot((a * a).astype(BF16), w2_ref[...])
    o_ref[...] = _rms(x, nf_ref[...]) if final else x


def _merge_mlp(yf, yb, xbc, pnat, d_full, ssd_norm, ym, yh, wb, wo, x, x_ctx, mod, norm_w, layer, w1, w2,
               norm_f, final, n_lat_tiles, n_tiles):
    nb, _, d = x.shape
    s = yf.shape[1]
    bw = yf.shape[-1]
    hidden = w1.shape[1]
    row = pl.BlockSpec((None, TOKEN_TILE, bw), lambda b, i: (b, i, 0))
    assert bw == d
    zg = pl.BlockSpec((None, TOKEN_TILE, (1 + N_BRANCH) * d), lambda b, i: (b, i, 0))
    xrow = pl.BlockSpec((None, TOKEN_TILE, d), lambda b, i: (b, i, 0))
    lay = lambda n: pl.BlockSpec((None, 1, n), lambda b, i: (layer, 0, 0))
    single = dict(pipeline_mode=pl.Buffered(1))
    const = lambda shape: pl.BlockSpec(shape, lambda b, i: (0,) * len(shape), **single)
    in_specs = [row, row, row, zg, lay(bw), lay(bw), row, row, const((N_BRANCH, bw, d)), const((d, d))]
    args = [yf, yb, xbc, pnat, d_full, ssd_norm, ym, yh, wb, wo]
    aliases = {}
    if x_ctx is None:
        in_specs.append(xrow)
        args.append(x)
        if not final:
            aliases = {len(args) - 1: 0}
    else:
        in_specs += [pl.BlockSpec((None, TOKEN_TILE, d), lambda b, i: (b, jnp.minimum(i, n_lat_tiles - 1), 0)),
                     pl.BlockSpec((None, TOKEN_TILE, d), lambda b, i: (b, jnp.maximum(i - n_lat_tiles, 0), 0))]
        args += [x, x_ctx]
    in_specs += [pl.BlockSpec((None, 6, d), _mod_row(nb, n_lat_tiles)), lay(d),
                 const((d, hidden)), const((hidden, d)), pl.BlockSpec((1, d), lambda b, i: (0, 0))]
    args += [mod, norm_w, w1, w2, norm_f]

    def body(*refs):
        head, tail = refs[:11], refs[11:]
        xc = None if x_ctx is None else tail[0]
        _merge_mlp_kernel(*head, xc, *tail[0 if x_ctx is None else 1:], final=final, n_lat_tiles=n_lat_tiles)

    return pl.pallas_call(
        body,
        out_shape=jax.ShapeDtypeStruct((nb, n_tiles * TOKEN_TILE if final else s, d), F32),
        grid=(nb, n_tiles),
        in_specs=in_specs,
        out_specs=xrow,
        input_output_aliases=aliases,
        compiler_params=_params(("parallel", "parallel")),
        name="merge_mlp",
    )(*args)


def _dir_rows(g, nb, s, per_dir):
    g = g[:, :2 * per_dir].reshape(nb, s, 2, per_dir)
    return jnp.transpose(g, (2, 0, 3, 1))


def kernel(x, c, ctx, c_ctx, norm1_w, mod_w, mod_b, w_in, ssd_conv_w, ssd_conv_b, ssd_dt_bias, ssd_a_log,
           ssd_d, ssd_norm_w, ml_conv_w, ml_conv_b, ml_gate_b, ml_norm_w, hy_conv_w, hy_conv_b, hy_ffn_w1,
           hy_ffn_b1, hy_ffn_w2, hy_ffn_b2, hy_ffn_w3, hy_decay, hy_skip, w_branch, w_out, norm2_w,
           mlp_w1, mlp_w2, norm_f_w):
    nb, seq, d = x.shape
    ctx_len = ctx.shape[1]
    depth = w_in.shape[0]
    s = seq + ctx_len
    assert seq % ctx_len == 0 and ctx_len % TOKEN_TILE == 0 and seq % GRID_W == 0
    n_lat_tiles, n_tiles = seq // TOKEN_TILE, s // TOKEN_TILE
    n_lat_ch, n_ch = seq // CHUNK, s // CHUNK
    assert n_lat_ch % SCAN_CHUNKS == 0 and n_ch % SCAN_CHUNKS == 0
    rows_g = seq // GRID_W

    ssd_heads = ssd_d.shape[-1]
    ssd_inner = ssd_norm_w.shape[-1]
    ssd_hd = ssd_inner // ssd_heads
    ssd_conv_ch = ssd_conv_w.shape[-1]
    ssd_ds = (ssd_conv_ch - ssd_inner) // (2 * SSD_GROUPS)
    ml_heads = ml_gate_b.shape[-1]
    ml_inner = ml_heads * ML_HEADDIM
    hy_inner = hy_skip.shape[-1]
    ssd_cols = ssd_conv_ch + ssd_inner + 2 * ssd_heads
    ml_cols = 4 * ml_inner + 4 * ml_heads
    rec_cols = ssd_cols + ml_cols
    hy_cols = (HY_ORDER + 1) * hy_inner

    o_z = ssd_conv_ch
    o_dt = ssd_conv_ch + ssd_inner
    o_ml = ssd_cols
    o_mlg = ssd_cols + 4 * ml_inner
    o_hy = rec_cols
    o_g = rec_cols + hy_cols

    xa, xa_ctx = x, ctx
    rpad = (-(nb + 1)) % 8
    c_all = jnp.concatenate([c, c_ctx[None], jnp.zeros((rpad, d), F32)], axis=0)

    tab_lat = _dft_tables(seq)
    tab_ctx = _dft_tables(ctx_len)
    k_scale = jnp.concatenate([jnp.ones((1, ml_inner), F32),
                               jnp.full((1, ml_inner), ML_HEADDIM ** -0.5, F32)], axis=1)

    norm1 = norm1_w[:, None, :]
    norm2 = norm2_w[:, None, :]
    ssd_norm = ssd_norm_w[:, None, :]
    ml_norm = ml_norm_w[:, None, :]
    d_full = jnp.repeat(ssd_d, ssd_hd, axis=-1)[:, None, :]
    w3 = hy_ffn_w3.reshape(depth, hy_ffn_w3.shape[1], HY_ORDER, 2, hy_inner).transpose(0, 2, 3, 1, 4)
    decay = hy_decay[:, :, :, None, :]
    skip = hy_skip[:, :, None, :]
    hb1 = hy_ffn_b1[:, None, :]
    hb2 = hy_ffn_b2[:, None, :]

    for l in range(depth):
        need_ctx = l < depth - 1
        used_tiles = n_tiles if need_ctx else n_lat_tiles
        mod = _mod_vectors(c_all, mod_w, mod_b[:, None, :], l).reshape(-1, 6, d)

        wl = w_in[l].astype(BF16)
        w_nat = jnp.concatenate([wl[:, o_z:o_dt], wl[:, o_g:], wl[:, o_hy:o_g]], axis=1)
        w_xbc = wl[:, :o_z]
        w_ml = wl[:, o_ml:o_mlg]
        gpad = lambda w: jnp.pad(w, ((0, 0), (0, 128 - w.shape[1])))
        hn, hn_cm = _normmod(xa, xa_ctx, norm1, l, mod, 0, rows_g, n_lat_tiles, n_tiles)
        hn2 = hn.reshape(nb * s, d)
        hn_cm2 = hn_cm.reshape(nb * s, d)
        pnat = _matmul(hn2, w_nat, BF16, "proj_nat").reshape(nb, s, -1)
        pml, p_mlg = _matmul(hn_cm2, w_ml, BF16, "proj_ml", gpad(wl[:, o_mlg:o_hy]))
        pxbc, p_dt = _matmul(hn2, w_xbc, BF16, "proj_xbc", gpad(wl[:, o_dt:o_ml]))
        pml = pml.reshape(nb, s, -1)
        pxbc = pxbc.reshape(nb, s, -1)
        c_g = ssd_inner
        c_hy = c_g + N_BRANCH * d

        xbc = _dwconv(pxbc, 0, ssd_conv_ch, ssd_conv_w[l], ssd_conv_b[l][None], None,
                      True, n_lat_tiles, n_tiles, "ssd_conv")
        dt_r = _dir_rows(p_dt, nb, s, ssd_heads)
        y_f, y_b = _ssd_scan(xbc, dt_r, ssd_dt_bias[l][:, :, None], ssd_a_log[l][:, :, None],
                             ssd_heads, ssd_hd, ssd_ds, n_lat_ch, n_ch)

        qk = _dwconv(pml, 0, 2 * ml_inner, ml_conv_w[l], ml_conv_b[l][None], k_scale,
                     True, n_lat_tiles, n_tiles, "ml_conv")
        g_r = _dir_rows(p_mlg, nb, s, 2 * ml_heads)
        gate_b = ml_gate_b[l]
        c_f, c_b = _ml_scan(qk, pml, g_r[:, :, :ml_heads], g_r[:, :, ml_heads:],
                            gate_b[:, 0, :, None], gate_b[:, 1, :, None], ml_heads, n_lat_ch, n_ch)
        ym = _ml_out(c_f, c_b, pml, ml_norm, l, ml_heads, rows_g, n_lat_tiles, used_tiles)

        hcw, hcb = hy_conv_w[l], hy_conv_b[l][None]
        fl = _hy_filters(seq, l, hy_ffn_w1, hb1, hy_ffn_w2, hb2, w3, decay, tab_lat)
        rows = s if need_ctx else seq
        yh = _hy_conv(pnat, c_hy, hy_inner, hcw, hcb, None, rows, 0, seq, l, *fl, skip, tab_lat)
        if need_ctx:
            fc = _hy_filters(ctx_len, l, hy_ffn_w1, hb1, hy_ffn_w2, hb2, w3, decay, tab_ctx)
            yh = _hy_conv(pnat, c_hy, hy_inner, hcw, hcb, yh, rows, seq // ctx_len, ctx_len, l, *fc, skip,
                          tab_ctx)

        xa = _merge_mlp(y_f, y_b, xbc, pnat, d_full, ssd_norm, ym, yh, w_branch[l].astype(BF16),
                        w_out[l].astype(BF16), xa, xa_ctx, mod, norm2, l, mlp_w1[l].astype(BF16),
                        mlp_w2[l].astype(BF16), norm_f_w[None], not need_ctx, n_lat_tiles, used_tiles)
        xa_ctx = None

    return xa
```

```python
import functools
import math

import jax
import jax.numpy as jnp
import numpy as np
from jax import lax
from jax.experimental import pallas as pl
from jax.experimental.pallas import tpu as pltpu

F32 = jnp.float32
BF16 = jnp.bfloat16
HIGHEST = lax.Precision.HIGHEST

GRID_W = 64
CHUNK = 128
EPS = 1e-6
SSD_GROUPS = 2
SSD_CONV = 5
ML_HEADDIM = 128
ML_CONV = 5
HY_ORDER = 2
HY_SHORT = 3
HY_BANDS = 16
N_BRANCH = 3

TOKEN_TILE = 256
LANE_TILE = 256
HALO = 16
SCAN_CHUNKS = 2
VMEM_LIMIT = 56 * 1024 * 1024

_hdot = functools.partial(jnp.dot, precision=HIGHEST, preferred_element_type=F32)
_dot = functools.partial(jnp.dot, preferred_element_type=F32)


def _params(sem, vmem=None):
    return pltpu.CompilerParams(dimension_semantics=sem, vmem_limit_bytes=vmem or VMEM_LIMIT)


def _softplus(x):
    return jnp.maximum(x, 0.0) + jnp.log(1.0 + jnp.exp(-jnp.abs(x)))


def _silu(x):
    return x * jax.nn.sigmoid(x)


def _mod_kernel(c_ref, w_ref, b_ref, o_ref):
    o_ref[...] = _hdot(_silu(c_ref[...]), w_ref[...]) + b_ref[...]


def _mod_vectors(c_all, mod_w, mod_b, layer):
    r, d = c_all.shape
    n = mod_w.shape[-1]
    tn = n // 6
    return pl.pallas_call(
        _mod_kernel,
        out_shape=jax.ShapeDtypeStruct((r, n), F32),
        grid=(n // tn,),
        in_specs=[pl.BlockSpec((r, d), lambda j: (0, 0)),
                  pl.BlockSpec((None, d, tn), lambda j: (layer, 0, j)),
                  pl.BlockSpec((None, 1, tn), lambda j: (layer, 0, j))],
        out_specs=pl.BlockSpec((r, tn), lambda j: (0, j)),
        compiler_params=_params(("parallel",)),
        name="mod_vectors",
    )(c_all, mod_w, mod_b)


def _grid_perm(rows_g):
    wpt = TOKEN_TILE // rows_g
    src = np.arange(TOKEN_TILE).reshape(rows_g, wpt).T.reshape(-1)
    p = np.zeros((TOKEN_TILE, TOKEN_TILE), np.float32)
    p[np.arange(TOKEN_TILE), src] = 1.0
    return p


def _normmod_kernel(x_ref, xc_ref, x4_ref, perm_ref, nw_ref, mod_ref, o_ref, ocm_ref, *, si, n_lat_tiles):
    i = pl.program_id(1)
    m = mod_ref[...]

    def normed(x):
        h = x * lax.rsqrt(jnp.mean(x * x, axis=-1, keepdims=True) + EPS) * nw_ref[...]
        return (h * (1.0 + m[si + 1:si + 2]) + m[si:si + 1]).astype(BF16)

    h = normed(x_ref[...] if xc_ref is None else jnp.where(i < n_lat_tiles, x_ref[...], xc_ref[...]))
    o_ref[...] = h

    @pl.when(i < n_lat_tiles)
    def _():
        x4 = x4_ref[...]
        ocm_ref[...] = _dot(perm_ref[...], normed(x4.reshape(TOKEN_TILE, x4.shape[-1]))).astype(BF16)

    @pl.when(i >= n_lat_tiles)
    def _():
        ocm_ref[...] = h


def _mod_row(nb, n_lat_tiles):
    return lambda b, i: (jnp.where(i < n_lat_tiles, b, nb), 0, 0)


def _normmod(x, x_ctx, norm_w, layer, mod, si, rows_g, n_lat_tiles, n_tiles):
    nb, rows, d = x.shape
    s = n_tiles * TOKEN_TILE
    wpt = TOKEN_TILE // rows_g
    assert wpt % 8 == 0 and GRID_W % wpt == 0 and rows % GRID_W == 0
    x4 = x.reshape(nb, rows // GRID_W, GRID_W, d)
    tile = pl.BlockSpec((None, TOKEN_TILE, d), lambda b, i: (b, i, 0))
    if x_ctx is None:
        srcs, src_specs = [x], [tile]
    else:
        srcs = [x, x_ctx]
        src_specs = [pl.BlockSpec((None, TOKEN_TILE, d), lambda b, i: (b, jnp.minimum(i, n_lat_tiles - 1), 0)),
                     pl.BlockSpec((None, TOKEN_TILE, d), lambda b, i: (b, jnp.maximum(i - n_lat_tiles, 0), 0))]

    def body(*refs):
        xc = None if x_ctx is None else refs[1]
        _normmod_kernel(refs[0], xc, *refs[len(srcs):], si=si, n_lat_tiles=n_lat_tiles)

    return pl.pallas_call(
        body,
        out_shape=(jax.ShapeDtypeStruct((nb, s, d), BF16),) * 2,
        grid=(nb, n_tiles),
        in_specs=src_specs + [
            pl.BlockSpec((None, rows_g, wpt, d), lambda b, i: (b, 0, jnp.minimum(i, n_lat_tiles - 1), 0)),
            pl.BlockSpec((TOKEN_TILE, TOKEN_TILE), lambda b, i: (0, 0)),
            pl.BlockSpec((None, 1, d), lambda b, i: (layer, 0, 0)),
            pl.BlockSpec((None, 6, d), _mod_row(nb, n_lat_tiles))],
        out_specs=(tile, tile),
        compiler_params=_params(("parallel", "parallel")),
        name="normmod",
    )(*srcs, x4, jnp.asarray(_grid_perm(rows_g), BF16), norm_w, mod)


def _mm_kernel(a_ref, w_ref, o_ref):
    o_ref[...] = _dot(a_ref[...], w_ref[...]).astype(o_ref.dtype)


def _pick(n, cands):
    for c in cands:
        if n % c == 0:
            return c
    return n


def _matmul(a, w, out_dtype, name):
    t, k = a.shape
    n = w.shape[1]
    tm = _pick(t, (2048, 1024, 768, 512, 256))
    tn = _pick(n, (1024, 512, 256, 128))
    return pl.pallas_call(
        _mm_kernel,
        out_shape=jax.ShapeDtypeStruct((t, n), out_dtype),
        grid=(t // tm, n // tn),
        in_specs=[pl.BlockSpec((tm, k), lambda i, j: (i, 0)),
                  pl.BlockSpec((k, tn), lambda i, j: (0, j))],
        out_specs=pl.BlockSpec((tm, tn), lambda i, j: (i, j)),
        compiler_params=_params(("parallel", "parallel")),
        name=name,
    )(a, w)


def _shift_stack(taps):
    pad = taps // 2
    return np.concatenate([np.eye(TOKEN_TILE, k=j - pad, dtype=np.float32) for j in range(taps) if j != pad])


def _conv_tile(u, prev, nxt, sh_ref, w, b, scale):
    tm = u.shape[0]
    taps = w.shape[0]
    pad = taps // 2

    def finish(acc):
        acc = _silu(acc)
        return (acc if scale is None else acc * scale).astype(BF16)

    def edge(window):
        rows = window.shape[0]
        acc = b + window[HALO:2 * HALO] * w[pad:pad + 1]
        for j in range(taps):
            if j != pad:
                acc = acc + pltpu.roll(window, (pad - j) % rows, 0)[HALO:2 * HALO] * w[j:j + 1]
        return finish(acc)

    shifted = _dot(sh_ref[...], u)
    acc = b + u.astype(F32) * w[pad:pad + 1]
    blk = 0
    for j in range(taps):
        if j != pad:
            acc = acc + shifted[blk * tm:(blk + 1) * tm] * w[j:j + 1]
            blk += 1
    head = edge(jnp.concatenate([prev, u[0:2 * HALO].astype(F32)], axis=0))
    tail = edge(jnp.concatenate([u[tm - 2 * HALO:tm].astype(F32), nxt], axis=0))
    return jnp.concatenate([head, finish(acc)[HALO:tm - HALO], tail], axis=0)


def _proj_conv_kernel(a_ref, w_ref, wg_ref, sh_ref, cw_ref, cb_ref, *rest, n_conv, bounds):
    o_ref, og_ref, buf_ref = rest[-3:]
    s_ref = rest[0] if len(rest) == 4 else None
    j = pl.program_id(1)
    acc = _dot(a_ref[...], w_ref[...])

    @pl.when(j == 0)
    def _():
        og_ref[...] = _dot(a_ref[...], wg_ref[...])

    @pl.when(j >= n_conv)
    def _():
        o_ref[...] = acc.astype(o_ref.dtype)

    @pl.when(j < n_conv)
    def _():
        buf_ref[...] = acc.astype(BF16)
        w, b = cw_ref[...], cb_ref[...]
        scale = None if s_ref is None else s_ref[...]
        zeros = jnp.zeros((HALO, buf_ref.shape[1]), F32)
        for blk in range(buf_ref.shape[0] // TOKEN_TILE):
            r0 = blk * TOKEN_TILE
            prev = zeros if blk in bounds else buf_ref[r0 - HALO:r0, :].astype(F32)
            nxt = zeros if blk + 1 in bounds else buf_ref[r0 + TOKEN_TILE:r0 + TOKEN_TILE + HALO, :].astype(F32)
            o_ref[r0:r0 + TOKEN_TILE, :] = _conv_tile(buf_ref[r0:r0 + TOKEN_TILE, :], prev, nxt, sh_ref, w, b, scale)


def _proj_conv(a, w, w_gate, conv_w, conv_b, scale, n_lat_tiles, n_tiles, name):
    nb, s, k = a.shape
    n = w.shape[1]
    taps, nc = conv_w.shape
    ng = w_gate.shape[1]
    tn = _pick(math.gcd(n, nc), (512, 256))
    n_conv = nc // tn
    cidx = lambda b, j: (0, jnp.minimum(j, n_conv - 1))
    in_specs = [pl.BlockSpec((None, s, k), lambda b, j: (b, 0, 0)),
                pl.BlockSpec((k, tn), lambda b, j: (0, j)),
                pl.BlockSpec((k, ng), lambda b, j: (0, 0)),
                pl.BlockSpec(((taps - 1) * TOKEN_TILE, TOKEN_TILE), lambda b, j: (0, 0)),
                pl.BlockSpec((taps, tn), cidx),
                pl.BlockSpec((1, tn), cidx)]
    args = [a, w, w_gate, jnp.asarray(_shift_stack(taps), BF16), conv_w, conv_b]
    if scale is not None:
        in_specs.append(pl.BlockSpec((1, tn), cidx))
        args.append(scale)
    return pl.pallas_call(
        functools.partial(_proj_conv_kernel, n_conv=n_conv, bounds=(0, n_lat_tiles, n_tiles)),
        out_shape=(jax.ShapeDtypeStruct((nb, s, n), BF16), jax.ShapeDtypeStruct((nb, s, ng), F32)),
        grid=(nb, n // tn),
        in_specs=in_specs,
        out_specs=(pl.BlockSpec((None, s, tn), lambda b, j: (b, 0, j)),
                   pl.BlockSpec((None, s, ng), lambda b, j: (b, 0, 0))),
        scratch_shapes=[pltpu.VMEM((s, tn), BF16)],
        compiler_params=_params(("parallel", "arbitrary")),
        name=name,
    )(*args)


def _chunk_index(n_lat_ch, n_ch):
    return lambda d, j: (j + n_lat_ch) % n_ch if d == 0 else n_ch - 1 - j


def _scan_masks(d, t):
    ii = lax.broadcasted_iota(jnp.int32, (t, t), 0)
    jj = lax.broadcasted_iota(jnp.int32, (t, t), 1)
    mask = jj <= ii if d == 0 else jj >= ii
    tri_t = jnp.where(ii <= jj if d == 0 else ii >= jj, 1.0, 0.0).astype(F32)
    return mask, tri_t


def _gate_rows(la_r, lw_r, tri_t, m_prev):
    cum_r = _hdot(la_r, tri_t)
    last = jnp.sum(la_r, axis=1, keepdims=True)
    g_r = last - cum_r + lw_r
    m_loc = jnp.max(g_r, axis=1, keepdims=True)
    e_r = jnp.exp(g_r - m_loc)
    m_new = jnp.maximum(last + m_prev, m_loc)
    s_old = jnp.exp(last + m_prev - m_new)
    s_new = jnp.exp(m_loc - m_new)
    return cum_r, cum_r - lw_r, e_r, m_new, s_old, s_new


def _head_probs(cum_row, crow_row, m_prev_h, mask, qk):
    t = qk.shape[0]
    colb = jnp.broadcast_to(cum_row, (t, t)).T
    dlog = jnp.where(mask, colb - crow_row, -jnp.inf)
    inter = colb[:, 0:1] + m_prev_h
    m_row = jnp.maximum(inter, jnp.max(dlog, axis=1, keepdims=True))
    p = jnp.exp(dlog - m_row) * qk
    return p.astype(BF16), jnp.exp(inter - m_row), m_row


def _ssd_scan_kernel(*refs, heads, groups, hd, ds):
    j = pl.program_id(1)
    ins, outs, states = refs[:12], refs[12:14], refs[14:16]

    @pl.when(j == 0)
    def _():
        for st_ref in states:
            st_ref[...] = jnp.zeros_like(st_ref)

    for sub in range(SCAN_CHUNKS):
        for d in range(2):
            _ssd_chunk(d, _scan_rows(d, sub), *ins[6 * d:6 * d + 6], outs[d], states[d],
                       heads=heads, groups=groups, hd=hd, ds=ds)


def _scan_rows(d, sub):
    first = (sub if d == 0 else SCAN_CHUNKS - 1 - sub) * CHUNK
    return slice(first, first + CHUNK)


def _ssd_chunk(d, rs, xs_ref, b_ref, c_ref, dt_ref, bias_ref, alog_ref, o_ref, st_ref, *, heads, groups, hd, ds):
    t = CHUNK
    hpg = heads // groups
    pw = 2 * hd
    mask, tri_t = _scan_masks(d, t)
    dt = _softplus(dt_ref[:, rs] + bias_ref[...])
    la_r = -dt * jnp.exp(alog_ref[...])
    cum_r = _hdot(la_r, tri_t)
    last = jnp.sum(la_r, axis=1, keepdims=True)
    crow = cum_r - jnp.log(dt)
    e_r = jnp.exp(last - crow)
    e_last = jnp.exp(last)
    lo = lax.broadcasted_iota(jnp.int32, (1, pw), 1) < hd

    for g in range(groups):
        q = c_ref[rs, g * ds:(g + 1) * ds]
        k_t = b_ref[rs, g * ds:(g + 1) * ds].astype(F32).T
        qk = _dot(q, k_t.astype(BF16))
        w0 = g * hpg * hd
        qs = _dot(q, st_ref[:, w0:w0 + hpg * hd].astype(BF16))
        for i in range(hpg // 2):
            h0 = g * hpg + 2 * i
            c0 = h0 * hd
            vp = xs_ref[rs, c0:c0 + pw]
            zero = jnp.zeros_like(vp)
            v_bd = jnp.concatenate([jnp.where(lo, vp, zero), jnp.where(lo, zero, vp)], axis=0)
            probs, carry, kte = [], [], []
            for h in (h0, h0 + 1):
                colb = jnp.broadcast_to(cum_r[h:h + 1], (t, t)).T
                dlog = jnp.where(mask, colb - crow[h:h + 1], -jnp.inf)
                probs.append((jnp.exp(dlog) * qk).astype(BF16))
                carry.append(jnp.exp(colb))
                kte.append((k_t * e_r[h:h + 1]).astype(BF16))
            lhs = jnp.concatenate([jnp.concatenate(probs, axis=1), jnp.concatenate(kte, axis=1)], axis=0)
            res = _dot(lhs, v_bd)
            y = res[:t] + qs[:, 2 * i * hd:2 * i * hd + pw] * jnp.where(lo, carry[0], carry[1])
            o_ref[rs, c0:c0 + pw] = y.astype(o_ref.dtype)
            decay = jnp.where(lo, e_last[h0:h0 + 1], e_last[h0 + 1:h0 + 2])
            st_ref[:, c0:c0 + pw] = decay * st_ref[:, c0:c0 + pw] + res[t:]


def _ssd_scan(xbc, dt_r, bias_c, alog_c, heads, hd, ds, n_lat_ch, n_ch):
    nb, s, _ = xbc.shape
    inner = heads * hd
    gn = SSD_GROUPS * ds
    assert 2 * hd == CHUNK and (heads // SSD_GROUPS) % 2 == 0 and inner % gn == 0
    rows = SCAN_CHUNKS * CHUNK
    n_blk = n_ch // SCAN_CHUNKS
    cidx = _chunk_index(n_lat_ch // SCAN_CHUNKS, n_blk)

    def specs(d):
        return [pl.BlockSpec((None, rows, inner), lambda b, j: (b, cidx(d, j), 0)),
                pl.BlockSpec((None, rows, gn), lambda b, j: (b, cidx(d, j), inner // gn)),
                pl.BlockSpec((None, rows, gn), lambda b, j: (b, cidx(d, j), inner // gn + 1)),
                pl.BlockSpec((None, None, heads, rows), lambda b, j: (d, b, 0, cidx(d, j))),
                pl.BlockSpec((None, heads, 1), lambda b, j: (d, 0, 0)),
                pl.BlockSpec((None, heads, 1), lambda b, j: (d, 0, 0))]

    out = lambda d: pl.BlockSpec((None, rows, inner), lambda b, j: (b, cidx(d, j), 0))
    args = (xbc, xbc, xbc, dt_r, bias_c, alog_c)
    return pl.pallas_call(
        functools.partial(_ssd_scan_kernel, heads=heads, groups=SSD_GROUPS, hd=hd, ds=ds),
        out_shape=(jax.ShapeDtypeStruct((nb, s, inner), BF16),) * 2,
        grid=(nb, n_blk),
        in_specs=specs(0) + specs(1),
        out_specs=(out(0), out(1)),
        scratch_shapes=[pltpu.VMEM((ds, inner), F32)] * 2,
        compiler_params=_params(("parallel", "arbitrary")),
        name="ssd_scan",
    )(*args, *args)


def _ml_scan_kernel(*refs, heads):
    j = pl.program_id(1)
    ins, outs, states = refs[:14], refs[14:16], refs[16:20]

    @pl.when(j == 0)
    def _():
        for ref in states:
            ref[...] = jnp.zeros_like(ref)

    for sub in range(SCAN_CHUNKS):
        for d in range(2):
            _ml_chunk(d, _scan_rows(d, sub), *ins[7 * d:7 * d + 7], outs[d], *states[2 * d:2 * d + 2],
                      heads=heads)


def _ml_chunk(d, rs, q_ref, k_ref, v_ref, gi_ref, gf_ref, bi_ref, bf_ref, o_ref, st_ref, m_ref, *, heads):
    t = CHUNK
    dh = ML_HEADDIM
    mask, tri_t = _scan_masks(d, t)
    lw_r = gi_ref[:, rs] + bi_ref[...]
    f = gf_ref[:, rs] + bf_ref[...]
    la_r = jnp.minimum(f, 0.0) - jnp.log(1.0 + jnp.exp(-jnp.abs(f)))
    m_prev = m_ref[:, 0:1]
    cum_r, crow, e_r, m_new, s_old, s_new = _gate_rows(la_r, lw_r, tri_t, m_prev)
    ones = jnp.ones((t, dh), BF16)

    for h in range(heads):
        c0 = h * dh
        q = q_ref[rs, c0:c0 + dh]
        k_t = k_ref[rs, c0:c0 + dh].astype(F32).T
        v_aug = jnp.concatenate([v_ref[rs, c0:c0 + dh], ones], axis=1)
        qk = _dot(q, k_t.astype(BF16))
        s0 = 2 * c0
        qs = _dot(q, st_ref[:, s0:s0 + 2 * dh].astype(BF16))
        p, cf, m_row = _head_probs(cum_r[h:h + 1], crow[h:h + 1], m_prev[h:h + 1], mask, qk)
        res = _dot(jnp.concatenate([p, (k_t * e_r[h:h + 1]).astype(BF16)], axis=0), v_aug)
        y = res[:t] + qs * cf
        cell = y[:, :dh] / jnp.maximum(jnp.abs(y[:, dh:]), jnp.exp(-m_row))
        o_ref[rs, c0:c0 + dh] = cell.astype(o_ref.dtype)
        st_ref[:, s0:s0 + 2 * dh] = s_old[h:h + 1] * st_ref[:, s0:s0 + 2 * dh] + s_new[h:h + 1] * res[t:]
    m_ref[...] = jnp.broadcast_to(m_new, m_ref.shape)


def _ml_scan(qk, pml, gi_r, gf_r, bi_c, bf_c, heads, n_lat_ch, n_ch):
    nb, s, _ = qk.shape
    inner = heads * ML_HEADDIM
    rows = SCAN_CHUNKS * CHUNK
    n_blk = n_ch // SCAN_CHUNKS
    cidx = _chunk_index(n_lat_ch // SCAN_CHUNKS, n_blk)

    def specs(d):
        gspec = pl.BlockSpec((None, None, heads, rows), lambda b, j: (d, b, 0, cidx(d, j)))
        bspec = pl.BlockSpec((None, heads, 1), lambda b, j: (d, 0, 0))
        col = lambda c: pl.BlockSpec((None, rows, inner), lambda b, j: (b, cidx(d, j), c))
        return [col(0), col(1), col(2), gspec, gspec, bspec, bspec]

    out = lambda d: pl.BlockSpec((None, rows, inner), lambda b, j: (b, cidx(d, j), 0))
    args = (qk, qk, pml, gi_r, gf_r, bi_c, bf_c)
    return pl.pallas_call(
        functools.partial(_ml_scan_kernel, heads=heads),
        out_shape=(jax.ShapeDtypeStruct((nb, s, inner), BF16),) * 2,
        grid=(nb, n_blk),
        in_specs=specs(0) + specs(1),
        out_specs=(out(0), out(1)),
        scratch_shapes=[pltpu.VMEM((ML_HEADDIM, 2 * inner), F32), pltpu.VMEM((heads, 128), F32)] * 2,
        compiler_params=_params(("parallel", "arbitrary")),
        name="ml_scan",
    )(*args, *args)


def _ml_out_kernel(cf_ref, cb_ref, o_in_ref, nw_ref, perm_ref, *rest, heads, permute):
    o_ref = rest[-1]
    dh = ML_HEADDIM
    parts = []
    for h in range(heads):
        sl = slice(h * dh, (h + 1) * dh)
        c = cf_ref[:, sl].astype(F32) + cb_ref[:, sl].astype(F32)
        c = c * lax.rsqrt(jnp.mean(c * c, axis=-1, keepdims=True) + EPS) * nw_ref[:, sl]
        parts.append((jax.nn.sigmoid(o_in_ref[:, sl].astype(F32)) * c).astype(BF16))
    y = jnp.concatenate(parts, axis=1)
    y = _dot(perm_ref[...], y) if permute else y.astype(F32)
    o_ref[...] = y.reshape(o_ref.shape)


def _ml_out(cf, cb, pml, norm_w, layer, heads, rows_g, n_lat_tiles, n_tiles):
    nb, _, inner = cf.shape
    wpt = TOKEN_TILE // rows_g
    rpt = TOKEN_TILE // GRID_W
    perm_t = jnp.asarray(_grid_perm(rows_g).T, BF16)
    out_shape = jax.ShapeDtypeStruct((nb, n_tiles * rpt, GRID_W, inner), F32)

    def call(tile0, tiles, out_spec, permute, prev):
        row = pl.BlockSpec((None, TOKEN_TILE, inner), lambda b, i: (b, tile0 + i, 0))
        in_specs = [row, row,
                    pl.BlockSpec((None, TOKEN_TILE, inner), lambda b, i: (b, tile0 + i, 3)),
                    pl.BlockSpec((None, 1, inner), lambda b, i: (layer, 0, 0)),
                    pl.BlockSpec((TOKEN_TILE, TOKEN_TILE), lambda b, i: (0, 0))]
        args = [cf, cb, pml, norm_w, perm_t]
        if prev is not None:
            in_specs.append(pl.BlockSpec(memory_space=pl.ANY))
            args.append(prev)
        return pl.pallas_call(
            functools.partial(_ml_out_kernel, heads=heads, permute=permute),
            out_shape=out_shape,
            grid=(nb, tiles),
            in_specs=in_specs,
            out_specs=out_spec,
            input_output_aliases={} if prev is None else {len(args) - 1: 0},
            compiler_params=_params(("parallel", "parallel")),
            name="ml_out_lat" if permute else "ml_out_ctx",
        )(*args)

    y = call(0, n_lat_tiles, pl.BlockSpec((None, rows_g, wpt, inner), lambda b, i: (b, 0, i, 0)), True, None)
    if n_tiles > n_lat_tiles:
        y = call(n_lat_tiles, n_tiles - n_lat_tiles,
                 pl.BlockSpec((None, rpt, GRID_W, inner), lambda b, i: (b, n_lat_tiles + i, 0, 0)), False, y)
    return y.reshape(nb, n_tiles * TOKEN_TILE, inner)


def _dft_tables(length):
    m = length // 2
    k = np.arange(m, dtype=np.int64)
    ang = ((k[:, None] * k[None, :]) % (2 * m)).astype(np.float64) * (np.pi / m)
    alt8 = np.broadcast_to(np.where(k % 2 == 0, 1.0, -1.0)[None, :], (8, m))
    tw = k.astype(np.float64)[:, None] * (np.pi / length) * np.ones((1, LANE_TILE))
    return (jnp.asarray(np.cos(ang), BF16), jnp.asarray(np.sin(ang), BF16), jnp.asarray(alt8, BF16),
            jnp.asarray(np.cos(tw), F32), jnp.asarray(np.sin(tw), F32))


def _hy_feats(length):
    t = jnp.arange(length, dtype=F32)
    t_norm = t / length
    bands = jnp.linspace(1e-4, HY_BANDS - 1, HY_BANDS, dtype=F32)
    ang = (2.0 * math.pi / length) * t[:, None] * bands[None, :]
    feats = jnp.concatenate([t_norm[:, None], jnp.cos(ang), -jnp.sin(ang)], axis=-1)
    return feats[0::2], feats[1::2], t_norm[0::2, None], t_norm[1::2, None]


def _split_spectrum(ae, be, ao, bo, twc, tws):
    tr = twc * ao - tws * bo
    tm = twc * bo + tws * ao
    return (ae + tr, ae - tr), (be + tm, tm - be)


def _hy_filter_kernel(fe_ref, fo_ref, tne_ref, tno_ref, w1_ref, b1_ref, w2_ref, b2_ref, w3f_ref, w3b_ref,
                      df_ref, db_ref, cm_ref, sm_ref, alt_ref, twc_ref, tws_ref, ha_ref, hb_ref, hm_ref,
                      hide_ref, hido_ref):
    m = cm_ref.shape[0]

    @pl.when((pl.program_id(0) == 0) & (pl.program_id(1) == 0))
    def _():
        for f_ref, h_ref in ((fe_ref, hide_ref), (fo_ref, hido_ref)):
            hid = jnp.sin(_hdot(f_ref[...], w1_ref[...]) + b1_ref[...])
            h_ref[...] = jnp.sin(_hdot(hid, w2_ref[...]) + b2_ref[...])

    def taps(hid, tn):
        h_f = _hdot(hid, w3f_ref[...]) * jnp.exp(-tn * jnp.abs(df_ref[...]))
        h_b = _hdot(hid, w3b_ref[...]) * jnp.exp(-tn * jnp.abs(db_ref[...]))
        return (h_f + h_b).astype(BF16), (h_f - h_b).astype(BF16)

    sum_e, dif_e = taps(hide_ref[...], tne_ref[...])
    sum_o, dif_o = taps(hido_ref[...], tno_ref[...])
    twc, tws = twc_ref[...], tws_ref[...]
    cm, sm = cm_ref[...], sm_ref[...]
    (ha_lo, ha_hi), _ = _split_spectrum(_dot(cm, sum_e), 0.0, _dot(cm, sum_o), _dot(sm, sum_o), twc, tws)
    _, (hb_lo, hb_hi) = _split_spectrum(0.0, _dot(sm, dif_e), _dot(cm, dif_o), _dot(sm, dif_o), twc, tws)
    ha_ref[0:m, :] = ha_lo
    ha_ref[m:2 * m, :] = ha_hi
    hb_ref[0:m, :] = hb_lo
    hb_ref[m:2 * m, :] = hb_hi
    row = lax.broadcasted_iota(jnp.int32, hm_ref.shape, 0)
    hm_ref[...] = jnp.where(row == 0, _dot(alt_ref[...], sum_e), _dot(alt_ref[...], dif_o))


def _hy_filters(length, layer, w1, b1, w2, b2, w3, decay, tables):
    cm, sm, alt8, twc, tws = tables
    m = length // 2
    feats = _hy_feats(length)
    nfeat, nf = w1.shape[-2:]
    ch = decay.shape[-1]
    nct = ch // LANE_TILE
    const = lambda shape: pl.BlockSpec(shape, lambda n, c: (0,) * len(shape))
    w3spec = lambda dr: pl.BlockSpec((None, None, None, nf, LANE_TILE), lambda n, c: (layer, n, dr, 0, c))
    dspec = lambda dr: pl.BlockSpec((None, None, None, 1, LANE_TILE), lambda n, c: (layer, n, dr, 0, c))
    lay = lambda a, b: pl.BlockSpec((None, a, b), lambda n, c: (layer, 0, 0))
    out = lambda rows: pl.BlockSpec((None, rows, LANE_TILE), lambda n, c: (n, 0, c))
    return pl.pallas_call(
        _hy_filter_kernel,
        out_shape=(jax.ShapeDtypeStruct((HY_ORDER, length, ch), F32),
                   jax.ShapeDtypeStruct((HY_ORDER, length, ch), F32),
                   jax.ShapeDtypeStruct((HY_ORDER, 8, ch), F32)),
        grid=(HY_ORDER, nct),
        in_specs=[const((m, nfeat)), const((m, nfeat)), const((m, 1)), const((m, 1)),
                  lay(nfeat, nf), lay(1, nf), lay(nf, nf), lay(1, nf),
                  w3spec(0), w3spec(1), dspec(0), dspec(1),
                  const((m, m)), const((m, m)), const((8, m)),
                  const((m, LANE_TILE)), const((m, LANE_TILE))],
        out_specs=(out(length), out(length), out(8)),
        scratch_shapes=[pltpu.VMEM((m, nf), F32)] * 2,
        compiler_params=_params(("arbitrary", "arbitrary")),
        name="hyena_filters_%d" % length,
    )(*feats, w1, b1, w2, b2, w3, w3, decay, decay, cm, sm, alt8, twc, tws)


def _short_conv(ue, uo, w, b):
    m = ue.shape[0]
    row = lax.broadcasted_iota(jnp.int32, ue.shape, 0)
    prev_odd = jnp.where(row == 0, 0.0, pltpu.roll(uo, 1, 0))
    next_even = jnp.where(row == m - 1, 0.0, pltpu.roll(ue, m - 1, 0))
    w0, w1, w2 = w[0:1], w[1:2], w[2:3]
    return b + w0 * prev_odd + w1 * ue + w2 * uo, b + w0 * ue + w1 * uo + w2 * next_even


def _hy_conv_kernel(z_ref, g_refs, wz_ref, bz_ref, wg_refs, bg_refs, ha_ref, hb_ref, hm_ref, skip_ref,
                    cm_ref, sm_ref, alt_ref, twc_ref, tws_ref, o_ref, tmp_ref):
    width = z_ref.shape[1]
    m = cm_ref.shape[0]
    lanes = tmp_ref.shape[-1]
    slabs = range(width // lanes)

    def split(ref):
        for h in slabs:
            tmp_ref[h] = ref[:, h * lanes:(h + 1) * lanes].astype(F32)
        return [jnp.concatenate([tmp_ref[h, pl.ds(first, m, stride=2), :] for h in slabs], axis=1)
                for first in (0, 1)]

    ze, zo = _short_conv(*split(z_ref), wz_ref[...], bz_ref[...])
    for n in range(HY_ORDER):
        ge, go = _short_conv(*split(g_refs[n]), wg_refs[n][...], bg_refs[n][...])
        ze, zo = _hy_order(ze, zo, ge, go, ha_ref[n], hb_ref[n], hm_ref[n], skip_ref[n],
                           cm_ref, sm_ref, alt_ref, twc_ref[...], tws_ref[...])
    for h in slabs:
        tmp_ref[h, pl.ds(0, m, stride=2), :] = ze[:, h * lanes:(h + 1) * lanes]
        tmp_ref[h, pl.ds(1, m, stride=2), :] = zo[:, h * lanes:(h + 1) * lanes]
        o_ref[:, h * lanes:(h + 1) * lanes] = tmp_ref[h].astype(o_ref.dtype)


def _hy_order(ze, zo, ge, go, ha, hb, hm, skip, cm_ref, sm_ref, alt_ref, twc, tws):
    m, width = ze.shape
    zz = jnp.concatenate([ze.astype(BF16), zo.astype(BF16)], axis=1)
    a = _dot(cm_ref[...], zz)
    b = _dot(sm_ref[...], zz)
    mid = _dot(alt_ref[...], zz)[0:1]
    (a_lo, a_hi), (b_lo, b_hi) = _split_spectrum(a[:, :width], b[:, :width], a[:, width:], b[:, width:], twc, tws)

    ha_lo, ha_hi, hb_lo, hb_hi = ha[0:m, :], ha[m:2 * m, :], hb[0:m, :], hb[m:2 * m, :]
    yr_lo, ym_lo = a_lo * ha_lo - b_lo * hb_lo, a_lo * hb_lo + b_lo * ha_lo
    yr_hi, ym_hi = a_hi * ha_hi - b_hi * hb_hi, a_hi * hb_hi + b_hi * ha_hi
    ha_m, hb_m = hm[0:1, :], hm[1:2, :]
    yr_m = mid[:, :width] * ha_m - mid[:, width:] * hb_m
    ym_m = mid[:, :width] * hb_m + mid[:, width:] * ha_m

    row = lax.broadcasted_iota(jnp.int32, (m, width), 0)
    half0 = jnp.where(row == 0, 0.5, 1.0)
    qr, qm = yr_lo - yr_hi, ym_lo + ym_hi
    pr = jnp.concatenate([((yr_lo + yr_hi) * half0).astype(BF16),
                          ((qr * twc + qm * tws) * half0).astype(BF16)], axis=1)
    pm = jnp.concatenate([(ym_lo - ym_hi).astype(BF16), (qm * twc - qr * tws).astype(BF16)], axis=1)
    y = _dot(cm_ref[...], pr) + _dot(sm_ref[...], pm)
    alt = jnp.where((row & 1) == 0, 1.0, -1.0)
    scale = 1.0 / (2 * m)
    return (ge * ((y[:, :width] + alt * yr_m) * scale + skip * ze),
            go * ((y[:, width:] + alt * ym_m) * scale + skip * zo))


def _hy_conv(u, col0, ch, conv_w, conv_b, out_prev, out_rows, row_block, length, layer, ha, hb, hm, skip,
             tables):
    cm, sm, alt8, twc, tws = tables
    nb = u.shape[0]
    s = out_rows
    m = length // 2
    nct = ch // LANE_TILE
    assert col0 % LANE_TILE == 0
    cb = col0 // LANE_TILE
    taps = conv_w.shape[0]
    assert taps == HY_SHORT == 3
    single = dict(pipeline_mode=pl.Buffered(1))
    const = lambda shape: pl.BlockSpec(shape, lambda c, b: (0,) * len(shape), **single)
    hspec = lambda rows: pl.BlockSpec((HY_ORDER, rows, LANE_TILE), lambda c, b: (0, 0, c), **single)
    ucol = lambda k: pl.BlockSpec((None, length, LANE_TILE), lambda c, b: (b, row_block, cb + k * nct + c))
    wcol = lambda rows, k: pl.BlockSpec((rows, LANE_TILE), lambda c, b: (0, k * nct + c))
    orders = range(1, HY_ORDER + 1)
    in_specs = ([ucol(0)] + [ucol(k) for k in orders] + [wcol(taps, 0), wcol(1, 0)]
                + [wcol(taps, k) for k in orders] + [wcol(1, k) for k in orders]
                + [hspec(length), hspec(length), hspec(8),
                   pl.BlockSpec((None, HY_ORDER, 1, LANE_TILE), lambda c, b: (layer, 0, 0, c)),
                   const((m, m)), const((m, m)), const((8, m)), const((m, LANE_TILE)), const((m, LANE_TILE))])
    args = ([u] * (1 + HY_ORDER) + [conv_w, conv_b] + [conv_w] * HY_ORDER + [conv_b] * HY_ORDER
            + [ha, hb, hm, skip, cm, sm, alt8, twc, tws])
    n_in = len(args)
    aliases = {}
    if out_prev is not None:
        in_specs.append(pl.BlockSpec(memory_space=pl.ANY))
        args.append(out_prev)
        aliases = {len(args) - 1: 0}

    def body(*refs):
        k = HY_ORDER
        z_ref, g_refs = refs[0], refs[1:1 + k]
        wz_ref, bz_ref = refs[1 + k], refs[2 + k]
        wg_refs, bg_refs = refs[3 + k:3 + 2 * k], refs[3 + 2 * k:3 + 3 * k]
        _hy_conv_kernel(z_ref, g_refs, wz_ref, bz_ref, wg_refs, bg_refs, *refs[3 + 3 * k:n_in], *refs[-2:])

    return pl.pallas_call(
        body,
        out_shape=jax.ShapeDtypeStruct((nb, s, ch), BF16),
        grid=(nct, nb),
        in_specs=in_specs,
        out_specs=pl.BlockSpec((None, length, LANE_TILE), lambda c, b: (b, row_block, c)),
        scratch_shapes=[pltpu.VMEM((LANE_TILE // 128, length, 128), F32)],
        input_output_aliases=aliases,
        compiler_params=_params(("parallel", "parallel")),
        name="hyena_conv_%d" % length,
    )(*args)


def _rms(x, w):
    return x * lax.rsqrt(jnp.mean(x * x, axis=-1, keepdims=True) + EPS) * w


def _merge_mlp_kernel(yf_ref, yb_ref, xs_ref, zg_ref, dsk_ref, sn_ref, ym_ref, yh_ref, wb_ref, wo_ref,
                      x_ref, xc_ref, mod_ref, nw_ref, w1_ref, w2_ref, nf_ref, o_ref, *, final, n_lat_tiles):
    m = mod_ref[...]
    bw = yf_ref.shape[1]
    ys = yf_ref[...].astype(F32) + yb_ref[...].astype(F32) + dsk_ref[...] * xs_ref[...].astype(F32)
    ys = _rms(ys * _silu(zg_ref[:, 0:bw].astype(F32)), sn_ref[...]).astype(BF16)
    acc = None
    for n, y in enumerate((ys, ym_ref[...].astype(BF16), yh_ref[...])):
        gate = zg_ref[:, (1 + n) * bw:(2 + n) * bw].astype(F32)
        term = jax.nn.sigmoid(gate) * _dot(y, wb_ref[n])
        acc = term if acc is None else acc + term
    x_in = x_ref[...] if xc_ref is None else jnp.where(pl.program_id(1) < n_lat_tiles, x_ref[...], xc_ref[...])
    x = x_in + m[2:3] * _dot(acc.astype(BF16), wo_ref[...])
    h = (_rms(x, nw_ref[...]) * (1.0 + m[4:5]) + m[3:4]).astype(BF16)
    a = jnp.maximum(_dot(h, w1_ref[...]), 0.0)
    x = x + m[5:6] * _dot((a * a).astype(BF16), w2_ref[...])
    o_ref[...] = _rms(x, nf_ref[...]) if final else x


def _merge_mlp(yf, yb, xbc, pnat, d_full, ssd_norm, ym, yh, wb, wo, x, x_ctx, mod, norm_w, layer, w1, w2,
               norm_f, final, n_lat_tiles, n_tiles):
    nb, _, d = x.shape
    s = yf.shape[1]
    bw = yf.shape[-1]
    hidden = w1.shape[1]
    row = pl.BlockSpec((None, TOKEN_TILE, bw), lambda b, i: (b, i, 0))
    assert bw == d
    zg = pl.BlockSpec((None, TOKEN_TILE, (1 + N_BRANCH) * d), lambda b, i: (b, i, 0))
    xrow = pl.BlockSpec((None, TOKEN_TILE, d), lambda b, i: (b, i, 0))
    lay = lambda n: pl.BlockSpec((None, 1, n), lambda b, i: (layer, 0, 0))
    single = dict(pipeline_mode=pl.Buffered(1))
    const = lambda shape: pl.BlockSpec(shape, lambda b, i: (0,) * len(shape), **single)
    in_specs = [row, row, row, zg, lay(bw), lay(bw), row, row, const((N_BRANCH, bw, d)), const((d, d))]
    args = [yf, yb, xbc, pnat, d_full, ssd_norm, ym, yh, wb, wo]
    aliases = {}
    if x_ctx is None:
        in_specs.append(xrow)
        args.append(x)
        if not final:
            aliases = {len(args) - 1: 0}
    else:
        in_specs += [pl.BlockSpec((None, TOKEN_TILE, d), lambda b, i: (b, jnp.minimum(i, n_lat_tiles - 1), 0)),
                     pl.BlockSpec((None, TOKEN_TILE, d), lambda b, i: (b, jnp.maximum(i - n_lat_tiles, 0), 0))]
        args += [x, x_ctx]
    in_specs += [pl.BlockSpec((None, 6, d), _mod_row(nb, n_lat_tiles)), lay(d),
                 const((d, hidden)), const((hidden, d)), pl.BlockSpec((1, d), lambda b, i: (0, 0))]
    args += [mod, norm_w, w1, w2, norm_f]

    def body(*refs):
        head, tail = refs[:11], refs[11:]
        xc = None if x_ctx is None else tail[0]
        _merge_mlp_kernel(*head, xc, *tail[0 if x_ctx is None else 1:], final=final, n_lat_tiles=n_lat_tiles)

    return pl.pallas_call(
        body,
        out_shape=jax.ShapeDtypeStruct((nb, n_tiles * TOKEN_TILE if final else s, d), F32),
        grid=(nb, n_tiles),
        in_specs=in_specs,
        out_specs=xrow,
        input_output_aliases=aliases,
        compiler_params=_params(("parallel", "parallel")),
        name="merge_mlp",
    )(*args)


def _dir_rows(g, nb, s, per_dir):
    g = g[:, :2 * per_dir].reshape(nb, s, 2, per_dir)
    return jnp.transpose(g, (2, 0, 3, 1))


def kernel(x, c, ctx, c_ctx, norm1_w, mod_w, mod_b, w_in, ssd_conv_w, ssd_conv_b, ssd_dt_bias, ssd_a_log,
           ssd_d, ssd_norm_w, ml_conv_w, ml_conv_b, ml_gate_b, ml_norm_w, hy_conv_w, hy_conv_b, hy_ffn_w1,
           hy_ffn_b1, hy_ffn_w2, hy_ffn_b2, hy_ffn_w3, hy_decay, hy_skip, w_branch, w_out, norm2_w,
           mlp_w1, mlp_w2, norm_f_w):
    nb, seq, d = x.shape
    ctx_len = ctx.shape[1]
    depth = w_in.shape[0]
    s = seq + ctx_len
    assert seq % ctx_len == 0 and ctx_len % TOKEN_TILE == 0 and seq % GRID_W == 0
    n_lat_tiles, n_tiles = seq // TOKEN_TILE, s // TOKEN_TILE
    n_lat_ch, n_ch = seq // CHUNK, s // CHUNK
    assert n_lat_ch % SCAN_CHUNKS == 0 and n_ch % SCAN_CHUNKS == 0
    rows_g = seq // GRID_W

    ssd_heads = ssd_d.shape[-1]
    ssd_inner = ssd_norm_w.shape[-1]
    ssd_hd = ssd_inner // ssd_heads
    ssd_conv_ch = ssd_conv_w.shape[-1]
    ssd_ds = (ssd_conv_ch - ssd_inner) // (2 * SSD_GROUPS)
    ml_heads = ml_gate_b.shape[-1]
    ml_inner = ml_heads * ML_HEADDIM
    hy_inner = hy_skip.shape[-1]
    ssd_cols = ssd_conv_ch + ssd_inner + 2 * ssd_heads
    ml_cols = 4 * ml_inner + 4 * ml_heads
    rec_cols = ssd_cols + ml_cols
    hy_cols = (HY_ORDER + 1) * hy_inner

    o_z = ssd_conv_ch
    o_dt = ssd_conv_ch + ssd_inner
    o_ml = ssd_cols
    o_mlg = ssd_cols + 4 * ml_inner
    o_hy = rec_cols
    o_g = rec_cols + hy_cols

    xa, xa_ctx = x, ctx
    rpad = (-(nb + 1)) % 8
    c_all = jnp.concatenate([c, c_ctx[None], jnp.zeros((rpad, d), F32)], axis=0)

    tab_lat = _dft_tables(seq)
    tab_ctx = _dft_tables(ctx_len)
    k_scale = jnp.concatenate([jnp.ones((1, ml_inner), F32),
                               jnp.full((1, ml_inner), ML_HEADDIM ** -0.5, F32)], axis=1)

    norm1 = norm1_w[:, None, :]
    norm2 = norm2_w[:, None, :]
    ssd_norm = ssd_norm_w[:, None, :]
    ml_norm = ml_norm_w[:, None, :]
    d_full = jnp.repeat(ssd_d, ssd_hd, axis=-1)[:, None, :]
    w3 = hy_ffn_w3.reshape(depth, hy_ffn_w3.shape[1], HY_ORDER, 2, hy_inner).transpose(0, 2, 3, 1, 4)
    decay = hy_decay[:, :, :, None, :]
    skip = hy_skip[:, :, None, :]
    hb1 = hy_ffn_b1[:, None, :]
    hb2 = hy_ffn_b2[:, None, :]

    for l in range(depth):
        need_ctx = l < depth - 1
        used_tiles = n_tiles if need_ctx else n_lat_tiles
        mod = _mod_vectors(c_all, mod_w, mod_b[:, None, :], l).reshape(-1, 6, d)

        wl = w_in[l].astype(BF16)
        w_nat = jnp.concatenate([wl[:, o_z:o_dt], wl[:, o_g:], wl[:, o_hy:o_g]], axis=1)
        w_xbc = wl[:, :o_z]
        w_ml = wl[:, o_ml:o_mlg]
        gpad = lambda w: jnp.pad(w, ((0, 0), (0, 128 - w.shape[1])))
        hn, hn_cm = _normmod(xa, xa_ctx, norm1, l, mod, 0, rows_g, n_lat_tiles, n_tiles)
        pnat = _matmul(hn.reshape(nb * s, d), w_nat, BF16, "proj_nat").reshape(nb, s, -1)
        c_g = ssd_inner
        c_hy = c_g + N_BRANCH * d

        xbc, p_dt = _proj_conv(hn, w_xbc, gpad(wl[:, o_dt:o_ml]), ssd_conv_w[l], ssd_conv_b[l][None], None,
                               n_lat_tiles, n_tiles, "proj_xbc")
        dt_r = _dir_rows(p_dt.reshape(nb * s, -1), nb, s, ssd_heads)
        y_f, y_b = _ssd_scan(xbc, dt_r, ssd_dt_bias[l][:, :, None], ssd_a_log[l][:, :, None],
                             ssd_heads, ssd_hd, ssd_ds, n_lat_ch, n_ch)

        pml, p_mlg = _proj_conv(hn_cm, w_ml, gpad(wl[:, o_mlg:o_hy]), ml_conv_w[l], ml_conv_b[l][None],
                                k_scale, n_lat_tiles, n_tiles, "proj_ml")
        g_r = _dir_rows(p_mlg.reshape(nb * s, -1), nb, s, 2 * ml_heads)
        gate_b = ml_gate_b[l]
        c_f, c_b = _ml_scan(pml, pml, g_r[:, :, :ml_heads], g_r[:, :, ml_heads:],
                            gate_b[:, 0, :, None], gate_b[:, 1, :, None], ml_heads, n_lat_ch, n_ch)
        ym = _ml_out(c_f, c_b, pml, ml_norm, l, ml_heads, rows_g, n_lat_tiles, used_tiles)

        hcw, hcb = hy_conv_w[l], hy_conv_b[l][None]
        fl = _hy_filters(seq, l, hy_ffn_w1, hb1, hy_ffn_w2, hb2, w3, decay, tab_lat)
        rows = s if need_ctx else seq
        yh = _hy_conv(pnat, c_hy, hy_inner, hcw, hcb, None, rows, 0, seq, l, *fl, skip, tab_lat)
        if need_ctx:
            fc = _hy_filters(ctx_len, l, hy_ffn_w1, hb1, hy_ffn_w2, hb2, w3, decay, tab_ctx)
            yh = _hy_conv(pnat, c_hy, hy_inner, hcw, hcb, yh, rows, seq // ctx_len, ctx_len, l, *fc, skip,
                          tab_ctx)

        xa = _merge_mlp(y_f, y_b, xbc, pnat, d_full, ssd_norm, ym, yh, w_branch[l].astype(BF16),
                        w_out[l].astype(BF16), xa, xa_ctx, mod, norm2, l, mlp_w1[l].astype(BF16),
                        mlp_w2[l].astype(BF16), norm_f_w[None], not need_ctx, n_lat_tiles, used_tiles)
        xa_ctx = None

    return xa
```

```python
import functools
import math

import jax
import jax.numpy as jnp
import numpy as np
from jax import lax
from jax.experimental import pallas as pl
from jax.experimental.pallas import tpu as pltpu

F32 = jnp.float32
BF16 = jnp.bfloat16
HIGHEST = lax.Precision.HIGHEST

GRID_W = 64
CHUNK = 128
EPS = 1e-6
SSD_GROUPS = 2
SSD_CONV = 5
ML_HEADDIM = 128
ML_CONV = 5
HY_ORDER = 2
HY_SHORT = 3
HY_BANDS = 16
N_BRANCH = 3

TOKEN_TILE = 256
LANE_TILE = 256
HALO = 16
SCAN_CHUNKS = 2
VMEM_LIMIT = 56 * 1024 * 1024

_hdot = functools.partial(jnp.dot, precision=HIGHEST, preferred_element_type=F32)
_dot = functools.partial(jnp.dot, preferred_element_type=F32)


def _params(sem, vmem=None):
    return pltpu.CompilerParams(dimension_semantics=sem, vmem_limit_bytes=vmem or VMEM_LIMIT)


def _softplus(x):
    return jnp.maximum(x, 0.0) + jnp.log(1.0 + jnp.exp(-jnp.abs(x)))


def _silu(x):
    return x * jax.nn.sigmoid(x)


def _mod_kernel(c_ref, w_ref, b_ref, o_ref):
    o_ref[...] = _hdot(_silu(c_ref[...]), w_ref[...]) + b_ref[...]


def _mod_vectors(c_all, mod_w, mod_b, layer):
    r, d = c_all.shape
    n = mod_w.shape[-1]
    tn = n // 6
    return pl.pallas_call(
        _mod_kernel,
        out_shape=jax.ShapeDtypeStruct((r, n), F32),
        grid=(n // tn,),
        in_specs=[pl.BlockSpec((r, d), lambda j: (0, 0)),
                  pl.BlockSpec((None, d, tn), lambda j: (layer, 0, j)),
                  pl.BlockSpec((None, 1, tn), lambda j: (layer, 0, j))],
        out_specs=pl.BlockSpec((r, tn), lambda j: (0, j)),
        compiler_params=_params(("parallel",)),
        name="mod_vectors",
    )(c_all, mod_w, mod_b)


def _grid_perm(rows_g):
    wpt = TOKEN_TILE // rows_g
    src = np.arange(TOKEN_TILE).reshape(rows_g, wpt).T.reshape(-1)
    p = np.zeros((TOKEN_TILE, TOKEN_TILE), np.float32)
    p[np.arange(TOKEN_TILE), src] = 1.0
    return p


def _normmod_kernel(x_ref, xc_ref, x4_ref, perm_ref, nw_ref, mod_ref, o_ref, ocm_ref, *, si, n_lat_tiles):
    i = pl.program_id(1)
    m = mod_ref[...]

    def normed(x):
        h = x * lax.rsqrt(jnp.mean(x * x, axis=-1, keepdims=True) + EPS) * nw_ref[...]
        return (h * (1.0 + m[si + 1:si + 2]) + m[si:si + 1]).astype(BF16)

    h = normed(x_ref[...] if xc_ref is None else jnp.where(i < n_lat_tiles, x_ref[...], xc_ref[...]))
    o_ref[...] = h

    @pl.when(i < n_lat_tiles)
    def _():
        x4 = x4_ref[...]
        ocm_ref[...] = _dot(perm_ref[...], normed(x4.reshape(TOKEN_TILE, x4.shape[-1]))).astype(BF16)

    @pl.when(i >= n_lat_tiles)
    def _():
        ocm_ref[...] = h


def _mod_row(nb, n_lat_tiles):
    return lambda b, i: (jnp.where(i < n_lat_tiles, b, nb), 0, 0)


def _normmod(x, x_ctx, norm_w, layer, mod, si, rows_g, n_lat_tiles, n_tiles):
    nb, rows, d = x.shape
    s = n_tiles * TOKEN_TILE
    wpt = TOKEN_TILE // rows_g
    assert wpt % 8 == 0 and GRID_W % wpt == 0 and rows % GRID_W == 0
    x4 = x.reshape(nb, rows // GRID_W, GRID_W, d)
    tile = pl.BlockSpec((None, TOKEN_TILE, d), lambda b, i: (b, i, 0))
    if x_ctx is None:
        srcs, src_specs = [x], [tile]
    else:
        srcs = [x, x_ctx]
        src_specs = [pl.BlockSpec((None, TOKEN_TILE, d), lambda b, i: (b, jnp.minimum(i, n_lat_tiles - 1), 0)),
                     pl.BlockSpec((None, TOKEN_TILE, d), lambda b, i: (b, jnp.maximum(i - n_lat_tiles, 0), 0))]

    def body(*refs):
        xc = None if x_ctx is None else refs[1]
        _normmod_kernel(refs[0], xc, *refs[len(srcs):], si=si, n_lat_tiles=n_lat_tiles)

    return pl.pallas_call(
        body,
        out_shape=(jax.ShapeDtypeStruct((nb, s, d), BF16),) * 2,
        grid=(nb, n_tiles),
        in_specs=src_specs + [
            pl.BlockSpec((None, rows_g, wpt, d), lambda b, i: (b, 0, jnp.minimum(i, n_lat_tiles - 1), 0)),
            pl.BlockSpec((TOKEN_TILE, TOKEN_TILE), lambda b, i: (0, 0)),
            pl.BlockSpec((None, 1, d), lambda b, i: (layer, 0, 0)),
            pl.BlockSpec((None, 6, d), _mod_row(nb, n_lat_tiles))],
        out_specs=(tile, tile),
        compiler_params=_params(("parallel", "parallel")),
        name="normmod",
    )(*srcs, x4, jnp.asarray(_grid_perm(rows_g), BF16), norm_w, mod)


def _mm_kernel(a_ref, w_ref, o_ref):
    o_ref[...] = _dot(a_ref[...], w_ref[...]).astype(o_ref.dtype)


def _pick(n, cands):
    for c in cands:
        if n % c == 0:
            return c
    return n


def _matmul(a, w, out_dtype, name):
    t, k = a.shape
    n = w.shape[1]
    tm = _pick(t, (2048, 1024, 768, 512, 256))
    tn = _pick(n, (1024, 512, 256, 128))
    return pl.pallas_call(
        _mm_kernel,
        out_shape=jax.ShapeDtypeStruct((t, n), out_dtype),
        grid=(t // tm, n // tn),
        in_specs=[pl.BlockSpec((tm, k), lambda i, j: (i, 0)),
                  pl.BlockSpec((k, tn), lambda i, j: (0, j))],
        out_specs=pl.BlockSpec((tm, tn), lambda i, j: (i, j)),
        compiler_params=_params(("parallel", "parallel")),
        name=name,
    )(a, w)


def _shift_stack(taps):
    pad = taps // 2
    return np.concatenate([np.eye(TOKEN_TILE, k=j - pad, dtype=np.float32) for j in range(taps) if j != pad])


def _conv_tile(u, prev, nxt, sh_ref, w, b, scale):
    tm = u.shape[0]
    taps = w.shape[0]
    pad = taps // 2

    def finish(acc):
        acc = _silu(acc)
        return (acc if scale is None else acc * scale).astype(BF16)

    def edge(window):
        rows = window.shape[0]
        acc = b + window[HALO:2 * HALO] * w[pad:pad + 1]
        for j in range(taps):
            if j != pad:
                acc = acc + pltpu.roll(window, (pad - j) % rows, 0)[HALO:2 * HALO] * w[j:j + 1]
        return finish(acc)

    shifted = _dot(sh_ref[...], u)
    acc = b + u.astype(F32) * w[pad:pad + 1]
    blk = 0
    for j in range(taps):
        if j != pad:
            acc = acc + shifted[blk * tm:(blk + 1) * tm] * w[j:j + 1]
            blk += 1
    head = edge(jnp.concatenate([prev, u[0:2 * HALO].astype(F32)], axis=0))
    tail = edge(jnp.concatenate([u[tm - 2 * HALO:tm].astype(F32), nxt], axis=0))
    return jnp.concatenate([head, finish(acc)[HALO:tm - HALO], tail], axis=0)


def _proj_conv_kernel(a_ref, w_ref, wg_ref, sh_ref, cw_ref, cb_ref, *rest, n_conv, bounds):
    o_ref, og_ref, buf_ref = rest[-3:]
    s_ref = rest[0] if len(rest) == 4 else None
    j = pl.program_id(1)
    acc = _dot(a_ref[...], w_ref[...])

    @pl.when(j == 0)
    def _():
        og_ref[...] = _dot(a_ref[...], wg_ref[...])

    @pl.when(j >= n_conv)
    def _():
        o_ref[...] = acc.astype(o_ref.dtype)

    @pl.when(j < n_conv)
    def _():
        buf_ref[...] = acc.astype(BF16)
        w, b = cw_ref[...], cb_ref[...]
        scale = None if s_ref is None else s_ref[...]
        zeros = jnp.zeros((HALO, buf_ref.shape[1]), F32)
        for blk in range(buf_ref.shape[0] // TOKEN_TILE):
            r0 = blk * TOKEN_TILE
            prev = zeros if blk in bounds else buf_ref[r0 - HALO:r0, :].astype(F32)
            nxt = zeros if blk + 1 in bounds else buf_ref[r0 + TOKEN_TILE:r0 + TOKEN_TILE + HALO, :].astype(F32)
            o_ref[r0:r0 + TOKEN_TILE, :] = _conv_tile(buf_ref[r0:r0 + TOKEN_TILE, :], prev, nxt, sh_ref, w, b, scale)


def _proj_conv(a, w, w_gate, conv_w, conv_b, scale, n_lat_tiles, n_tiles, name):
    nb, s, k = a.shape
    n = w.shape[1]
    taps, nc = conv_w.shape
    ng = w_gate.shape[1]
    tn = _pick(math.gcd(n, nc), (512, 256))
    n_conv = nc // tn
    cidx = lambda b, j: (0, jnp.minimum(j, n_conv - 1))
    in_specs = [pl.BlockSpec((None, s, k), lambda b, j: (b, 0, 0)),
                pl.BlockSpec((k, tn), lambda b, j: (0, j)),
                pl.BlockSpec((k, ng), lambda b, j: (0, 0)),
                pl.BlockSpec(((taps - 1) * TOKEN_TILE, TOKEN_TILE), lambda b, j: (0, 0)),
                pl.BlockSpec((taps, tn), cidx),
                pl.BlockSpec((1, tn), cidx)]
    args = [a, w, w_gate, jnp.asarray(_shift_stack(taps), BF16), conv_w, conv_b]
    if scale is not None:
        in_specs.append(pl.BlockSpec((1, tn), cidx))
        args.append(scale)
    return pl.pallas_call(
        functools.partial(_proj_conv_kernel, n_conv=n_conv, bounds=(0, n_lat_tiles, n_tiles)),
        out_shape=(jax.ShapeDtypeStruct((nb, s, n), BF16), jax.ShapeDtypeStruct((nb, s, ng), F32)),
        grid=(nb, n // tn),
        in_specs=in_specs,
        out_specs=(pl.BlockSpec((None, s, tn), lambda b, j: (b, 0, j)),
                   pl.BlockSpec((None, s, ng), lambda b, j: (b, 0, 0))),
        scratch_shapes=[pltpu.VMEM((s, tn), BF16)],
        compiler_params=_params(("parallel", "arbitrary")),
        name=name,
    )(*args)


def _chunk_index(n_lat_ch, n_ch):
    return lambda d, j: (j + n_lat_ch) % n_ch if d == 0 else n_ch - 1 - j


def _scan_masks(d, t):
    ii = lax.broadcasted_iota(jnp.int32, (t, t), 0)
    jj = lax.broadcasted_iota(jnp.int32, (t, t), 1)
    mask = jj <= ii if d == 0 else jj >= ii
    tri_t = jnp.where(ii <= jj if d == 0 else ii >= jj, 1.0, 0.0).astype(F32)
    return mask, tri_t


def _gate_rows(la_r, lw_r, tri_t, m_prev):
    cum_r = _hdot(la_r, tri_t)
    last = jnp.sum(la_r, axis=1, keepdims=True)
    g_r = last - cum_r + lw_r
    m_loc = jnp.max(g_r, axis=1, keepdims=True)
    e_r = jnp.exp(g_r - m_loc)
    m_new = jnp.maximum(last + m_prev, m_loc)
    s_old = jnp.exp(last + m_prev - m_new)
    s_new = jnp.exp(m_loc - m_new)
    return cum_r, cum_r - lw_r, e_r, m_new, s_old, s_new


def _head_probs(cum_row, crow_row, m_prev_h, mask, qk):
    t = qk.shape[0]
    colb = jnp.broadcast_to(cum_row, (t, t)).T
    dlog = jnp.where(mask, colb - crow_row, -jnp.inf)
    inter = colb[:, 0:1] + m_prev_h
    m_row = jnp.maximum(inter, jnp.max(dlog, axis=1, keepdims=True))
    p = jnp.exp(dlog - m_row) * qk
    return p.astype(BF16), jnp.exp(inter - m_row), m_row


def _ssd_scan_kernel(*refs, heads, groups, hd, ds):
    j = pl.program_id(1)
    ins, outs, states = refs[:12], refs[12:14], refs[14:16]

    @pl.when(j == 0)
    def _():
        for st_ref in states:
            st_ref[...] = jnp.zeros_like(st_ref)

    for sub in range(SCAN_CHUNKS):
        for d in range(2):
            _ssd_chunk(d, _scan_rows(d, sub), *ins[6 * d:6 * d + 6], outs[d], states[d],
                       heads=heads, groups=groups, hd=hd, ds=ds)


def _scan_rows(d, sub):
    first = (sub if d == 0 else SCAN_CHUNKS - 1 - sub) * CHUNK
    return slice(first, first + CHUNK)


def _ssd_chunk(d, rs, xs_ref, b_ref, c_ref, dt_ref, bias_ref, alog_ref, o_ref, st_ref, *, heads, groups, hd, ds):
    t = CHUNK
    hpg = heads // groups
    pw = 2 * hd
    mask, tri_t = _scan_masks(d, t)
    dt = _softplus(dt_ref[:, rs] + bias_ref[...])
    la_r = -dt * jnp.exp(alog_ref[...])
    cum_r = _hdot(la_r, tri_t)
    last = jnp.sum(la_r, axis=1, keepdims=True)
    crow = cum_r - jnp.log(dt)
    e_r = jnp.exp(last - crow)
    e_last = jnp.exp(last)
    lo = lax.broadcasted_iota(jnp.int32, (1, pw), 1) < hd

    for g in range(groups):
        q = c_ref[rs, g * ds:(g + 1) * ds]
        k_t = b_ref[rs, g * ds:(g + 1) * ds].astype(F32).T
        qk = _dot(q, k_t.astype(BF16))
        w0 = g * hpg * hd
        qs = _dot(q, st_ref[:, w0:w0 + hpg * hd].astype(BF16))
        for i in range(hpg // 2):
            h0 = g * hpg + 2 * i
            c0 = h0 * hd
            vp = xs_ref[rs, c0:c0 + pw]
            zero = jnp.zeros_like(vp)
            v_bd = jnp.concatenate([jnp.where(lo, vp, zero), jnp.where(lo, zero, vp)], axis=0)
            probs, carry, kte = [], [], []
            for h in (h0, h0 + 1):
                colb = jnp.broadcast_to(cum_r[h:h + 1], (t, t)).T
                dlog = jnp.where(mask, colb - crow[h:h + 1], -jnp.inf)
                probs.append((jnp.exp(dlog) * qk).astype(BF16))
                carry.append(jnp.exp(colb))
                kte.append((k_t * e_r[h:h + 1]).astype(BF16))
            lhs = jnp.concatenate([jnp.concatenate(probs, axis=1), jnp.concatenate(kte, axis=1)], axis=0)
            res = _dot(lhs, v_bd)
            y = res[:t] + qs[:, 2 * i * hd:2 * i * hd + pw] * jnp.where(lo, carry[0], carry[1])
            o_ref[rs, c0:c0 + pw] = y.astype(o_ref.dtype)
            decay = jnp.where(lo, e_last[h0:h0 + 1], e_last[h0 + 1:h0 + 2])
            st_ref[:, c0:c0 + pw] = decay * st_ref[:, c0:c0 + pw] + res[t:]


def _ssd_scan(xbc, dt_r, bias_c, alog_c, heads, hd, ds, n_lat_ch, n_ch):
    nb, s, _ = xbc.shape
    inner = heads * hd
    gn = SSD_GROUPS * ds
    assert 2 * hd == CHUNK and (heads // SSD_GROUPS) % 2 == 0 and inner % gn == 0
    rows = SCAN_CHUNKS * CHUNK
    n_blk = n_ch // SCAN_CHUNKS
    cidx = _chunk_index(n_lat_ch // SCAN_CHUNKS, n_blk)

    def specs(d):
        return [pl.BlockSpec((None, rows, inner), lambda b, j: (b, cidx(d, j), 0)),
                pl.BlockSpec((None, rows, gn), lambda b, j: (b, cidx(d, j), inner // gn)),
                pl.BlockSpec((None, rows, gn), lambda b, j: (b, cidx(d, j), inner // gn + 1)),
                pl.BlockSpec((None, None, heads, rows), lambda b, j: (d, b, 0, cidx(d, j))),
                pl.BlockSpec((None, heads, 1), lambda b, j: (d, 0, 0)),
                pl.BlockSpec((None, heads, 1), lambda b, j: (d, 0, 0))]

    out = lambda d: pl.BlockSpec((None, rows, inner), lambda b, j: (b, cidx(d, j), 0))
    args = (xbc, xbc, xbc, dt_r, bias_c, alog_c)
    return pl.pallas_call(
        functools.partial(_ssd_scan_kernel, heads=heads, groups=SSD_GROUPS, hd=hd, ds=ds),
        out_shape=(jax.ShapeDtypeStruct((nb, s, inner), BF16),) * 2,
        grid=(nb, n_blk),
        in_specs=specs(0) + specs(1),
        out_specs=(out(0), out(1)),
        scratch_shapes=[pltpu.VMEM((ds, inner), F32)] * 2,
        compiler_params=_params(("parallel", "arbitrary")),
        name="ssd_scan",
    )(*args, *args)


def _ml_scan_kernel(*refs, heads):
    j = pl.program_id(1)
    ins, outs, states = refs[:14], refs[14:16], refs[16:20]

    @pl.when(j == 0)
    def _():
        for ref in states:
            ref[...] = jnp.zeros_like(ref)

    for sub in range(SCAN_CHUNKS):
        for d in range(2):
            _ml_chunk(d, _scan_rows(d, sub), *ins[7 * d:7 * d + 7], outs[d], *states[2 * d:2 * d + 2],
                      heads=heads)


def _ml_chunk(d, rs, q_ref, k_ref, v_ref, gi_ref, gf_ref, bi_ref, bf_ref, o_ref, st_ref, m_ref, *, heads):
    t = CHUNK
    dh = ML_HEADDIM
    mask, tri_t = _scan_masks(d, t)
    lw_r = gi_ref[:, rs] + bi_ref[...]
    f = gf_ref[:, rs] + bf_ref[...]
    la_r = jnp.minimum(f, 0.0) - jnp.log(1.0 + jnp.exp(-jnp.abs(f)))
    m_prev = m_ref[:, 0:1]
    cum_r, crow, e_r, m_new, s_old, s_new = _gate_rows(la_r, lw_r, tri_t, m_prev)
    ones = jnp.ones((t, dh), BF16)

    for h in range(heads):
        c0 = h * dh
        q = q_ref[rs, c0:c0 + dh]
        k_t = k_ref[rs, c0:c0 + dh].astype(F32).T
        v_aug = jnp.concatenate([v_ref[rs, c0:c0 + dh], ones], axis=1)
        qk = _dot(q, k_t.astype(BF16))
        s0 = 2 * c0
        qs = _dot(q, st_ref[:, s0:s0 + 2 * dh].astype(BF16))
        p, cf, m_row = _head_probs(cum_r[h:h + 1], crow[h:h + 1], m_prev[h:h + 1], mask, qk)
        res = _dot(jnp.concatenate([p, (k_t * e_r[h:h + 1]).astype(BF16)], axis=0), v_aug)
        y = res[:t] + qs * cf
        cell = y[:, :dh] / jnp.maximum(jnp.abs(y[:, dh:]), jnp.exp(-m_row))
        o_ref[rs, c0:c0 + dh] = cell.astype(o_ref.dtype)
        st_ref[:, s0:s0 + 2 * dh] = s_old[h:h + 1] * st_ref[:, s0:s0 + 2 * dh] + s_new[h:h + 1] * res[t:]
    m_ref[...] = jnp.broadcast_to(m_new, m_ref.shape)


def _ml_scan(qk, pml, gi_r, gf_r, bi_c, bf_c, heads, n_lat_ch, n_ch):
    nb, s, _ = qk.shape
    inner = heads * ML_HEADDIM
    rows = SCAN_CHUNKS * CHUNK
    n_blk = n_ch // SCAN_CHUNKS
    cidx = _chunk_index(n_lat_ch // SCAN_CHUNKS, n_blk)

    def specs(d):
        gspec = pl.BlockSpec((None, None, heads, rows), lambda b, j: (d, b, 0, cidx(d, j)))
        bspec = pl.BlockSpec((None, heads, 1), lambda b, j: (d, 0, 0))
        col = lambda c: pl.BlockSpec((None, rows, inner), lambda b, j: (b, cidx(d, j), c))
        return [col(0), col(1), col(2), gspec, gspec, bspec, bspec]

    out = lambda d: pl.BlockSpec((None, rows, inner), lambda b, j: (b, cidx(d, j), 0))
    args = (qk, qk, pml, gi_r, gf_r, bi_c, bf_c)
    return pl.pallas_call(
        functools.partial(_ml_scan_kernel, heads=heads),
        out_shape=(jax.ShapeDtypeStruct((nb, s, inner), BF16),) * 2,
        grid=(nb, n_blk),
        in_specs=specs(0) + specs(1),
        out_specs=(out(0), out(1)),
        scratch_shapes=[pltpu.VMEM((ML_HEADDIM, 2 * inner), F32), pltpu.VMEM((heads, 128), F32)] * 2,
        compiler_params=_params(("parallel", "arbitrary")),
        name="ml_scan",
    )(*args, *args)


def _ml_out_kernel(cf_ref, cb_ref, o_in_ref, nw_ref, perm_ref, *rest, heads, permute):
    o_ref = rest[-1]
    dh = ML_HEADDIM
    parts = []
    for h in range(heads):
        sl = slice(h * dh, (h + 1) * dh)
        c = cf_ref[:, sl].astype(F32) + cb_ref[:, sl].astype(F32)
        c = c * lax.rsqrt(jnp.mean(c * c, axis=-1, keepdims=True) + EPS) * nw_ref[:, sl]
        parts.append((jax.nn.sigmoid(o_in_ref[:, sl].astype(F32)) * c).astype(BF16))
    y = jnp.concatenate(parts, axis=1)
    y = _dot(perm_ref[...], y) if permute else y.astype(F32)
    o_ref[...] = y.reshape(o_ref.shape)


def _ml_out(cf, cb, pml, norm_w, layer, heads, rows_g, n_lat_tiles, n_tiles):
    nb, _, inner = cf.shape
    wpt = TOKEN_TILE // rows_g
    rpt = TOKEN_TILE // GRID_W
    perm_t = jnp.asarray(_grid_perm(rows_g).T, BF16)
    out_shape = jax.ShapeDtypeStruct((nb, n_tiles * rpt, GRID_W, inner), F32)

    def call(tile0, tiles, out_spec, permute, prev):
        row = pl.BlockSpec((None, TOKEN_TILE, inner), lambda b, i: (b, tile0 + i, 0))
        in_specs = [row, row,
                    pl.BlockSpec((None, TOKEN_TILE, inner), lambda b, i: (b, tile0 + i, 3)),
                    pl.BlockSpec((None, 1, inner), lambda b, i: (layer, 0, 0)),
                    pl.BlockSpec((TOKEN_TILE, TOKEN_TILE), lambda b, i: (0, 0))]
        args = [cf, cb, pml, norm_w, perm_t]
        if prev is not None:
            in_specs.append(pl.BlockSpec(memory_space=pl.ANY))
            args.append(prev)
        return pl.pallas_call(
            functools.partial(_ml_out_kernel, heads=heads, permute=permute),
            out_shape=out_shape,
            grid=(nb, tiles),
            in_specs=in_specs,
            out_specs=out_spec,
            input_output_aliases={} if prev is None else {len(args) - 1: 0},
            compiler_params=_params(("parallel", "parallel")),
            name="ml_out_lat" if permute else "ml_out_ctx",
        )(*args)

    y = call(0, n_lat_tiles, pl.BlockSpec((None, rows_g, wpt, inner), lambda b, i: (b, 0, i, 0)), True, None)
    if n_tiles > n_lat_tiles:
        y = call(n_lat_tiles, n_tiles - n_lat_tiles,
                 pl.BlockSpec((None, rpt, GRID_W, inner), lambda b, i: (b, n_lat_tiles + i, 0, 0)), False, y)
    return y.reshape(nb, n_tiles * TOKEN_TILE, inner)


def _dft_tables(length):
    m = length // 2
    k = np.arange(m, dtype=np.int64)
    ang = ((k[:, None] * k[None, :]) % (2 * m)).astype(np.float64) * (np.pi / m)
    alt8 = np.broadcast_to(np.where(k % 2 == 0, 1.0, -1.0)[None, :], (8, m))
    tw = k.astype(np.float64)[:, None] * (np.pi / length) * np.ones((1, LANE_TILE))
    return (jnp.asarray(np.cos(ang), BF16), jnp.asarray(np.sin(ang), BF16), jnp.asarray(alt8, BF16),
            jnp.asarray(np.cos(tw), F32), jnp.asarray(np.sin(tw), F32))


def _hy_feats(length):
    t = jnp.arange(length, dtype=F32)
    t_norm = t / length
    bands = jnp.linspace(1e-4, HY_BANDS - 1, HY_BANDS, dtype=F32)
    ang = (2.0 * math.pi / length) * t[:, None] * bands[None, :]
    feats = jnp.concatenate([t_norm[:, None], jnp.cos(ang), -jnp.sin(ang)], axis=-1)
    return feats[0::2], feats[1::2], t_norm[0::2, None], t_norm[1::2, None]


def _split_spectrum(ae, be, ao, bo, twc, tws):
    tr = twc * ao - tws * bo
    tm = twc * bo + tws * ao
    return (ae + tr, ae - tr), (be + tm, tm - be)


def _hy_filter_kernel(fe_ref, fo_ref, tne_ref, tno_ref, w1_ref, b1_ref, w2_ref, b2_ref, w3f_ref, w3b_ref,
                      df_ref, db_ref, cm_ref, sm_ref, alt_ref, twc_ref, tws_ref, ha_ref, hb_ref, hm_ref,
                      hide_ref, hido_ref):
    m = cm_ref.shape[0]

    @pl.when((pl.program_id(0) == 0) & (pl.program_id(1) == 0))
    def _():
        for f_ref, h_ref in ((fe_ref, hide_ref), (fo_ref, hido_ref)):
            hid = jnp.sin(_hdot(f_ref[...], w1_ref[...]) + b1_ref[...])
            h_ref[...] = jnp.sin(_hdot(hid, w2_ref[...]) + b2_ref[...])

    def taps(hid, tn):
        h_f = _hdot(hid, w3f_ref[...]) * jnp.exp(-tn * jnp.abs(df_ref[...]))
        h_b = _hdot(hid, w3b_ref[...]) * jnp.exp(-tn * jnp.abs(db_ref[...]))
        return (h_f + h_b).astype(BF16), (h_f - h_b).astype(BF16)

    sum_e, dif_e = taps(hide_ref[...], tne_ref[...])
    sum_o, dif_o = taps(hido_ref[...], tno_ref[...])
    twc, tws = twc_ref[...], tws_ref[...]
    cm, sm = cm_ref[...], sm_ref[...]
    (ha_lo, ha_hi), _ = _split_spectrum(_dot(cm, sum_e), 0.0, _dot(cm, sum_o), _dot(sm, sum_o), twc, tws)
    _, (hb_lo, hb_hi) = _split_spectrum(0.0, _dot(sm, dif_e), _dot(cm, dif_o), _dot(sm, dif_o), twc, tws)
    ha_ref[0:m, :] = ha_lo
    ha_ref[m:2 * m, :] = ha_hi
    hb_ref[0:m, :] = hb_lo
    hb_ref[m:2 * m, :] = hb_hi
    row = lax.broadcasted_iota(jnp.int32, hm_ref.shape, 0)
    hm_ref[...] = jnp.where(row == 0, _dot(alt_ref[...], sum_e), _dot(alt_ref[...], dif_o))


def _hy_filters(length, layer, w1, b1, w2, b2, w3, decay, tables):
    cm, sm, alt8, twc, tws = tables
    m = length // 2
    feats = _hy_feats(length)
    nfeat, nf = w1.shape[-2:]
    ch = decay.shape[-1]
    nct = ch // LANE_TILE
    const = lambda shape: pl.BlockSpec(shape, lambda n, c: (0,) * len(shape))
    w3spec = lambda dr: pl.BlockSpec((None, None, None, nf, LANE_TILE), lambda n, c: (layer, n, dr, 0, c))
    dspec = lambda dr: pl.BlockSpec((None, None, None, 1, LANE_TILE), lambda n, c: (layer, n, dr, 0, c))
    lay = lambda a, b: pl.BlockSpec((None, a, b), lambda n, c: (layer, 0, 0))
    out = lambda rows: pl.BlockSpec((None, rows, LANE_TILE), lambda n, c: (n, 0, c))
    return pl.pallas_call(
        _hy_filter_kernel,
        out_shape=(jax.ShapeDtypeStruct((HY_ORDER, length, ch), F32),
                   jax.ShapeDtypeStruct((HY_ORDER, length, ch), F32),
                   jax.ShapeDtypeStruct((HY_ORDER, 8, ch), F32)),
        grid=(HY_ORDER, nct),
        in_specs=[const((m, nfeat)), const((m, nfeat)), const((m, 1)), const((m, 1)),
                  lay(nfeat, nf), lay(1, nf), lay(nf, nf), lay(1, nf),
                  w3spec(0), w3spec(1), dspec(0), dspec(1),
                  const((m, m)), const((m, m)), const((8, m)),
                  const((m, LANE_TILE)), const((m, LANE_TILE))],
        out_specs=(out(length), out(length), out(8)),
        scratch_shapes=[pltpu.VMEM((m, nf), F32)] * 2,
        compiler_params=_params(("arbitrary", "arbitrary")),
        name="hyena_filters_%d" % length,
    )(*feats, w1, b1, w2, b2, w3, w3, decay, decay, cm, sm, alt8, twc, tws)


def _short_conv(ue, uo, w, b):
    m = ue.shape[0]
    row = lax.broadcasted_iota(jnp.int32, ue.shape, 0)
    prev_odd = jnp.where(row == 0, 0.0, pltpu.roll(uo, 1, 0))
    next_even = jnp.where(row == m - 1, 0.0, pltpu.roll(ue, m - 1, 0))
    w0, w1, w2 = w[0:1], w[1:2], w[2:3]
    return b + w0 * prev_odd + w1 * ue + w2 * uo, b + w0 * ue + w1 * uo + w2 * next_even


def _hy_conv_kernel(z_ref, g_refs, wz_ref, bz_ref, wg_refs, bg_refs, ha_ref, hb_ref, hm_ref, skip_ref,
                    cm_ref, sm_ref, alt_ref, twc_ref, tws_ref, o_ref, tmp_ref):
    width = z_ref.shape[1]
    m = cm_ref.shape[0]
    lanes = tmp_ref.shape[-1]
    slabs = range(width // lanes)

    def split(ref):
        for h in slabs:
            tmp_ref[h] = ref[:, h * lanes:(h + 1) * lanes].astype(F32)
        return [jnp.concatenate([tmp_ref[h, pl.ds(first, m, stride=2), :] for h in slabs], axis=1)
                for first in (0, 1)]

    ze, zo = _short_conv(*split(z_ref), wz_ref[...], bz_ref[...])
    for n in range(HY_ORDER):
        ge, go = _short_conv(*split(g_refs[n]), wg_refs[n][...], bg_refs[n][...])
        ze, zo = _hy_order(ze, zo, ge, go, ha_ref[n], hb_ref[n], hm_ref[n], skip_ref[n],
                           cm_ref, sm_ref, alt_ref, twc_ref[...], tws_ref[...])
    for h in slabs:
        tmp_ref[h, pl.ds(0, m, stride=2), :] = ze[:, h * lanes:(h + 1) * lanes]
        tmp_ref[h, pl.ds(1, m, stride=2), :] = zo[:, h * lanes:(h + 1) * lanes]
        o_ref[:, h * lanes:(h + 1) * lanes] = tmp_ref[h].astype(o_ref.dtype)


def _hy_order(ze, zo, ge, go, ha, hb, hm, skip, cm_ref, sm_ref, alt_ref, twc, tws):
    m, width = ze.shape
    zz = jnp.concatenate([ze.astype(BF16), zo.astype(BF16)], axis=1)
    a = _dot(cm_ref[...], zz)
    b = _dot(sm_ref[...], zz)
    mid = _dot(alt_ref[...], zz)[0:1]
    (a_lo, a_hi), (b_lo, b_hi) = _split_spectrum(a[:, :width], b[:, :width], a[:, width:], b[:, width:], twc, tws)

    ha_lo, ha_hi, hb_lo, hb_hi = ha[0:m, :], ha[m:2 * m, :], hb[0:m, :], hb[m:2 * m, :]
    yr_lo, ym_lo = a_lo * ha_lo - b_lo * hb_lo, a_lo * hb_lo + b_lo * ha_lo
    yr_hi, ym_hi = a_hi * ha_hi - b_hi * hb_hi, a_hi * hb_hi + b_hi * ha_hi
    ha_m, hb_m = hm[0:1, :], hm[1:2, :]
    yr_m = mid[:, :width] * ha_m - mid[:, width:] * hb_m
    ym_m = mid[:, :width] * hb_m + mid[:, width:] * ha_m

    row = lax.broadcasted_iota(jnp.int32, (m, width), 0)
    half0 = jnp.where(row == 0, 0.5, 1.0)
    qr, qm = yr_lo - yr_hi, ym_lo + ym_hi
    pr = jnp.concatenate([((yr_lo + yr_hi) * half0).astype(BF16),
                          ((qr * twc + qm * tws) * half0).astype(BF16)], axis=1)
    pm = jnp.concatenate([(ym_lo - ym_hi).astype(BF16), (qm * twc - qr * tws).astype(BF16)], axis=1)
    y = _dot(cm_ref[...], pr) + _dot(sm_ref[...], pm)
    alt = jnp.where((row & 1) == 0, 1.0, -1.0)
    scale = 1.0 / (2 * m)
    return (ge * ((y[:, :width] + alt * yr_m) * scale + skip * ze),
            go * ((y[:, width:] + alt * ym_m) * scale + skip * zo))


def _proj_tiles(a, w):
    nb, s, k = a.shape
    n = w.shape[1]
    return pl.pallas_call(
        _mm_kernel,
        out_shape=jax.ShapeDtypeStruct((n // LANE_TILE, nb, s, LANE_TILE), BF16),
        grid=(nb, n // LANE_TILE),
        in_specs=[pl.BlockSpec((None, s, k), lambda b, j: (b, 0, 0)),
                  pl.BlockSpec((k, LANE_TILE), lambda b, j: (0, j))],
        out_specs=pl.BlockSpec((None, None, s, LANE_TILE), lambda b, j: (j, b, 0, 0)),
        compiler_params=_params(("parallel", "parallel")),
        name="proj_hyena",
    )(a, w)


def _hy_conv(u, ch, conv_w, conv_b, out_prev, out_rows, row_block, length, layer, ha, hb, hm, skip, tables):
    cm, sm, alt8, twc, tws = tables
    nb = u.shape[1]
    s = out_rows
    m = length // 2
    nct = ch // LANE_TILE
    taps = conv_w.shape[0]
    assert taps == HY_SHORT == 3
    single = dict(pipeline_mode=pl.Buffered(1))
    const = lambda shape: pl.BlockSpec(shape, lambda c, b: (0,) * len(shape), **single)
    hspec = lambda rows: pl.BlockSpec((HY_ORDER, rows, LANE_TILE), lambda c, b: (0, 0, c), **single)
    ucol = lambda k: pl.BlockSpec((None, None, length, LANE_TILE), lambda c, b: (k * nct + c, b, row_block, 0))
    wcol = lambda rows, k: pl.BlockSpec((rows, LANE_TILE), lambda c, b: (0, k * nct + c))
    orders = range(1, HY_ORDER + 1)
    in_specs = ([ucol(0)] + [ucol(k) for k in orders] + [wcol(taps, 0), wcol(1, 0)]
                + [wcol(taps, k) for k in orders] + [wcol(1, k) for k in orders]
                + [hspec(length), hspec(length), hspec(8),
                   pl.BlockSpec((None, HY_ORDER, 1, LANE_TILE), lambda c, b: (layer, 0, 0, c)),
                   const((m, m)), const((m, m)), const((8, m)), const((m, LANE_TILE)), const((m, LANE_TILE))])
    args = ([u] * (1 + HY_ORDER) + [conv_w, conv_b] + [conv_w] * HY_ORDER + [conv_b] * HY_ORDER
            + [ha, hb, hm, skip, cm, sm, alt8, twc, tws])
    n_in = len(args)
    aliases = {}
    if out_prev is not None:
        in_specs.append(pl.BlockSpec(memory_space=pl.ANY))
        args.append(out_prev)
        aliases = {len(args) - 1: 0}

    def body(*refs):
        k = HY_ORDER
        z_ref, g_refs = refs[0], refs[1:1 + k]
        wz_ref, bz_ref = refs[1 + k], refs[2 + k]
        wg_refs, bg_refs = refs[3 + k:3 + 2 * k], refs[3 + 2 * k:3 + 3 * k]
        _hy_conv_kernel(z_ref, g_refs, wz_ref, bz_ref, wg_refs, bg_refs, *refs[3 + 3 * k:n_in], *refs[-2:])

    return pl.pallas_call(
        body,
        out_shape=jax.ShapeDtypeStruct((nb, s, ch), BF16),
        grid=(nct, nb),
        in_specs=in_specs,
        out_specs=pl.BlockSpec((None, length, LANE_TILE), lambda c, b: (b, row_block, c)),
        scratch_shapes=[pltpu.VMEM((LANE_TILE // 128, length, 128), F32)],
        input_output_aliases=aliases,
        compiler_params=_params(("parallel", "parallel")),
        name="hyena_conv_%d" % length,
    )(*args)


def _rms(x, w):
    return x * lax.rsqrt(jnp.mean(x * x, axis=-1, keepdims=True) + EPS) * w


def _merge_mlp_kernel(yf_ref, yb_ref, xs_ref, zg_ref, dsk_ref, sn_ref, ym_ref, yh_ref, wb_ref, wo_ref,
                      x_ref, xc_ref, mod_ref, nw_ref, w1_ref, w2_ref, nf_ref, o_ref, *, final, n_lat_tiles):
    m = mod_ref[...]
    bw = yf_ref.shape[1]
    ys = yf_ref[...].astype(F32) + yb_ref[...].astype(F32) + dsk_ref[...] * xs_ref[...].astype(F32)
    ys = _rms(ys * _silu(zg_ref[:, 0:bw].astype(F32)), sn_ref[...]).astype(BF16)
    acc = None
    for n, y in enumerate((ys, ym_ref[...].astype(BF16), yh_ref[...])):
        gate = zg_ref[:, (1 + n) * bw:(2 + n) * bw].astype(F32)
        term = jax.nn.sigmoid(gate) * _dot(y, wb_ref[n])
        acc = term if acc is None else acc + term
    x_in = x_ref[...] if xc_ref is None else jnp.where(pl.program_id(1) < n_lat_tiles, x_ref[...], xc_ref[...])
    x = x_in + m[2:3] * _dot(acc.astype(BF16), wo_ref[...])
    h = (_rms(x, nw_ref[...]) * (1.0 + m[4:5]) + m[3:4]).astype(BF16)
    a = jnp.maximum(_dot(h, w1_ref[...]), 0.0)
    x = x + m[5:6] * _dot((a * a).astype(BF16), w2_ref[...])
    o_ref[...] = _rms(x, nf_ref[...]) if final else x


def _merge_mlp(yf, yb, xbc, pnat, d_full, ssd_norm, ym, yh, wb, wo, x, x_ctx, mod, norm_w, layer, w1, w2,
               norm_f, final, n_lat_tiles, n_tiles):
    nb, _, d = x.shape
    s = yf.shape[1]
    bw = yf.shape[-1]
    hidden = w1.shape[1]
    row = pl.BlockSpec((None, TOKEN_TILE, bw), lambda b, i: (b, i, 0))
    assert bw == d
    zg = pl.BlockSpec((None, TOKEN_TILE, (1 + N_BRANCH) * d), lambda b, i: (b, i, 0))
    xrow = pl.BlockSpec((None, TOKEN_TILE, d), lambda b, i: (b, i, 0))
    lay = lambda n: pl.BlockSpec((None, 1, n), lambda b, i: (layer, 0, 0))
    single = dict(pipeline_mode=pl.Buffered(1))
    const = lambda shape: pl.BlockSpec(shape, lambda b, i: (0,) * len(shape), **single)
    in_specs = [row, row, row, zg, lay(bw), lay(bw), row, row, const((N_BRANCH, bw, d)), const((d, d))]
    args = [yf, yb, xbc, pnat, d_full, ssd_norm, ym, yh, wb, wo]
    aliases = {}
    if x_ctx is None:
        in_specs.append(xrow)
        args.append(x)
        if not final:
            aliases = {len(args) - 1: 0}
    else:
        in_specs += [pl.BlockSpec((None, TOKEN_TILE, d), lambda b, i: (b, jnp.minimum(i, n_lat_tiles - 1), 0)),
                     pl.BlockSpec((None, TOKEN_TILE, d), lambda b, i: (b, jnp.maximum(i - n_lat_tiles, 0), 0))]
        args += [x, x_ctx]
    in_specs += [pl.BlockSpec((None, 6, d), _mod_row(nb, n_lat_tiles)), lay(d),
                 const((d, hidden)), const((hidden, d)), pl.BlockSpec((1, d), lambda b, i: (0, 0))]
    args += [mod, norm_w, w1, w2, norm_f]

    def body(*refs):
        head, tail = refs[:11], refs[11:]
        xc = None if x_ctx is None else tail[0]
        _merge_mlp_kernel(*head, xc, *tail[0 if x_ctx is None else 1:], final=final, n_lat_tiles=n_lat_tiles)

    return pl.pallas_call(
        body,
        out_shape=jax.ShapeDtypeStruct((nb, n_tiles * TOKEN_TILE if final else s, d), F32),
        grid=(nb, n_tiles),
        in_specs=in_specs,
        out_specs=xrow,
        input_output_aliases=aliases,
        compiler_params=_params(("parallel", "parallel")),
        name="merge_mlp",
    )(*args)


def _dir_rows(g, nb, s, per_dir):
    g = g[:, :2 * per_dir].reshape(nb, s, 2, per_dir)
    return jnp.transpose(g, (2, 0, 3, 1))


def kernel(x, c, ctx, c_ctx, norm1_w, mod_w, mod_b, w_in, ssd_conv_w, ssd_conv_b, ssd_dt_bias, ssd_a_log,
           ssd_d, ssd_norm_w, ml_conv_w, ml_conv_b, ml_gate_b, ml_norm_w, hy_conv_w, hy_conv_b, hy_ffn_w1,
           hy_ffn_b1, hy_ffn_w2, hy_ffn_b2, hy_ffn_w3, hy_decay, hy_skip, w_branch, w_out, norm2_w,
           mlp_w1, mlp_w2, norm_f_w):
    nb, seq, d = x.shape
    ctx_len = ctx.shape[1]
    depth = w_in.shape[0]
    s = seq + ctx_len
    assert seq % ctx_len == 0 and ctx_len % TOKEN_TILE == 0 and seq % GRID_W == 0
    n_lat_tiles, n_tiles = seq // TOKEN_TILE, s // TOKEN_TILE
    n_lat_ch, n_ch = seq // CHUNK, s // CHUNK
    assert n_lat_ch % SCAN_CHUNKS == 0 and n_ch % SCAN_CHUNKS == 0
    rows_g = seq // GRID_W

    ssd_heads = ssd_d.shape[-1]
    ssd_inner = ssd_norm_w.shape[-1]
    ssd_hd = ssd_inner // ssd_heads
    ssd_conv_ch = ssd_conv_w.shape[-1]
    ssd_ds = (ssd_conv_ch - ssd_inner) // (2 * SSD_GROUPS)
    ml_heads = ml_gate_b.shape[-1]
    ml_inner = ml_heads * ML_HEADDIM
    hy_inner = hy_skip.shape[-1]
    ssd_cols = ssd_conv_ch + ssd_inner + 2 * ssd_heads
    ml_cols = 4 * ml_inner + 4 * ml_heads
    rec_cols = ssd_cols + ml_cols
    hy_cols = (HY_ORDER + 1) * hy_inner

    o_z = ssd_conv_ch
    o_dt = ssd_conv_ch + ssd_inner
    o_ml = ssd_cols
    o_mlg = ssd_cols + 4 * ml_inner
    o_hy = rec_cols
    o_g = rec_cols + hy_cols

    xa, xa_ctx = x, ctx
    rpad = (-(nb + 1)) % 8
    c_all = jnp.concatenate([c, c_ctx[None], jnp.zeros((rpad, d), F32)], axis=0)

    tab_lat = _dft_tables(seq)
    tab_ctx = _dft_tables(ctx_len)
    k_scale = jnp.concatenate([jnp.ones((1, ml_inner), F32),
                               jnp.full((1, ml_inner), ML_HEADDIM ** -0.5, F32)], axis=1)

    norm1 = norm1_w[:, None, :]
    norm2 = norm2_w[:, None, :]
    ssd_norm = ssd_norm_w[:, None, :]
    ml_norm = ml_norm_w[:, None, :]
    d_full = jnp.repeat(ssd_d, ssd_hd, axis=-1)[:, None, :]
    w3 = hy_ffn_w3.reshape(depth, hy_ffn_w3.shape[1], HY_ORDER, 2, hy_inner).transpose(0, 2, 3, 1, 4)
    decay = hy_decay[:, :, :, None, :]
    skip = hy_skip[:, :, None, :]
    hb1 = hy_ffn_b1[:, None, :]
    hb2 = hy_ffn_b2[:, None, :]

    for l in range(depth):
        need_ctx = l < depth - 1
        used_tiles = n_tiles if need_ctx else n_lat_tiles
        mod = _mod_vectors(c_all, mod_w, mod_b[:, None, :], l).reshape(-1, 6, d)

        wl = w_in[l].astype(BF16)
        w_nat = jnp.concatenate([wl[:, o_z:o_dt], wl[:, o_g:]], axis=1)
        w_xbc = wl[:, :o_z]
        w_ml = wl[:, o_ml:o_mlg]
        gpad = lambda w: jnp.pad(w, ((0, 0), (0, 128 - w.shape[1])))
        hn, hn_cm = _normmod(xa, xa_ctx, norm1, l, mod, 0, rows_g, n_lat_tiles, n_tiles)
        pnat = _matmul(hn.reshape(nb * s, d), w_nat, BF16, "proj_nat").reshape(nb, s, -1)
        phy = _proj_tiles(hn, wl[:, o_hy:o_g])

        xbc, p_dt = _proj_conv(hn, w_xbc, gpad(wl[:, o_dt:o_ml]), ssd_conv_w[l], ssd_conv_b[l][None], None,
                               n_lat_tiles, n_tiles, "proj_xbc")
        dt_r = _dir_rows(p_dt.reshape(nb * s, -1), nb, s, ssd_heads)
        y_f, y_b = _ssd_scan(xbc, dt_r, ssd_dt_bias[l][:, :, None], ssd_a_log[l][:, :, None],
                             ssd_heads, ssd_hd, ssd_ds, n_lat_ch, n_ch)

        pml, p_mlg = _proj_conv(hn_cm, w_ml, gpad(wl[:, o_mlg:o_hy]), ml_conv_w[l], ml_conv_b[l][None],
                                k_scale, n_lat_tiles, n_tiles, "proj_ml")
        g_r = _dir_rows(p_mlg.reshape(nb * s, -1), nb, s, 2 * ml_heads)
        gate_b = ml_gate_b[l]
        c_f, c_b = _ml_scan(pml, pml, g_r[:, :, :ml_heads], g_r[:, :, ml_heads:],
                            gate_b[:, 0, :, None], gate_b[:, 1, :, None], ml_heads, n_lat_ch, n_ch)
        ym = _ml_out(c_f, c_b, pml, ml_norm, l, ml_heads, rows_g, n_lat_tiles, used_tiles)

        hcw, hcb = hy_conv_w[l], hy_conv_b[l][None]
        fl = _hy_filters(seq, l, hy_ffn_w1, hb1, hy_ffn_w2, hb2, w3, decay, tab_lat)
        rows = s if need_ctx else seq
        yh = _hy_conv(phy, hy_inner, hcw, hcb, None, rows, 0, seq, l, *fl, skip, tab_lat)
        if need_ctx:
            fc = _hy_filters(ctx_len, l, hy_ffn_w1, hb1, hy_ffn_w2, hb2, w3, decay, tab_ctx)
            yh = _hy_conv(phy, hy_inner, hcw, hcb, yh, rows, seq // ctx_len, ctx_len, l, *fc, skip, tab_ctx)

        xa = _merge_mlp(y_f, y_b, xbc, pnat, d_full, ssd_norm, ym, yh, w_branch[l].astype(BF16),
                        w_out[l].astype(BF16), xa, xa_ctx, mod, norm2, l, mlp_w1[l].astype(BF16),
                        mlp_w2[l].astype(BF16), norm_f_w[None], not need_ctx, n_lat_tiles, used_tiles)
        xa_ctx = None

    return xa
```

```python
import functools
import math

import jax
import jax.numpy as jnp
import numpy as np
from jax import lax
from jax.experimental import pallas as pl
from jax.experimental.pallas import tpu as pltpu

F32 = jnp.float32
BF16 = jnp.bfloat16
HIGHEST = lax.Precision.HIGHEST

GRID_W = 64
CHUNK = 128
EPS = 1e-6
SSD_GROUPS = 2
SSD_CONV = 5
ML_HEADDIM = 128
ML_CONV = 5
HY_ORDER = 2
HY_SHORT = 3
HY_BANDS = 16
N_BRANCH = 3

TOKEN_TILE = 256
LANE_TILE = 256
HALO = 16
SCAN_CHUNKS = 2
HY_ROWS = 16
VMEM_LIMIT = 56 * 1024 * 1024

_hdot = functools.partial(jnp.dot, precision=HIGHEST, preferred_element_type=F32)
_dot = functools.partial(jnp.dot, preferred_element_type=F32)


def _params(sem, vmem=None):
    return pltpu.CompilerParams(dimension_semantics=sem, vmem_limit_bytes=vmem or VMEM_LIMIT)


def _softplus(x):
    return jnp.maximum(x, 0.0) + jnp.log(1.0 + jnp.exp(-jnp.abs(x)))


def _silu(x):
    return x * jax.nn.sigmoid(x)


def _mod_kernel(c_ref, w_ref, b_ref, o_ref):
    o_ref[...] = _hdot(_silu(c_ref[...]), w_ref[...]) + b_ref[...]


def _mod_vectors(c_all, mod_w, mod_b, layer):
    r, d = c_all.shape
    n = mod_w.shape[-1]
    tn = n // 6
    return pl.pallas_call(
        _mod_kernel,
        out_shape=jax.ShapeDtypeStruct((r, n), F32),
        grid=(n // tn,),
        in_specs=[pl.BlockSpec((r, d), lambda j: (0, 0)),
                  pl.BlockSpec((None, d, tn), lambda j: (layer, 0, j)),
                  pl.BlockSpec((None, 1, tn), lambda j: (layer, 0, j))],
        out_specs=pl.BlockSpec((r, tn), lambda j: (0, j)),
        compiler_params=_params(("parallel",)),
        name="mod_vectors",
    )(c_all, mod_w, mod_b)


def _grid_perm(rows_g):
    wpt = TOKEN_TILE // rows_g
    src = np.arange(TOKEN_TILE).reshape(rows_g, wpt).T.reshape(-1)
    p = np.zeros((TOKEN_TILE, TOKEN_TILE), np.float32)
    p[np.arange(TOKEN_TILE), src] = 1.0
    return p


def _normmod_kernel(x_ref, xc_ref, x4_ref, perm_ref, nw_ref, mod_ref, o_ref, ocm_ref, *, si, n_lat_tiles):
    i = pl.program_id(1)
    m = mod_ref[...]

    def normed(x):
        h = x * lax.rsqrt(jnp.mean(x * x, axis=-1, keepdims=True) + EPS) * nw_ref[...]
        return (h * (1.0 + m[si + 1:si + 2]) + m[si:si + 1]).astype(BF16)

    h = normed(x_ref[...] if xc_ref is None else jnp.where(i < n_lat_tiles, x_ref[...], xc_ref[...]))
    o_ref[...] = h

    @pl.when(i < n_lat_tiles)
    def _():
        x4 = x4_ref[...]
        ocm_ref[...] = _dot(perm_ref[...], normed(x4.reshape(TOKEN_TILE, x4.shape[-1]))).astype(BF16)

    @pl.when(i >= n_lat_tiles)
    def _():
        ocm_ref[...] = h


def _mod_row(nb, n_lat_tiles):
    return lambda b, i: (jnp.where(i < n_lat_tiles, b, nb), 0, 0)


def _normmod(x, x_ctx, norm_w, layer, mod, si, rows_g, n_lat_tiles, n_tiles):
    nb, rows, d = x.shape
    s = n_tiles * TOKEN_TILE
    wpt = TOKEN_TILE // rows_g
    assert wpt % 8 == 0 and GRID_W % wpt == 0 and rows % GRID_W == 0
    x4 = x.reshape(nb, rows // GRID_W, GRID_W, d)
    tile = pl.BlockSpec((None, TOKEN_TILE, d), lambda b, i: (b, i, 0))
    if x_ctx is None:
        srcs, src_specs = [x], [tile]
    else:
        srcs = [x, x_ctx]
        src_specs = [pl.BlockSpec((None, TOKEN_TILE, d), lambda b, i: (b, jnp.minimum(i, n_lat_tiles - 1), 0)),
                     pl.BlockSpec((None, TOKEN_TILE, d), lambda b, i: (b, jnp.maximum(i - n_lat_tiles, 0), 0))]

    def body(*refs):
        xc = None if x_ctx is None else refs[1]
        _normmod_kernel(refs[0], xc, *refs[len(srcs):], si=si, n_lat_tiles=n_lat_tiles)

    return pl.pallas_call(
        body,
        out_shape=(jax.ShapeDtypeStruct((nb, s, d), BF16),) * 2,
        grid=(nb, n_tiles),
        in_specs=src_specs + [
            pl.BlockSpec((None, rows_g, wpt, d), lambda b, i: (b, 0, jnp.minimum(i, n_lat_tiles - 1), 0)),
            pl.BlockSpec((TOKEN_TILE, TOKEN_TILE), lambda b, i: (0, 0)),
            pl.BlockSpec((None, 1, d), lambda b, i: (layer, 0, 0)),
            pl.BlockSpec((None, 6, d), _mod_row(nb, n_lat_tiles))],
        out_specs=(tile, tile),
        compiler_params=_params(("parallel", "parallel")),
        name="normmod",
    )(*srcs, x4, jnp.asarray(_grid_perm(rows_g), BF16), norm_w, mod)


def _mm_kernel(a_ref, w_ref, o_ref):
    o_ref[...] = _dot(a_ref[...], w_ref[...]).astype(o_ref.dtype)


def _pick(n, cands):
    for c in cands:
        if n % c == 0:
            return c
    return n


def _matmul(a, w, out_dtype, name):
    t, k = a.shape
    n = w.shape[1]
    tm = _pick(t, (2048, 1024, 768, 512, 256))
    tn = _pick(n, (1024, 512, 256, 128))
    return pl.pallas_call(
        _mm_kernel,
        out_shape=jax.ShapeDtypeStruct((t, n), out_dtype),
        grid=(t // tm, n // tn),
        in_specs=[pl.BlockSpec((tm, k), lambda i, j: (i, 0)),
                  pl.BlockSpec((k, tn), lambda i, j: (0, j))],
        out_specs=pl.BlockSpec((tm, tn), lambda i, j: (i, j)),
        compiler_params=_params(("parallel", "parallel")),
        name=name,
    )(a, w)


def _shift_stack(taps):
    pad = taps // 2
    return np.concatenate([np.eye(TOKEN_TILE, k=j - pad, dtype=np.float32) for j in range(taps) if j != pad])


def _conv_tile(u, prev, nxt, sh_ref, w, b, scale):
    tm = u.shape[0]
    taps = w.shape[0]
    pad = taps // 2

    def finish(acc):
        acc = _silu(acc)
        return (acc if scale is None else acc * scale).astype(BF16)

    def edge(window):
        rows = window.shape[0]
        acc = b + window[HALO:2 * HALO] * w[pad:pad + 1]
        for j in range(taps):
            if j != pad:
                acc = acc + pltpu.roll(window, (pad - j) % rows, 0)[HALO:2 * HALO] * w[j:j + 1]
        return finish(acc)

    shifted = _dot(sh_ref[...], u)
    acc = b + u.astype(F32) * w[pad:pad + 1]
    blk = 0
    for j in range(taps):
        if j != pad:
            acc = acc + shifted[blk * tm:(blk + 1) * tm] * w[j:j + 1]
            blk += 1
    head = edge(jnp.concatenate([prev, u[0:2 * HALO].astype(F32)], axis=0))
    tail = edge(jnp.concatenate([u[tm - 2 * HALO:tm].astype(F32), nxt], axis=0))
    return jnp.concatenate([head, finish(acc)[HALO:tm - HALO], tail], axis=0)


def _proj_conv_kernel(a_ref, w_ref, wg_ref, sh_ref, cw_ref, cb_ref, *rest, n_conv, bounds):
    o_ref, og_ref, buf_ref = rest[-3:]
    s_ref = rest[0] if len(rest) == 4 else None
    j = pl.program_id(1)
    acc = _dot(a_ref[...], w_ref[...])

    @pl.when(j == 0)
    def _():
        og_ref[...] = _dot(a_ref[...], wg_ref[...])

    @pl.when(j >= n_conv)
    def _():
        o_ref[...] = acc.astype(o_ref.dtype)

    @pl.when(j < n_conv)
    def _():
        buf_ref[...] = acc.astype(BF16)
        w, b = cw_ref[...], cb_ref[...]
        scale = None if s_ref is None else s_ref[...]
        zeros = jnp.zeros((HALO, buf_ref.shape[1]), F32)
        for blk in range(buf_ref.shape[0] // TOKEN_TILE):
            r0 = blk * TOKEN_TILE
            prev = zeros if blk in bounds else buf_ref[r0 - HALO:r0, :].astype(F32)
            nxt = zeros if blk + 1 in bounds else buf_ref[r0 + TOKEN_TILE:r0 + TOKEN_TILE + HALO, :].astype(F32)
            o_ref[r0:r0 + TOKEN_TILE, :] = _conv_tile(buf_ref[r0:r0 + TOKEN_TILE, :], prev, nxt, sh_ref, w, b, scale)


def _proj_conv(a, w, w_gate, conv_w, conv_b, scale, n_lat_tiles, n_tiles, name):
    nb, s, k = a.shape
    n = w.shape[1]
    taps, nc = conv_w.shape
    ng = w_gate.shape[1]
    tn = _pick(math.gcd(n, nc), (512, 256))
    n_conv = nc // tn
    cidx = lambda b, j: (0, jnp.minimum(j, n_conv - 1))
    in_specs = [pl.BlockSpec((None, s, k), lambda b, j: (b, 0, 0)),
                pl.BlockSpec((k, tn), lambda b, j: (0, j)),
                pl.BlockSpec((k, ng), lambda b, j: (0, 0)),
                pl.BlockSpec(((taps - 1) * TOKEN_TILE, TOKEN_TILE), lambda b, j: (0, 0)),
                pl.BlockSpec((taps, tn), cidx),
                pl.BlockSpec((1, tn), cidx)]
    args = [a, w, w_gate, jnp.asarray(_shift_stack(taps), BF16), conv_w, conv_b]
    if scale is not None:
        in_specs.append(pl.BlockSpec((1, tn), cidx))
        args.append(scale)
    return pl.pallas_call(
        functools.partial(_proj_conv_kernel, n_conv=n_conv, bounds=(0, n_lat_tiles, n_tiles)),
        out_shape=(jax.ShapeDtypeStruct((nb, s, n), BF16), jax.ShapeDtypeStruct((nb, s, ng), F32)),
        grid=(nb, n // tn),
        in_specs=in_specs,
        out_specs=(pl.BlockSpec((None, s, tn), lambda b, j: (b, 0, j)),
                   pl.BlockSpec((None, s, ng), lambda b, j: (b, 0, 0))),
        scratch_shapes=[pltpu.VMEM((s, tn), BF16)],
        compiler_params=_params(("parallel", "arbitrary")),
        name=name,
    )(*args)


def _chunk_index(n_lat_ch, n_ch):
    return lambda d, j: (j + n_lat_ch) % n_ch if d == 0 else n_ch - 1 - j


def _scan_masks(d, t):
    ii = lax.broadcasted_iota(jnp.int32, (t, t), 0)
    jj = lax.broadcasted_iota(jnp.int32, (t, t), 1)
    mask = jj <= ii if d == 0 else jj >= ii
    tri_t = jnp.where(ii <= jj if d == 0 else ii >= jj, 1.0, 0.0).astype(F32)
    return mask, tri_t


def _gate_rows(la_r, lw_r, tri_t, m_prev):
    cum_r = _hdot(la_r, tri_t)
    last = jnp.sum(la_r, axis=1, keepdims=True)
    g_r = last - cum_r + lw_r
    m_loc = jnp.max(g_r, axis=1, keepdims=True)
    e_r = jnp.exp(g_r - m_loc)
    m_new = jnp.maximum(last + m_prev, m_loc)
    s_old = jnp.exp(last + m_prev - m_new)
    s_new = jnp.exp(m_loc - m_new)
    return cum_r, cum_r - lw_r, e_r, m_new, s_old, s_new


def _head_probs(cum_row, crow_row, m_prev_h, mask, qk):
    t = qk.shape[0]
    colb = jnp.broadcast_to(cum_row, (t, t)).T
    dlog = jnp.where(mask, colb - crow_row, -jnp.inf)
    inter = colb[:, 0:1] + m_prev_h
    m_row = jnp.maximum(inter, jnp.max(dlog, axis=1, keepdims=True))
    p = jnp.exp(dlog - m_row) * qk
    return p.astype(BF16), jnp.exp(inter - m_row), m_row


def _ssd_scan_kernel(*refs, heads, groups, hd, ds):
    j = pl.program_id(1)
    ins, outs, states = refs[:12], refs[12:14], refs[14:16]

    @pl.when(j == 0)
    def _():
        for st_ref in states:
            st_ref[...] = jnp.zeros_like(st_ref)

    for sub in range(SCAN_CHUNKS):
        for d in range(2):
            _ssd_chunk(d, _scan_rows(d, sub), *ins[6 * d:6 * d + 6], outs[d], states[d],
                       heads=heads, groups=groups, hd=hd, ds=ds)


def _scan_rows(d, sub):
    first = (sub if d == 0 else SCAN_CHUNKS - 1 - sub) * CHUNK
    return slice(first, first + CHUNK)


def _ssd_chunk(d, rs, xs_ref, b_ref, c_ref, dt_ref, bias_ref, alog_ref, o_ref, st_ref, *, heads, groups, hd, ds):
    t = CHUNK
    hpg = heads // groups
    pw = 2 * hd
    mask, tri_t = _scan_masks(d, t)
    dt = _softplus(dt_ref[:, rs] + bias_ref[...])
    la_r = -dt * jnp.exp(alog_ref[...])
    cum_r = _hdot(la_r, tri_t)
    last = jnp.sum(la_r, axis=1, keepdims=True)
    crow = cum_r - jnp.log(dt)
    e_r = jnp.exp(last - crow)
    e_last = jnp.exp(last)
    lo = lax.broadcasted_iota(jnp.int32, (1, pw), 1) < hd

    for g in range(groups):
        q = c_ref[rs, g * ds:(g + 1) * ds]
        k_t = b_ref[rs, g * ds:(g + 1) * ds].astype(F32).T
        qk = _dot(q, k_t.astype(BF16))
        w0 = g * hpg * hd
        qs = _dot(q, st_ref[:, w0:w0 + hpg * hd].astype(BF16))
        for i in range(hpg // 2):
            h0 = g * hpg + 2 * i
            c0 = h0 * hd
            vp = xs_ref[rs, c0:c0 + pw]
            zero = jnp.zeros_like(vp)
            v_bd = jnp.concatenate([jnp.where(lo, vp, zero), jnp.where(lo, zero, vp)], axis=0)
            probs, carry, kte = [], [], []
            for h in (h0, h0 + 1):
                colb = jnp.broadcast_to(cum_r[h:h + 1], (t, t)).T
                dlog = jnp.where(mask, colb - crow[h:h + 1], -jnp.inf)
                probs.append((jnp.exp(dlog) * qk).astype(BF16))
                carry.append(jnp.exp(colb))
                kte.append((k_t * e_r[h:h + 1]).astype(BF16))
            lhs = jnp.concatenate([jnp.concatenate(probs, axis=1), jnp.concatenate(kte, axis=1)], axis=0)
            res = _dot(lhs, v_bd)
            y = res[:t] + qs[:, 2 * i * hd:2 * i * hd + pw] * jnp.where(lo, carry[0], carry[1])
            o_ref[rs, c0:c0 + pw] = y.astype(o_ref.dtype)
            decay = jnp.where(lo, e_last[h0:h0 + 1], e_last[h0 + 1:h0 + 2])
            st_ref[:, c0:c0 + pw] = decay * st_ref[:, c0:c0 + pw] + res[t:]


def _ssd_scan(xbc, dt_r, bias_c, alog_c, heads, hd, ds, n_lat_ch, n_ch):
    nb, s, _ = xbc.shape
    inner = heads * hd
    gn = SSD_GROUPS * ds
    assert 2 * hd == CHUNK and (heads // SSD_GROUPS) % 2 == 0 and inner % gn == 0
    rows = SCAN_CHUNKS * CHUNK
    n_blk = n_ch // SCAN_CHUNKS
    cidx = _chunk_index(n_lat_ch // SCAN_CHUNKS, n_blk)

    def specs(d):
        return [pl.BlockSpec((None, rows, inner), lambda b, j: (b, cidx(d, j), 0)),
                pl.BlockSpec((None, rows, gn), lambda b, j: (b, cidx(d, j), inner // gn)),
                pl.BlockSpec((None, rows, gn), lambda b, j: (b, cidx(d, j), inner // gn + 1)),
                pl.BlockSpec((None, None, heads, rows), lambda b, j: (d, b, 0, cidx(d, j))),
                pl.BlockSpec((None, heads, 1), lambda b, j: (d, 0, 0)),
                pl.BlockSpec((None, heads, 1), lambda b, j: (d, 0, 0))]

    out = lambda d: pl.BlockSpec((None, rows, inner), lambda b, j: (b, cidx(d, j), 0))
    args = (xbc, xbc, xbc, dt_r, bias_c, alog_c)
    return pl.pallas_call(
        functools.partial(_ssd_scan_kernel, heads=heads, groups=SSD_GROUPS, hd=hd, ds=ds),
        out_shape=(jax.ShapeDtypeStruct((nb, s, inner), BF16),) * 2,
        grid=(nb, n_blk),
        in_specs=specs(0) + specs(1),
        out_specs=(out(0), out(1)),
        scratch_shapes=[pltpu.VMEM((ds, inner), F32)] * 2,
        compiler_params=_params(("parallel", "arbitrary")),
        name="ssd_scan",
    )(*args, *args)


def _ml_scan_kernel(*refs, heads):
    j = pl.program_id(1)
    ins, outs, states = refs[:14], refs[14:16], refs[16:20]

    @pl.when(j == 0)
    def _():
        for ref in states:
            ref[...] = jnp.zeros_like(ref)

    for sub in range(SCAN_CHUNKS):
        for d in range(2):
            _ml_chunk(d, _scan_rows(d, sub), *ins[7 * d:7 * d + 7], outs[d], *states[2 * d:2 * d + 2],
                      heads=heads)


def _ml_chunk(d, rs, q_ref, k_ref, v_ref, gi_ref, gf_ref, bi_ref, bf_ref, o_ref, st_ref, m_ref, *, heads):
    t = CHUNK
    dh = ML_HEADDIM
    mask, tri_t = _scan_masks(d, t)
    lw_r = gi_ref[:, rs] + bi_ref[...]
    f = gf_ref[:, rs] + bf_ref[...]
    la_r = jnp.minimum(f, 0.0) - jnp.log(1.0 + jnp.exp(-jnp.abs(f)))
    m_prev = m_ref[:, 0:1]
    cum_r, crow, e_r, m_new, s_old, s_new = _gate_rows(la_r, lw_r, tri_t, m_prev)
    ones = jnp.ones((t, dh), BF16)

    for h in range(heads):
        c0 = h * dh
        q = q_ref[rs, c0:c0 + dh]
        k_t = k_ref[rs, c0:c0 + dh].astype(F32).T
        v_aug = jnp.concatenate([v_ref[rs, c0:c0 + dh], ones], axis=1)
        qk = _dot(q, k_t.astype(BF16))
        s0 = 2 * c0
        qs = _dot(q, st_ref[:, s0:s0 + 2 * dh].astype(BF16))
        p, cf, m_row = _head_probs(cum_r[h:h + 1], crow[h:h + 1], m_prev[h:h + 1], mask, qk)
        res = _dot(jnp.concatenate([p, (k_t * e_r[h:h + 1]).astype(BF16)], axis=0), v_aug)
        y = res[:t] + qs * cf
        cell = y[:, :dh] / jnp.maximum(jnp.abs(y[:, dh:]), jnp.exp(-m_row))
        o_ref[rs, c0:c0 + dh] = cell.astype(o_ref.dtype)
        st_ref[:, s0:s0 + 2 * dh] = s_old[h:h + 1] * st_ref[:, s0:s0 + 2 * dh] + s_new[h:h + 1] * res[t:]
    m_ref[...] = jnp.broadcast_to(m_new, m_ref.shape)


def _ml_scan(qk, pml, gi_r, gf_r, bi_c, bf_c, heads, n_lat_ch, n_ch):
    nb, s, _ = qk.shape
    inner = heads * ML_HEADDIM
    rows = SCAN_CHUNKS * CHUNK
    n_blk = n_ch // SCAN_CHUNKS
    cidx = _chunk_index(n_lat_ch // SCAN_CHUNKS, n_blk)

    def specs(d):
        gspec = pl.BlockSpec((None, None, heads, rows), lambda b, j: (d, b, 0, cidx(d, j)))
        bspec = pl.BlockSpec((None, heads, 1), lambda b, j: (d, 0, 0))
        col = lambda c: pl.BlockSpec((None, rows, inner), lambda b, j: (b, cidx(d, j), c))
        return [col(0), col(1), col(2), gspec, gspec, bspec, bspec]

    out = lambda d: pl.BlockSpec((None, rows, inner), lambda b, j: (b, cidx(d, j), 0))
    args = (qk, qk, pml, gi_r, gf_r, bi_c, bf_c)
    return pl.pallas_call(
        functools.partial(_ml_scan_kernel, heads=heads),
        out_shape=(jax.ShapeDtypeStruct((nb, s, inner), BF16),) * 2,
        grid=(nb, n_blk),
        in_specs=specs(0) + specs(1),
        out_specs=(out(0), out(1)),
        scratch_shapes=[pltpu.VMEM((ML_HEADDIM, 2 * inner), F32), pltpu.VMEM((heads, 128), F32)] * 2,
        compiler_params=_params(("parallel", "arbitrary")),
        name="ml_scan",
    )(*args, *args)


def _ml_out_kernel(cf_ref, cb_ref, o_in_ref, nw_ref, perm_ref, *rest, heads, permute):
    o_ref = rest[-1]
    dh = ML_HEADDIM
    parts = []
    for h in range(heads):
        sl = slice(h * dh, (h + 1) * dh)
        c = cf_ref[:, sl].astype(F32) + cb_ref[:, sl].astype(F32)
        c = c * lax.rsqrt(jnp.mean(c * c, axis=-1, keepdims=True) + EPS) * nw_ref[:, sl]
        parts.append((jax.nn.sigmoid(o_in_ref[:, sl].astype(F32)) * c).astype(BF16))
    y = jnp.concatenate(parts, axis=1)
    y = _dot(perm_ref[...], y) if permute else y.astype(F32)
    o_ref[...] = y.reshape(o_ref.shape)


def _ml_out(cf, cb, pml, norm_w, layer, heads, rows_g, n_lat_tiles, n_tiles):
    nb, _, inner = cf.shape
    wpt = TOKEN_TILE // rows_g
    rpt = TOKEN_TILE // GRID_W
    perm_t = jnp.asarray(_grid_perm(rows_g).T, BF16)
    out_shape = jax.ShapeDtypeStruct((nb, n_tiles * rpt, GRID_W, inner), F32)

    def call(tile0, tiles, out_spec, permute, prev):
        row = pl.BlockSpec((None, TOKEN_TILE, inner), lambda b, i: (b, tile0 + i, 0))
        in_specs = [row, row,
                    pl.BlockSpec((None, TOKEN_TILE, inner), lambda b, i: (b, tile0 + i, 3)),
                    pl.BlockSpec((None, 1, inner), lambda b, i: (layer, 0, 0)),
                    pl.BlockSpec((TOKEN_TILE, TOKEN_TILE), lambda b, i: (0, 0))]
        args = [cf, cb, pml, norm_w, perm_t]
        if prev is not None:
            in_specs.append(pl.BlockSpec(memory_space=pl.ANY))
            args.append(prev)
        return pl.pallas_call(
            functools.partial(_ml_out_kernel, heads=heads, permute=permute),
            out_shape=out_shape,
            grid=(nb, tiles),
            in_specs=in_specs,
            out_specs=out_spec,
            input_output_aliases={} if prev is None else {len(args) - 1: 0},
            compiler_params=_params(("parallel", "parallel")),
            name="ml_out_lat" if permute else "ml_out_ctx",
        )(*args)

    y = call(0, n_lat_tiles, pl.BlockSpec((None, rows_g, wpt, inner), lambda b, i: (b, 0, i, 0)), True, None)
    if n_tiles > n_lat_tiles:
        y = call(n_lat_tiles, n_tiles - n_lat_tiles,
                 pl.BlockSpec((None, rpt, GRID_W, inner), lambda b, i: (b, n_lat_tiles + i, 0, 0)), False, y)
    return y.reshape(nb, n_tiles * TOKEN_TILE, inner)


def _dft_tables(length):
    m = length // 2
    k = np.arange(m, dtype=np.int64)
    ang = ((k[:, None] * k[None, :]) % (2 * m)).astype(np.float64) * (np.pi / m)
    alt8 = np.broadcast_to(np.where(k % 2 == 0, 1.0, -1.0)[None, :], (8, m))
    tw = k.astype(np.float64)[:, None] * (np.pi / length) * np.ones((1, LANE_TILE))
    return (jnp.asarray(np.cos(ang), BF16), jnp.asarray(np.sin(ang), BF16), jnp.asarray(alt8, BF16),
            jnp.asarray(np.cos(tw), F32), jnp.asarray(np.sin(tw), F32))


def _hy_feats(length):
    t = jnp.arange(length, dtype=F32)
    t_norm = t / length
    bands = jnp.linspace(1e-4, HY_BANDS - 1, HY_BANDS, dtype=F32)
    ang = (2.0 * math.pi / length) * t[:, None] * bands[None, :]
    feats = jnp.concatenate([t_norm[:, None], jnp.cos(ang), -jnp.sin(ang)], axis=-1)
    return feats[0::2], feats[1::2], t_norm[0::2, None], t_norm[1::2, None]


def _split_spectrum(ae, be, ao, bo, twc, tws):
    tr = twc * ao - tws * bo
    tm = twc * bo + tws * ao
    return (ae + tr, ae - tr), (be + tm, tm - be)


def _hy_filter_kernel(fe_ref, fo_ref, tne_ref, tno_ref, w1_ref, b1_ref, w2_ref, b2_ref, w3f_ref, w3b_ref,
                      df_ref, db_ref, cm_ref, sm_ref, alt_ref, twc_ref, tws_ref, ha_ref, hb_ref, hm_ref,
                      hide_ref, hido_ref):
    m = cm_ref.shape[0]

    @pl.when((pl.program_id(0) == 0) & (pl.program_id(1) == 0))
    def _():
        for f_ref, h_ref in ((fe_ref, hide_ref), (fo_ref, hido_ref)):
            hid = jnp.sin(_hdot(f_ref[...], w1_ref[...]) + b1_ref[...])
            h_ref[...] = jnp.sin(_hdot(hid, w2_ref[...]) + b2_ref[...])

    def taps(hid, tn):
        h_f = _hdot(hid, w3f_ref[...]) * jnp.exp(-tn * jnp.abs(df_ref[...]))
        h_b = _hdot(hid, w3b_ref[...]) * jnp.exp(-tn * jnp.abs(db_ref[...]))
        return (h_f + h_b).astype(BF16), (h_f - h_b).astype(BF16)

    sum_e, dif_e = taps(hide_ref[...], tne_ref[...])
    sum_o, dif_o = taps(hido_ref[...], tno_ref[...])
    twc, tws = twc_ref[...], tws_ref[...]
    cm, sm = cm_ref[...], sm_ref[...]
    (ha_lo, ha_hi), _ = _split_spectrum(_dot(cm, sum_e), 0.0, _dot(cm, sum_o), _dot(sm, sum_o), twc, tws)
    _, (hb_lo, hb_hi) = _split_spectrum(0.0, _dot(sm, dif_e), _dot(cm, dif_o), _dot(sm, dif_o), twc, tws)
    ha_ref[0:m, :] = ha_lo
    ha_ref[m:2 * m, :] = ha_hi
    hb_ref[0:m, :] = hb_lo
    hb_ref[m:2 * m, :] = hb_hi
    row = lax.broadcasted_iota(jnp.int32, hm_ref.shape, 0)
    hm_ref[...] = jnp.where(row == 0, _dot(alt_ref[...], sum_e), _dot(alt_ref[...], dif_o))


def _hy_filters(length, layer, w1, b1, w2, b2, w3, decay, tables):
    cm, sm, alt8, twc, tws = tables
    m = length // 2
    feats = _hy_feats(length)
    nfeat, nf = w1.shape[-2:]
    ch = decay.shape[-1]
    nct = ch // LANE_TILE
    const = lambda shape: pl.BlockSpec(shape, lambda n, c: (0,) * len(shape))
    w3spec = lambda dr: pl.BlockSpec((None, None, None, nf, LANE_TILE), lambda n, c: (layer, n, dr, 0, c))
    dspec = lambda dr: pl.BlockSpec((None, None, None, 1, LANE_TILE), lambda n, c: (layer, n, dr, 0, c))
    lay = lambda a, b: pl.BlockSpec((None, a, b), lambda n, c: (layer, 0, 0))
    out = lambda rows: pl.BlockSpec((None, rows, LANE_TILE), lambda n, c: (n, 0, c))
    return pl.pallas_call(
        _hy_filter_kernel,
        out_shape=(jax.ShapeDtypeStruct((HY_ORDER, length, ch), F32),
                   jax.ShapeDtypeStruct((HY_ORDER, length, ch), F32),
                   jax.ShapeDtypeStruct((HY_ORDER, 8, ch), F32)),
        grid=(HY_ORDER, nct),
        in_specs=[const((m, nfeat)), const((m, nfeat)), const((m, 1)), const((m, 1)),
                  lay(nfeat, nf), lay(1, nf), lay(nf, nf), lay(1, nf),
                  w3spec(0), w3spec(1), dspec(0), dspec(1),
                  const((m, m)), const((m, m)), const((8, m)),
                  const((m, LANE_TILE)), const((m, LANE_TILE))],
        out_specs=(out(length), out(length), out(8)),
        scratch_shapes=[pltpu.VMEM((m, nf), F32)] * 2,
        compiler_params=_params(("arbitrary", "arbitrary")),
        name="hyena_filters_%d" % length,
    )(*feats, w1, b1, w2, b2, w3, w3, decay, decay, cm, sm, alt8, twc, tws)


def _short_conv(ue, uo, w, b):
    m = ue.shape[0]
    row = lax.broadcasted_iota(jnp.int32, ue.shape, 0)
    prev_odd = jnp.where(row == 0, 0.0, pltpu.roll(uo, 1, 0))
    next_even = jnp.where(row == m - 1, 0.0, pltpu.roll(ue, m - 1, 0))
    w0, w1, w2 = w[0:1], w[1:2], w[2:3]
    return b + w0 * prev_odd + w1 * ue + w2 * uo, b + w0 * ue + w1 * uo + w2 * next_even


def _hy_conv_kernel(z_ref, g_refs, wz_ref, bz_ref, wg_refs, bg_refs, ha_ref, hb_ref, hm_ref, skip_ref,
                    cm_ref, sm_ref, alt_ref, twc_ref, tws_ref, o_ref,
                    tmp_ref, ze_ref, zo_ref, ge_ref, go_ref, a_ref, b_ref, p_ref):
    width = z_ref.shape[1]
    m = cm_ref.shape[0]
    lanes = tmp_ref.shape[-1]
    slabs = range(width // lanes)

    def split(ref):
        for h in slabs:
            tmp_ref[h] = ref[:, h * lanes:(h + 1) * lanes].astype(F32)
        return [jnp.concatenate([tmp_ref[h, pl.ds(first, m, stride=2), :] for h in slabs], axis=1)
                for first in (0, 1)]

    ze_ref[...], zo_ref[...] = _short_conv(*split(z_ref), wz_ref[...], bz_ref[...])
    for n in range(HY_ORDER):
        ge_ref[...], go_ref[...] = _short_conv(*split(g_refs[n]), wg_refs[n][...], bg_refs[n][...])
        _hy_order(n, ze_ref, zo_ref, ge_ref, go_ref, ha_ref, hb_ref, hm_ref[n], skip_ref[n],
                  cm_ref, sm_ref, alt_ref, twc_ref, tws_ref, a_ref, b_ref, p_ref)
    for h in slabs:
        tmp_ref[h, pl.ds(0, m, stride=2), :] = ze_ref[:, h * lanes:(h + 1) * lanes]
        tmp_ref[h, pl.ds(1, m, stride=2), :] = zo_ref[:, h * lanes:(h + 1) * lanes]
        o_ref[:, h * lanes:(h + 1) * lanes] = tmp_ref[h].astype(o_ref.dtype)


def _hy_order(n, ze_ref, zo_ref, ge_ref, go_ref, ha_ref, hb_ref, hm, skip, cm_ref, sm_ref, alt_ref,
              twc_ref, tws_ref, a_ref, b_ref, p_ref):
    m, width = ze_ref.shape
    zz = jnp.concatenate([ze_ref[...].astype(BF16), zo_ref[...].astype(BF16)], axis=1)
    a_ref[...] = _dot(cm_ref[...], zz)
    b_ref[...] = _dot(sm_ref[...], zz)
    mid = _dot(alt_ref[...], zz)[0:1]
    ha_m, hb_m = hm[0:1, :], hm[1:2, :]
    yr_m = mid[:, :width] * ha_m - mid[:, width:] * hb_m
    ym_m = mid[:, :width] * hb_m + mid[:, width:] * ha_m
    local = lax.broadcasted_iota(jnp.int32, (HY_ROWS, width), 0)

    def spectrum(r, carry):
        lo = pl.ds(pl.multiple_of(r * HY_ROWS, HY_ROWS), HY_ROWS)
        hi = pl.ds(pl.multiple_of(m + r * HY_ROWS, HY_ROWS), HY_ROWS)
        twc, tws = twc_ref[lo, :], tws_ref[lo, :]
        (a_lo, a_hi), (b_lo, b_hi) = _split_spectrum(a_ref[lo, 0:width], b_ref[lo, 0:width],
                                                     a_ref[lo, width:2 * width], b_ref[lo, width:2 * width],
                                                     twc, tws)
        ha_lo, ha_hi, hb_lo, hb_hi = ha_ref[n, lo, :], ha_ref[n, hi, :], hb_ref[n, lo, :], hb_ref[n, hi, :]
        yr_lo, ym_lo = a_lo * ha_lo - b_lo * hb_lo, a_lo * hb_lo + b_lo * ha_lo
        yr_hi, ym_hi = a_hi * ha_hi - b_hi * hb_hi, a_hi * hb_hi + b_hi * ha_hi
        half0 = jnp.where(local + r * HY_ROWS == 0, 0.5, 1.0)
        qr, qm = yr_lo - yr_hi, ym_lo + ym_hi
        p_ref[lo, 0:width] = ((yr_lo + yr_hi) * half0).astype(BF16)
        p_ref[lo, width:2 * width] = ((qr * twc + qm * tws) * half0).astype(BF16)
        p_ref[hi, 0:width] = (ym_lo - ym_hi).astype(BF16)
        p_ref[hi, width:2 * width] = (qm * twc - qr * tws).astype(BF16)
        return carry

    lax.fori_loop(0, m // HY_ROWS, spectrum, 0, unroll=2)
    a_ref[...] = _dot(cm_ref[...], p_ref[0:m, :]) + _dot(sm_ref[...], p_ref[m:2 * m, :])
    alt = jnp.where((local & 1) == 0, 1.0, -1.0)
    scale = 1.0 / (2 * m)

    def gate(r, carry):
        rows = pl.ds(pl.multiple_of(r * HY_ROWS, HY_ROWS), HY_ROWS)
        ze_ref[rows, :] = ge_ref[rows, :] * ((a_ref[rows, 0:width] + alt * yr_m) * scale + skip * ze_ref[rows, :])
        zo_ref[rows, :] = go_ref[rows, :] * ((a_ref[rows, width:2 * width] + alt * ym_m) * scale
                                             + skip * zo_ref[rows, :])
        return carry

    lax.fori_loop(0, m // HY_ROWS, gate, 0, unroll=2)


def _hy_conv(u, col0, ch, conv_w, conv_b, out_prev, out_rows, row_block, length, layer, ha, hb, hm, skip,
             tables):
    cm, sm, alt8, twc, tws = tables
    nb = u.shape[0]
    s = out_rows
    m = length // 2
    nct = ch // LANE_TILE
    assert col0 % LANE_TILE == 0
    cb = col0 // LANE_TILE
    taps = conv_w.shape[0]
    assert taps == HY_SHORT == 3
    single = dict(pipeline_mode=pl.Buffered(1))
    const = lambda shape: pl.BlockSpec(shape, lambda c, b: (0,) * len(shape), **single)
    hspec = lambda rows: pl.BlockSpec((HY_ORDER, rows, LANE_TILE), lambda c, b: (0, 0, c), **single)
    ucol = lambda k: pl.BlockSpec((None, length, LANE_TILE), lambda c, b: (b, row_block, cb + k * nct + c))
    wcol = lambda rows, k: pl.BlockSpec((rows, LANE_TILE), lambda c, b: (0, k * nct + c))
    orders = range(1, HY_ORDER + 1)
    in_specs = ([ucol(0)] + [ucol(k) for k in orders] + [wcol(taps, 0), wcol(1, 0)]
                + [wcol(taps, k) for k in orders] + [wcol(1, k) for k in orders]
                + [hspec(length), hspec(length), hspec(8),
                   pl.BlockSpec((None, HY_ORDER, 1, LANE_TILE), lambda c, b: (layer, 0, 0, c)),
                   const((m, m)), const((m, m)), const((8, m)), const((m, LANE_TILE)), const((m, LANE_TILE))])
    args = ([u] * (1 + HY_ORDER) + [conv_w, conv_b] + [conv_w] * HY_ORDER + [conv_b] * HY_ORDER
            + [ha, hb, hm, skip, cm, sm, alt8, twc, tws])
    n_in = len(args)
    aliases = {}
    if out_prev is not None:
        in_specs.append(pl.BlockSpec(memory_space=pl.ANY))
        args.append(out_prev)
        aliases = {len(args) - 1: 0}

    def body(*refs):
        k = HY_ORDER
        z_ref, g_refs = refs[0], refs[1:1 + k]
        wz_ref, bz_ref = refs[1 + k], refs[2 + k]
        wg_refs, bg_refs = refs[3 + k:3 + 2 * k], refs[3 + 2 * k:3 + 3 * k]
        _hy_conv_kernel(z_ref, g_refs, wz_ref, bz_ref, wg_refs, bg_refs, *refs[3 + 3 * k:n_in], *refs[-9:])

    return pl.pallas_call(
        body,
        out_shape=jax.ShapeDtypeStruct((nb, s, ch), BF16),
        grid=(nct, nb),
        in_specs=in_specs,
        out_specs=pl.BlockSpec((None, length, LANE_TILE), lambda c, b: (b, row_block, c)),
        scratch_shapes=([pltpu.VMEM((LANE_TILE // 128, length, 128), F32)]
                        + [pltpu.VMEM((m, LANE_TILE), F32)] * 4
                        + [pltpu.VMEM((m, 2 * LANE_TILE), F32)] * 2
                        + [pltpu.VMEM((length, 2 * LANE_TILE), BF16)]),
        input_output_aliases=aliases,
        compiler_params=_params(("parallel", "parallel")),
        name="hyena_conv_%d" % length,
    )(*args)


def _rms(x, w):
    return x * lax.rsqrt(jnp.mean(x * x, axis=-1, keepdims=True) + EPS) * w


def _merge_mlp_kernel(yf_ref, yb_ref, xs_ref, zg_ref, dsk_ref, sn_ref, ym_ref, yh_ref, wb_ref, wo_ref,
                      x_ref, xc_ref, mod_ref, nw_ref, w1_ref, w2_ref, nf_ref, o_ref, *, final, n_lat_tiles):
    m = mod_ref[...]
    bw = yf_ref.shape[1]
    ys = yf_ref[...].astype(F32) + yb_ref[...].astype(F32) + dsk_ref[...] * xs_ref[...].astype(F32)
    ys = _rms(ys * _silu(zg_ref[:, 0:bw].astype(F32)), sn_ref[...]).astype(BF16)
    acc = None
    for n, y in enumerate((ys, ym_ref[...].astype(BF16), yh_ref[...])):
        gate = zg_ref[:, (1 + n) * bw:(2 + n) * bw].astype(F32)
        term = jax.nn.sigmoid(gate) * _dot(y, wb_ref[n])
        acc = term if acc is None else acc + term
    x_in = x_ref[...] if xc_ref is None else jnp.where(pl.program_id(1) < n_lat_tiles, x_ref[...], xc_ref[...])
    x = x_in + m[2:3] * _dot(acc.astype(BF16), wo_ref[...])
    h = (_rms(x, nw_ref[...]) * (1.0 + m[4:5]) + m[3:4]).astype(BF16)
    a = jnp.maximum(_dot(h, w1_ref[...]), 0.0)
    x = x + m[5:6] * _dot((a * a).astype(BF16), w2_ref[...])
    o_ref[...] = _rms(x, nf_ref[...]) if final else x


def _merge_mlp(yf, yb, xbc, pnat, d_full, ssd_norm, ym, yh, wb, wo, x, x_ctx, mod, norm_w, layer, w1, w2,
               norm_f, final, n_lat_tiles, n_tiles):
    nb, _, d = x.shape
    s = yf.shape[1]
    bw = yf.shape[-1]
    hidden = w1.shape[1]
    row = pl.BlockSpec((None, TOKEN_TILE, bw), lambda b, i: (b, i, 0))
    assert bw == d
    zg = pl.BlockSpec((None, TOKEN_TILE, (1 + N_BRANCH) * d), lambda b, i: (b, i, 0))
    xrow = pl.BlockSpec((None, TOKEN_TILE, d), lambda b, i: (b, i, 0))
    lay = lambda n: pl.BlockSpec((None, 1, n), lambda b, i: (layer, 0, 0))
    single = dict(pipeline_mode=pl.Buffered(1))
    const = lambda shape: pl.BlockSpec(shape, lambda b, i: (0,) * len(shape), **single)
    in_specs = [row, row, row, zg, lay(bw), lay(bw), row, row, const((N_BRANCH, bw, d)), const((d, d))]
    args = [yf, yb, xbc, pnat, d_full, ssd_norm, ym, yh, wb, wo]
    aliases = {}
    if x_ctx is None:
        in_specs.append(xrow)
        args.append(x)
        if not final:
            aliases = {len(args) - 1: 0}
    else:
        in_specs += [pl.BlockSpec((None, TOKEN_TILE, d), lambda b, i: (b, jnp.minimum(i, n_lat_tiles - 1), 0)),
                     pl.BlockSpec((None, TOKEN_TILE, d), lambda b, i: (b, jnp.maximum(i - n_lat_tiles, 0), 0))]
        args += [x, x_ctx]
    in_specs += [pl.BlockSpec((None, 6, d), _mod_row(nb, n_lat_tiles)), lay(d),
                 const((d, hidden)), const((hidden, d)), pl.BlockSpec((1, d), lambda b, i: (0, 0))]
    args += [mod, norm_w, w1, w2, norm_f]

    def body(*refs):
        head, tail = refs[:11], refs[11:]
        xc = None if x_ctx is None else tail[0]
        _merge_mlp_kernel(*head, xc, *tail[0 if x_ctx is None else 1:], final=final, n_lat_tiles=n_lat_tiles)

    return pl.pallas_call(
        body,
        out_shape=jax.ShapeDtypeStruct((nb, n_tiles * TOKEN_TILE if final else s, d), F32),
        grid=(nb, n_tiles),
        in_specs=in_specs,
        out_specs=xrow,
        input_output_aliases=aliases,
        compiler_params=_params(("parallel", "parallel")),
        name="merge_mlp",
    )(*args)


def _dir_rows(g, nb, s, per_dir):
    g = g[:, :2 * per_dir].reshape(nb, s, 2, per_dir)
    return jnp.transpose(g, (2, 0, 3, 1))


def kernel(x, c, ctx, c_ctx, norm1_w, mod_w, mod_b, w_in, ssd_conv_w, ssd_conv_b, ssd_dt_bias, ssd_a_log,
           ssd_d, ssd_norm_w, ml_conv_w, ml_conv_b, ml_gate_b, ml_norm_w, hy_conv_w, hy_conv_b, hy_ffn_w1,
           hy_ffn_b1, hy_ffn_w2, hy_ffn_b2, hy_ffn_w3, hy_decay, hy_skip, w_branch, w_out, norm2_w,
           mlp_w1, mlp_w2, norm_f_w):
    nb, seq, d = x.shape
    ctx_len = ctx.shape[1]
    depth = w_in.shape[0]
    s = seq + ctx_len
    assert seq % ctx_len == 0 and ctx_len % TOKEN_TILE == 0 and seq % GRID_W == 0
    n_lat_tiles, n_tiles = seq // TOKEN_TILE, s // TOKEN_TILE
    n_lat_ch, n_ch = seq // CHUNK, s // CHUNK
    assert n_lat_ch % SCAN_CHUNKS == 0 and n_ch % SCAN_CHUNKS == 0
    rows_g = seq // GRID_W

    ssd_heads = ssd_d.shape[-1]
    ssd_inner = ssd_norm_w.shape[-1]
    ssd_hd = ssd_inner // ssd_heads
    ssd_conv_ch = ssd_conv_w.shape[-1]
    ssd_ds = (ssd_conv_ch - ssd_inner) // (2 * SSD_GROUPS)
    ml_heads = ml_gate_b.shape[-1]
    ml_inner = ml_heads * ML_HEADDIM
    hy_inner = hy_skip.shape[-1]
    ssd_cols = ssd_conv_ch + ssd_inner + 2 * ssd_heads
    ml_cols = 4 * ml_inner + 4 * ml_heads
    rec_cols = ssd_cols + ml_cols
    hy_cols = (HY_ORDER + 1) * hy_inner

    o_z = ssd_conv_ch
    o_dt = ssd_conv_ch + ssd_inner
    o_ml = ssd_cols
    o_mlg = ssd_cols + 4 * ml_inner
    o_hy = rec_cols
    o_g = rec_cols + hy_cols

    xa, xa_ctx = x, ctx
    rpad = (-(nb + 1)) % 8
    c_all = jnp.concatenate([c, c_ctx[None], jnp.zeros((rpad, d), F32)], axis=0)

    tab_lat = _dft_tables(seq)
    tab_ctx = _dft_tables(ctx_len)
    k_scale = jnp.concatenate([jnp.ones((1, ml_inner), F32),
                               jnp.full((1, ml_inner), ML_HEADDIM ** -0.5, F32)], axis=1)

    norm1 = norm1_w[:, None, :]
    norm2 = norm2_w[:, None, :]
    ssd_norm = ssd_norm_w[:, None, :]
    ml_norm = ml_norm_w[:, None, :]
    d_full = jnp.repeat(ssd_d, ssd_hd, axis=-1)[:, None, :]
    w3 = hy_ffn_w3.reshape(depth, hy_ffn_w3.shape[1], HY_ORDER, 2, hy_inner).transpose(0, 2, 3, 1, 4)
    decay = hy_decay[:, :, :, None, :]
    skip = hy_skip[:, :, None, :]
    hb1 = hy_ffn_b1[:, None, :]
    hb2 = hy_ffn_b2[:, None, :]

    for l in range(depth):
        need_ctx = l < depth - 1
        used_tiles = n_tiles if need_ctx else n_lat_tiles
        mod = _mod_vectors(c_all, mod_w, mod_b[:, None, :], l).reshape(-1, 6, d)

        wl = w_in[l].astype(BF16)
        w_nat = jnp.concatenate([wl[:, o_z:o_dt], wl[:, o_g:], wl[:, o_hy:o_g]], axis=1)
        w_xbc = wl[:, :o_z]
        w_ml = wl[:, o_ml:o_mlg]
        gpad = lambda w: jnp.pad(w, ((0, 0), (0, 128 - w.shape[1])))
        hn, hn_cm = _normmod(xa, xa_ctx, norm1, l, mod, 0, rows_g, n_lat_tiles, n_tiles)
        pnat = _matmul(hn.reshape(nb * s, d), w_nat, BF16, "proj_nat").reshape(nb, s, -1)
        c_g = ssd_inner
        c_hy = c_g + N_BRANCH * d

        xbc, p_dt = _proj_conv(hn, w_xbc, gpad(wl[:, o_dt:o_ml]), ssd_conv_w[l], ssd_conv_b[l][None], None,
                               n_lat_tiles, n_tiles, "proj_xbc")
        dt_r = _dir_rows(p_dt.reshape(nb * s, -1), nb, s, ssd_heads)
        y_f, y_b = _ssd_scan(xbc, dt_r, ssd_dt_bias[l][:, :, None], ssd_a_log[l][:, :, None],
                             ssd_heads, ssd_hd, ssd_ds, n_lat_ch, n_ch)

        pml, p_mlg = _proj_conv(hn_cm, w_ml, gpad(wl[:, o_mlg:o_hy]), ml_conv_w[l], ml_conv_b[l][None],
                                k_scale, n_lat_tiles, n_tiles, "proj_ml")
        g_r = _dir_rows(p_mlg.reshape(nb * s, -1), nb, s, 2 * ml_heads)
        gate_b = ml_gate_b[l]
        c_f, c_b = _ml_scan(pml, pml, g_r[:, :, :ml_heads], g_r[:, :, ml_heads:],
                            gate_b[:, 0, :, None], gate_b[:, 1, :, None], ml_heads, n_lat_ch, n_ch)
        ym = _ml_out(c_f, c_b, pml, ml_norm, l, ml_heads, rows_g, n_lat_tiles, used_tiles)

        hcw, hcb = hy_conv_w[l], hy_conv_b[l][None]
        fl = _hy_filters(seq, l, hy_ffn_w1, hb1, hy_ffn_w2, hb2, w3, decay, tab_lat)
        rows = s if need_ctx else seq
        yh = _hy_conv(pnat, c_hy, hy_inner, hcw, hcb, None, rows, 0, seq, l, *fl, skip, tab_lat)
        if need_ctx:
            fc = _hy_filters(ctx_len, l, hy_ffn_w1, hb1, hy_ffn_w2, hb2, w3, decay, tab_ctx)
            yh = _hy_conv(pnat, c_hy, hy_inner, hcw, hcb, yh, rows, seq // ctx_len, ctx_len, l, *fc, skip,
                          tab_ctx)

        xa = _merge_mlp(y_f, y_b, xbc, pnat, d_full, ssd_norm, ym, yh, w_branch[l].astype(BF16),
                        w_out[l].astype(BF16), xa, xa_ctx, mod, norm2, l, mlp_w1[l].astype(BF16),
                        mlp_w2[l].astype(BF16), norm_f_w[None], not need_ctx, n_lat_tiles, used_tiles)
        xa_ctx = None

    return xa
```

```python
import functools
import math

import jax
import jax.numpy as jnp
import numpy as np
from jax import lax
from jax.experimental import pallas as pl
from jax.experimental.pallas import tpu as pltpu

F32 = jnp.float32
BF16 = jnp.bfloat16
HIGHEST = lax.Precision.HIGHEST

GRID_W = 64
CHUNK = 128
EPS = 1e-6
SSD_GROUPS = 2
SSD_CONV = 5
ML_HEADDIM = 128
ML_CONV = 5
HY_ORDER = 2
HY_SHORT = 3
HY_BANDS = 16
N_BRANCH = 3

TOKEN_TILE = 256
LANE_TILE = 256
HALO = 16
SCAN_CHUNKS = 2
VMEM_LIMIT = 56 * 1024 * 1024

_hdot = functools.partial(jnp.dot, precision=HIGHEST, preferred_element_type=F32)
_dot = functools.partial(jnp.dot, preferred_element_type=F32)


def _params(sem, vmem=None):
    return pltpu.CompilerParams(dimension_semantics=sem, vmem_limit_bytes=vmem or VMEM_LIMIT)


def _softplus(x):
    return jnp.maximum(x, 0.0) + jnp.log(1.0 + jnp.exp(-jnp.abs(x)))


def _silu(x):
    return x * jax.nn.sigmoid(x)


def _mod_kernel(c_ref, w_ref, b_ref, o_ref):
    o_ref[...] = _hdot(_silu(c_ref[...]), w_ref[...]) + b_ref[...]


def _mod_vectors(c_all, mod_w, mod_b, layer):
    r, d = c_all.shape
    n = mod_w.shape[-1]
    tn = n // 6
    return pl.pallas_call(
        _mod_kernel,
        out_shape=jax.ShapeDtypeStruct((r, n), F32),
        grid=(n // tn,),
        in_specs=[pl.BlockSpec((r, d), lambda j: (0, 0)),
                  pl.BlockSpec((None, d, tn), lambda j: (layer, 0, j)),
                  pl.BlockSpec((None, 1, tn), lambda j: (layer, 0, j))],
        out_specs=pl.BlockSpec((r, tn), lambda j: (0, j)),
        compiler_params=_params(("parallel",)),
        name="mod_vectors",
    )(c_all, mod_w, mod_b)


def _grid_perm(rows_g):
    wpt = TOKEN_TILE // rows_g
    src = np.arange(TOKEN_TILE).reshape(rows_g, wpt).T.reshape(-1)
    p = np.zeros((TOKEN_TILE, TOKEN_TILE), np.float32)
    p[np.arange(TOKEN_TILE), src] = 1.0
    return p


def _normmod_kernel(x_ref, xc_ref, x4_ref, perm_ref, nw_ref, mod_ref, o_ref, ocm_ref, *, si, n_lat_tiles):
    i = pl.program_id(1)
    m = mod_ref[...]

    def normed(x):
        h = x * lax.rsqrt(jnp.mean(x * x, axis=-1, keepdims=True) + EPS) * nw_ref[...]
        return (h * (1.0 + m[si + 1:si + 2]) + m[si:si + 1]).astype(BF16)

    h = normed(x_ref[...] if xc_ref is None else jnp.where(i < n_lat_tiles, x_ref[...], xc_ref[...]))
    o_ref[...] = h

    @pl.when(i < n_lat_tiles)
    def _():
        x4 = x4_ref[...]
        ocm_ref[...] = _dot(perm_ref[...], normed(x4.reshape(TOKEN_TILE, x4.shape[-1]))).astype(BF16)

    @pl.when(i >= n_lat_tiles)
    def _():
        ocm_ref[...] = h


def _mod_row(nb, n_lat_tiles):
    return lambda b, i: (jnp.where(i < n_lat_tiles, b, nb), 0, 0)


def _normmod(x, x_ctx, norm_w, layer, mod, si, rows_g, n_lat_tiles, n_tiles):
    nb, rows, d = x.shape
    s = n_tiles * TOKEN_TILE
    wpt = TOKEN_TILE // rows_g
    assert wpt % 8 == 0 and GRID_W % wpt == 0 and rows % GRID_W == 0
    x4 = x.reshape(nb, rows // GRID_W, GRID_W, d)
    tile = pl.BlockSpec((None, TOKEN_TILE, d), lambda b, i: (b, i, 0))
    if x_ctx is None:
        srcs, src_specs = [x], [tile]
    else:
        srcs = [x, x_ctx]
        src_specs = [pl.BlockSpec((None, TOKEN_TILE, d), lambda b, i: (b, jnp.minimum(i, n_lat_tiles - 1), 0)),
                     pl.BlockSpec((None, TOKEN_TILE, d), lambda b, i: (b, jnp.maximum(i - n_lat_tiles, 0), 0))]

    def body(*refs):
        xc = None if x_ctx is None else refs[1]
        _normmod_kernel(refs[0], xc, *refs[len(srcs):], si=si, n_lat_tiles=n_lat_tiles)

    return pl.pallas_call(
        body,
        out_shape=(jax.ShapeDtypeStruct((nb, s, d), BF16),) * 2,
        grid=(nb, n_tiles),
        in_specs=src_specs + [
            pl.BlockSpec((None, rows_g, wpt, d), lambda b, i: (b, 0, jnp.minimum(i, n_lat_tiles - 1), 0)),
            pl.BlockSpec((TOKEN_TILE, TOKEN_TILE), lambda b, i: (0, 0)),
            pl.BlockSpec((None, 1, d), lambda b, i: (layer, 0, 0)),
            pl.BlockSpec((None, 6, d), _mod_row(nb, n_lat_tiles))],
        out_specs=(tile, tile),
        compiler_params=_params(("parallel", "parallel")),
        name="normmod",
    )(*srcs, x4, jnp.asarray(_grid_perm(rows_g), BF16), norm_w, mod)


def _mm_kernel(a_ref, w_ref, o_ref):
    o_ref[...] = _dot(a_ref[...], w_ref[...]).astype(o_ref.dtype)


def _pick(n, cands):
    for c in cands:
        if n % c == 0:
            return c
    return n


def _matmul(a, w, out_dtype, name):
    t, k = a.shape
    n = w.shape[1]
    tm = _pick(t, (2048, 1024, 768, 512, 256))
    tn = _pick(n, (1024, 512, 256, 128))
    return pl.pallas_call(
        _mm_kernel,
        out_shape=jax.ShapeDtypeStruct((t, n), out_dtype),
        grid=(t // tm, n // tn),
        in_specs=[pl.BlockSpec((tm, k), lambda i, j: (i, 0)),
                  pl.BlockSpec((k, tn), lambda i, j: (0, j))],
        out_specs=pl.BlockSpec((tm, tn), lambda i, j: (i, j)),
        compiler_params=_params(("parallel", "parallel")),
        name=name,
    )(a, w)


def _shift_stack(taps):
    pad = taps // 2
    return np.concatenate([np.eye(TOKEN_TILE, k=j - pad, dtype=np.float32) for j in range(taps) if j != pad])


def _conv_tile(u, prev, nxt, sh_ref, w, b, scale):
    tm = u.shape[0]
    taps = w.shape[0]
    pad = taps // 2

    def finish(acc):
        acc = _silu(acc)
        return (acc if scale is None else acc * scale).astype(BF16)

    def edge(window):
        rows = window.shape[0]
        acc = b + window[HALO:2 * HALO] * w[pad:pad + 1]
        for j in range(taps):
            if j != pad:
                acc = acc + pltpu.roll(window, (pad - j) % rows, 0)[HALO:2 * HALO] * w[j:j + 1]
        return finish(acc)

    shifted = _dot(sh_ref[...], u)
    acc = b + u.astype(F32) * w[pad:pad + 1]
    blk = 0
    for j in range(taps):
        if j != pad:
            acc = acc + shifted[blk * tm:(blk + 1) * tm] * w[j:j + 1]
            blk += 1
    head = edge(jnp.concatenate([prev, u[0:2 * HALO].astype(F32)], axis=0))
    tail = edge(jnp.concatenate([u[tm - 2 * HALO:tm].astype(F32), nxt], axis=0))
    return jnp.concatenate([head, finish(acc)[HALO:tm - HALO], tail], axis=0)


def _proj_conv_kernel(a_ref, w_ref, wg_ref, sh_ref, cw_ref, cb_ref, *rest, n_conv, bounds):
    o_ref, og_ref, buf_ref = rest[-3:]
    s_ref = rest[0] if len(rest) == 4 else None
    j = pl.program_id(1)
    acc = _dot(a_ref[...], w_ref[...])

    @pl.when(j == 0)
    def _():
        og_ref[...] = _dot(a_ref[...], wg_ref[...])

    @pl.when(j >= n_conv)
    def _():
        o_ref[...] = acc.astype(o_ref.dtype)

    @pl.when(j < n_conv)
    def _():
        buf_ref[...] = acc.astype(BF16)
        w, b = cw_ref[...], cb_ref[...]
        scale = None if s_ref is None else s_ref[...]
        zeros = jnp.zeros((HALO, buf_ref.shape[1]), F32)
        for blk in range(buf_ref.shape[0] // TOKEN_TILE):
            r0 = blk * TOKEN_TILE
            prev = zeros if blk in bounds else buf_ref[r0 - HALO:r0, :].astype(F32)
            nxt = zeros if blk + 1 in bounds else buf_ref[r0 + TOKEN_TILE:r0 + TOKEN_TILE + HALO, :].astype(F32)
            o_ref[r0:r0 + TOKEN_TILE, :] = _conv_tile(buf_ref[r0:r0 + TOKEN_TILE, :], prev, nxt, sh_ref, w, b, scale)


def _proj_conv(a, w, w_gate, conv_w, conv_b, scale, n_lat_tiles, n_tiles, name):
    nb, s, k = a.shape
    n = w.shape[1]
    taps, nc = conv_w.shape
    ng = w_gate.shape[1]
    tn = _pick(math.gcd(n, nc), (512, 256))
    n_conv = nc // tn
    cidx = lambda b, j: (0, jnp.minimum(j, n_conv - 1))
    in_specs = [pl.BlockSpec((None, s, k), lambda b, j: (b, 0, 0)),
                pl.BlockSpec((k, tn), lambda b, j: (0, j)),
                pl.BlockSpec((k, ng), lambda b, j: (0, 0)),
                pl.BlockSpec(((taps - 1) * TOKEN_TILE, TOKEN_TILE), lambda b, j: (0, 0)),
                pl.BlockSpec((taps, tn), cidx),
                pl.BlockSpec((1, tn), cidx)]
    args = [a, w, w_gate, jnp.asarray(_shift_stack(taps), BF16), conv_w, conv_b]
    if scale is not None:
        in_specs.append(pl.BlockSpec((1, tn), cidx))
        args.append(scale)
    return pl.pallas_call(
        functools.partial(_proj_conv_kernel, n_conv=n_conv, bounds=(0, n_lat_tiles, n_tiles)),
        out_shape=(jax.ShapeDtypeStruct((nb, s, n), BF16), jax.ShapeDtypeStruct((nb, s, ng), F32)),
        grid=(nb, n // tn),
        in_specs=in_specs,
        out_specs=(pl.BlockSpec((None, s, tn), lambda b, j: (b, 0, j)),
                   pl.BlockSpec((None, s, ng), lambda b, j: (b, 0, 0))),
        scratch_shapes=[pltpu.VMEM((s, tn), BF16)],
        compiler_params=_params(("parallel", "arbitrary")),
        name=name,
    )(*args)


def _chunk_index(n_lat_ch, n_ch):
    return lambda d, j: (j + n_lat_ch) % n_ch if d == 0 else n_ch - 1 - j


def _scan_masks(d, t):
    ii = lax.broadcasted_iota(jnp.int32, (t, t), 0)
    jj = lax.broadcasted_iota(jnp.int32, (t, t), 1)
    mask = jj <= ii if d == 0 else jj >= ii
    tri_t = jnp.where(ii <= jj if d == 0 else ii >= jj, 1.0, 0.0).astype(F32)
    return mask, tri_t


def _gate_rows(la_r, lw_r, tri_t, m_prev):
    cum_r = _hdot(la_r, tri_t)
    last = jnp.sum(la_r, axis=1, keepdims=True)
    g_r = last - cum_r + lw_r
    m_loc = jnp.max(g_r, axis=1, keepdims=True)
    e_r = jnp.exp(g_r - m_loc)
    m_new = jnp.maximum(last + m_prev, m_loc)
    s_old = jnp.exp(last + m_prev - m_new)
    s_new = jnp.exp(m_loc - m_new)
    return cum_r, cum_r - lw_r, e_r, m_new, s_old, s_new


def _head_probs(cum_row, crow_row, m_prev_h, mask, qk):
    t = qk.shape[0]
    colb = jnp.broadcast_to(cum_row, (t, t)).T
    dlog = jnp.where(mask, colb - crow_row, -jnp.inf)
    inter = colb[:, 0:1] + m_prev_h
    m_row = jnp.maximum(inter, jnp.max(dlog, axis=1, keepdims=True))
    p = jnp.exp(dlog - m_row) * qk
    return p.astype(BF16), jnp.exp(inter - m_row), m_row


def _ssd_scan_kernel(*refs, heads, groups, hd, ds):
    j = pl.program_id(1)
    ins, outs, states = refs[:12], refs[12:14], refs[14:16]

    @pl.when(j == 0)
    def _():
        for st_ref in states:
            st_ref[...] = jnp.zeros_like(st_ref)

    for sub in range(SCAN_CHUNKS):
        for d in range(2):
            _ssd_chunk(d, _scan_rows(d, sub), *ins[6 * d:6 * d + 6], outs[d], states[d],
                       heads=heads, groups=groups, hd=hd, ds=ds)


def _scan_rows(d, sub):
    first = (sub if d == 0 else SCAN_CHUNKS - 1 - sub) * CHUNK
    return slice(first, first + CHUNK)


def _ssd_chunk(d, rs, xs_ref, b_ref, c_ref, dt_ref, bias_ref, alog_ref, o_ref, st_ref, *, heads, groups, hd, ds):
    t = CHUNK
    hpg = heads // groups
    pw = 2 * hd
    mask, tri_t = _scan_masks(d, t)
    dt = _softplus(dt_ref[:, rs] + bias_ref[...])
    la_r = -dt * jnp.exp(alog_ref[...])
    cum_r = _hdot(la_r, tri_t)
    last = jnp.sum(la_r, axis=1, keepdims=True)
    crow = cum_r - jnp.log(dt)
    e_r = jnp.exp(last - crow)
    e_last = jnp.exp(last)
    lo = lax.broadcasted_iota(jnp.int32, (1, pw), 1) < hd

    for g in range(groups):
        q = c_ref[rs, g * ds:(g + 1) * ds]
        k_t = b_ref[rs, g * ds:(g + 1) * ds].astype(F32).T
        qk = _dot(q, k_t.astype(BF16))
        w0 = g * hpg * hd
        qs = _dot(q, st_ref[:, w0:w0 + hpg * hd].astype(BF16))
        for i in range(hpg // 2):
            h0 = g * hpg + 2 * i
            c0 = h0 * hd
            vp = xs_ref[rs, c0:c0 + pw]
            zero = jnp.zeros_like(vp)
            v_bd = jnp.concatenate([jnp.where(lo, vp, zero), jnp.where(lo, zero, vp)], axis=0)
            probs, carry, kte = [], [], []
            for h in (h0, h0 + 1):
                colb = jnp.broadcast_to(cum_r[h:h + 1], (t, t)).T
                dlog = jnp.where(mask, colb - crow[h:h + 1], -jnp.inf)
                probs.append((jnp.exp(dlog) * qk).astype(BF16))
                carry.append(jnp.exp(colb))
                kte.append((k_t * e_r[h:h + 1]).astype(BF16))
            lhs = jnp.concatenate([jnp.concatenate(probs, axis=1), jnp.concatenate(kte, axis=1)], axis=0)
            res = _dot(lhs, v_bd)
            y = res[:t] + qs[:, 2 * i * hd:2 * i * hd + pw] * jnp.where(lo, carry[0], carry[1])
            o_ref[rs, c0:c0 + pw] = y.astype(o_ref.dtype)
            decay = jnp.where(lo, e_last[h0:h0 + 1], e_last[h0 + 1:h0 + 2])
            st_ref[:, c0:c0 + pw] = decay * st_ref[:, c0:c0 + pw] + res[t:]


def _ssd_scan(xbc, dt_r, bias_c, alog_c, heads, hd, ds, n_lat_ch, n_ch):
    nb, s, _ = xbc.shape
    inner = heads * hd
    gn = SSD_GROUPS * ds
    assert 2 * hd == CHUNK and (heads // SSD_GROUPS) % 2 == 0 and inner % gn == 0
    rows = SCAN_CHUNKS * CHUNK
    n_blk = n_ch // SCAN_CHUNKS
    cidx = _chunk_index(n_lat_ch // SCAN_CHUNKS, n_blk)

    def specs(d):
        return [pl.BlockSpec((None, rows, inner), lambda b, j: (b, cidx(d, j), 0)),
                pl.BlockSpec((None, rows, gn), lambda b, j: (b, cidx(d, j), inner // gn)),
                pl.BlockSpec((None, rows, gn), lambda b, j: (b, cidx(d, j), inner // gn + 1)),
                pl.BlockSpec((None, None, heads, rows), lambda b, j: (d, b, 0, cidx(d, j))),
                pl.BlockSpec((None, heads, 1), lambda b, j: (d, 0, 0)),
                pl.BlockSpec((None, heads, 1), lambda b, j: (d, 0, 0))]

    out = lambda d: pl.BlockSpec((None, rows, inner), lambda b, j: (b, cidx(d, j), 0))
    args = (xbc, xbc, xbc, dt_r, bias_c, alog_c)
    return pl.pallas_call(
        functools.partial(_ssd_scan_kernel, heads=heads, groups=SSD_GROUPS, hd=hd, ds=ds),
        out_shape=(jax.ShapeDtypeStruct((nb, s, inner), BF16),) * 2,
        grid=(nb, n_blk),
        in_specs=specs(0) + specs(1),
        out_specs=(out(0), out(1)),
        scratch_shapes=[pltpu.VMEM((ds, inner), F32)] * 2,
        compiler_params=_params(("parallel", "arbitrary")),
        name="ssd_scan",
    )(*args, *args)


def _ml_scan_kernel(*refs, heads):
    j = pl.program_id(1)
    ins, outs, states = refs[:14], refs[14:16], refs[16:20]

    @pl.when(j == 0)
    def _():
        for ref in states:
            ref[...] = jnp.zeros_like(ref)

    for sub in range(SCAN_CHUNKS):
        for d in range(2):
            _ml_chunk(d, _scan_rows(d, sub), *ins[7 * d:7 * d + 7], outs[d], *states[2 * d:2 * d + 2],
                      heads=heads)


def _ml_chunk(d, rs, q_ref, k_ref, v_ref, gi_ref, gf_ref, bi_ref, bf_ref, o_ref, st_ref, m_ref, *, heads):
    t = CHUNK
    dh = ML_HEADDIM
    mask, tri_t = _scan_masks(d, t)
    lw_r = gi_ref[:, rs] + bi_ref[...]
    f = gf_ref[:, rs] + bf_ref[...]
    la_r = jnp.minimum(f, 0.0) - jnp.log(1.0 + jnp.exp(-jnp.abs(f)))
    m_prev = m_ref[:, 0:1]
    cum_r, crow, e_r, m_new, s_old, s_new = _gate_rows(la_r, lw_r, tri_t, m_prev)
    ones = jnp.ones((t, dh), BF16)

    for h in range(heads):
        c0 = h * dh
        q = q_ref[rs, c0:c0 + dh]
        k_t = k_ref[rs, c0:c0 + dh].astype(F32).T
        v_aug = jnp.concatenate([v_ref[rs, c0:c0 + dh], ones], axis=1)
        qk = _dot(q, k_t.astype(BF16))
        s0 = 2 * c0
        qs = _dot(q, st_ref[:, s0:s0 + 2 * dh].astype(BF16))
        p, cf, m_row = _head_probs(cum_r[h:h + 1], crow[h:h + 1], m_prev[h:h + 1], mask, qk)
        res = _dot(jnp.concatenate([p, (k_t * e_r[h:h + 1]).astype(BF16)], axis=0), v_aug)
        y = res[:t] + qs * cf
        cell = y[:, :dh] / jnp.maximum(jnp.abs(y[:, dh:]), jnp.exp(-m_row))
        o_ref[rs, c0:c0 + dh] = cell.astype(o_ref.dtype)
        st_ref[:, s0:s0 + 2 * dh] = s_old[h:h + 1] * st_ref[:, s0:s0 + 2 * dh] + s_new[h:h + 1] * res[t:]
    m_ref[...] = jnp.broadcast_to(m_new, m_ref.shape)


def _ml_scan(qk, pml, gi_r, gf_r, bi_c, bf_c, heads, n_lat_ch, n_ch):
    nb, s, _ = qk.shape
    inner = heads * ML_HEADDIM
    rows = SCAN_CHUNKS * CHUNK
    n_blk = n_ch // SCAN_CHUNKS
    cidx = _chunk_index(n_lat_ch // SCAN_CHUNKS, n_blk)

    def specs(d):
        gspec = pl.BlockSpec((None, None, heads, rows), lambda b, j: (d, b, 0, cidx(d, j)))
        bspec = pl.BlockSpec((None, heads, 1), lambda b, j: (d, 0, 0))
        col = lambda c: pl.BlockSpec((None, rows, inner), lambda b, j: (b, cidx(d, j), c))
        return [col(0), col(1), col(2), gspec, gspec, bspec, bspec]

    out = lambda d: pl.BlockSpec((None, rows, inner), lambda b, j: (b, cidx(d, j), 0))
    args = (qk, qk, pml, gi_r, gf_r, bi_c, bf_c)
    return pl.pallas_call(
        functools.partial(_ml_scan_kernel, heads=heads),
        out_shape=(jax.ShapeDtypeStruct((nb, s, inner), BF16),) * 2,
        grid=(nb, n_blk),
        in_specs=specs(0) + specs(1),
        out_specs=(out(0), out(1)),
        scratch_shapes=[pltpu.VMEM((ML_HEADDIM, 2 * inner), F32), pltpu.VMEM((heads, 128), F32)] * 2,
        compiler_params=_params(("parallel", "arbitrary")),
        name="ml_scan",
    )(*args, *args)


def _ml_out_kernel(cf_ref, cb_ref, o_in_ref, nw_ref, perm_ref, *rest, heads, permute):
    o_ref = rest[-1]
    dh = ML_HEADDIM
    parts = []
    for h in range(heads):
        sl = slice(h * dh, (h + 1) * dh)
        c = cf_ref[:, sl].astype(F32) + cb_ref[:, sl].astype(F32)
        c = c * lax.rsqrt(jnp.mean(c * c, axis=-1, keepdims=True) + EPS) * nw_ref[:, sl]
        parts.append((jax.nn.sigmoid(o_in_ref[:, sl].astype(F32)) * c).astype(BF16))
    y = jnp.concatenate(parts, axis=1)
    y = _dot(perm_ref[...], y) if permute else y.astype(F32)
    o_ref[...] = y.reshape(o_ref.shape)


def _ml_out(cf, cb, pml, norm_w, layer, heads, rows_g, n_lat_tiles, n_tiles):
    nb, _, inner = cf.shape
    wpt = TOKEN_TILE // rows_g
    rpt = TOKEN_TILE // GRID_W
    perm_t = jnp.asarray(_grid_perm(rows_g).T, BF16)
    out_shape = jax.ShapeDtypeStruct((nb, n_tiles * rpt, GRID_W, inner), F32)

    def call(tile0, tiles, out_spec, permute, prev):
        row = pl.BlockSpec((None, TOKEN_TILE, inner), lambda b, i: (b, tile0 + i, 0))
        in_specs = [row, row,
                    pl.BlockSpec((None, TOKEN_TILE, inner), lambda b, i: (b, tile0 + i, 3)),
                    pl.BlockSpec((None, 1, inner), lambda b, i: (layer, 0, 0)),
                    pl.BlockSpec((TOKEN_TILE, TOKEN_TILE), lambda b, i: (0, 0))]
        args = [cf, cb, pml, norm_w, perm_t]
        if prev is not None:
            in_specs.append(pl.BlockSpec(memory_space=pl.ANY))
            args.append(prev)
        return pl.pallas_call(
            functools.partial(_ml_out_kernel, heads=heads, permute=permute),
            out_shape=out_shape,
            grid=(nb, tiles),
            in_specs=in_specs,
            out_specs=out_spec,
            input_output_aliases={} if prev is None else {len(args) - 1: 0},
            compiler_params=_params(("parallel", "parallel")),
            name="ml_out_lat" if permute else "ml_out_ctx",
        )(*args)

    y = call(0, n_lat_tiles, pl.BlockSpec((None, rows_g, wpt, inner), lambda b, i: (b, 0, i, 0)), True, None)
    if n_tiles > n_lat_tiles:
        y = call(n_lat_tiles, n_tiles - n_lat_tiles,
                 pl.BlockSpec((None, rpt, GRID_W, inner), lambda b, i: (b, n_lat_tiles + i, 0, 0)), False, y)
    return y.reshape(nb, n_tiles * TOKEN_TILE, inner)


def _dft_tables(length):
    m = length // 2
    k = np.arange(m, dtype=np.int64)
    ang = ((k[:, None] * k[None, :]) % (2 * m)).astype(np.float64) * (np.pi / m)
    alt8 = np.broadcast_to(np.where(k % 2 == 0, 1.0, -1.0)[None, :], (8, m))
    tw = k.astype(np.float64)[:, None] * (np.pi / length) * np.ones((1, LANE_TILE))
    return (jnp.asarray(np.cos(ang), BF16), jnp.asarray(np.sin(ang), BF16), jnp.asarray(alt8, BF16),
            jnp.asarray(np.cos(tw), F32), jnp.asarray(np.sin(tw), F32))


def _hy_feats(length):
    t = jnp.arange(length, dtype=F32)
    t_norm = t / length
    bands = jnp.linspace(1e-4, HY_BANDS - 1, HY_BANDS, dtype=F32)
    ang = (2.0 * math.pi / length) * t[:, None] * bands[None, :]
    feats = jnp.concatenate([t_norm[:, None], jnp.cos(ang), -jnp.sin(ang)], axis=-1)
    return feats[0::2], feats[1::2], t_norm[0::2, None], t_norm[1::2, None]


def _split_spectrum(ae, be, ao, bo, twc, tws):
    tr = twc * ao - tws * bo
    tm = twc * bo + tws * ao
    return (ae + tr, ae - tr), (be + tm, tm - be)


def _hy_filter_kernel(fe_ref, fo_ref, tne_ref, tno_ref, w1_ref, b1_ref, w2_ref, b2_ref, w3f_ref, w3b_ref,
                      df_ref, db_ref, cm_ref, sm_ref, alt_ref, twc_ref, tws_ref, ha_ref, hb_ref, hm_ref,
                      hide_ref, hido_ref):
    m = cm_ref.shape[0]

    @pl.when((pl.program_id(0) == 0) & (pl.program_id(1) == 0))
    def _():
        for f_ref, h_ref in ((fe_ref, hide_ref), (fo_ref, hido_ref)):
            hid = jnp.sin(_hdot(f_ref[...], w1_ref[...]) + b1_ref[...])
            h_ref[...] = jnp.sin(_hdot(hid, w2_ref[...]) + b2_ref[...])

    def taps(hid, tn):
        hid = hid.astype(BF16)
        h_f = _dot(hid, w3f_ref[...].astype(BF16)) * jnp.exp(-tn * jnp.abs(df_ref[...]))
        h_b = _dot(hid, w3b_ref[...].astype(BF16)) * jnp.exp(-tn * jnp.abs(db_ref[...]))
        return (h_f + h_b).astype(BF16), (h_f - h_b).astype(BF16)

    sum_e, dif_e = taps(hide_ref[...], tne_ref[...])
    sum_o, dif_o = taps(hido_ref[...], tno_ref[...])
    twc, tws = twc_ref[...], tws_ref[...]
    cm, sm = cm_ref[...], sm_ref[...]
    (ha_lo, ha_hi), _ = _split_spectrum(_dot(cm, sum_e), 0.0, _dot(cm, sum_o), _dot(sm, sum_o), twc, tws)
    _, (hb_lo, hb_hi) = _split_spectrum(0.0, _dot(sm, dif_e), _dot(cm, dif_o), _dot(sm, dif_o), twc, tws)
    ha_ref[0:m, :] = ha_lo
    ha_ref[m:2 * m, :] = ha_hi
    hb_ref[0:m, :] = hb_lo
    hb_ref[m:2 * m, :] = hb_hi
    row = lax.broadcasted_iota(jnp.int32, hm_ref.shape, 0)
    hm_ref[...] = jnp.where(row == 0, _dot(alt_ref[...], sum_e), _dot(alt_ref[...], dif_o))


def _hy_filters(length, layer, w1, b1, w2, b2, w3, decay, tables):
    cm, sm, alt8, twc, tws = tables
    m = length // 2
    feats = _hy_feats(length)
    nfeat, nf = w1.shape[-2:]
    ch = decay.shape[-1]
    nct = ch // LANE_TILE
    const = lambda shape: pl.BlockSpec(shape, lambda n, c: (0,) * len(shape))
    w3spec = lambda dr: pl.BlockSpec((None, None, None, nf, LANE_TILE), lambda n, c: (layer, n, dr, 0, c))
    dspec = lambda dr: pl.BlockSpec((None, None, None, 1, LANE_TILE), lambda n, c: (layer, n, dr, 0, c))
    lay = lambda a, b: pl.BlockSpec((None, a, b), lambda n, c: (layer, 0, 0))
    out = lambda rows: pl.BlockSpec((None, rows, LANE_TILE), lambda n, c: (n, 0, c))
    return pl.pallas_call(
        _hy_filter_kernel,
        out_shape=(jax.ShapeDtypeStruct((HY_ORDER, length, ch), F32),
                   jax.ShapeDtypeStruct((HY_ORDER, length, ch), F32),
                   jax.ShapeDtypeStruct((HY_ORDER, 8, ch), F32)),
        grid=(HY_ORDER, nct),
        in_specs=[const((m, nfeat)), const((m, nfeat)), const((m, 1)), const((m, 1)),
                  lay(nfeat, nf), lay(1, nf), lay(nf, nf), lay(1, nf),
                  w3spec(0), w3spec(1), dspec(0), dspec(1),
                  const((m, m)), const((m, m)), const((8, m)),
                  const((m, LANE_TILE)), const((m, LANE_TILE))],
        out_specs=(out(length), out(length), out(8)),
        scratch_shapes=[pltpu.VMEM((m, nf), F32)] * 2,
        compiler_params=_params(("arbitrary", "arbitrary")),
        name="hyena_filters_%d" % length,
    )(*feats, w1, b1, w2, b2, w3, w3, decay, decay, cm, sm, alt8, twc, tws)


def _short_conv(ue, uo, w, b):
    m = ue.shape[0]
    row = lax.broadcasted_iota(jnp.int32, ue.shape, 0)
    prev_odd = jnp.where(row == 0, 0.0, pltpu.roll(uo, 1, 0))
    next_even = jnp.where(row == m - 1, 0.0, pltpu.roll(ue, m - 1, 0))
    w0, w1, w2 = w[0:1], w[1:2], w[2:3]
    return b + w0 * prev_odd + w1 * ue + w2 * uo, b + w0 * ue + w1 * uo + w2 * next_even


def _hy_conv_kernel(z_ref, g_refs, wz_ref, bz_ref, wg_refs, bg_refs, ha_ref, hb_ref, hm_ref, skip_ref,
                    cm_ref, sm_ref, alt_ref, twc_ref, tws_ref, o_ref, tmp_ref):
    width = z_ref.shape[1]
    m = cm_ref.shape[0]
    lanes = tmp_ref.shape[-1]
    slabs = range(width // lanes)

    def split(ref):
        for h in slabs:
            tmp_ref[h] = ref[:, h * lanes:(h + 1) * lanes].astype(F32)
        return [jnp.concatenate([tmp_ref[h, pl.ds(first, m, stride=2), :] for h in slabs], axis=1)
                for first in (0, 1)]

    ze, zo = _short_conv(*split(z_ref), wz_ref[...], bz_ref[...])
    for n in range(HY_ORDER):
        ge, go = _short_conv(*split(g_refs[n]), wg_refs[n][...], bg_refs[n][...])
        ze, zo = _hy_order(ze, zo, ge, go, ha_ref[n], hb_ref[n], hm_ref[n], skip_ref[n],
                           cm_ref, sm_ref, alt_ref, twc_ref[...], tws_ref[...])
    for h in slabs:
        tmp_ref[h, pl.ds(0, m, stride=2), :] = ze[:, h * lanes:(h + 1) * lanes]
        tmp_ref[h, pl.ds(1, m, stride=2), :] = zo[:, h * lanes:(h + 1) * lanes]
        o_ref[:, h * lanes:(h + 1) * lanes] = tmp_ref[h].astype(o_ref.dtype)


def _hy_order(ze, zo, ge, go, ha, hb, hm, skip, cm_ref, sm_ref, alt_ref, twc, tws):
    m, width = ze.shape
    zz = jnp.concatenate([ze.astype(BF16), zo.astype(BF16)], axis=1)
    a = _dot(cm_ref[...], zz)
    b = _dot(sm_ref[...], zz)
    mid = _dot(alt_ref[...], zz)[0:1]
    (a_lo, a_hi), (b_lo, b_hi) = _split_spectrum(a[:, :width], b[:, :width], a[:, width:], b[:, width:], twc, tws)

    ha_lo, ha_hi, hb_lo, hb_hi = ha[0:m, :], ha[m:2 * m, :], hb[0:m, :], hb[m:2 * m, :]
    yr_lo, ym_lo = a_lo * ha_lo - b_lo * hb_lo, a_lo * hb_lo + b_lo * ha_lo
    yr_hi, ym_hi = a_hi * ha_hi - b_hi * hb_hi, a_hi * hb_hi + b_hi * ha_hi
    ha_m, hb_m = hm[0:1, :], hm[1:2, :]
    yr_m = mid[:, :width] * ha_m - mid[:, width:] * hb_m
    ym_m = mid[:, :width] * hb_m + mid[:, width:] * ha_m

    row = lax.broadcasted_iota(jnp.int32, (m, width), 0)
    half0 = jnp.where(row == 0, 0.5, 1.0)
    qr, qm = yr_lo - yr_hi, ym_lo + ym_hi
    pr = jnp.concatenate([((yr_lo + yr_hi) * half0).astype(BF16),
                          ((qr * twc + qm * tws) * half0).astype(BF16)], axis=1)
    pm = jnp.concatenate([(ym_lo - ym_hi).astype(BF16), (qm * twc - qr * tws).astype(BF16)], axis=1)
    y = _dot(cm_ref[...], pr) + _dot(sm_ref[...], pm)
    alt = jnp.where((row & 1) == 0, 1.0, -1.0)
    scale = 1.0 / (2 * m)
    return (ge * ((y[:, :width] + alt * yr_m) * scale + skip * ze),
            go * ((y[:, width:] + alt * ym_m) * scale + skip * zo))


def _hy_conv(u, col0, ch, conv_w, conv_b, out_prev, out_rows, row_block, length, layer, ha, hb, hm, skip,
             tables):
    cm, sm, alt8, twc, tws = tables
    nb = u.shape[0]
    s = out_rows
    m = length // 2
    nct = ch // LANE_TILE
    assert col0 % LANE_TILE == 0
    cb = col0 // LANE_TILE
    taps = conv_w.shape[0]
    assert taps == HY_SHORT == 3
    single = dict(pipeline_mode=pl.Buffered(1))
    const = lambda shape: pl.BlockSpec(shape, lambda c, b: (0,) * len(shape), **single)
    hspec = lambda rows: pl.BlockSpec((HY_ORDER, rows, LANE_TILE), lambda c, b: (0, 0, c), **single)
    ucol = lambda k: pl.BlockSpec((None, length, LANE_TILE), lambda c, b: (b, row_block, cb + k * nct + c))
    wcol = lambda rows, k: pl.BlockSpec((rows, LANE_TILE), lambda c, b: (0, k * nct + c))
    orders = range(1, HY_ORDER + 1)
    in_specs = ([ucol(0)] + [ucol(k) for k in orders] + [wcol(taps, 0), wcol(1, 0)]
                + [wcol(taps, k) for k in orders] + [wcol(1, k) for k in orders]
                + [hspec(length), hspec(length), hspec(8),
                   pl.BlockSpec((None, HY_ORDER, 1, LANE_TILE), lambda c, b: (layer, 0, 0, c)),
                   const((m, m)), const((m, m)), const((8, m)), const((m, LANE_TILE)), const((m, LANE_TILE))])
    args = ([u] * (1 + HY_ORDER) + [conv_w, conv_b] + [conv_w] * HY_ORDER + [conv_b] * HY_ORDER
            + [ha, hb, hm, skip, cm, sm, alt8, twc, tws])
    n_in = len(args)
    aliases = {}
    if out_prev is not None:
        in_specs.append(pl.BlockSpec(memory_space=pl.ANY))
        args.append(out_prev)
        aliases = {len(args) - 1: 0}

    def body(*refs):
        k = HY_ORDER
        z_ref, g_refs = refs[0], refs[1:1 + k]
        wz_ref, bz_ref = refs[1 + k], refs[2 + k]
        wg_refs, bg_refs = refs[3 + k:3 + 2 * k], refs[3 + 2 * k:3 + 3 * k]
        _hy_conv_kernel(z_ref, g_refs, wz_ref, bz_ref, wg_refs, bg_refs, *refs[3 + 3 * k:n_in], *refs[-2:])

    return pl.pallas_call(
        body,
        out_shape=jax.ShapeDtypeStruct((nb, s, ch), BF16),
        grid=(nct, nb),
        in_specs=in_specs,
        out_specs=pl.BlockSpec((None, length, LANE_TILE), lambda c, b: (b, row_block, c)),
        scratch_shapes=[pltpu.VMEM((LANE_TILE // 128, length, 128), F32)],
        input_output_aliases=aliases,
        compiler_params=_params(("parallel", "parallel")),
        name="hyena_conv_%d" % length,
    )(*args)


def _rms(x, w):
    return x * lax.rsqrt(jnp.mean(x * x, axis=-1, keepdims=True) + EPS) * w


def _merge_mlp_kernel(yf_ref, yb_ref, xs_ref, zg_ref, dsk_ref, sn_ref, ym_ref, yh_ref, wb_ref, wo_ref,
                      x_ref, xc_ref, mod_ref, nw_ref, w1_ref, w2_ref, nf_ref, o_ref, *, final, n_lat_tiles):
    m = mod_ref[...]
    bw = yf_ref.shape[1]
    ys = yf_ref[...].astype(F32) + yb_ref[...].astype(F32) + dsk_ref[...] * xs_ref[...].astype(F32)
    ys = _rms(ys * _silu(zg_ref[:, 0:bw].astype(F32)), sn_ref[...]).astype(BF16)
    acc = None
    for n, y in enumerate((ys, ym_ref[...].astype(BF16), yh_ref[...])):
        gate = zg_ref[:, (1 + n) * bw:(2 + n) * bw].astype(F32)
        term = jax.nn.sigmoid(gate) * _dot(y, wb_ref[n])
        acc = term if acc is None else acc + term
    x_in = x_ref[...] if xc_ref is None else jnp.where(pl.program_id(1) < n_lat_tiles, x_ref[...], xc_ref[...])
    x = x_in + m[2:3] * _dot(acc.astype(BF16), wo_ref[...])
    h = (_rms(x, nw_ref[...]) * (1.0 + m[4:5]) + m[3:4]).astype(BF16)
    a = jnp.maximum(_dot(h, w1_ref[...]), 0.0)
    x = x + m[5:6] * _dot((a * a).astype(BF16), w2_ref[...])
    o_ref[...] = _rms(x, nf_ref[...]) if final else x


def _merge_mlp(yf, yb, xbc, pnat, d_full, ssd_norm, ym, yh, wb, wo, x, x_ctx, mod, norm_w, layer, w1, w2,
               norm_f, final, n_lat_tiles, n_tiles):
    nb, _, d = x.shape
    s = yf.shape[1]
    bw = yf.shape[-1]
    hidden = w1.shape[1]
    row = pl.BlockSpec((None, TOKEN_TILE, bw), lambda b, i: (b, i, 0))
    assert bw == d
    zg = pl.BlockSpec((None, TOKEN_TILE, (1 + N_BRANCH) * d), lambda b, i: (b, i, 0))
    xrow = pl.BlockSpec((None, TOKEN_TILE, d), lambda b, i: (b, i, 0))
    lay = lambda n: pl.BlockSpec((None, 1, n), lambda b, i: (layer, 0, 0))
    single = dict(pipeline_mode=pl.Buffered(1))
    const = lambda shape: pl.BlockSpec(shape, lambda b, i: (0,) * len(shape), **single)
    in_specs = [row, row, row, zg, lay(bw), lay(bw), row, row, const((N_BRANCH, bw, d)), const((d, d))]
    args = [yf, yb, xbc, pnat, d_full, ssd_norm, ym, yh, wb, wo]
    aliases = {}
    if x_ctx is None:
        in_specs.append(xrow)
        args.append(x)
        if not final:
            aliases = {len(args) - 1: 0}
    else:
        in_specs += [pl.BlockSpec((None, TOKEN_TILE, d), lambda b, i: (b, jnp.minimum(i, n_lat_tiles - 1), 0)),
                     pl.BlockSpec((None, TOKEN_TILE, d), lambda b, i: (b, jnp.maximum(i - n_lat_tiles, 0), 0))]
        args += [x, x_ctx]
    in_specs += [pl.BlockSpec((None, 6, d), _mod_row(nb, n_lat_tiles)), lay(d),
                 const((d, hidden)), const((hidden, d)), pl.BlockSpec((1, d), lambda b, i: (0, 0))]
    args += [mod, norm_w, w1, w2, norm_f]

    def body(*refs):
        head, tail = refs[:11], refs[11:]
        xc = None if x_ctx is None else tail[0]
        _merge_mlp_kernel(*head, xc, *tail[0 if x_ctx is None else 1:], final=final, n_lat_tiles=n_lat_tiles)

    return pl.pallas_call(
        body,
        out_shape=jax.ShapeDtypeStruct((nb, n_tiles * TOKEN_TILE if final else s, d), F32),
        grid=(nb, n_tiles),
        in_specs=in_specs,
        out_specs=xrow,
        input_output_aliases=aliases,
        compiler_params=_params(("parallel", "parallel")),
        name="merge_mlp",
    )(*args)


def _dir_rows(g, nb, s, per_dir):
    g = g[:, :2 * per_dir].reshape(nb, s, 2, per_dir)
    return jnp.transpose(g, (2, 0, 3, 1))


def kernel(x, c, ctx, c_ctx, norm1_w, mod_w, mod_b, w_in, ssd_conv_w, ssd_conv_b, ssd_dt_bias, ssd_a_log,
           ssd_d, ssd_norm_w, ml_conv_w, ml_conv_b, ml_gate_b, ml_norm_w, hy_conv_w, hy_conv_b, hy_ffn_w1,
           hy_ffn_b1, hy_ffn_w2, hy_ffn_b2, hy_ffn_w3, hy_decay, hy_skip, w_branch, w_out, norm2_w,
           mlp_w1, mlp_w2, norm_f_w):
    nb, seq, d = x.shape
    ctx_len = ctx.shape[1]
    depth = w_in.shape[0]
    s = seq + ctx_len
    assert seq % ctx_len == 0 and ctx_len % TOKEN_TILE == 0 and seq % GRID_W == 0
    n_lat_tiles, n_tiles = seq // TOKEN_TILE, s // TOKEN_TILE
    n_lat_ch, n_ch = seq // CHUNK, s // CHUNK
    assert n_lat_ch % SCAN_CHUNKS == 0 and n_ch % SCAN_CHUNKS == 0
    rows_g = seq // GRID_W

    ssd_heads = ssd_d.shape[-1]
    ssd_inner = ssd_norm_w.shape[-1]
    ssd_hd = ssd_inner // ssd_heads
    ssd_conv_ch = ssd_conv_w.shape[-1]
    ssd_ds = (ssd_conv_ch - ssd_inner) // (2 * SSD_GROUPS)
    ml_heads = ml_gate_b.shape[-1]
    ml_inner = ml_heads * ML_HEADDIM
    hy_inner = hy_skip.shape[-1]
    ssd_cols = ssd_conv_ch + ssd_inner + 2 * ssd_heads
    ml_cols = 4 * ml_inner + 4 * ml_heads
    rec_cols = ssd_cols + ml_cols
    hy_cols = (HY_ORDER + 1) * hy_inner

    o_z = ssd_conv_ch
    o_dt = ssd_conv_ch + ssd_inner
    o_ml = ssd_cols
    o_mlg = ssd_cols + 4 * ml_inner
    o_hy = rec_cols
    o_g = rec_cols + hy_cols

    xa, xa_ctx = x, ctx
    rpad = (-(nb + 1)) % 8
    c_all = jnp.concatenate([c, c_ctx[None], jnp.zeros((rpad, d), F32)], axis=0)

    tab_lat = _dft_tables(seq)
    tab_ctx = _dft_tables(ctx_len)
    k_scale = jnp.concatenate([jnp.ones((1, ml_inner), F32),
                               jnp.full((1, ml_inner), ML_HEADDIM ** -0.5, F32)], axis=1)

    norm1 = norm1_w[:, None, :]
    norm2 = norm2_w[:, None, :]
    ssd_norm = ssd_norm_w[:, None, :]
    ml_norm = ml_norm_w[:, None, :]
    d_full = jnp.repeat(ssd_d, ssd_hd, axis=-1)[:, None, :]
    w3 = hy_ffn_w3.reshape(depth, hy_ffn_w3.shape[1], HY_ORDER, 2, hy_inner).transpose(0, 2, 3, 1, 4)
    decay = hy_decay[:, :, :, None, :]
    skip = hy_skip[:, :, None, :]
    hb1 = hy_ffn_b1[:, None, :]
    hb2 = hy_ffn_b2[:, None, :]

    for l in range(depth):
        need_ctx = l < depth - 1
        used_tiles = n_tiles if need_ctx else n_lat_tiles
        mod = _mod_vectors(c_all, mod_w, mod_b[:, None, :], l).reshape(-1, 6, d)

        wl = w_in[l].astype(BF16)
        w_nat = jnp.concatenate([wl[:, o_z:o_dt], wl[:, o_g:], wl[:, o_hy:o_g]], axis=1)
        w_xbc = wl[:, :o_z]
        w_ml = wl[:, o_ml:o_mlg]
        gpad = lambda w: jnp.pad(w, ((0, 0), (0, 128 - w.shape[1])))
        hn, hn_cm = _normmod(xa, xa_ctx, norm1, l, mod, 0, rows_g, n_lat_tiles, n_tiles)
        pnat = _matmul(hn.reshape(nb * s, d), w_nat, BF16, "proj_nat").reshape(nb, s, -1)
        c_g = ssd_inner
        c_hy = c_g + N_BRANCH * d

        xbc, p_dt = _proj_conv(hn, w_xbc, gpad(wl[:, o_dt:o_ml]), ssd_conv_w[l], ssd_conv_b[l][None], None,
                               n_lat_tiles, n_tiles, "proj_xbc")
        dt_r = _dir_rows(p_dt.reshape(nb * s, -1), nb, s, ssd_heads)
        y_f, y_b = _ssd_scan(xbc, dt_r, ssd_dt_bias[l][:, :, None], ssd_a_log[l][:, :, None],
                             ssd_heads, ssd_hd, ssd_ds, n_lat_ch, n_ch)

        pml, p_mlg = _proj_conv(hn_cm, w_ml, gpad(wl[:, o_mlg:o_hy]), ml_conv_w[l], ml_conv_b[l][None],
                                k_scale, n_lat_tiles, n_tiles, "proj_ml")
        g_r = _dir_rows(p_mlg.reshape(nb * s, -1), nb, s, 2 * ml_heads)
        gate_b = ml_gate_b[l]
        c_f, c_b = _ml_scan(pml, pml, g_r[:, :, :ml_heads], g_r[:, :, ml_heads:],
                            gate_b[:, 0, :, None], gate_b[:, 1, :, None], ml_heads, n_lat_ch, n_ch)
        ym = _ml_out(c_f, c_b, pml, ml_norm, l, ml_heads, rows_g, n_lat_tiles, used_tiles)

        hcw, hcb = hy_conv_w[l], hy_conv_b[l][None]
        fl = _hy_filters(seq, l, hy_ffn_w1, hb1, hy_ffn_w2, hb2, w3, decay, tab_lat)
        rows = s if need_ctx else seq
        yh = _hy_conv(pnat, c_hy, hy_inner, hcw, hcb, None, rows, 0, seq, l, *fl, skip, tab_lat)
        if need_ctx:
            fc = _hy_filters(ctx_len, l, hy_ffn_w1, hb1, hy_ffn_w2, hb2, w3, decay, tab_ctx)
            yh = _hy_conv(pnat, c_hy, hy_inner, hcw, hcb, yh, rows, seq // ctx_len, ctx_len, l, *fc, skip,
                          tab_ctx)

        xa = _merge_mlp(y_f, y_b, xbc, pnat, d_full, ssd_norm, ym, yh, w_branch[l].astype(BF16),
                        w_out[l].astype(BF16), xa, xa_ctx, mod, norm2, l, mlp_w1[l].astype(BF16),
                        mlp_w2[l].astype(BF16), norm_f_w[None], not need_ctx, n_lat_tiles, used_tiles)
        xa_ctx = None

    return xa
```

```python
import functools
import math

import jax
import jax.numpy as jnp
import numpy as np
from jax import lax
from jax.experimental import pallas as pl
from jax.experimental.pallas import tpu as pltpu

F32 = jnp.float32
BF16 = jnp.bfloat16
HIGHEST = lax.Precision.HIGHEST

GRID_W = 64
CHUNK = 128
EPS = 1e-6
SSD_GROUPS = 2
SSD_CONV = 5
ML_HEADDIM = 128
ML_CONV = 5
HY_ORDER = 2
HY_SHORT = 3
HY_BANDS = 16
N_BRANCH = 3

TOKEN_TILE = 256
LANE_TILE = 256
HALO = 16
SCAN_CHUNKS = 2
VMEM_LIMIT = 56 * 1024 * 1024

_hdot = functools.partial(jnp.dot, precision=HIGHEST, preferred_element_type=F32)
_dot = functools.partial(jnp.dot, preferred_element_type=F32)


def _masked_sums(x, sel):
    r = x.shape[0]
    hi = x.astype(BF16).astype(F32)
    mid = (x - hi).astype(BF16).astype(F32)
    lo = x - hi - mid
    parts = _dot(jnp.concatenate([hi, mid, lo], axis=0).astype(BF16), sel.astype(BF16))
    return parts[0:r] + parts[r:2 * r] + parts[2 * r:3 * r]


def _params(sem, vmem=None):
    return pltpu.CompilerParams(dimension_semantics=sem, vmem_limit_bytes=vmem or VMEM_LIMIT)


def _softplus(x):
    return jnp.maximum(x, 0.0) + jnp.log(1.0 + jnp.exp(-jnp.abs(x)))


def _silu(x):
    return x * jax.nn.sigmoid(x)


def _mod_kernel(c_ref, w_ref, b_ref, o_ref):
    o_ref[...] = _hdot(_silu(c_ref[...]), w_ref[...]) + b_ref[...]


def _mod_vectors(c_all, mod_w, mod_b, layer):
    r, d = c_all.shape
    n = mod_w.shape[-1]
    tn = n // 6
    return pl.pallas_call(
        _mod_kernel,
        out_shape=jax.ShapeDtypeStruct((r, n), F32),
        grid=(n // tn,),
        in_specs=[pl.BlockSpec((r, d), lambda j: (0, 0)),
                  pl.BlockSpec((None, d, tn), lambda j: (layer, 0, j)),
                  pl.BlockSpec((None, 1, tn), lambda j: (layer, 0, j))],
        out_specs=pl.BlockSpec((r, tn), lambda j: (0, j)),
        compiler_params=_params(("parallel",)),
        name="mod_vectors",
    )(c_all, mod_w, mod_b)


def _grid_perm(rows_g):
    wpt = TOKEN_TILE // rows_g
    src = np.arange(TOKEN_TILE).reshape(rows_g, wpt).T.reshape(-1)
    p = np.zeros((TOKEN_TILE, TOKEN_TILE), np.float32)
    p[np.arange(TOKEN_TILE), src] = 1.0
    return p


def _normmod_kernel(x_ref, xc_ref, x4_ref, perm_ref, nw_ref, mod_ref, o_ref, ocm_ref, *, si, n_lat_tiles):
    i = pl.program_id(1)
    m = mod_ref[...]

    def normed(x):
        h = x * lax.rsqrt(jnp.mean(x * x, axis=-1, keepdims=True) + EPS) * nw_ref[...]
        return (h * (1.0 + m[si + 1:si + 2]) + m[si:si + 1]).astype(BF16)

    h = normed(x_ref[...] if xc_ref is None else jnp.where(i < n_lat_tiles, x_ref[...], xc_ref[...]))
    o_ref[...] = h

    @pl.when(i < n_lat_tiles)
    def _():
        x4 = x4_ref[...]
        ocm_ref[...] = _dot(perm_ref[...], normed(x4.reshape(TOKEN_TILE, x4.shape[-1]))).astype(BF16)

    @pl.when(i >= n_lat_tiles)
    def _():
        ocm_ref[...] = h


def _mod_row(nb, n_lat_tiles):
    return lambda b, i: (jnp.where(i < n_lat_tiles, b, nb), 0, 0)


def _normmod(x, x_ctx, norm_w, layer, mod, si, rows_g, n_lat_tiles, n_tiles):
    nb, rows, d = x.shape
    s = n_tiles * TOKEN_TILE
    wpt = TOKEN_TILE // rows_g
    assert wpt % 8 == 0 and GRID_W % wpt == 0 and rows % GRID_W == 0
    x4 = x.reshape(nb, rows // GRID_W, GRID_W, d)
    tile = pl.BlockSpec((None, TOKEN_TILE, d), lambda b, i: (b, i, 0))
    if x_ctx is None:
        srcs, src_specs = [x], [tile]
    else:
        srcs = [x, x_ctx]
        src_specs = [pl.BlockSpec((None, TOKEN_TILE, d), lambda b, i: (b, jnp.minimum(i, n_lat_tiles - 1), 0)),
                     pl.BlockSpec((None, TOKEN_TILE, d), lambda b, i: (b, jnp.maximum(i - n_lat_tiles, 0), 0))]

    def body(*refs):
        xc = None if x_ctx is None else refs[1]
        _normmod_kernel(refs[0], xc, *refs[len(srcs):], si=si, n_lat_tiles=n_lat_tiles)

    return pl.pallas_call(
        body,
        out_shape=(jax.ShapeDtypeStruct((nb, s, d), BF16),) * 2,
        grid=(nb, n_tiles),
        in_specs=src_specs + [
            pl.BlockSpec((None, rows_g, wpt, d), lambda b, i: (b, 0, jnp.minimum(i, n_lat_tiles - 1), 0)),
            pl.BlockSpec((TOKEN_TILE, TOKEN_TILE), lambda b, i: (0, 0)),
            pl.BlockSpec((None, 1, d), lambda b, i: (layer, 0, 0)),
            pl.BlockSpec((None, 6, d), _mod_row(nb, n_lat_tiles))],
        out_specs=(tile, tile),
        compiler_params=_params(("parallel", "parallel")),
        name="normmod",
    )(*srcs, x4, jnp.asarray(_grid_perm(rows_g), BF16), norm_w, mod)


def _mm_kernel(a_ref, w_ref, o_ref):
    o_ref[...] = _dot(a_ref[...], w_ref[...]).astype(o_ref.dtype)


def _pick(n, cands):
    for c in cands:
        if n % c == 0:
            return c
    return n


def _matmul(a, w, out_dtype, name):
    t, k = a.shape
    n = w.shape[1]
    tm = _pick(t, (2048, 1024, 768, 512, 256))
    tn = _pick(n, (1024, 512, 256, 128))
    return pl.pallas_call(
        _mm_kernel,
        out_shape=jax.ShapeDtypeStruct((t, n), out_dtype),
        grid=(t // tm, n // tn),
        in_specs=[pl.BlockSpec((tm, k), lambda i, j: (i, 0)),
                  pl.BlockSpec((k, tn), lambda i, j: (0, j))],
        out_specs=pl.BlockSpec((tm, tn), lambda i, j: (i, j)),
        compiler_params=_params(("parallel", "parallel")),
        name=name,
    )(a, w)


def _shift_stack(taps):
    pad = taps // 2
    return np.concatenate([np.eye(TOKEN_TILE, k=j - pad, dtype=np.float32) for j in range(taps) if j != pad])


def _conv_tile(u, prev, nxt, sh_ref, w, b, scale):
    tm = u.shape[0]
    taps = w.shape[0]
    pad = taps // 2

    def finish(acc):
        acc = _silu(acc)
        return (acc if scale is None else acc * scale).astype(BF16)

    def edge(window):
        rows = window.shape[0]
        acc = b + window[HALO:2 * HALO] * w[pad:pad + 1]
        for j in range(taps):
            if j != pad:
                acc = acc + pltpu.roll(window, (pad - j) % rows, 0)[HALO:2 * HALO] * w[j:j + 1]
        return finish(acc)

    shifted = _dot(sh_ref[...], u)
    acc = b + u.astype(F32) * w[pad:pad + 1]
    blk = 0
    for j in range(taps):
        if j != pad:
            acc = acc + shifted[blk * tm:(blk + 1) * tm] * w[j:j + 1]
            blk += 1
    head = edge(jnp.concatenate([prev, u[0:2 * HALO].astype(F32)], axis=0))
    tail = edge(jnp.concatenate([u[tm - 2 * HALO:tm].astype(F32), nxt], axis=0))
    return jnp.concatenate([head, finish(acc)[HALO:tm - HALO], tail], axis=0)


def _proj_conv_kernel(a_ref, w_ref, wg_ref, sh_ref, cw_ref, cb_ref, *rest, n_conv, bounds):
    o_ref, og_ref, buf_ref = rest[-3:]
    s_ref = rest[0] if len(rest) == 4 else None
    j = pl.program_id(1)
    acc = _dot(a_ref[...], w_ref[...])

    @pl.when(j == 0)
    def _():
        og_ref[...] = _dot(a_ref[...], wg_ref[...])

    @pl.when(j >= n_conv)
    def _():
        o_ref[...] = acc.astype(o_ref.dtype)

    @pl.when(j < n_conv)
    def _():
        buf_ref[...] = acc.astype(BF16)
        w, b = cw_ref[...], cb_ref[...]
        scale = None if s_ref is None else s_ref[...]
        zeros = jnp.zeros((HALO, buf_ref.shape[1]), F32)
        for blk in range(buf_ref.shape[0] // TOKEN_TILE):
            r0 = blk * TOKEN_TILE
            prev = zeros if blk in bounds else buf_ref[r0 - HALO:r0, :].astype(F32)
            nxt = zeros if blk + 1 in bounds else buf_ref[r0 + TOKEN_TILE:r0 + TOKEN_TILE + HALO, :].astype(F32)
            o_ref[r0:r0 + TOKEN_TILE, :] = _conv_tile(buf_ref[r0:r0 + TOKEN_TILE, :], prev, nxt, sh_ref, w, b, scale)


def _proj_conv(a, w, w_gate, conv_w, conv_b, scale, n_lat_tiles, n_tiles, name):
    nb, s, k = a.shape
    n = w.shape[1]
    taps, nc = conv_w.shape
    ng = w_gate.shape[1]
    tn = _pick(math.gcd(n, nc), (512, 256))
    n_conv = nc // tn
    cidx = lambda b, j: (0, jnp.minimum(j, n_conv - 1))
    in_specs = [pl.BlockSpec((None, s, k), lambda b, j: (b, 0, 0)),
                pl.BlockSpec((k, tn), lambda b, j: (0, j)),
                pl.BlockSpec((k, ng), lambda b, j: (0, 0)),
                pl.BlockSpec(((taps - 1) * TOKEN_TILE, TOKEN_TILE), lambda b, j: (0, 0)),
                pl.BlockSpec((taps, tn), cidx),
                pl.BlockSpec((1, tn), cidx)]
    args = [a, w, w_gate, jnp.asarray(_shift_stack(taps), BF16), conv_w, conv_b]
    if scale is not None:
        in_specs.append(pl.BlockSpec((1, tn), cidx))
        args.append(scale)
    return pl.pallas_call(
        functools.partial(_proj_conv_kernel, n_conv=n_conv, bounds=(0, n_lat_tiles, n_tiles)),
        out_shape=(jax.ShapeDtypeStruct((nb, s, n), BF16), jax.ShapeDtypeStruct((nb, s, ng), F32)),
        grid=(nb, n // tn),
        in_specs=in_specs,
        out_specs=(pl.BlockSpec((None, s, tn), lambda b, j: (b, 0, j)),
                   pl.BlockSpec((None, s, ng), lambda b, j: (b, 0, 0))),
        scratch_shapes=[pltpu.VMEM((s, tn), BF16)],
        compiler_params=_params(("parallel", "arbitrary")),
        name=name,
    )(*args)


def _chunk_index(n_lat_ch, n_ch):
    return lambda d, j: (j + n_lat_ch) % n_ch if d == 0 else n_ch - 1 - j


def _scan_masks(d, t):
    ii = lax.broadcasted_iota(jnp.int32, (t, t), 0)
    jj = lax.broadcasted_iota(jnp.int32, (t, t), 1)
    mask = jj <= ii if d == 0 else jj >= ii
    tri_t = jnp.where(ii <= jj if d == 0 else ii >= jj, 1.0, 0.0).astype(F32)
    return mask, tri_t


def _gate_rows(la_r, lw_r, tri_t, m_prev):
    cum_r = _masked_sums(la_r, tri_t)
    last = jnp.sum(la_r, axis=1, keepdims=True)
    g_r = last - cum_r + lw_r
    m_loc = jnp.max(g_r, axis=1, keepdims=True)
    e_r = jnp.exp(g_r - m_loc)
    m_new = jnp.maximum(last + m_prev, m_loc)
    s_old = jnp.exp(last + m_prev - m_new)
    s_new = jnp.exp(m_loc - m_new)
    return cum_r, cum_r - lw_r, e_r, m_new, s_old, s_new


def _head_probs(cum_row, crow_row, m_prev_h, mask, qk):
    t = qk.shape[0]
    colb = jnp.broadcast_to(cum_row, (t, t)).T
    dlog = jnp.where(mask, colb - crow_row, -jnp.inf)
    inter = colb[:, 0:1] + m_prev_h
    m_row = jnp.maximum(inter, jnp.max(dlog, axis=1, keepdims=True))
    p = jnp.exp(dlog - m_row) * qk
    return p.astype(BF16), jnp.exp(inter - m_row), m_row


def _ssd_scan_kernel(*refs, heads, groups, hd, ds):
    j = pl.program_id(1)
    ins, outs, states = refs[:12], refs[12:14], refs[14:16]

    @pl.when(j == 0)
    def _():
        for st_ref in states:
            st_ref[...] = jnp.zeros_like(st_ref)

    for sub in range(SCAN_CHUNKS):
        for d in range(2):
            _ssd_chunk(d, _scan_rows(d, sub), *ins[6 * d:6 * d + 6], outs[d], states[d],
                       heads=heads, groups=groups, hd=hd, ds=ds)


def _scan_rows(d, sub):
    first = (sub if d == 0 else SCAN_CHUNKS - 1 - sub) * CHUNK
    return slice(first, first + CHUNK)


def _ssd_chunk(d, rs, xs_ref, b_ref, c_ref, dt_ref, bias_ref, alog_ref, o_ref, st_ref, *, heads, groups, hd, ds):
    t = CHUNK
    hpg = heads // groups
    pw = 2 * hd
    mask, tri_t = _scan_masks(d, t)
    dt = _softplus(dt_ref[:, rs] + bias_ref[...])
    la_r = -dt * jnp.exp(alog_ref[...])
    cum_r = _masked_sums(la_r, tri_t)
    last = jnp.sum(la_r, axis=1, keepdims=True)
    crow = cum_r - jnp.log(dt)
    e_r = jnp.exp(last - crow)
    e_last = jnp.exp(last)
    lo = lax.broadcasted_iota(jnp.int32, (1, pw), 1) < hd

    for g in range(groups):
        q = c_ref[rs, g * ds:(g + 1) * ds]
        k_t = b_ref[rs, g * ds:(g + 1) * ds].astype(F32).T
        qk = _dot(q, k_t.astype(BF16))
        w0 = g * hpg * hd
        qs = _dot(q, st_ref[:, w0:w0 + hpg * hd].astype(BF16))
        for i in range(hpg // 2):
            h0 = g * hpg + 2 * i
            c0 = h0 * hd
            vp = xs_ref[rs, c0:c0 + pw]
            zero = jnp.zeros_like(vp)
            v_bd = jnp.concatenate([jnp.where(lo, vp, zero), jnp.where(lo, zero, vp)], axis=0)
            probs, carry, kte = [], [], []
            for h in (h0, h0 + 1):
                colb = jnp.broadcast_to(cum_r[h:h + 1], (t, t)).T
                dlog = jnp.where(mask, colb - crow[h:h + 1], -jnp.inf)
                probs.append((jnp.exp(dlog) * qk).astype(BF16))
                carry.append(jnp.exp(colb))
                kte.append((k_t * e_r[h:h + 1]).astype(BF16))
            lhs = jnp.concatenate([jnp.concatenate(probs, axis=1), jnp.concatenate(kte, axis=1)], axis=0)
            res = _dot(lhs, v_bd)
            y = res[:t] + qs[:, 2 * i * hd:2 * i * hd + pw] * jnp.where(lo, carry[0], carry[1])
            o_ref[rs, c0:c0 + pw] = y.astype(o_ref.dtype)
            decay = jnp.where(lo, e_last[h0:h0 + 1], e_last[h0 + 1:h0 + 2])
            st_ref[:, c0:c0 + pw] = decay * st_ref[:, c0:c0 + pw] + res[t:]


def _ssd_scan(xbc, dt_r, bias_c, alog_c, heads, hd, ds, n_lat_ch, n_ch):
    nb, s, _ = xbc.shape
    inner = heads * hd
    gn = SSD_GROUPS * ds
    assert 2 * hd == CHUNK and (heads // SSD_GROUPS) % 2 == 0 and inner % gn == 0
    rows = SCAN_CHUNKS * CHUNK
    n_blk = n_ch // SCAN_CHUNKS
    cidx = _chunk_index(n_lat_ch // SCAN_CHUNKS, n_blk)

    def specs(d):
        return [pl.BlockSpec((None, rows, inner), lambda b, j: (b, cidx(d, j), 0)),
                pl.BlockSpec((None, rows, gn), lambda b, j: (b, cidx(d, j), inner // gn)),
                pl.BlockSpec((None, rows, gn), lambda b, j: (b, cidx(d, j), inner // gn + 1)),
                pl.BlockSpec((None, None, heads, rows), lambda b, j: (d, b, 0, cidx(d, j))),
                pl.BlockSpec((None, heads, 1), lambda b, j: (d, 0, 0)),
                pl.BlockSpec((None, heads, 1), lambda b, j: (d, 0, 0))]

    out = lambda d: pl.BlockSpec((None, rows, inner), lambda b, j: (b, cidx(d, j), 0))
    args = (xbc, xbc, xbc, dt_r, bias_c, alog_c)
    return pl.pallas_call(
        functools.partial(_ssd_scan_kernel, heads=heads, groups=SSD_GROUPS, hd=hd, ds=ds),
        out_shape=(jax.ShapeDtypeStruct((nb, s, inner), BF16),) * 2,
        grid=(nb, n_blk),
        in_specs=specs(0) + specs(1),
        out_specs=(out(0), out(1)),
        scratch_shapes=[pltpu.VMEM((ds, inner), F32)] * 2,
        compiler_params=_params(("parallel", "arbitrary")),
        name="ssd_scan",
    )(*args, *args)


def _ml_scan_kernel(*refs, heads):
    j = pl.program_id(1)
    ins, outs, states = refs[:14], refs[14:16], refs[16:20]

    @pl.when(j == 0)
    def _():
        for ref in states:
            ref[...] = jnp.zeros_like(ref)

    for sub in range(SCAN_CHUNKS):
        for d in range(2):
            _ml_chunk(d, _scan_rows(d, sub), *ins[7 * d:7 * d + 7], outs[d], *states[2 * d:2 * d + 2],
                      heads=heads)


def _ml_chunk(d, rs, q_ref, k_ref, v_ref, gi_ref, gf_ref, bi_ref, bf_ref, o_ref, st_ref, m_ref, *, heads):
    t = CHUNK
    dh = ML_HEADDIM
    mask, tri_t = _scan_masks(d, t)
    lw_r = gi_ref[:, rs] + bi_ref[...]
    f = gf_ref[:, rs] + bf_ref[...]
    la_r = jnp.minimum(f, 0.0) - jnp.log(1.0 + jnp.exp(-jnp.abs(f)))
    m_prev = m_ref[:, 0:1]
    cum_r, crow, e_r, m_new, s_old, s_new = _gate_rows(la_r, lw_r, tri_t, m_prev)
    ones = jnp.ones((t, dh), BF16)

    for h in range(heads):
        c0 = h * dh
        q = q_ref[rs, c0:c0 + dh]
        k_t = k_ref[rs, c0:c0 + dh].astype(F32).T
        v_aug = jnp.concatenate([v_ref[rs, c0:c0 + dh], ones], axis=1)
        qk = _dot(q, k_t.astype(BF16))
        s0 = 2 * c0
        qs = _dot(q, st_ref[:, s0:s0 + 2 * dh].astype(BF16))
        p, cf, m_row = _head_probs(cum_r[h:h + 1], crow[h:h + 1], m_prev[h:h + 1], mask, qk)
        res = _dot(jnp.concatenate([p, (k_t * e_r[h:h + 1]).astype(BF16)], axis=0), v_aug)
        y = res[:t] + qs * cf
        cell = y[:, :dh] / jnp.maximum(jnp.abs(y[:, dh:]), jnp.exp(-m_row))
        o_ref[rs, c0:c0 + dh] = cell.astype(o_ref.dtype)
        st_ref[:, s0:s0 + 2 * dh] = s_old[h:h + 1] * st_ref[:, s0:s0 + 2 * dh] + s_new[h:h + 1] * res[t:]
    m_ref[...] = jnp.broadcast_to(m_new, m_ref.shape)


def _ml_scan(qk, pml, gi_r, gf_r, bi_c, bf_c, heads, n_lat_ch, n_ch):
    nb, s, _ = qk.shape
    inner = heads * ML_HEADDIM
    rows = SCAN_CHUNKS * CHUNK
    n_blk = n_ch // SCAN_CHUNKS
    cidx = _chunk_index(n_lat_ch // SCAN_CHUNKS, n_blk)

    def specs(d):
        gspec = pl.BlockSpec((None, None, heads, rows), lambda b, j: (d, b, 0, cidx(d, j)))
        bspec = pl.BlockSpec((None, heads, 1), lambda b, j: (d, 0, 0))
        col = lambda c: pl.BlockSpec((None, rows, inner), lambda b, j: (b, cidx(d, j), c))
        return [col(0), col(1), col(2), gspec, gspec, bspec, bspec]

    out = lambda d: pl.BlockSpec((None, rows, inner), lambda b, j: (b, cidx(d, j), 0))
    args = (qk, qk, pml, gi_r, gf_r, bi_c, bf_c)
    return pl.pallas_call(
        functools.partial(_ml_scan_kernel, heads=heads),
        out_shape=(jax.ShapeDtypeStruct((nb, s, inner), BF16),) * 2,
        grid=(nb, n_blk),
        in_specs=specs(0) + specs(1),
        out_specs=(out(0), out(1)),
        scratch_shapes=[pltpu.VMEM((ML_HEADDIM, 2 * inner), F32), pltpu.VMEM((heads, 128), F32)] * 2,
        compiler_params=_params(("parallel", "arbitrary")),
        name="ml_scan",
    )(*args, *args)


def _ml_out_kernel(cf_ref, cb_ref, o_in_ref, nw_ref, perm_ref, *rest, heads, permute):
    o_ref = rest[-1]
    dh = ML_HEADDIM
    parts = []
    for h in range(heads):
        sl = slice(h * dh, (h + 1) * dh)
        c = cf_ref[:, sl].astype(F32) + cb_ref[:, sl].astype(F32)
        c = c * lax.rsqrt(jnp.mean(c * c, axis=-1, keepdims=True) + EPS) * nw_ref[:, sl]
        parts.append((jax.nn.sigmoid(o_in_ref[:, sl].astype(F32)) * c).astype(BF16))
    y = jnp.concatenate(parts, axis=1)
    y = _dot(perm_ref[...], y) if permute else y.astype(F32)
    o_ref[...] = y.reshape(o_ref.shape)


def _ml_out(cf, cb, pml, norm_w, layer, heads, rows_g, n_lat_tiles, n_tiles):
    nb, _, inner = cf.shape
    wpt = TOKEN_TILE // rows_g
    rpt = TOKEN_TILE // GRID_W
    perm_t = jnp.asarray(_grid_perm(rows_g).T, BF16)
    out_shape = jax.ShapeDtypeStruct((nb, n_tiles * rpt, GRID_W, inner), F32)

    def call(tile0, tiles, out_spec, permute, prev):
        row = pl.BlockSpec((None, TOKEN_TILE, inner), lambda b, i: (b, tile0 + i, 0))
        in_specs = [row, row,
                    pl.BlockSpec((None, TOKEN_TILE, inner), lambda b, i: (b, tile0 + i, 3)),
                    pl.BlockSpec((None, 1, inner), lambda b, i: (layer, 0, 0)),
                    pl.BlockSpec((TOKEN_TILE, TOKEN_TILE), lambda b, i: (0, 0))]
        args = [cf, cb, pml, norm_w, perm_t]
        if prev is not None:
            in_specs.append(pl.BlockSpec(memory_space=pl.ANY))
            args.append(prev)
        return pl.pallas_call(
            functools.partial(_ml_out_kernel, heads=heads, permute=permute),
            out_shape=out_shape,
            grid=(nb, tiles),
            in_specs=in_specs,
            out_specs=out_spec,
            input_output_aliases={} if prev is None else {len(args) - 1: 0},
            compiler_params=_params(("parallel", "parallel")),
            name="ml_out_lat" if permute else "ml_out_ctx",
        )(*args)

    y = call(0, n_lat_tiles, pl.BlockSpec((None, rows_g, wpt, inner), lambda b, i: (b, 0, i, 0)), True, None)
    if n_tiles > n_lat_tiles:
        y = call(n_lat_tiles, n_tiles - n_lat_tiles,
                 pl.BlockSpec((None, rpt, GRID_W, inner), lambda b, i: (b, n_lat_tiles + i, 0, 0)), False, y)
    return y.reshape(nb, n_tiles * TOKEN_TILE, inner)


def _dft_tables(length):
    m = length // 2
    k = np.arange(m, dtype=np.int64)
    ang = ((k[:, None] * k[None, :]) % (2 * m)).astype(np.float64) * (np.pi / m)
    alt8 = np.broadcast_to(np.where(k % 2 == 0, 1.0, -1.0)[None, :], (8, m))
    tw = k.astype(np.float64)[:, None] * (np.pi / length) * np.ones((1, LANE_TILE))
    return (jnp.asarray(np.cos(ang), BF16), jnp.asarray(np.sin(ang), BF16), jnp.asarray(alt8, BF16),
            jnp.asarray(np.cos(tw), F32), jnp.asarray(np.sin(tw), F32))


def _hy_feats(length):
    t = jnp.arange(length, dtype=F32)
    t_norm = t / length
    bands = jnp.linspace(1e-4, HY_BANDS - 1, HY_BANDS, dtype=F32)
    ang = (2.0 * math.pi / length) * t[:, None] * bands[None, :]
    feats = jnp.concatenate([t_norm[:, None], jnp.cos(ang), -jnp.sin(ang)], axis=-1)
    return feats[0::2], feats[1::2], t_norm[0::2, None], t_norm[1::2, None]


def _split_spectrum(ae, be, ao, bo, twc, tws):
    tr = twc * ao - tws * bo
    tm = twc * bo + tws * ao
    return (ae + tr, ae - tr), (be + tm, tm - be)


def _hy_filter_kernel(fe_ref, fo_ref, tne_ref, tno_ref, w1_ref, b1_ref, w2_ref, b2_ref, w3f_ref, w3b_ref,
                      df_ref, db_ref, cm_ref, sm_ref, alt_ref, twc_ref, tws_ref, ha_ref, hb_ref, hm_ref,
                      hide_ref, hido_ref):
    m = cm_ref.shape[0]

    @pl.when((pl.program_id(0) == 0) & (pl.program_id(1) == 0))
    def _():
        for f_ref, h_ref in ((fe_ref, hide_ref), (fo_ref, hido_ref)):
            hid = jnp.sin(_hdot(f_ref[...], w1_ref[...]) + b1_ref[...])
            h_ref[...] = jnp.sin(_hdot(hid, w2_ref[...]) + b2_ref[...])

    def taps(hid, tn):
        hid = hid.astype(BF16)
        h_f = _dot(hid, w3f_ref[...].astype(BF16)) * jnp.exp(-tn * jnp.abs(df_ref[...]))
        h_b = _dot(hid, w3b_ref[...].astype(BF16)) * jnp.exp(-tn * jnp.abs(db_ref[...]))
        return (h_f + h_b).astype(BF16), (h_f - h_b).astype(BF16)

    sum_e, dif_e = taps(hide_ref[...], tne_ref[...])
    sum_o, dif_o = taps(hido_ref[...], tno_ref[...])
    twc, tws = twc_ref[...], tws_ref[...]
    cm, sm = cm_ref[...], sm_ref[...]
    (ha_lo, ha_hi), _ = _split_spectrum(_dot(cm, sum_e), 0.0, _dot(cm, sum_o), _dot(sm, sum_o), twc, tws)
    _, (hb_lo, hb_hi) = _split_spectrum(0.0, _dot(sm, dif_e), _dot(cm, dif_o), _dot(sm, dif_o), twc, tws)
    ha_ref[0:m, :] = ha_lo
    ha_ref[m:2 * m, :] = ha_hi
    hb_ref[0:m, :] = hb_lo
    hb_ref[m:2 * m, :] = hb_hi
    row = lax.broadcasted_iota(jnp.int32, hm_ref.shape, 0)
    hm_ref[...] = jnp.where(row == 0, _dot(alt_ref[...], sum_e), _dot(alt_ref[...], dif_o))


def _hy_filters(length, layer, w1, b1, w2, b2, w3, decay, tables):
    cm, sm, alt8, twc, tws = tables
    m = length // 2
    feats = _hy_feats(length)
    nfeat, nf = w1.shape[-2:]
    ch = decay.shape[-1]
    nct = ch // LANE_TILE
    const = lambda shape: pl.BlockSpec(shape, lambda n, c: (0,) * len(shape))
    w3spec = lambda dr: pl.BlockSpec((None, None, None, nf, LANE_TILE), lambda n, c: (layer, n, dr, 0, c))
    dspec = lambda dr: pl.BlockSpec((None, None, None, 1, LANE_TILE), lambda n, c: (layer, n, dr, 0, c))
    lay = lambda a, b: pl.BlockSpec((None, a, b), lambda n, c: (layer, 0, 0))
    out = lambda rows: pl.BlockSpec((None, rows, LANE_TILE), lambda n, c: (n, 0, c))
    return pl.pallas_call(
        _hy_filter_kernel,
        out_shape=(jax.ShapeDtypeStruct((HY_ORDER, length, ch), F32),
                   jax.ShapeDtypeStruct((HY_ORDER, length, ch), F32),
                   jax.ShapeDtypeStruct((HY_ORDER, 8, ch), F32)),
        grid=(HY_ORDER, nct),
        in_specs=[const((m, nfeat)), const((m, nfeat)), const((m, 1)), const((m, 1)),
                  lay(nfeat, nf), lay(1, nf), lay(nf, nf), lay(1, nf),
                  w3spec(0), w3spec(1), dspec(0), dspec(1),
                  const((m, m)), const((m, m)), const((8, m)),
                  const((m, LANE_TILE)), const((m, LANE_TILE))],
        out_specs=(out(length), out(length), out(8)),
        scratch_shapes=[pltpu.VMEM((m, nf), F32)] * 2,
        compiler_params=_params(("arbitrary", "arbitrary")),
        name="hyena_filters_%d" % length,
    )(*feats, w1, b1, w2, b2, w3, w3, decay, decay, cm, sm, alt8, twc, tws)


def _short_conv(ue, uo, w, b):
    m = ue.shape[0]
    row = lax.broadcasted_iota(jnp.int32, ue.shape, 0)
    prev_odd = jnp.where(row == 0, 0.0, pltpu.roll(uo, 1, 0))
    next_even = jnp.where(row == m - 1, 0.0, pltpu.roll(ue, m - 1, 0))
    w0, w1, w2 = w[0:1], w[1:2], w[2:3]
    return b + w0 * prev_odd + w1 * ue + w2 * uo, b + w0 * ue + w1 * uo + w2 * next_even


def _hy_conv_kernel(z_ref, g_refs, wz_ref, bz_ref, wg_refs, bg_refs, ha_ref, hb_ref, hm_ref, skip_ref,
                    cm_ref, sm_ref, alt_ref, twc_ref, tws_ref, o_ref, tmp_ref):
    width = z_ref.shape[1]
    m = cm_ref.shape[0]
    lanes = tmp_ref.shape[-1]
    slabs = range(width // lanes)

    def split(ref):
        for h in slabs:
            tmp_ref[h] = ref[:, h * lanes:(h + 1) * lanes].astype(F32)
        return [jnp.concatenate([tmp_ref[h, pl.ds(first, m, stride=2), :] for h in slabs], axis=1)
                for first in (0, 1)]

    ze, zo = _short_conv(*split(z_ref), wz_ref[...], bz_ref[...])
    for n in range(HY_ORDER):
        ge, go = _short_conv(*split(g_refs[n]), wg_refs[n][...], bg_refs[n][...])
        ze, zo = _hy_order(ze, zo, ge, go, ha_ref[n], hb_ref[n], hm_ref[n], skip_ref[n],
                           cm_ref, sm_ref, alt_ref, twc_ref[...], tws_ref[...])
    for h in slabs:
        tmp_ref[h, pl.ds(0, m, stride=2), :] = ze[:, h * lanes:(h + 1) * lanes]
        tmp_ref[h, pl.ds(1, m, stride=2), :] = zo[:, h * lanes:(h + 1) * lanes]
        o_ref[:, h * lanes:(h + 1) * lanes] = tmp_ref[h].astype(o_ref.dtype)


def _hy_order(ze, zo, ge, go, ha, hb, hm, skip, cm_ref, sm_ref, alt_ref, twc, tws):
    m, width = ze.shape
    zz = jnp.concatenate([ze.astype(BF16), zo.astype(BF16)], axis=1)
    a = _dot(cm_ref[...], zz)
    b = _dot(sm_ref[...], zz)
    mid = _dot(alt_ref[...], zz)[0:1]
    (a_lo, a_hi), (b_lo, b_hi) = _split_spectrum(a[:, :width], b[:, :width], a[:, width:], b[:, width:], twc, tws)

    ha_lo, ha_hi, hb_lo, hb_hi = ha[0:m, :], ha[m:2 * m, :], hb[0:m, :], hb[m:2 * m, :]
    yr_lo, ym_lo = a_lo * ha_lo - b_lo * hb_lo, a_lo * hb_lo + b_lo * ha_lo
    yr_hi, ym_hi = a_hi * ha_hi - b_hi * hb_hi, a_hi * hb_hi + b_hi * ha_hi
    ha_m, hb_m = hm[0:1, :], hm[1:2, :]
    yr_m = mid[:, :width] * ha_m - mid[:, width:] * hb_m
    ym_m = mid[:, :width] * hb_m + mid[:, width:] * ha_m

    row = lax.broadcasted_iota(jnp.int32, (m, width), 0)
    half0 = jnp.where(row == 0, 0.5, 1.0)
    qr, qm = yr_lo - yr_hi, ym_lo + ym_hi
    pr = jnp.concatenate([((yr_lo + yr_hi) * half0).astype(BF16),
                          ((qr * twc + qm * tws) * half0).astype(BF16)], axis=1)
    pm = jnp.concatenate([(ym_lo - ym_hi).astype(BF16), (qm * twc - qr * tws).astype(BF16)], axis=1)
    y = _dot(cm_ref[...], pr) + _dot(sm_ref[...], pm)
    alt = jnp.where((row & 1) == 0, 1.0, -1.0)
    scale = 1.0 / (2 * m)
    return (ge * ((y[:, :width] + alt * yr_m) * scale + skip * ze),
            go * ((y[:, width:] + alt * ym_m) * scale + skip * zo))


def _hy_conv(u, col0, ch, conv_w, conv_b, out_prev, out_rows, row_block, length, layer, ha, hb, hm, skip,
             tables):
    cm, sm, alt8, twc, tws = tables
    nb = u.shape[0]
    s = out_rows
    m = length // 2
    nct = ch // LANE_TILE
    assert col0 % LANE_TILE == 0
    cb = col0 // LANE_TILE
    taps = conv_w.shape[0]
    assert taps == HY_SHORT == 3
    single = dict(pipeline_mode=pl.Buffered(1))
    const = lambda shape: pl.BlockSpec(shape, lambda c, b: (0,) * len(shape), **single)
    hspec = lambda rows: pl.BlockSpec((HY_ORDER, rows, LANE_TILE), lambda c, b: (0, 0, c), **single)
    ucol = lambda k: pl.BlockSpec((None, length, LANE_TILE), lambda c, b: (b, row_block, cb + k * nct + c))
    wcol = lambda rows, k: pl.BlockSpec((rows, LANE_TILE), lambda c, b: (0, k * nct + c))
    orders = range(1, HY_ORDER + 1)
    in_specs = ([ucol(0)] + [ucol(k) for k in orders] + [wcol(taps, 0), wcol(1, 0)]
                + [wcol(taps, k) for k in orders] + [wcol(1, k) for k in orders]
                + [hspec(length), hspec(length), hspec(8),
                   pl.BlockSpec((None, HY_ORDER, 1, LANE_TILE), lambda c, b: (layer, 0, 0, c)),
                   const((m, m)), const((m, m)), const((8, m)), const((m, LANE_TILE)), const((m, LANE_TILE))])
    args = ([u] * (1 + HY_ORDER) + [conv_w, conv_b] + [conv_w] * HY_ORDER + [conv_b] * HY_ORDER
            + [ha, hb, hm, skip, cm, sm, alt8, twc, tws])
    n_in = len(args)
    aliases = {}
    if out_prev is not None:
        in_specs.append(pl.BlockSpec(memory_space=pl.ANY))
        args.append(out_prev)
        aliases = {len(args) - 1: 0}

    def body(*refs):
        k = HY_ORDER
        z_ref, g_refs = refs[0], refs[1:1 + k]
        wz_ref, bz_ref = refs[1 + k], refs[2 + k]
        wg_refs, bg_refs = refs[3 + k:3 + 2 * k], refs[3 + 2 * k:3 + 3 * k]
        _hy_conv_kernel(z_ref, g_refs, wz_ref, bz_ref, wg_refs, bg_refs, *refs[3 + 3 * k:n_in], *refs[-2:])

    return pl.pallas_call(
        body,
        out_shape=jax.ShapeDtypeStruct((nb, s, ch), BF16),
        grid=(nct, nb),
        in_specs=in_specs,
        out_specs=pl.BlockSpec((None, length, LANE_TILE), lambda c, b: (b, row_block, c)),
        scratch_shapes=[pltpu.VMEM((LANE_TILE // 128, length, 128), F32)],
        input_output_aliases=aliases,
        compiler_params=_params(("parallel", "parallel")),
        name="hyena_conv_%d" % length,
    )(*args)


def _rms(x, w):
    return x * lax.rsqrt(jnp.mean(x * x, axis=-1, keepdims=True) + EPS) * w


def _merge_mlp_kernel(yf_ref, yb_ref, xs_ref, zg_ref, dsk_ref, sn_ref, ym_ref, yh_ref, wb_ref, wo_ref,
                      x_ref, xc_ref, mod_ref, nw_ref, w1_ref, w2_ref, nf_ref, o_ref, *, final, n_lat_tiles):
    m = mod_ref[...]
    bw = yf_ref.shape[1]
    ys = yf_ref[...].astype(F32) + yb_ref[...].astype(F32) + dsk_ref[...] * xs_ref[...].astype(F32)
    ys = _rms(ys * _silu(zg_ref[:, 0:bw].astype(F32)), sn_ref[...]).astype(BF16)
    acc = None
    for n, y in enumerate((ys, ym_ref[...].astype(BF16), yh_ref[...])):
        gate = zg_ref[:, (1 + n) * bw:(2 + n) * bw].astype(F32)
        term = jax.nn.sigmoid(gate) * _dot(y, wb_ref[n])
        acc = term if acc is None else acc + term
    x_in = x_ref[...] if xc_ref is None else jnp.where(pl.program_id(1) < n_lat_tiles, x_ref[...], xc_ref[...])
    x = x_in + m[2:3] * _dot(acc.astype(BF16), wo_ref[...])
    h = (_rms(x, nw_ref[...]) * (1.0 + m[4:5]) + m[3:4]).astype(BF16)
    a = jnp.maximum(_dot(h, w1_ref[...]), 0.0)
    x = x + m[5:6] * _dot((a * a).astype(BF16), w2_ref[...])
    o_ref[...] = _rms(x, nf_ref[...]) if final else x


def _merge_mlp(yf, yb, xbc, pnat, d_full, ssd_norm, ym, yh, wb, wo, x, x_ctx, mod, norm_w, layer, w1, w2,
               norm_f, final, n_lat_tiles, n_tiles):
    nb, _, d = x.shape
    s = yf.shape[1]
    bw = yf.shape[-1]
    hidden = w1.shape[1]
    row = pl.BlockSpec((None, TOKEN_TILE, bw), lambda b, i: (b, i, 0))
    assert bw == d
    zg = pl.BlockSpec((None, TOKEN_TILE, (1 + N_BRANCH) * d), lambda b, i: (b, i, 0))
    xrow = pl.BlockSpec((None, TOKEN_TILE, d), lambda b, i: (b, i, 0))
    lay = lambda n: pl.BlockSpec((None, 1, n), lambda b, i: (layer, 0, 0))
    single = dict(pipeline_mode=pl.Buffered(1))
    const = lambda shape: pl.BlockSpec(shape, lambda b, i: (0,) * len(shape), **single)
    in_specs = [row, row, row, zg, lay(bw), lay(bw), row, row, const((N_BRANCH, bw, d)), const((d, d))]
    args = [yf, yb, xbc, pnat, d_full, ssd_norm, ym, yh, wb, wo]
    aliases = {}
    if x_ctx is None:
        in_specs.append(xrow)
        args.append(x)
        if not final:
            aliases = {len(args) - 1: 0}
    else:
        in_specs += [pl.BlockSpec((None, TOKEN_TILE, d), lambda b, i: (b, jnp.minimum(i, n_lat_tiles - 1), 0)),
                     pl.BlockSpec((None, TOKEN_TILE, d), lambda b, i: (b, jnp.maximum(i - n_lat_tiles, 0), 0))]
        args += [x, x_ctx]
    in_specs += [pl.BlockSpec((None, 6, d), _mod_row(nb, n_lat_tiles)), lay(d),
                 const((d, hidden)), const((hidden, d)), pl.BlockSpec((1, d), lambda b, i: (0, 0))]
    args += [mod, norm_w, w1, w2, norm_f]

    def body(*refs):
        head, tail = refs[:11], refs[11:]
        xc = None if x_ctx is None else tail[0]
        _merge_mlp_kernel(*head, xc, *tail[0 if x_ctx is None else 1:], final=final, n_lat_tiles=n_lat_tiles)

    return pl.pallas_call(
        body,
        out_shape=jax.ShapeDtypeStruct((nb, n_tiles * TOKEN_TILE if final else s, d), F32),
        grid=(nb, n_tiles),
        in_specs=in_specs,
        out_specs=xrow,
        input_output_aliases=aliases,
        compiler_params=_params(("parallel", "parallel")),
        name="merge_mlp",
    )(*args)


def _dir_rows(g, nb, s, per_dir):
    g = g[:, :2 * per_dir].reshape(nb, s, 2, per_dir)
    return jnp.transpose(g, (2, 0, 3, 1))


def kernel(x, c, ctx, c_ctx, norm1_w, mod_w, mod_b, w_in, ssd_conv_w, ssd_conv_b, ssd_dt_bias, ssd_a_log,
           ssd_d, ssd_norm_w, ml_conv_w, ml_conv_b, ml_gate_b, ml_norm_w, hy_conv_w, hy_conv_b, hy_ffn_w1,
           hy_ffn_b1, hy_ffn_w2, hy_ffn_b2, hy_ffn_w3, hy_decay, hy_skip, w_branch, w_out, norm2_w,
           mlp_w1, mlp_w2, norm_f_w):
    nb, seq, d = x.shape
    ctx_len = ctx.shape[1]
    depth = w_in.shape[0]
    s = seq + ctx_len
    assert seq % ctx_len == 0 and ctx_len % TOKEN_TILE == 0 and seq % GRID_W == 0
    n_lat_tiles, n_tiles = seq // TOKEN_TILE, s // TOKEN_TILE
    n_lat_ch, n_ch = seq // CHUNK, s // CHUNK
    assert n_lat_ch % SCAN_CHUNKS == 0 and n_ch % SCAN_CHUNKS == 0
    rows_g = seq // GRID_W

    ssd_heads = ssd_d.shape[-1]
    ssd_inner = ssd_norm_w.shape[-1]
    ssd_hd = ssd_inner // ssd_heads
    ssd_conv_ch = ssd_conv_w.shape[-1]
    ssd_ds = (ssd_conv_ch - ssd_inner) // (2 * SSD_GROUPS)
    ml_heads = ml_gate_b.shape[-1]
    ml_inner = ml_heads * ML_HEADDIM
    hy_inner = hy_skip.shape[-1]
    ssd_cols = ssd_conv_ch + ssd_inner + 2 * ssd_heads
    ml_cols = 4 * ml_inner + 4 * ml_heads
    rec_cols = ssd_cols + ml_cols
    hy_cols = (HY_ORDER + 1) * hy_inner

    o_z = ssd_conv_ch
    o_dt = ssd_conv_ch + ssd_inner
    o_ml = ssd_cols
    o_mlg = ssd_cols + 4 * ml_inner
    o_hy = rec_cols
    o_g = rec_cols + hy_cols

    xa, xa_ctx = x, ctx
    rpad = (-(nb + 1)) % 8
    c_all = jnp.concatenate([c, c_ctx[None], jnp.zeros((rpad, d), F32)], axis=0)

    tab_lat = _dft_tables(seq)
    tab_ctx = _dft_tables(ctx_len)
    k_scale = jnp.concatenate([jnp.ones((1, ml_inner), F32),
                               jnp.full((1, ml_inner), ML_HEADDIM ** -0.5, F32)], axis=1)

    norm1 = norm1_w[:, None, :]
    norm2 = norm2_w[:, None, :]
    ssd_norm = ssd_norm_w[:, None, :]
    ml_norm = ml_norm_w[:, None, :]
    d_full = jnp.repeat(ssd_d, ssd_hd, axis=-1)[:, None, :]
    w3 = hy_ffn_w3.reshape(depth, hy_ffn_w3.shape[1], HY_ORDER, 2, hy_inner).transpose(0, 2, 3, 1, 4)
    decay = hy_decay[:, :, :, None, :]
    skip = hy_skip[:, :, None, :]
    hb1 = hy_ffn_b1[:, None, :]
    hb2 = hy_ffn_b2[:, None, :]

    for l in range(depth):
        need_ctx = l < depth - 1
        used_tiles = n_tiles if need_ctx else n_lat_tiles
        mod = _mod_vectors(c_all, mod_w, mod_b[:, None, :], l).reshape(-1, 6, d)

        wl = w_in[l].astype(BF16)
        w_nat = jnp.concatenate([wl[:, o_z:o_dt], wl[:, o_g:], wl[:, o_hy:o_g]], axis=1)
        w_xbc = wl[:, :o_z]
        w_ml = wl[:, o_ml:o_mlg]
        gpad = lambda w: jnp.pad(w, ((0, 0), (0, 128 - w.shape[1])))
        hn, hn_cm = _normmod(xa, xa_ctx, norm1, l, mod, 0, rows_g, n_lat_tiles, n_tiles)
        pnat = _matmul(hn.reshape(nb * s, d), w_nat, BF16, "proj_nat").reshape(nb, s, -1)
        c_g = ssd_inner
        c_hy = c_g + N_BRANCH * d

        xbc, p_dt = _proj_conv(hn, w_xbc, gpad(wl[:, o_dt:o_ml]), ssd_conv_w[l], ssd_conv_b[l][None], None,
                               n_lat_tiles, n_tiles, "proj_xbc")
        dt_r = _dir_rows(p_dt.reshape(nb * s, -1), nb, s, ssd_heads)
        y_f, y_b = _ssd_scan(xbc, dt_r, ssd_dt_bias[l][:, :, None], ssd_a_log[l][:, :, None],
                             ssd_heads, ssd_hd, ssd_ds, n_lat_ch, n_ch)

        pml, p_mlg = _proj_conv(hn_cm, w_ml, gpad(wl[:, o_mlg:o_hy]), ml_conv_w[l], ml_conv_b[l][None],
                                k_scale, n_lat_tiles, n_tiles, "proj_ml")
        g_r = _dir_rows(p_mlg.reshape(nb * s, -1), nb, s, 2 * ml_heads)
        gate_b = ml_gate_b[l]
        c_f, c_b = _ml_scan(pml, pml, g_r[:, :, :ml_heads], g_r[:, :, ml_heads:],
                            gate_b[:, 0, :, None], gate_b[:, 1, :, None], ml_heads, n_lat_ch, n_ch)
        ym = _ml_out(c_f, c_b, pml, ml_norm, l, ml_heads, rows_g, n_lat_tiles, used_tiles)

        hcw, hcb = hy_conv_w[l], hy_conv_b[l][None]
        fl = _hy_filters(seq, l, hy_ffn_w1, hb1, hy_ffn_w2, hb2, w3, decay, tab_lat)
        rows = s if need_ctx else seq
        yh = _hy_conv(pnat, c_hy, hy_inner, hcw, hcb, None, rows, 0, seq, l, *fl, skip, tab_lat)
        if need_ctx:
            fc = _hy_filters(ctx_len, l, hy_ffn_w1, hb1, hy_ffn_w2, hb2, w3, decay, tab_ctx)
            yh = _hy_conv(pnat, c_hy, hy_inner, hcw, hcb, yh, rows, seq // ctx_len, ctx_len, l, *fc, skip,
                          tab_ctx)

        xa = _merge_mlp(y_f, y_b, xbc, pnat, d_full, ssd_norm, ym, yh, w_branch[l].astype(BF16),
                        w_out[l].astype(BF16), xa, xa_ctx, mod, norm2, l, mlp_w1[l].astype(BF16),
                        mlp_w2[l].astype(BF16), norm_f_w[None], not need_ctx, n_lat_tiles, used_tiles)
        xa_ctx = None

    return xa
```

```python
import functools
import math

import jax
import jax.numpy as jnp
import numpy as np
from jax import lax
from jax.experimental import pallas as pl
from jax.experimental.pallas import tpu as pltpu

F32 = jnp.float32
BF16 = jnp.bfloat16
HIGHEST = lax.Precision.HIGHEST

GRID_W = 64
CHUNK = 128
EPS = 1e-6
SSD_GROUPS = 2
SSD_CONV = 5
ML_HEADDIM = 128
ML_CONV = 5
HY_ORDER = 2
HY_SHORT = 3
HY_BANDS = 16
N_BRANCH = 3

TOKEN_TILE = 256
LANE_TILE = 256
HALO = 16
SCAN_CHUNKS = 2
VMEM_LIMIT = 56 * 1024 * 1024

_hdot = functools.partial(jnp.dot, precision=HIGHEST, preferred_element_type=F32)
_dot = functools.partial(jnp.dot, preferred_element_type=F32)


def _masked_sums(x, sel):
    r = x.shape[0]
    hi = x.astype(BF16).astype(F32)
    mid = (x - hi).astype(BF16).astype(F32)
    lo = x - hi - mid
    parts = _dot(jnp.concatenate([hi, mid, lo], axis=0).astype(BF16), sel.astype(BF16))
    return parts[0:r] + parts[r:2 * r] + parts[2 * r:3 * r]


def _params(sem, vmem=None):
    return pltpu.CompilerParams(dimension_semantics=sem, vmem_limit_bytes=vmem or VMEM_LIMIT)


def _softplus(x):
    return jnp.maximum(x, 0.0) + jnp.log(1.0 + jnp.exp(-jnp.abs(x)))


def _silu(x):
    return x * jax.nn.sigmoid(x)


def _mod_kernel(c_ref, w_ref, b_ref, o_ref):
    o_ref[...] = _hdot(_silu(c_ref[...]), w_ref[...]) + b_ref[...]


def _mod_vectors(c_all, mod_w, mod_b, layer):
    r, d = c_all.shape
    n = mod_w.shape[-1]
    tn = n // 6
    return pl.pallas_call(
        _mod_kernel,
        out_shape=jax.ShapeDtypeStruct((r, n), F32),
        grid=(n // tn,),
        in_specs=[pl.BlockSpec((r, d), lambda j: (0, 0)),
                  pl.BlockSpec((None, d, tn), lambda j: (layer, 0, j)),
                  pl.BlockSpec((None, 1, tn), lambda j: (layer, 0, j))],
        out_specs=pl.BlockSpec((r, tn), lambda j: (0, j)),
        compiler_params=_params(("parallel",)),
        name="mod_vectors",
    )(c_all, mod_w, mod_b)


def _grid_perm(rows_g):
    wpt = TOKEN_TILE // rows_g
    src = np.arange(TOKEN_TILE).reshape(rows_g, wpt).T.reshape(-1)
    p = np.zeros((TOKEN_TILE, TOKEN_TILE), np.float32)
    p[np.arange(TOKEN_TILE), src] = 1.0
    return p


def _normmod_kernel(x_ref, xc_ref, x4_ref, perm_ref, nw_ref, mod_ref, o_ref, ocm_ref, *, si, n_lat_tiles):
    i = pl.program_id(1)
    m = mod_ref[...]

    def normed(x):
        h = x * lax.rsqrt(jnp.mean(x * x, axis=-1, keepdims=True) + EPS) * nw_ref[...]
        return (h * (1.0 + m[si + 1:si + 2]) + m[si:si + 1]).astype(BF16)

    h = normed(x_ref[...] if xc_ref is None else jnp.where(i < n_lat_tiles, x_ref[...], xc_ref[...]))
    o_ref[...] = h

    @pl.when(i < n_lat_tiles)
    def _():
        x4 = x4_ref[...]
        ocm_ref[...] = _dot(perm_ref[...], normed(x4.reshape(TOKEN_TILE, x4.shape[-1]))).astype(BF16)

    @pl.when(i >= n_lat_tiles)
    def _():
        ocm_ref[...] = h


def _mod_row(nb, n_lat_tiles):
    return lambda b, i: (jnp.where(i < n_lat_tiles, b, nb), 0, 0)


def _normmod(x, x_ctx, norm_w, layer, mod, si, rows_g, n_lat_tiles, n_tiles):
    nb, rows, d = x.shape
    s = n_tiles * TOKEN_TILE
    wpt = TOKEN_TILE // rows_g
    assert wpt % 8 == 0 and GRID_W % wpt == 0 and rows % GRID_W == 0
    x4 = x.reshape(nb, rows // GRID_W, GRID_W, d)
    tile = pl.BlockSpec((None, TOKEN_TILE, d), lambda b, i: (b, i, 0))
    if x_ctx is None:
        srcs, src_specs = [x], [tile]
    else:
        srcs = [x, x_ctx]
        src_specs = [pl.BlockSpec((None, TOKEN_TILE, d), lambda b, i: (b, jnp.minimum(i, n_lat_tiles - 1), 0)),
                     pl.BlockSpec((None, TOKEN_TILE, d), lambda b, i: (b, jnp.maximum(i - n_lat_tiles, 0), 0))]

    def body(*refs):
        xc = None if x_ctx is None else refs[1]
        _normmod_kernel(refs[0], xc, *refs[len(srcs):], si=si, n_lat_tiles=n_lat_tiles)

    return pl.pallas_call(
        body,
        out_shape=(jax.ShapeDtypeStruct((nb, s, d), BF16),) * 2,
        grid=(nb, n_tiles),
        in_specs=src_specs + [
            pl.BlockSpec((None, rows_g, wpt, d), lambda b, i: (b, 0, jnp.minimum(i, n_lat_tiles - 1), 0)),
            pl.BlockSpec((TOKEN_TILE, TOKEN_TILE), lambda b, i: (0, 0)),
            pl.BlockSpec((None, 1, d), lambda b, i: (layer, 0, 0)),
            pl.BlockSpec((None, 6, d), _mod_row(nb, n_lat_tiles))],
        out_specs=(tile, tile),
        compiler_params=_params(("parallel", "parallel")),
        name="normmod",
    )(*srcs, x4, jnp.asarray(_grid_perm(rows_g), BF16), norm_w, mod)


def _mm_kernel(a_ref, w_ref, o_ref):
    o_ref[...] = _dot(a_ref[...], w_ref[...]).astype(o_ref.dtype)


def _pick(n, cands):
    for c in cands:
        if n % c == 0:
            return c
    return n


def _matmul(a, w, out_dtype, name):
    t, k = a.shape
    n = w.shape[1]
    tm = _pick(t, (2048, 1024, 768, 512, 256))
    tn = _pick(n, (1024, 512, 256, 128))
    return pl.pallas_call(
        _mm_kernel,
        out_shape=jax.ShapeDtypeStruct((t, n), out_dtype),
        grid=(t // tm, n // tn),
        in_specs=[pl.BlockSpec((tm, k), lambda i, j: (i, 0)),
                  pl.BlockSpec((k, tn), lambda i, j: (0, j))],
        out_specs=pl.BlockSpec((tm, tn), lambda i, j: (i, j)),
        compiler_params=_params(("parallel", "parallel")),
        name=name,
    )(a, w)


def _shift_stack(taps):
    pad = taps // 2
    return np.concatenate([np.eye(TOKEN_TILE, k=j - pad, dtype=np.float32) for j in range(taps) if j != pad])


def _conv_tile(u, prev, nxt, sh_ref, w, b, scale):
    tm = u.shape[0]
    taps = w.shape[0]
    pad = taps // 2

    def finish(acc):
        acc = _silu(acc)
        return (acc if scale is None else acc * scale).astype(BF16)

    def edge(window):
        rows = window.shape[0]
        acc = b + window[HALO:2 * HALO] * w[pad:pad + 1]
        for j in range(taps):
            if j != pad:
                acc = acc + pltpu.roll(window, (pad - j) % rows, 0)[HALO:2 * HALO] * w[j:j + 1]
        return finish(acc)

    shifted = _dot(sh_ref[...], u)
    acc = b + u.astype(F32) * w[pad:pad + 1]
    blk = 0
    for j in range(taps):
        if j != pad:
            acc = acc + shifted[blk * tm:(blk + 1) * tm] * w[j:j + 1]
            blk += 1
    head = edge(jnp.concatenate([prev, u[0:2 * HALO].astype(F32)], axis=0))
    tail = edge(jnp.concatenate([u[tm - 2 * HALO:tm].astype(F32), nxt], axis=0))
    return jnp.concatenate([head, finish(acc)[HALO:tm - HALO], tail], axis=0)


def _proj_conv_kernel(a_ref, w_ref, wg_ref, sh_ref, cw_ref, cb_ref, *rest, n_conv, bounds):
    o_ref, og_ref, buf_ref = rest[-3:]
    s_ref = rest[0] if len(rest) == 4 else None
    j = pl.program_id(1)
    acc = _dot(a_ref[...], w_ref[...])

    @pl.when(j == 0)
    def _():
        og_ref[...] = _dot(a_ref[...], wg_ref[...])

    @pl.when(j >= n_conv)
    def _():
        o_ref[...] = acc.astype(o_ref.dtype)

    @pl.when(j < n_conv)
    def _():
        buf_ref[...] = acc.astype(BF16)
        w, b = cw_ref[...], cb_ref[...]
        scale = None if s_ref is None else s_ref[...]
        zeros = jnp.zeros((HALO, buf_ref.shape[1]), F32)
        for blk in range(buf_ref.shape[0] // TOKEN_TILE):
            r0 = blk * TOKEN_TILE
            prev = zeros if blk in bounds else buf_ref[r0 - HALO:r0, :].astype(F32)
            nxt = zeros if blk + 1 in bounds else buf_ref[r0 + TOKEN_TILE:r0 + TOKEN_TILE + HALO, :].astype(F32)
            o_ref[r0:r0 + TOKEN_TILE, :] = _conv_tile(buf_ref[r0:r0 + TOKEN_TILE, :], prev, nxt, sh_ref, w, b, scale)


def _proj_conv(a, w, w_gate, conv_w, conv_b, scale, n_lat_tiles, n_tiles, name):
    nb, s, k = a.shape
    n = w.shape[1]
    taps, nc = conv_w.shape
    ng = w_gate.shape[1]
    tn = _pick(math.gcd(n, nc), (512, 256))
    n_conv = nc // tn
    cidx = lambda b, j: (0, jnp.minimum(j, n_conv - 1))
    in_specs = [pl.BlockSpec((None, s, k), lambda b, j: (b, 0, 0)),
                pl.BlockSpec((k, tn), lambda b, j: (0, j)),
                pl.BlockSpec((k, ng), lambda b, j: (0, 0)),
                pl.BlockSpec(((taps - 1) * TOKEN_TILE, TOKEN_TILE), lambda b, j: (0, 0)),
                pl.BlockSpec((taps, tn), cidx),
                pl.BlockSpec((1, tn), cidx)]
    args = [a, w, w_gate, jnp.asarray(_shift_stack(taps), BF16), conv_w, conv_b]
    if scale is not None:
        in_specs.append(pl.BlockSpec((1, tn), cidx))
        args.append(scale)
    return pl.pallas_call(
        functools.partial(_proj_conv_kernel, n_conv=n_conv, bounds=(0, n_lat_tiles, n_tiles)),
        out_shape=(jax.ShapeDtypeStruct((nb, s, n), BF16), jax.ShapeDtypeStruct((nb, s, ng), F32)),
        grid=(nb, n // tn),
        in_specs=in_specs,
        out_specs=(pl.BlockSpec((None, s, tn), lambda b, j: (b, 0, j)),
                   pl.BlockSpec((None, s, ng), lambda b, j: (b, 0, 0))),
        scratch_shapes=[pltpu.VMEM((s, tn), BF16)],
        compiler_params=_params(("parallel", "arbitrary")),
        name=name,
    )(*args)


def _chunk_index(n_lat_ch, n_ch):
    return lambda d, j: (j + n_lat_ch) % n_ch if d == 0 else n_ch - 1 - j


def _scan_masks(d, t):
    ii = lax.broadcasted_iota(jnp.int32, (t, t), 0)
    jj = lax.broadcasted_iota(jnp.int32, (t, t), 1)
    mask = jj <= ii if d == 0 else jj >= ii
    tri_t = jnp.where(ii <= jj if d == 0 else ii >= jj, 1.0, 0.0).astype(F32)
    return mask, tri_t


def _gate_rows(la_r, lw_r, tri_t, m_prev):
    cum_r = _hdot(la_r, tri_t)
    last = jnp.sum(la_r, axis=1, keepdims=True)
    g_r = last - cum_r + lw_r
    m_loc = jnp.max(g_r, axis=1, keepdims=True)
    e_r = jnp.exp(g_r - m_loc)
    m_new = jnp.maximum(last + m_prev, m_loc)
    s_old = jnp.exp(last + m_prev - m_new)
    s_new = jnp.exp(m_loc - m_new)
    return cum_r, cum_r - lw_r, e_r, m_new, s_old, s_new


def _head_probs(cum_row, crow_row, m_prev_h, mask, qk):
    t = qk.shape[0]
    colb = jnp.broadcast_to(cum_row, (t, t)).T
    dlog = jnp.where(mask, colb - crow_row, -jnp.inf)
    inter = colb[:, 0:1] + m_prev_h
    m_row = jnp.maximum(inter, jnp.max(dlog, axis=1, keepdims=True))
    p = jnp.exp(dlog - m_row) * qk
    return p.astype(BF16), jnp.exp(inter - m_row), m_row


def _ssd_scan_kernel(*refs, heads, groups, hd, ds):
    j = pl.program_id(1)
    ins, outs, states = refs[:12], refs[12:14], refs[14:16]

    @pl.when(j == 0)
    def _():
        for st_ref in states:
            st_ref[...] = jnp.zeros_like(st_ref)

    for sub in range(SCAN_CHUNKS):
        for d in range(2):
            _ssd_chunk(d, _scan_rows(d, sub), *ins[6 * d:6 * d + 6], outs[d], states[d],
                       heads=heads, groups=groups, hd=hd, ds=ds)


def _scan_rows(d, sub):
    first = (sub if d == 0 else SCAN_CHUNKS - 1 - sub) * CHUNK
    return slice(first, first + CHUNK)


def _ssd_chunk(d, rs, xs_ref, b_ref, c_ref, dt_ref, bias_ref, alog_ref, o_ref, st_ref, *, heads, groups, hd, ds):
    t = CHUNK
    hpg = heads // groups
    pw = 2 * hd
    mask, tri_t = _scan_masks(d, t)
    dt = _softplus(dt_ref[:, rs] + bias_ref[...])
    la_r = -dt * jnp.exp(alog_ref[...])
    cum_r = _masked_sums(la_r, tri_t)
    last = jnp.sum(la_r, axis=1, keepdims=True)
    crow = cum_r - jnp.log(dt)
    e_r = jnp.exp(last - crow)
    e_last = jnp.exp(last)
    lo = lax.broadcasted_iota(jnp.int32, (1, pw), 1) < hd

    for g in range(groups):
        q = c_ref[rs, g * ds:(g + 1) * ds]
        k_t = b_ref[rs, g * ds:(g + 1) * ds].astype(F32).T
        qk = _dot(q, k_t.astype(BF16))
        w0 = g * hpg * hd
        qs = _dot(q, st_ref[:, w0:w0 + hpg * hd].astype(BF16))
        for i in range(hpg // 2):
            h0 = g * hpg + 2 * i
            c0 = h0 * hd
            vp = xs_ref[rs, c0:c0 + pw]
            zero = jnp.zeros_like(vp)
            v_bd = jnp.concatenate([jnp.where(lo, vp, zero), jnp.where(lo, zero, vp)], axis=0)
            probs, carry, kte = [], [], []
            for h in (h0, h0 + 1):
                colb = jnp.broadcast_to(cum_r[h:h + 1], (t, t)).T
                dlog = jnp.where(mask, colb - crow[h:h + 1], -jnp.inf)
                probs.append((jnp.exp(dlog) * qk).astype(BF16))
                carry.append(jnp.exp(colb))
                kte.append((k_t * e_r[h:h + 1]).astype(BF16))
            lhs = jnp.concatenate([jnp.concatenate(probs, axis=1), jnp.concatenate(kte, axis=1)], axis=0)
            res = _dot(lhs, v_bd)
            y = res[:t] + qs[:, 2 * i * hd:2 * i * hd + pw] * jnp.where(lo, carry[0], carry[1])
            o_ref[rs, c0:c0 + pw] = y.astype(o_ref.dtype)
            decay = jnp.where(lo, e_last[h0:h0 + 1], e_last[h0 + 1:h0 + 2])
            st_ref[:, c0:c0 + pw] = decay * st_ref[:, c0:c0 + pw] + res[t:]


def _ssd_scan(xbc, dt_r, bias_c, alog_c, heads, hd, ds, n_lat_ch, n_ch):
    nb, s, _ = xbc.shape
    inner = heads * hd
    gn = SSD_GROUPS * ds
    assert 2 * hd == CHUNK and (heads // SSD_GROUPS) % 2 == 0 and inner % gn == 0
    rows = SCAN_CHUNKS * CHUNK
    n_blk = n_ch // SCAN_CHUNKS
    cidx = _chunk_index(n_lat_ch // SCAN_CHUNKS, n_blk)

    def specs(d):
        return [pl.BlockSpec((None, rows, inner), lambda b, j: (b, cidx(d, j), 0)),
                pl.BlockSpec((None, rows, gn), lambda b, j: (b, cidx(d, j), inner // gn)),
                pl.BlockSpec((None, rows, gn), lambda b, j: (b, cidx(d, j), inner // gn + 1)),
                pl.BlockSpec((None, None, heads, rows), lambda b, j: (d, b, 0, cidx(d, j))),
                pl.BlockSpec((None, heads, 1), lambda b, j: (d, 0, 0)),
                pl.BlockSpec((None, heads, 1), lambda b, j: (d, 0, 0))]

    out = lambda d: pl.BlockSpec((None, rows, inner), lambda b, j: (b, cidx(d, j), 0))
    args = (xbc, xbc, xbc, dt_r, bias_c, alog_c)
    return pl.pallas_call(
        functools.partial(_ssd_scan_kernel, heads=heads, groups=SSD_GROUPS, hd=hd, ds=ds),
        out_shape=(jax.ShapeDtypeStruct((nb, s, inner), BF16),) * 2,
        grid=(nb, n_blk),
        in_specs=specs(0) + specs(1),
        out_specs=(out(0), out(1)),
        scratch_shapes=[pltpu.VMEM((ds, inner), F32)] * 2,
        compiler_params=_params(("parallel", "arbitrary")),
        name="ssd_scan",
    )(*args, *args)


def _ml_scan_kernel(*refs, heads):
    j = pl.program_id(1)
    ins, outs, states = refs[:14], refs[14:16], refs[16:20]

    @pl.when(j == 0)
    def _():
        for ref in states:
            ref[...] = jnp.zeros_like(ref)

    for sub in range(SCAN_CHUNKS):
        for d in range(2):
            _ml_chunk(d, _scan_rows(d, sub), *ins[7 * d:7 * d + 7], outs[d], *states[2 * d:2 * d + 2],
                      heads=heads)


def _ml_chunk(d, rs, q_ref, k_ref, v_ref, gi_ref, gf_ref, bi_ref, bf_ref, o_ref, st_ref, m_ref, *, heads):
    t = CHUNK
    dh = ML_HEADDIM
    mask, tri_t = _scan_masks(d, t)
    lw_r = gi_ref[:, rs] + bi_ref[...]
    f = gf_ref[:, rs] + bf_ref[...]
    la_r = jnp.minimum(f, 0.0) - jnp.log(1.0 + jnp.exp(-jnp.abs(f)))
    m_prev = m_ref[:, 0:1]
    cum_r, crow, e_r, m_new, s_old, s_new = _gate_rows(la_r, lw_r, tri_t, m_prev)
    ones = jnp.ones((t, dh), BF16)

    for h in range(heads):
        c0 = h * dh
        q = q_ref[rs, c0:c0 + dh]
        k_t = k_ref[rs, c0:c0 + dh].astype(F32).T
        v_aug = jnp.concatenate([v_ref[rs, c0:c0 + dh], ones], axis=1)
        qk = _dot(q, k_t.astype(BF16))
        s0 = 2 * c0
        qs = _dot(q, st_ref[:, s0:s0 + 2 * dh].astype(BF16))
        p, cf, m_row = _head_probs(cum_r[h:h + 1], crow[h:h + 1], m_prev[h:h + 1], mask, qk)
        res = _dot(jnp.concatenate([p, (k_t * e_r[h:h + 1]).astype(BF16)], axis=0), v_aug)
        y = res[:t] + qs * cf
        cell = y[:, :dh] / jnp.maximum(jnp.abs(y[:, dh:]), jnp.exp(-m_row))
        o_ref[rs, c0:c0 + dh] = cell.astype(o_ref.dtype)
        st_ref[:, s0:s0 + 2 * dh] = s_old[h:h + 1] * st_ref[:, s0:s0 + 2 * dh] + s_new[h:h + 1] * res[t:]
    m_ref[...] = jnp.broadcast_to(m_new, m_ref.shape)


def _ml_scan(qk, pml, gi_r, gf_r, bi_c, bf_c, heads, n_lat_ch, n_ch):
    nb, s, _ = qk.shape
    inner = heads * ML_HEADDIM
    rows = SCAN_CHUNKS * CHUNK
    n_blk = n_ch // SCAN_CHUNKS
    cidx = _chunk_index(n_lat_ch // SCAN_CHUNKS, n_blk)

    def specs(d):
        gspec = pl.BlockSpec((None, None, heads, rows), lambda b, j: (d, b, 0, cidx(d, j)))
        bspec = pl.BlockSpec((None, heads, 1), lambda b, j: (d, 0, 0))
        col = lambda c: pl.BlockSpec((None, rows, inner), lambda b, j: (b, cidx(d, j), c))
        return [col(0), col(1), col(2), gspec, gspec, bspec, bspec]

    out = lambda d: pl.BlockSpec((None, rows, inner), lambda b, j: (b, cidx(d, j), 0))
    args = (qk, qk, pml, gi_r, gf_r, bi_c, bf_c)
    return pl.pallas_call(
        functools.partial(_ml_scan_kernel, heads=heads),
        out_shape=(jax.ShapeDtypeStruct((nb, s, inner), BF16),) * 2,
        grid=(nb, n_blk),
        in_specs=specs(0) + specs(1),
        out_specs=(out(0), out(1)),
        scratch_shapes=[pltpu.VMEM((ML_HEADDIM, 2 * inner), F32), pltpu.VMEM((heads, 128), F32)] * 2,
        compiler_params=_params(("parallel", "arbitrary")),
        name="ml_scan",
    )(*args, *args)


def _ml_out_kernel(cf_ref, cb_ref, o_in_ref, nw_ref, perm_ref, *rest, heads, permute):
    o_ref = rest[-1]
    dh = ML_HEADDIM
    parts = []
    for h in range(heads):
        sl = slice(h * dh, (h + 1) * dh)
        c = cf_ref[:, sl].astype(F32) + cb_ref[:, sl].astype(F32)
        c = c * lax.rsqrt(jnp.mean(c * c, axis=-1, keepdims=True) + EPS) * nw_ref[:, sl]
        parts.append((jax.nn.sigmoid(o_in_ref[:, sl].astype(F32)) * c).astype(BF16))
    y = jnp.concatenate(parts, axis=1)
    y = _dot(perm_ref[...], y) if permute else y.astype(F32)
    o_ref[...] = y.reshape(o_ref.shape)


def _ml_out(cf, cb, pml, norm_w, layer, heads, rows_g, n_lat_tiles, n_tiles):
    nb, _, inner = cf.shape
    wpt = TOKEN_TILE // rows_g
    rpt = TOKEN_TILE // GRID_W
    perm_t = jnp.asarray(_grid_perm(rows_g).T, BF16)
    out_shape = jax.ShapeDtypeStruct((nb, n_tiles * rpt, GRID_W, inner), F32)

    def call(tile0, tiles, out_spec, permute, prev):
        row = pl.BlockSpec((None, TOKEN_TILE, inner), lambda b, i: (b, tile0 + i, 0))
        in_specs = [row, row,
                    pl.BlockSpec((None, TOKEN_TILE, inner), lambda b, i: (b, tile0 + i, 3)),
                    pl.BlockSpec((None, 1, inner), lambda b, i: (layer, 0, 0)),
                    pl.BlockSpec((TOKEN_TILE, TOKEN_TILE), lambda b, i: (0, 0))]
        args = [cf, cb, pml, norm_w, perm_t]
        if prev is not None:
            in_specs.append(pl.BlockSpec(memory_space=pl.ANY))
            args.append(prev)
        return pl.pallas_call(
            functools.partial(_ml_out_kernel, heads=heads, permute=permute),
            out_shape=out_shape,
            grid=(nb, tiles),
            in_specs=in_specs,
            out_specs=out_spec,
            input_output_aliases={} if prev is None else {len(args) - 1: 0},
            compiler_params=_params(("parallel", "parallel")),
            name="ml_out_lat" if permute else "ml_out_ctx",
        )(*args)

    y = call(0, n_lat_tiles, pl.BlockSpec((None, rows_g, wpt, inner), lambda b, i: (b, 0, i, 0)), True, None)
    if n_tiles > n_lat_tiles:
        y = call(n_lat_tiles, n_tiles - n_lat_tiles,
                 pl.BlockSpec((None, rpt, GRID_W, inner), lambda b, i: (b, n_lat_tiles + i, 0, 0)), False, y)
    return y.reshape(nb, n_tiles * TOKEN_TILE, inner)


def _dft_tables(length):
    m = length // 2
    k = np.arange(m, dtype=np.int64)
    ang = ((k[:, None] * k[None, :]) % (2 * m)).astype(np.float64) * (np.pi / m)
    alt8 = np.broadcast_to(np.where(k % 2 == 0, 1.0, -1.0)[None, :], (8, m))
    tw = k.astype(np.float64)[:, None] * (np.pi / length) * np.ones((1, LANE_TILE))
    return (jnp.asarray(np.cos(ang), BF16), jnp.asarray(np.sin(ang), BF16), jnp.asarray(alt8, BF16),
            jnp.asarray(np.cos(tw), F32), jnp.asarray(np.sin(tw), F32))


def _hy_feats(length):
    t = jnp.arange(length, dtype=F32)
    t_norm = t / length
    bands = jnp.linspace(1e-4, HY_BANDS - 1, HY_BANDS, dtype=F32)
    ang = (2.0 * math.pi / length) * t[:, None] * bands[None, :]
    feats = jnp.concatenate([t_norm[:, None], jnp.cos(ang), -jnp.sin(ang)], axis=-1)
    return feats[0::2], feats[1::2], t_norm[0::2, None], t_norm[1::2, None]


def _split_spectrum(ae, be, ao, bo, twc, tws):
    tr = twc * ao - tws * bo
    tm = twc * bo + tws * ao
    return (ae + tr, ae - tr), (be + tm, tm - be)


def _hy_filter_kernel(fe_ref, fo_ref, tne_ref, tno_ref, w1_ref, b1_ref, w2_ref, b2_ref, w3f_ref, w3b_ref,
                      df_ref, db_ref, cm_ref, sm_ref, alt_ref, twc_ref, tws_ref, ha_ref, hb_ref, hm_ref,
                      hide_ref, hido_ref):
    m = cm_ref.shape[0]

    @pl.when((pl.program_id(0) == 0) & (pl.program_id(1) == 0))
    def _():
        for f_ref, h_ref in ((fe_ref, hide_ref), (fo_ref, hido_ref)):
            hid = jnp.sin(_hdot(f_ref[...], w1_ref[...]) + b1_ref[...])
            h_ref[...] = jnp.sin(_hdot(hid, w2_ref[...]) + b2_ref[...])

    def taps(hid, tn):
        hid = hid.astype(BF16)
        h_f = _dot(hid, w3f_ref[...].astype(BF16)) * jnp.exp(-tn * jnp.abs(df_ref[...]))
        h_b = _dot(hid, w3b_ref[...].astype(BF16)) * jnp.exp(-tn * jnp.abs(db_ref[...]))
        return (h_f + h_b).astype(BF16), (h_f - h_b).astype(BF16)

    sum_e, dif_e = taps(hide_ref[...], tne_ref[...])
    sum_o, dif_o = taps(hido_ref[...], tno_ref[...])
    twc, tws = twc_ref[...], tws_ref[...]
    cm, sm = cm_ref[...], sm_ref[...]
    (ha_lo, ha_hi), _ = _split_spectrum(_dot(cm, sum_e), 0.0, _dot(cm, sum_o), _dot(sm, sum_o), twc, tws)
    _, (hb_lo, hb_hi) = _split_spectrum(0.0, _dot(sm, dif_e), _dot(cm, dif_o), _dot(sm, dif_o), twc, tws)
    ha_ref[0:m, :] = ha_lo
    ha_ref[m:2 * m, :] = ha_hi
    hb_ref[0:m, :] = hb_lo
    hb_ref[m:2 * m, :] = hb_hi
    row = lax.broadcasted_iota(jnp.int32, hm_ref.shape, 0)
    hm_ref[...] = jnp.where(row == 0, _dot(alt_ref[...], sum_e), _dot(alt_ref[...], dif_o))


def _hy_filters(length, layer, w1, b1, w2, b2, w3, decay, tables):
    cm, sm, alt8, twc, tws = tables
    m = length // 2
    feats = _hy_feats(length)
    nfeat, nf = w1.shape[-2:]
    ch = decay.shape[-1]
    nct = ch // LANE_TILE
    const = lambda shape: pl.BlockSpec(shape, lambda n, c: (0,) * len(shape))
    w3spec = lambda dr: pl.BlockSpec((None, None, None, nf, LANE_TILE), lambda n, c: (layer, n, dr, 0, c))
    dspec = lambda dr: pl.BlockSpec((None, None, None, 1, LANE_TILE), lambda n, c: (layer, n, dr, 0, c))
    lay = lambda a, b: pl.BlockSpec((None, a, b), lambda n, c: (layer, 0, 0))
    out = lambda rows: pl.BlockSpec((None, rows, LANE_TILE), lambda n, c: (n, 0, c))
    return pl.pallas_call(
        _hy_filter_kernel,
        out_shape=(jax.ShapeDtypeStruct((HY_ORDER, length, ch), F32),
                   jax.ShapeDtypeStruct((HY_ORDER, length, ch), F32),
                   jax.ShapeDtypeStruct((HY_ORDER, 8, ch), F32)),
        grid=(HY_ORDER, nct),
        in_specs=[const((m, nfeat)), const((m, nfeat)), const((m, 1)), const((m, 1)),
                  lay(nfeat, nf), lay(1, nf), lay(nf, nf), lay(1, nf),
                  w3spec(0), w3spec(1), dspec(0), dspec(1),
                  const((m, m)), const((m, m)), const((8, m)),
                  const((m, LANE_TILE)), const((m, LANE_TILE))],
        out_specs=(out(length), out(length), out(8)),
        scratch_shapes=[pltpu.VMEM((m, nf), F32)] * 2,
        compiler_params=_params(("arbitrary", "arbitrary")),
        name="hyena_filters_%d" % length,
    )(*feats, w1, b1, w2, b2, w3, w3, decay, decay, cm, sm, alt8, twc, tws)


def _short_conv(ue, uo, w, b):
    m = ue.shape[0]
    row = lax.broadcasted_iota(jnp.int32, ue.shape, 0)
    prev_odd = jnp.where(row == 0, 0.0, pltpu.roll(uo, 1, 0))
    next_even = jnp.where(row == m - 1, 0.0, pltpu.roll(ue, m - 1, 0))
    w0, w1, w2 = w[0:1], w[1:2], w[2:3]
    return b + w0 * prev_odd + w1 * ue + w2 * uo, b + w0 * ue + w1 * uo + w2 * next_even


def _hy_conv_kernel(z_ref, g_refs, wz_ref, bz_ref, wg_refs, bg_refs, ha_ref, hb_ref, hm_ref, skip_ref,
                    cm_ref, sm_ref, alt_ref, twc_ref, tws_ref, o_ref, tmp_ref):
    width = z_ref.shape[1]
    m = cm_ref.shape[0]
    lanes = tmp_ref.shape[-1]
    slabs = range(width // lanes)

    def split(ref):
        for h in slabs:
            tmp_ref[h] = ref[:, h * lanes:(h + 1) * lanes].astype(F32)
        return [jnp.concatenate([tmp_ref[h, pl.ds(first, m, stride=2), :] for h in slabs], axis=1)
                for first in (0, 1)]

    ze, zo = _short_conv(*split(z_ref), wz_ref[...], bz_ref[...])
    for n in range(HY_ORDER):
        ge, go = _short_conv(*split(g_refs[n]), wg_refs[n][...], bg_refs[n][...])
        ze, zo = _hy_order(ze, zo, ge, go, ha_ref[n], hb_ref[n], hm_ref[n], skip_ref[n],
                           cm_ref, sm_ref, alt_ref, twc_ref[...], tws_ref[...])
    for h in slabs:
        tmp_ref[h, pl.ds(0, m, stride=2), :] = ze[:, h * lanes:(h + 1) * lanes]
        tmp_ref[h, pl.ds(1, m, stride=2), :] = zo[:, h * lanes:(h + 1) * lanes]
        o_ref[:, h * lanes:(h + 1) * lanes] = tmp_ref[h].astype(o_ref.dtype)


def _hy_order(ze, zo, ge, go, ha, hb, hm, skip, cm_ref, sm_ref, alt_ref, twc, tws):
    m, width = ze.shape
    zz = jnp.concatenate([ze.astype(BF16), zo.astype(BF16)], axis=1)
    a = _dot(cm_ref[...], zz)
    b = _dot(sm_ref[...], zz)
    mid = _dot(alt_ref[...], zz)[0:1]
    (a_lo, a_hi), (b_lo, b_hi) = _split_spectrum(a[:, :width], b[:, :width], a[:, width:], b[:, width:], twc, tws)

    ha_lo, ha_hi, hb_lo, hb_hi = ha[0:m, :], ha[m:2 * m, :], hb[0:m, :], hb[m:2 * m, :]
    yr_lo, ym_lo = a_lo * ha_lo - b_lo * hb_lo, a_lo * hb_lo + b_lo * ha_lo
    yr_hi, ym_hi = a_hi * ha_hi - b_hi * hb_hi, a_hi * hb_hi + b_hi * ha_hi
    ha_m, hb_m = hm[0:1, :], hm[1:2, :]
    yr_m = mid[:, :width] * ha_m - mid[:, width:] * hb_m
    ym_m = mid[:, :width] * hb_m + mid[:, width:] * ha_m

    row = lax.broadcasted_iota(jnp.int32, (m, width), 0)
    half0 = jnp.where(row == 0, 0.5, 1.0)
    qr, qm = yr_lo - yr_hi, ym_lo + ym_hi
    pr = jnp.concatenate([((yr_lo + yr_hi) * half0).astype(BF16),
                          ((qr * twc + qm * tws) * half0).astype(BF16)], axis=1)
    pm = jnp.concatenate([(ym_lo - ym_hi).astype(BF16), (qm * twc - qr * tws).astype(BF16)], axis=1)
    y = _dot(cm_ref[...], pr) + _dot(sm_ref[...], pm)
    alt = jnp.where((row & 1) == 0, 1.0, -1.0)
    scale = 1.0 / (2 * m)
    return (ge * ((y[:, :width] + alt * yr_m) * scale + skip * ze),
            go * ((y[:, width:] + alt * ym_m) * scale + skip * zo))


def _hy_conv(u, col0, ch, conv_w, conv_b, out_prev, out_rows, row_block, length, layer, ha, hb, hm, skip,
             tables):
    cm, sm, alt8, twc, tws = tables
    nb = u.shape[0]
    s = out_rows
    m = length // 2
    nct = ch // LANE_TILE
    assert col0 % LANE_TILE == 0
    cb = col0 // LANE_TILE
    taps = conv_w.shape[0]
    assert taps == HY_SHORT == 3
    single = dict(pipeline_mode=pl.Buffered(1))
    const = lambda shape: pl.BlockSpec(shape, lambda c, b: (0,) * len(shape), **single)
    hspec = lambda rows: pl.BlockSpec((HY_ORDER, rows, LANE_TILE), lambda c, b: (0, 0, c), **single)
    ucol = lambda k: pl.BlockSpec((None, length, LANE_TILE), lambda c, b: (b, row_block, cb + k * nct + c))
    wcol = lambda rows, k: pl.BlockSpec((rows, LANE_TILE), lambda c, b: (0, k * nct + c))
    orders = range(1, HY_ORDER + 1)
    in_specs = ([ucol(0)] + [ucol(k) for k in orders] + [wcol(taps, 0), wcol(1, 0)]
                + [wcol(taps, k) for k in orders] + [wcol(1, k) for k in orders]
                + [hspec(length), hspec(length), hspec(8),
                   pl.BlockSpec((None, HY_ORDER, 1, LANE_TILE), lambda c, b: (layer, 0, 0, c)),
                   const((m, m)), const((m, m)), const((8, m)), const((m, LANE_TILE)), const((m, LANE_TILE))])
    args = ([u] * (1 + HY_ORDER) + [conv_w, conv_b] + [conv_w] * HY_ORDER + [conv_b] * HY_ORDER
            + [ha, hb, hm, skip, cm, sm, alt8, twc, tws])
    n_in = len(args)
    aliases = {}
    if out_prev is not None:
        in_specs.append(pl.BlockSpec(memory_space=pl.ANY))
        args.append(out_prev)
        aliases = {len(args) - 1: 0}

    def body(*refs):
        k = HY_ORDER
        z_ref, g_refs = refs[0], refs[1:1 + k]
        wz_ref, bz_ref = refs[1 + k], refs[2 + k]
        wg_refs, bg_refs = refs[3 + k:3 + 2 * k], refs[3 + 2 * k:3 + 3 * k]
        _hy_conv_kernel(z_ref, g_refs, wz_ref, bz_ref, wg_refs, bg_refs, *refs[3 + 3 * k:n_in], *refs[-2:])

    return pl.pallas_call(
        body,
        out_shape=jax.ShapeDtypeStruct((nb, s, ch), BF16),
        grid=(nct, nb),
        in_specs=in_specs,
        out_specs=pl.BlockSpec((None, length, LANE_TILE), lambda c, b: (b, row_block, c)),
        scratch_shapes=[pltpu.VMEM((LANE_TILE // 128, length, 128), F32)],
        input_output_aliases=aliases,
        compiler_params=_params(("parallel", "parallel")),
        name="hyena_conv_%d" % length,
    )(*args)


def _rms(x, w):
    return x * lax.rsqrt(jnp.mean(x * x, axis=-1, keepdims=True) + EPS) * w


def _merge_mlp_kernel(yf_ref, yb_ref, xs_ref, zg_ref, dsk_ref, sn_ref, ym_ref, yh_ref, wb_ref, wo_ref,
                      x_ref, xc_ref, mod_ref, nw_ref, w1_ref, w2_ref, nf_ref, o_ref, *, final, n_lat_tiles):
    m = mod_ref[...]
    bw = yf_ref.shape[1]
    ys = yf_ref[...].astype(F32) + yb_ref[...].astype(F32) + dsk_ref[...] * xs_ref[...].astype(F32)
    ys = _rms(ys * _silu(zg_ref[:, 0:bw].astype(F32)), sn_ref[...]).astype(BF16)
    acc = None
    for n, y in enumerate((ys, ym_ref[...].astype(BF16), yh_ref[...])):
        gate = zg_ref[:, (1 + n) * bw:(2 + n) * bw].astype(F32)
        term = jax.nn.sigmoid(gate) * _dot(y, wb_ref[n])
        acc = term if acc is None else acc + term
    x_in = x_ref[...] if xc_ref is None else jnp.where(pl.program_id(1) < n_lat_tiles, x_ref[...], xc_ref[...])
    x = x_in + m[2:3] * _dot(acc.astype(BF16), wo_ref[...])
    h = (_rms(x, nw_ref[...]) * (1.0 + m[4:5]) + m[3:4]).astype(BF16)
    a = jnp.maximum(_dot(h, w1_ref[...]), 0.0)
    x = x + m[5:6] * _dot((a * a).astype(BF16), w2_ref[...])
    o_ref[...] = _rms(x, nf_ref[...]) if final else x


def _merge_mlp(yf, yb, xbc, pnat, d_full, ssd_norm, ym, yh, wb, wo, x, x_ctx, mod, norm_w, layer, w1, w2,
               norm_f, final, n_lat_tiles, n_tiles):
    nb, _, d = x.shape
    s = yf.shape[1]
    bw = yf.shape[-1]
    hidden = w1.shape[1]
    row = pl.BlockSpec((None, TOKEN_TILE, bw), lambda b, i: (b, i, 0))
    assert bw == d
    zg = pl.BlockSpec((None, TOKEN_TILE, (1 + N_BRANCH) * d), lambda b, i: (b, i, 0))
    xrow = pl.BlockSpec((None, TOKEN_TILE, d), lambda b, i: (b, i, 0))
    lay = lambda n: pl.BlockSpec((None, 1, n), lambda b, i: (layer, 0, 0))
    single = dict(pipeline_mode=pl.Buffered(1))
    const = lambda shape: pl.BlockSpec(shape, lambda b, i: (0,) * len(shape), **single)
    in_specs = [row, row, row, zg, lay(bw), lay(bw), row, row, const((N_BRANCH, bw, d)), const((d, d))]
    args = [yf, yb, xbc, pnat, d_full, ssd_norm, ym, yh, wb, wo]
    aliases = {}
    if x_ctx is None:
        in_specs.append(xrow)
        args.append(x)
        if not final:
            aliases = {len(args) - 1: 0}
    else:
        in_specs += [pl.BlockSpec((None, TOKEN_TILE, d), lambda b, i: (b, jnp.minimum(i, n_lat_tiles - 1), 0)),
                     pl.BlockSpec((None, TOKEN_TILE, d), lambda b, i: (b, jnp.maximum(i - n_lat_tiles, 0), 0))]
        args += [x, x_ctx]
    in_specs += [pl.BlockSpec((None, 6, d), _mod_row(nb, n_lat_tiles)), lay(d),
                 const((d, hidden)), const((hidden, d)), pl.BlockSpec((1, d), lambda b, i: (0, 0))]
    args += [mod, norm_w, w1, w2, norm_f]

    def body(*refs):
        head, tail = refs[:11], refs[11:]
        xc = None if x_ctx is None else tail[0]
        _merge_mlp_kernel(*head, xc, *tail[0 if x_ctx is None else 1:], final=final, n_lat_tiles=n_lat_tiles)

    return pl.pallas_call(
        body,
        out_shape=jax.ShapeDtypeStruct((nb, n_tiles * TOKEN_TILE if final else s, d), F32),
        grid=(nb, n_tiles),
        in_specs=in_specs,
        out_specs=xrow,
        input_output_aliases=aliases,
        compiler_params=_params(("parallel", "parallel")),
        name="merge_mlp",
    )(*args)


def _dir_rows(g, nb, s, per_dir):
    g = g[:, :2 * per_dir].reshape(nb, s, 2, per_dir)
    return jnp.transpose(g, (2, 0, 3, 1))


def kernel(x, c, ctx, c_ctx, norm1_w, mod_w, mod_b, w_in, ssd_conv_w, ssd_conv_b, ssd_dt_bias, ssd_a_log,
           ssd_d, ssd_norm_w, ml_conv_w, ml_conv_b, ml_gate_b, ml_norm_w, hy_conv_w, hy_conv_b, hy_ffn_w1,
           hy_ffn_b1, hy_ffn_w2, hy_ffn_b2, hy_ffn_w3, hy_decay, hy_skip, w_branch, w_out, norm2_w,
           mlp_w1, mlp_w2, norm_f_w):
    nb, seq, d = x.shape
    ctx_len = ctx.shape[1]
    depth = w_in.shape[0]
    s = seq + ctx_len
    assert seq % ctx_len == 0 and ctx_len % TOKEN_TILE == 0 and seq % GRID_W == 0
    n_lat_tiles, n_tiles = seq // TOKEN_TILE, s // TOKEN_TILE
    n_lat_ch, n_ch = seq // CHUNK, s // CHUNK
    assert n_lat_ch % SCAN_CHUNKS == 0 and n_ch % SCAN_CHUNKS == 0
    rows_g = seq // GRID_W

    ssd_heads = ssd_d.shape[-1]
    ssd_inner = ssd_norm_w.shape[-1]
    ssd_hd = ssd_inner // ssd_heads
    ssd_conv_ch = ssd_conv_w.shape[-1]
    ssd_ds = (ssd_conv_ch - ssd_inner) // (2 * SSD_GROUPS)
    ml_heads = ml_gate_b.shape[-1]
    ml_inner = ml_heads * ML_HEADDIM
    hy_inner = hy_skip.shape[-1]
    ssd_cols = ssd_conv_ch + ssd_inner + 2 * ssd_heads
    ml_cols = 4 * ml_inner + 4 * ml_heads
    rec_cols = ssd_cols + ml_cols
    hy_cols = (HY_ORDER + 1) * hy_inner

    o_z = ssd_conv_ch
    o_dt = ssd_conv_ch + ssd_inner
    o_ml = ssd_cols
    o_mlg = ssd_cols + 4 * ml_inner
    o_hy = rec_cols
    o_g = rec_cols + hy_cols

    xa, xa_ctx = x, ctx
    rpad = (-(nb + 1)) % 8
    c_all = jnp.concatenate([c, c_ctx[None], jnp.zeros((rpad, d), F32)], axis=0)

    tab_lat = _dft_tables(seq)
    tab_ctx = _dft_tables(ctx_len)
    k_scale = jnp.concatenate([jnp.ones((1, ml_inner), F32),
                               jnp.full((1, ml_inner), ML_HEADDIM ** -0.5, F32)], axis=1)

    norm1 = norm1_w[:, None, :]
    norm2 = norm2_w[:, None, :]
    ssd_norm = ssd_norm_w[:, None, :]
    ml_norm = ml_norm_w[:, None, :]
    d_full = jnp.repeat(ssd_d, ssd_hd, axis=-1)[:, None, :]
    w3 = hy_ffn_w3.reshape(depth, hy_ffn_w3.shape[1], HY_ORDER, 2, hy_inner).transpose(0, 2, 3, 1, 4)
    decay = hy_decay[:, :, :, None, :]
    skip = hy_skip[:, :, None, :]
    hb1 = hy_ffn_b1[:, None, :]
    hb2 = hy_ffn_b2[:, None, :]

    for l in range(depth):
        need_ctx = l < depth - 1
        used_tiles = n_tiles if need_ctx else n_lat_tiles
        mod = _mod_vectors(c_all, mod_w, mod_b[:, None, :], l).reshape(-1, 6, d)

        wl = w_in[l].astype(BF16)
        w_nat = jnp.concatenate([wl[:, o_z:o_dt], wl[:, o_g:], wl[:, o_hy:o_g]], axis=1)
        w_xbc = wl[:, :o_z]
        w_ml = wl[:, o_ml:o_mlg]
        gpad = lambda w: jnp.pad(w, ((0, 0), (0, 128 - w.shape[1])))
        hn, hn_cm = _normmod(xa, xa_ctx, norm1, l, mod, 0, rows_g, n_lat_tiles, n_tiles)
        pnat = _matmul(hn.reshape(nb * s, d), w_nat, BF16, "proj_nat").reshape(nb, s, -1)
        c_g = ssd_inner
        c_hy = c_g + N_BRANCH * d

        xbc, p_dt = _proj_conv(hn, w_xbc, gpad(wl[:, o_dt:o_ml]), ssd_conv_w[l], ssd_conv_b[l][None], None,
                               n_lat_tiles, n_tiles, "proj_xbc")
        dt_r = _dir_rows(p_dt.reshape(nb * s, -1), nb, s, ssd_heads)
        y_f, y_b = _ssd_scan(xbc, dt_r, ssd_dt_bias[l][:, :, None], ssd_a_log[l][:, :, None],
                             ssd_heads, ssd_hd, ssd_ds, n_lat_ch, n_ch)

        pml, p_mlg = _proj_conv(hn_cm, w_ml, gpad(wl[:, o_mlg:o_hy]), ml_conv_w[l], ml_conv_b[l][None],
                                k_scale, n_lat_tiles, n_tiles, "proj_ml")
        g_r = _dir_rows(p_mlg.reshape(nb * s, -1), nb, s, 2 * ml_heads)
        gate_b = ml_gate_b[l]
        c_f, c_b = _ml_scan(pml, pml, g_r[:, :, :ml_heads], g_r[:, :, ml_heads:],
                            gate_b[:, 0, :, None], gate_b[:, 1, :, None], ml_heads, n_lat_ch, n_ch)
        ym = _ml_out(c_f, c_b, pml, ml_norm, l, ml_heads, rows_g, n_lat_tiles, used_tiles)

        hcw, hcb = hy_conv_w[l], hy_conv_b[l][None]
        fl = _hy_filters(seq, l, hy_ffn_w1, hb1, hy_ffn_w2, hb2, w3, decay, tab_lat)
        rows = s if need_ctx else seq
        yh = _hy_conv(pnat, c_hy, hy_inner, hcw, hcb, None, rows, 0, seq, l, *fl, skip, tab_lat)
        if need_ctx:
            fc = _hy_filters(ctx_len, l, hy_ffn_w1, hb1, hy_ffn_w2, hb2, w3, decay, tab_ctx)
            yh = _hy_conv(pnat, c_hy, hy_inner, hcw, hcb, yh, rows, seq // ctx_len, ctx_len, l, *fc, skip,
                          tab_ctx)

        xa = _merge_mlp(y_f, y_b, xbc, pnat, d_full, ssd_norm, ym, yh, w_branch[l].astype(BF16),
                        w_out[l].astype(BF16), xa, xa_ctx, mod, norm2, l, mlp_w1[l].astype(BF16),
                        mlp_w2[l].astype(BF16), norm_f_w[None], not need_ctx, n_lat_tiles, used_tiles)
        xa_ctx = None

    return xa
```

```python
import functools
import math

import jax
import jax.numpy as jnp
import numpy as np
from jax import lax
from jax.experimental import pallas as pl
from jax.experimental.pallas import tpu as pltpu

F32 = jnp.float32
BF16 = jnp.bfloat16
HIGHEST = lax.Precision.HIGHEST

GRID_W = 64
CHUNK = 128
EPS = 1e-6
SSD_GROUPS = 2
SSD_CONV = 5
ML_HEADDIM = 128
ML_CONV = 5
HY_ORDER = 2
HY_SHORT = 3
HY_BANDS = 16
N_BRANCH = 3

TOKEN_TILE = 256
LANE_TILE = 256
HALO = 16
SCAN_CHUNKS = 2
VMEM_LIMIT = 56 * 1024 * 1024

_hdot = functools.partial(jnp.dot, precision=HIGHEST, preferred_element_type=F32)
_dot = functools.partial(jnp.dot, preferred_element_type=F32)


def _masked_sums(x, sel):
    r = x.shape[0]
    hi = x.astype(BF16).astype(F32)
    mid = (x - hi).astype(BF16).astype(F32)
    lo = x - hi - mid
    parts = _dot(jnp.concatenate([hi, mid, lo], axis=0).astype(BF16), sel.astype(BF16))
    return parts[0:r] + parts[r:2 * r] + parts[2 * r:3 * r]


def _params(sem, vmem=None):
    return pltpu.CompilerParams(dimension_semantics=sem, vmem_limit_bytes=vmem or VMEM_LIMIT)


def _softplus(x):
    return jnp.maximum(x, 0.0) + jnp.log(1.0 + jnp.exp(-jnp.abs(x)))


def _silu(x):
    return x * jax.nn.sigmoid(x)


def _mod_kernel(c_ref, w_ref, b_ref, o_ref):
    o_ref[...] = _hdot(_silu(c_ref[...]), w_ref[...]) + b_ref[...]


def _mod_vectors(c_all, mod_w, mod_b, layer):
    r, d = c_all.shape
    n = mod_w.shape[-1]
    tn = n // 6
    return pl.pallas_call(
        _mod_kernel,
        out_shape=jax.ShapeDtypeStruct((r, n), F32),
        grid=(n // tn,),
        in_specs=[pl.BlockSpec((r, d), lambda j: (0, 0)),
                  pl.BlockSpec((None, d, tn), lambda j: (layer, 0, j)),
                  pl.BlockSpec((None, 1, tn), lambda j: (layer, 0, j))],
        out_specs=pl.BlockSpec((r, tn), lambda j: (0, j)),
        compiler_params=_params(("parallel",)),
        name="mod_vectors",
    )(c_all, mod_w, mod_b)


def _grid_perm(rows_g):
    wpt = TOKEN_TILE // rows_g
    src = np.arange(TOKEN_TILE).reshape(rows_g, wpt).T.reshape(-1)
    p = np.zeros((TOKEN_TILE, TOKEN_TILE), np.float32)
    p[np.arange(TOKEN_TILE), src] = 1.0
    return p


def _normmod_kernel(x_ref, xc_ref, x4_ref, perm_ref, nw_ref, mod_ref, o_ref, ocm_ref, *, si, n_lat_tiles):
    i = pl.program_id(1)
    m = mod_ref[...]

    def normed(x):
        h = x * lax.rsqrt(jnp.mean(x * x, axis=-1, keepdims=True) + EPS) * nw_ref[...]
        return (h * (1.0 + m[si + 1:si + 2]) + m[si:si + 1]).astype(BF16)

    h = normed(x_ref[...] if xc_ref is None else jnp.where(i < n_lat_tiles, x_ref[...], xc_ref[...]))
    o_ref[...] = h

    @pl.when(i < n_lat_tiles)
    def _():
        x4 = x4_ref[...]
        ocm_ref[...] = _dot(perm_ref[...], normed(x4.reshape(TOKEN_TILE, x4.shape[-1]))).astype(BF16)

    @pl.when(i >= n_lat_tiles)
    def _():
        ocm_ref[...] = h


def _mod_row(nb, n_lat_tiles):
    return lambda b, i: (jnp.where(i < n_lat_tiles, b, nb), 0, 0)


def _normmod(x, x_ctx, norm_w, layer, mod, si, rows_g, n_lat_tiles, n_tiles):
    nb, rows, d = x.shape
    s = n_tiles * TOKEN_TILE
    wpt = TOKEN_TILE // rows_g
    assert wpt % 8 == 0 and GRID_W % wpt == 0 and rows % GRID_W == 0
    x4 = x.reshape(nb, rows // GRID_W, GRID_W, d)
    tile = pl.BlockSpec((None, TOKEN_TILE, d), lambda b, i: (b, i, 0))
    if x_ctx is None:
        srcs, src_specs = [x], [tile]
    else:
        srcs = [x, x_ctx]
        src_specs = [pl.BlockSpec((None, TOKEN_TILE, d), lambda b, i: (b, jnp.minimum(i, n_lat_tiles - 1), 0)),
                     pl.BlockSpec((None, TOKEN_TILE, d), lambda b, i: (b, jnp.maximum(i - n_lat_tiles, 0), 0))]

    def body(*refs):
        xc = None if x_ctx is None else refs[1]
        _normmod_kernel(refs[0], xc, *refs[len(srcs):], si=si, n_lat_tiles=n_lat_tiles)

    return pl.pallas_call(
        body,
        out_shape=(jax.ShapeDtypeStruct((nb, s, d), BF16),) * 2,
        grid=(nb, n_tiles),
        in_specs=src_specs + [
            pl.BlockSpec((None, rows_g, wpt, d), lambda b, i: (b, 0, jnp.minimum(i, n_lat_tiles - 1), 0)),
            pl.BlockSpec((TOKEN_TILE, TOKEN_TILE), lambda b, i: (0, 0)),
            pl.BlockSpec((None, 1, d), lambda b, i: (layer, 0, 0)),
            pl.BlockSpec((None, 6, d), _mod_row(nb, n_lat_tiles))],
        out_specs=(tile, tile),
        compiler_params=_params(("parallel", "parallel")),
        name="normmod",
    )(*srcs, x4, jnp.asarray(_grid_perm(rows_g), BF16), norm_w, mod)


def _mm_kernel(a_ref, w_ref, o_ref):
    o_ref[...] = _dot(a_ref[...], w_ref[...]).astype(o_ref.dtype)


def _pick(n, cands):
    for c in cands:
        if n % c == 0:
            return c
    return n


def _matmul(a, w, out_dtype, name):
    t, k = a.shape
    n = w.shape[1]
    tm = _pick(t, (2048, 1024, 768, 512, 256))
    tn = _pick(n, (1024, 512, 256, 128))
    return pl.pallas_call(
        _mm_kernel,
        out_shape=jax.ShapeDtypeStruct((t, n), out_dtype),
        grid=(t // tm, n // tn),
        in_specs=[pl.BlockSpec((tm, k), lambda i, j: (i, 0)),
                  pl.BlockSpec((k, tn), lambda i, j: (0, j))],
        out_specs=pl.BlockSpec((tm, tn), lambda i, j: (i, j)),
        compiler_params=_params(("parallel", "parallel")),
        name=name,
    )(a, w)


def _shift_stack(taps):
    pad = taps // 2
    return np.concatenate([np.eye(TOKEN_TILE, k=j - pad, dtype=np.float32) for j in range(taps) if j != pad])


def _conv_tile(u, prev, nxt, sh_ref, w, b, scale):
    tm = u.shape[0]
    taps = w.shape[0]
    pad = taps // 2

    def finish(acc):
        acc = _silu(acc)
        return (acc if scale is None else acc * scale).astype(BF16)

    def edge(window):
        rows = window.shape[0]
        acc = b + window[HALO:2 * HALO] * w[pad:pad + 1]
        for j in range(taps):
            if j != pad:
                acc = acc + pltpu.roll(window, (pad - j) % rows, 0)[HALO:2 * HALO] * w[j:j + 1]
        return finish(acc)

    shifted = _dot(sh_ref[...], u)
    acc = b + u.astype(F32) * w[pad:pad + 1]
    blk = 0
    for j in range(taps):
        if j != pad:
            acc = acc + shifted[blk * tm:(blk + 1) * tm] * w[j:j + 1]
            blk += 1
    head = edge(jnp.concatenate([prev, u[0:2 * HALO].astype(F32)], axis=0))
    tail = edge(jnp.concatenate([u[tm - 2 * HALO:tm].astype(F32), nxt], axis=0))
    return jnp.concatenate([head, finish(acc)[HALO:tm - HALO], tail], axis=0)


def _proj_conv_kernel(a_ref, w_ref, wg_ref, sh_ref, cw_ref, cb_ref, *rest, n_conv, bounds):
    o_ref, og_ref, buf_ref = rest[-3:]
    s_ref = rest[0] if len(rest) == 4 else None
    j = pl.program_id(1)
    acc = _dot(a_ref[...], w_ref[...])

    @pl.when(j == 0)
    def _():
        og_ref[...] = _dot(a_ref[...], wg_ref[...])

    @pl.when(j >= n_conv)
    def _():
        o_ref[...] = acc.astype(o_ref.dtype)

    @pl.when(j < n_conv)
    def _():
        buf_ref[...] = acc.astype(BF16)
        w, b = cw_ref[...], cb_ref[...]
        scale = None if s_ref is None else s_ref[...]
        zeros = jnp.zeros((HALO, buf_ref.shape[1]), F32)
        for blk in range(buf_ref.shape[0] // TOKEN_TILE):
            r0 = blk * TOKEN_TILE
            prev = zeros if blk in bounds else buf_ref[r0 - HALO:r0, :].astype(F32)
            nxt = zeros if blk + 1 in bounds else buf_ref[r0 + TOKEN_TILE:r0 + TOKEN_TILE + HALO, :].astype(F32)
            o_ref[r0:r0 + TOKEN_TILE, :] = _conv_tile(buf_ref[r0:r0 + TOKEN_TILE, :], prev, nxt, sh_ref, w, b, scale)


def _proj_conv(a, w, w_gate, conv_w, conv_b, scale, n_lat_tiles, n_tiles, name):
    nb, s, k = a.shape
    n = w.shape[1]
    taps, nc = conv_w.shape
    ng = w_gate.shape[1]
    tn = _pick(math.gcd(n, nc), (512, 256))
    n_conv = nc // tn
    cidx = lambda b, j: (0, jnp.minimum(j, n_conv - 1))
    in_specs = [pl.BlockSpec((None, s, k), lambda b, j: (b, 0, 0)),
                pl.BlockSpec((k, tn), lambda b, j: (0, j)),
                pl.BlockSpec((k, ng), lambda b, j: (0, 0)),
                pl.BlockSpec(((taps - 1) * TOKEN_TILE, TOKEN_TILE), lambda b, j: (0, 0)),
                pl.BlockSpec((taps, tn), cidx),
                pl.BlockSpec((1, tn), cidx)]
    args = [a, w, w_gate, jnp.asarray(_shift_stack(taps), BF16), conv_w, conv_b]
    if scale is not None:
        in_specs.append(pl.BlockSpec((1, tn), cidx))
        args.append(scale)
    return pl.pallas_call(
        functools.partial(_proj_conv_kernel, n_conv=n_conv, bounds=(0, n_lat_tiles, n_tiles)),
        out_shape=(jax.ShapeDtypeStruct((nb, s, n), BF16), jax.ShapeDtypeStruct((nb, s, ng), F32)),
        grid=(nb, n // tn),
        in_specs=in_specs,
        out_specs=(pl.BlockSpec((None, s, tn), lambda b, j: (b, 0, j)),
                   pl.BlockSpec((None, s, ng), lambda b, j: (b, 0, 0))),
        scratch_shapes=[pltpu.VMEM((s, tn), BF16)],
        compiler_params=_params(("parallel", "arbitrary")),
        name=name,
    )(*args)


def _chunk_index(n_lat_ch, n_ch):
    return lambda d, j: (j + n_lat_ch) % n_ch if d == 0 else n_ch - 1 - j


def _scan_masks(d, t):
    ii = lax.broadcasted_iota(jnp.int32, (t, t), 0)
    jj = lax.broadcasted_iota(jnp.int32, (t, t), 1)
    mask = jj <= ii if d == 0 else jj >= ii
    tri_t = jnp.where(ii <= jj if d == 0 else ii >= jj, 1.0, 0.0).astype(F32)
    return mask, tri_t


def _gate_rows(la_r, lw_r, tri_t, m_prev):
    cum_r = _hdot(la_r, tri_t)
    last = jnp.sum(la_r, axis=1, keepdims=True)
    g_r = last - cum_r + lw_r
    m_loc = jnp.max(g_r, axis=1, keepdims=True)
    e_r = jnp.exp(g_r - m_loc)
    m_new = jnp.maximum(last + m_prev, m_loc)
    s_old = jnp.exp(last + m_prev - m_new)
    s_new = jnp.exp(m_loc - m_new)
    return cum_r, cum_r - lw_r, e_r, m_new, s_old, s_new


def _head_probs(cum_row, crow_row, m_prev_h, mask, qk):
    t = qk.shape[0]
    colb = jnp.broadcast_to(cum_row, (t, t)).T
    dlog = jnp.where(mask, colb - crow_row, -jnp.inf)
    inter = colb[:, 0:1] + m_prev_h
    m_row = jnp.maximum(inter, jnp.max(dlog, axis=1, keepdims=True))
    p = jnp.exp(dlog - m_row) * qk
    return p.astype(BF16), jnp.exp(inter - m_row), m_row


def _ssd_scan_kernel(*refs, heads, groups, hd, ds):
    j = pl.program_id(1)
    ins, outs, states = refs[:12], refs[12:14], refs[14:16]

    @pl.when(j == 0)
    def _():
        for st_ref in states:
            st_ref[...] = jnp.zeros_like(st_ref)

    for sub in range(SCAN_CHUNKS):
        for d in range(2):
            _ssd_chunk(d, _scan_rows(d, sub), *ins[6 * d:6 * d + 6], outs[d], states[d],
                       heads=heads, groups=groups, hd=hd, ds=ds)


def _scan_rows(d, sub):
    first = (sub if d == 0 else SCAN_CHUNKS - 1 - sub) * CHUNK
    return slice(first, first + CHUNK)


def _ssd_chunk(d, rs, xs_ref, b_ref, c_ref, dt_ref, bias_ref, alog_ref, o_ref, st_ref, *, heads, groups, hd, ds):
    t = CHUNK
    hpg = heads // groups
    pw = 2 * hd
    mask, tri_t = _scan_masks(d, t)
    dt = _softplus(dt_ref[:, rs] + bias_ref[...])
    la_r = -dt * jnp.exp(alog_ref[...])
    cum_r = _masked_sums(la_r, tri_t)
    last = jnp.sum(la_r, axis=1, keepdims=True)
    crow = cum_r - jnp.log(dt)
    e_r = jnp.exp(last - crow)
    e_last = jnp.exp(last)
    lo = lax.broadcasted_iota(jnp.int32, (1, pw), 1) < hd

    for g in range(groups):
        q = c_ref[rs, g * ds:(g + 1) * ds]
        k_t = b_ref[rs, g * ds:(g + 1) * ds].astype(F32).T
        qk = _dot(q, k_t.astype(BF16))
        w0 = g * hpg * hd
        qs = _dot(q, st_ref[:, w0:w0 + hpg * hd].astype(BF16))
        for i in range(hpg // 2):
            h0 = g * hpg + 2 * i
            c0 = h0 * hd
            vp = xs_ref[rs, c0:c0 + pw]
            zero = jnp.zeros_like(vp)
            v_bd = jnp.concatenate([jnp.where(lo, vp, zero), jnp.where(lo, zero, vp)], axis=0)
            probs, carry, kte = [], [], []
            for h in (h0, h0 + 1):
                colb = jnp.broadcast_to(cum_r[h:h + 1], (t, t)).T
                dlog = jnp.where(mask, colb - crow[h:h + 1], -jnp.inf)
                probs.append((jnp.exp(dlog) * qk).astype(BF16))
                carry.append(jnp.exp(colb))
                kte.append((k_t * e_r[h:h + 1]).astype(BF16))
            lhs = jnp.concatenate([jnp.concatenate(probs, axis=1), jnp.concatenate(kte, axis=1)], axis=0)
            res = _dot(lhs, v_bd)
            y = res[:t] + qs[:, 2 * i * hd:2 * i * hd + pw] * jnp.where(lo, carry[0], carry[1])
            o_ref[rs, c0:c0 + pw] = y.astype(o_ref.dtype)
            decay = jnp.where(lo, e_last[h0:h0 + 1], e_last[h0 + 1:h0 + 2])
            st_ref[:, c0:c0 + pw] = decay * st_ref[:, c0:c0 + pw] + res[t:]


def _ssd_scan(xbc, dt_r, bias_c, alog_c, heads, hd, ds, n_lat_ch, n_ch):
    nb, s, _ = xbc.shape
    inner = heads * hd
    gn = SSD_GROUPS * ds
    assert 2 * hd == CHUNK and (heads // SSD_GROUPS) % 2 == 0 and inner % gn == 0
    rows = SCAN_CHUNKS * CHUNK
    n_blk = n_ch // SCAN_CHUNKS
    cidx = _chunk_index(n_lat_ch // SCAN_CHUNKS, n_blk)

    def specs(d):
        return [pl.BlockSpec((None, rows, inner), lambda b, j: (b, cidx(d, j), 0)),
                pl.BlockSpec((None, rows, gn), lambda b, j: (b, cidx(d, j), inner // gn)),
                pl.BlockSpec((None, rows, gn), lambda b, j: (b, cidx(d, j), inner // gn + 1)),
                pl.BlockSpec((None, None, heads, rows), lambda b, j: (d, b, 0, cidx(d, j))),
                pl.BlockSpec((None, heads, 1), lambda b, j: (d, 0, 0)),
                pl.BlockSpec((None, heads, 1), lambda b, j: (d, 0, 0))]

    out = lambda d: pl.BlockSpec((None, rows, inner), lambda b, j: (b, cidx(d, j), 0))
    args = (xbc, xbc, xbc, dt_r, bias_c, alog_c)
    return pl.pallas_call(
        functools.partial(_ssd_scan_kernel, heads=heads, groups=SSD_GROUPS, hd=hd, ds=ds),
        out_shape=(jax.ShapeDtypeStruct((nb, s, inner), BF16),) * 2,
        grid=(nb, n_blk),
        in_specs=specs(0) + specs(1),
        out_specs=(out(0), out(1)),
        scratch_shapes=[pltpu.VMEM((ds, inner), F32)] * 2,
        compiler_params=_params(("parallel", "arbitrary")),
        name="ssd_scan",
    )(*args, *args)


def _ml_scan_kernel(*refs, heads):
    j = pl.program_id(1)
    ins, outs, states = refs[:14], refs[14:16], refs[16:20]

    @pl.when(j == 0)
    def _():
        for ref in states:
            ref[...] = jnp.zeros_like(ref)

    for sub in range(SCAN_CHUNKS):
        for d in range(2):
            _ml_chunk(d, _scan_rows(d, sub), *ins[7 * d:7 * d + 7], outs[d], *states[2 * d:2 * d + 2],
                      heads=heads)


def _ml_chunk(d, rs, q_ref, k_ref, v_ref, gi_ref, gf_ref, bi_ref, bf_ref, o_ref, st_ref, m_ref, *, heads):
    t = CHUNK
    dh = ML_HEADDIM
    mask, tri_t = _scan_masks(d, t)
    lw_r = gi_ref[:, rs] + bi_ref[...]
    f = gf_ref[:, rs] + bf_ref[...]
    la_r = jnp.minimum(f, 0.0) - jnp.log(1.0 + jnp.exp(-jnp.abs(f)))
    m_prev = m_ref[:, 0:1]
    cum_r, crow, e_r, m_new, s_old, s_new = _gate_rows(la_r, lw_r, tri_t, m_prev)
    ones = jnp.ones((t, dh), BF16)

    for h in range(heads):
        c0 = h * dh
        q = q_ref[rs, c0:c0 + dh]
        k_t = k_ref[rs, c0:c0 + dh].astype(F32).T
        v_aug = jnp.concatenate([v_ref[rs, c0:c0 + dh], ones], axis=1)
        qk = _dot(q, k_t.astype(BF16))
        s0 = 2 * c0
        qs = _dot(q, st_ref[:, s0:s0 + 2 * dh].astype(BF16))
        p, cf, m_row = _head_probs(cum_r[h:h + 1], crow[h:h + 1], m_prev[h:h + 1], mask, qk)
        res = _dot(jnp.concatenate([p, (k_t * e_r[h:h + 1]).astype(BF16)], axis=0), v_aug)
        y = res[:t] + qs * cf
        cell = y[:, :dh] / jnp.maximum(jnp.abs(y[:, dh:]), jnp.exp(-m_row))
        o_ref[rs, c0:c0 + dh] = cell.astype(o_ref.dtype)
        st_ref[:, s0:s0 + 2 * dh] = s_old[h:h + 1] * st_ref[:, s0:s0 + 2 * dh] + s_new[h:h + 1] * res[t:]
    m_ref[...] = jnp.broadcast_to(m_new, m_ref.shape)


def _ml_scan(qk, pml, gi_r, gf_r, bi_c, bf_c, heads, n_lat_ch, n_ch):
    nb, s, _ = qk.shape
    inner = heads * ML_HEADDIM
    rows = SCAN_CHUNKS * CHUNK
    n_blk = n_ch // SCAN_CHUNKS
    cidx = _chunk_index(n_lat_ch // SCAN_CHUNKS, n_blk)

    def specs(d):
        gspec = pl.BlockSpec((None, None, heads, rows), lambda b, j: (d, b, 0, cidx(d, j)))
        bspec = pl.BlockSpec((None, heads, 1), lambda b, j: (d, 0, 0))
        col = lambda c: pl.BlockSpec((None, rows, inner), lambda b, j: (b, cidx(d, j), c))
        return [col(0), col(1), col(2), gspec, gspec, bspec, bspec]

    out = lambda d: pl.BlockSpec((None, rows, inner), lambda b, j: (b, cidx(d, j), 0))
    args = (qk, qk, pml, gi_r, gf_r, bi_c, bf_c)
    return pl.pallas_call(
        functools.partial(_ml_scan_kernel, heads=heads),
        out_shape=(jax.ShapeDtypeStruct((nb, s, inner), BF16),) * 2,
        grid=(nb, n_blk),
        in_specs=specs(0) + specs(1),
        out_specs=(out(0), out(1)),
        scratch_shapes=[pltpu.VMEM((ML_HEADDIM, 2 * inner), F32), pltpu.VMEM((heads, 128), F32)] * 2,
        compiler_params=_params(("parallel", "arbitrary")),
        name="ml_scan",
    )(*args, *args)


def _ml_out_kernel(cf_ref, cb_ref, o_in_ref, nw_ref, perm_ref, *rest, heads, permute):
    o_ref = rest[-1]
    dh = ML_HEADDIM
    parts = []
    for h in range(heads):
        sl = slice(h * dh, (h + 1) * dh)
        c = cf_ref[:, sl].astype(F32) + cb_ref[:, sl].astype(F32)
        c = c * lax.rsqrt(jnp.mean(c * c, axis=-1, keepdims=True) + EPS) * nw_ref[:, sl]
        parts.append((jax.nn.sigmoid(o_in_ref[:, sl].astype(F32)) * c).astype(BF16))
    y = jnp.concatenate(parts, axis=1)
    y = _dot(perm_ref[...], y) if permute else y.astype(F32)
    o_ref[...] = y.reshape(o_ref.shape)


def _ml_out(cf, cb, pml, norm_w, layer, heads, rows_g, n_lat_tiles, n_tiles):
    nb, _, inner = cf.shape
    wpt = TOKEN_TILE // rows_g
    rpt = TOKEN_TILE // GRID_W
    perm_t = jnp.asarray(_grid_perm(rows_g).T, BF16)
    out_shape = jax.ShapeDtypeStruct((nb, n_tiles * rpt, GRID_W, inner), F32)

    def call(tile0, tiles, out_spec, permute, prev):
        row = pl.BlockSpec((None, TOKEN_TILE, inner), lambda b, i: (b, tile0 + i, 0))
        in_specs = [row, row,
                    pl.BlockSpec((None, TOKEN_TILE, inner), lambda b, i: (b, tile0 + i, 3)),
                    pl.BlockSpec((None, 1, inner), lambda b, i: (layer, 0, 0)),
                    pl.BlockSpec((TOKEN_TILE, TOKEN_TILE), lambda b, i: (0, 0))]
        args = [cf, cb, pml, norm_w, perm_t]
        if prev is not None:
            in_specs.append(pl.BlockSpec(memory_space=pl.ANY))
            args.append(prev)
        return pl.pallas_call(
            functools.partial(_ml_out_kernel, heads=heads, permute=permute),
            out_shape=out_shape,
            grid=(nb, tiles),
            in_specs=in_specs,
            out_specs=out_spec,
            input_output_aliases={} if prev is None else {len(args) - 1: 0},
            compiler_params=_params(("parallel", "parallel")),
            name="ml_out_lat" if permute else "ml_out_ctx",
        )(*args)

    y = call(0, n_lat_tiles, pl.BlockSpec((None, rows_g, wpt, inner), lambda b, i: (b, 0, i, 0)), True, None)
    if n_tiles > n_lat_tiles:
        y = call(n_lat_tiles, n_tiles - n_lat_tiles,
                 pl.BlockSpec((None, rpt, GRID_W, inner), lambda b, i: (b, n_lat_tiles + i, 0, 0)), False, y)
    return y.reshape(nb, n_tiles * TOKEN_TILE, inner)


def _dft_tables(length):
    m = length // 2
    k = np.arange(m, dtype=np.int64)
    ang = ((k[:, None] * k[None, :]) % (2 * m)).astype(np.float64) * (np.pi / m)
    alt8 = np.broadcast_to(np.where(k % 2 == 0, 1.0, -1.0)[None, :], (8, m))
    tw = k.astype(np.float64)[:, None] * (np.pi / length) * np.ones((1, LANE_TILE))
    return (jnp.asarray(np.cos(ang), BF16), jnp.asarray(np.sin(ang), BF16), jnp.asarray(alt8, BF16),
            jnp.asarray(np.cos(tw), F32), jnp.asarray(np.sin(tw), F32))


def _hy_feats(length):
    t = jnp.arange(length, dtype=F32)
    t_norm = t / length
    bands = jnp.linspace(1e-4, HY_BANDS - 1, HY_BANDS, dtype=F32)
    ang = (2.0 * math.pi / length) * t[:, None] * bands[None, :]
    feats = jnp.concatenate([t_norm[:, None], jnp.cos(ang), -jnp.sin(ang)], axis=-1)
    return feats[0::2], feats[1::2], t_norm[0::2, None], t_norm[1::2, None]


def _split_spectrum(ae, be, ao, bo, twc, tws):
    tr = twc * ao - tws * bo
    tm = twc * bo + tws * ao
    return (ae + tr, ae - tr), (be + tm, tm - be)


def _hy_filter_kernel(fe_ref, fo_ref, tne_ref, tno_ref, w1_ref, b1_ref, w2_ref, b2_ref, w3f_ref, w3b_ref,
                      df_ref, db_ref, cm_ref, sm_ref, alt_ref, twc_ref, tws_ref, ha_ref, hb_ref, hm_ref,
                      hide_ref, hido_ref):
    m = cm_ref.shape[0]

    @pl.when((pl.program_id(0) == 0) & (pl.program_id(1) == 0))
    def _():
        for f_ref, h_ref in ((fe_ref, hide_ref), (fo_ref, hido_ref)):
            hid = jnp.sin(_hdot(f_ref[...], w1_ref[...]) + b1_ref[...])
            h_ref[...] = jnp.sin(_hdot(hid, w2_ref[...]) + b2_ref[...])

    def taps(hid, tn):
        hid = hid.astype(BF16)
        h_f = _dot(hid, w3f_ref[...].astype(BF16)) * jnp.exp(-tn * jnp.abs(df_ref[...]))
        h_b = _dot(hid, w3b_ref[...].astype(BF16)) * jnp.exp(-tn * jnp.abs(db_ref[...]))
        return (h_f + h_b).astype(BF16), (h_f - h_b).astype(BF16)

    sum_e, dif_e = taps(hide_ref[...], tne_ref[...])
    sum_o, dif_o = taps(hido_ref[...], tno_ref[...])
    twc, tws = twc_ref[...], tws_ref[...]
    cm, sm = cm_ref[...], sm_ref[...]
    (ha_lo, ha_hi), _ = _split_spectrum(_dot(cm, sum_e), 0.0, _dot(cm, sum_o), _dot(sm, sum_o), twc, tws)
    _, (hb_lo, hb_hi) = _split_spectrum(0.0, _dot(sm, dif_e), _dot(cm, dif_o), _dot(sm, dif_o), twc, tws)
    ha_ref[0:m, :] = ha_lo
    ha_ref[m:2 * m, :] = ha_hi
    hb_ref[0:m, :] = hb_lo
    hb_ref[m:2 * m, :] = hb_hi
    row = lax.broadcasted_iota(jnp.int32, hm_ref.shape, 0)
    hm_ref[...] = jnp.where(row == 0, _dot(alt_ref[...], sum_e), _dot(alt_ref[...], dif_o))


def _hy_filters(length, layer, w1, b1, w2, b2, w3, decay, tables):
    cm, sm, alt8, twc, tws = tables
    m = length // 2
    feats = _hy_feats(length)
    nfeat, nf = w1.shape[-2:]
    ch = decay.shape[-1]
    nct = ch // LANE_TILE
    const = lambda shape: pl.BlockSpec(shape, lambda n, c: (0,) * len(shape))
    w3spec = lambda dr: pl.BlockSpec((None, None, None, nf, LANE_TILE), lambda n, c: (layer, n, dr, 0, c))
    dspec = lambda dr: pl.BlockSpec((None, None, None, 1, LANE_TILE), lambda n, c: (layer, n, dr, 0, c))
    lay = lambda a, b: pl.BlockSpec((None, a, b), lambda n, c: (layer, 0, 0))
    out = lambda rows: pl.BlockSpec((None, rows, LANE_TILE), lambda n, c: (n, 0, c))
    return pl.pallas_call(
        _hy_filter_kernel,
        out_shape=(jax.ShapeDtypeStruct((HY_ORDER, length, ch), F32),
                   jax.ShapeDtypeStruct((HY_ORDER, length, ch), F32),
                   jax.ShapeDtypeStruct((HY_ORDER, 8, ch), F32)),
        grid=(HY_ORDER, nct),
        in_specs=[const((m, nfeat)), const((m, nfeat)), const((m, 1)), const((m, 1)),
                  lay(nfeat, nf), lay(1, nf), lay(nf, nf), lay(1, nf),
                  w3spec(0), w3spec(1), dspec(0), dspec(1),
                  const((m, m)), const((m, m)), const((8, m)),
                  const((m, LANE_TILE)), const((m, LANE_TILE))],
        out_specs=(out(length), out(length), out(8)),
        scratch_shapes=[pltpu.VMEM((m, nf), F32)] * 2,
        compiler_params=_params(("arbitrary", "arbitrary")),
        name="hyena_filters_%d" % length,
    )(*feats, w1, b1, w2, b2, w3, w3, decay, decay, cm, sm, alt8, twc, tws)


def _short_conv(ue, uo, w, b):
    m = ue.shape[0]
    row = lax.broadcasted_iota(jnp.int32, ue.shape, 0)
    prev_odd = jnp.where(row == 0, 0.0, pltpu.roll(uo, 1, 0))
    next_even = jnp.where(row == m - 1, 0.0, pltpu.roll(ue, m - 1, 0))
    w0, w1, w2 = w[0:1], w[1:2], w[2:3]
    return b + w0 * prev_odd + w1 * ue + w2 * uo, b + w0 * ue + w1 * uo + w2 * next_even


def _hy_conv_kernel(z_ref, g_refs, wz_ref, bz_ref, wg_refs, bg_refs, ha_ref, hb_ref, hm_ref, skip_ref,
                    cm_ref, sm_ref, alt_ref, twc_ref, tws_ref, o_ref, tmp_ref):
    width = z_ref.shape[1]
    m = cm_ref.shape[0]
    lanes = tmp_ref.shape[-1]
    slabs = range(width // lanes)

    def split(ref):
        for h in slabs:
            tmp_ref[h] = ref[:, h * lanes:(h + 1) * lanes].astype(F32)
        return [jnp.concatenate([tmp_ref[h, pl.ds(first, m, stride=2), :] for h in slabs], axis=1)
                for first in (0, 1)]

    ze, zo = _short_conv(*split(z_ref), wz_ref[...], bz_ref[...])
    for n in range(HY_ORDER):
        ge, go = _short_conv(*split(g_refs[n]), wg_refs[n][...], bg_refs[n][...])
        ze, zo = _hy_order(ze, zo, ge, go, ha_ref[n], hb_ref[n], hm_ref[n], skip_ref[n],
                           cm_ref, sm_ref, alt_ref, twc_ref[...], tws_ref[...])
    for h in slabs:
        tmp_ref[h, pl.ds(0, m, stride=2), :] = ze[:, h * lanes:(h + 1) * lanes]
        tmp_ref[h, pl.ds(1, m, stride=2), :] = zo[:, h * lanes:(h + 1) * lanes]
        o_ref[:, h * lanes:(h + 1) * lanes] = tmp_ref[h].astype(o_ref.dtype)


def _hy_order(ze, zo, ge, go, ha, hb, hm, skip, cm_ref, sm_ref, alt_ref, twc, tws):
    m, width = ze.shape
    zz = jnp.concatenate([ze.astype(BF16), zo.astype(BF16)], axis=1)
    a = _dot(cm_ref[...], zz)
    b = _dot(sm_ref[...], zz)
    mid = _dot(alt_ref[...], zz)[0:1]
    (a_lo, a_hi), (b_lo, b_hi) = _split_spectrum(a[:, :width], b[:, :width], a[:, width:], b[:, width:], twc, tws)

    ha_lo, ha_hi, hb_lo, hb_hi = ha[0:m, :], ha[m:2 * m, :], hb[0:m, :], hb[m:2 * m, :]
    yr_lo, ym_lo = a_lo * ha_lo - b_lo * hb_lo, a_lo * hb_lo + b_lo * ha_lo
    yr_hi, ym_hi = a_hi * ha_hi - b_hi * hb_hi, a_hi * hb_hi + b_hi * ha_hi
    ha_m, hb_m = hm[0:1, :], hm[1:2, :]
    yr_m = mid[:, :width] * ha_m - mid[:, width:] * hb_m
    ym_m = mid[:, :width] * hb_m + mid[:, width:] * ha_m

    row = lax.broadcasted_iota(jnp.int32, (m, width), 0)
    half0 = jnp.where(row == 0, 0.5, 1.0)
    qr, qm = yr_lo - yr_hi, ym_lo + ym_hi
    pr = jnp.concatenate([((yr_lo + yr_hi) * half0).astype(BF16),
                          ((qr * twc + qm * tws) * half0).astype(BF16)], axis=1)
    pm = jnp.concatenate([(ym_lo - ym_hi).astype(BF16), (qm * twc - qr * tws).astype(BF16)], axis=1)
    y = _dot(cm_ref[...], pr) + _dot(sm_ref[...], pm)
    alt = jnp.where((row & 1) == 0, 1.0, -1.0)
    scale = 1.0 / (2 * m)
    return (ge * ((y[:, :width] + alt * yr_m) * scale + skip * ze),
            go * ((y[:, width:] + alt * ym_m) * scale + skip * zo))


def _hy_conv(u, col0, ch, conv_w, conv_b, out_prev, out_rows, row_block, length, layer, ha, hb, hm, skip,
             tables):
    cm, sm, alt8, twc, tws = tables
    nb = u.shape[0]
    s = out_rows
    m = length // 2
    nct = ch // LANE_TILE
    assert col0 % LANE_TILE == 0
    cb = col0 // LANE_TILE
    taps = conv_w.shape[0]
    assert taps == HY_SHORT == 3
    const = lambda shape: pl.BlockSpec(shape, lambda c, b: (0,) * len(shape))
    hspec = lambda rows: pl.BlockSpec((HY_ORDER, rows, LANE_TILE), lambda c, b: (0, 0, c))
    ucol = lambda k: pl.BlockSpec((None, length, LANE_TILE), lambda c, b: (b, row_block, cb + k * nct + c))
    wcol = lambda rows, k: pl.BlockSpec((rows, LANE_TILE), lambda c, b: (0, k * nct + c))
    orders = range(1, HY_ORDER + 1)
    in_specs = ([ucol(0)] + [ucol(k) for k in orders] + [wcol(taps, 0), wcol(1, 0)]
                + [wcol(taps, k) for k in orders] + [wcol(1, k) for k in orders]
                + [hspec(length), hspec(length), hspec(8),
                   pl.BlockSpec((None, HY_ORDER, 1, LANE_TILE), lambda c, b: (layer, 0, 0, c)),
                   const((m, m)), const((m, m)), const((8, m)), const((m, LANE_TILE)), const((m, LANE_TILE))])
    args = ([u] * (1 + HY_ORDER) + [conv_w, conv_b] + [conv_w] * HY_ORDER + [conv_b] * HY_ORDER
            + [ha, hb, hm, skip, cm, sm, alt8, twc, tws])
    n_in = len(args)
    aliases = {}
    if out_prev is not None:
        in_specs.append(pl.BlockSpec(memory_space=pl.ANY))
        args.append(out_prev)
        aliases = {len(args) - 1: 0}

    def body(*refs):
        k = HY_ORDER
        z_ref, g_refs = refs[0], refs[1:1 + k]
        wz_ref, bz_ref = refs[1 + k], refs[2 + k]
        wg_refs, bg_refs = refs[3 + k:3 + 2 * k], refs[3 + 2 * k:3 + 3 * k]
        _hy_conv_kernel(z_ref, g_refs, wz_ref, bz_ref, wg_refs, bg_refs, *refs[3 + 3 * k:n_in], *refs[-2:])

    return pl.pallas_call(
        body,
        out_shape=jax.ShapeDtypeStruct((nb, s, ch), BF16),
        grid=(nct, nb),
        in_specs=in_specs,
        out_specs=pl.BlockSpec((None, length, LANE_TILE), lambda c, b: (b, row_block, c)),
        scratch_shapes=[pltpu.VMEM((LANE_TILE // 128, length, 128), F32)],
        input_output_aliases=aliases,
        compiler_params=_params(("parallel", "parallel")),
        name="hyena_conv_%d" % length,
    )(*args)


def _rms(x, w):
    return x * lax.rsqrt(jnp.mean(x * x, axis=-1, keepdims=True) + EPS) * w


def _merge_mlp_kernel(yf_ref, yb_ref, xs_ref, zg_ref, dsk_ref, sn_ref, ym_ref, yh_ref, wb_ref, wo_ref,
                      x_ref, xc_ref, mod_ref, nw_ref, w1_ref, w2_ref, nf_ref, o_ref, *, final, n_lat_tiles):
    m = mod_ref[...]
    bw = yf_ref.shape[1]
    ys = yf_ref[...].astype(F32) + yb_ref[...].astype(F32) + dsk_ref[...] * xs_ref[...].astype(F32)
    ys = _rms(ys * _silu(zg_ref[:, 0:bw].astype(F32)), sn_ref[...]).astype(BF16)
    acc = None
    for n, y in enumerate((ys, ym_ref[...].astype(BF16), yh_ref[...])):
        gate = zg_ref[:, (1 + n) * bw:(2 + n) * bw].astype(F32)
        term = jax.nn.sigmoid(gate) * _dot(y, wb_ref[n])
        acc = term if acc is None else acc + term
    x_in = x_ref[...] if xc_ref is None else jnp.where(pl.program_id(1) < n_lat_tiles, x_ref[...], xc_ref[...])
    x = x_in + m[2:3] * _dot(acc.astype(BF16), wo_ref[...])
    h = (_rms(x, nw_ref[...]) * (1.0 + m[4:5]) + m[3:4]).astype(BF16)
    a = jnp.maximum(_dot(h, w1_ref[...]), 0.0)
    x = x + m[5:6] * _dot((a * a).astype(BF16), w2_ref[...])
    o_ref[...] = _rms(x, nf_ref[...]) if final else x


def _merge_mlp(yf, yb, xbc, pnat, d_full, ssd_norm, ym, yh, wb, wo, x, x_ctx, mod, norm_w, layer, w1, w2,
               norm_f, final, n_lat_tiles, n_tiles):
    nb, _, d = x.shape
    s = yf.shape[1]
    bw = yf.shape[-1]
    hidden = w1.shape[1]
    row = pl.BlockSpec((None, TOKEN_TILE, bw), lambda b, i: (b, i, 0))
    assert bw == d
    zg = pl.BlockSpec((None, TOKEN_TILE, (1 + N_BRANCH) * d), lambda b, i: (b, i, 0))
    xrow = pl.BlockSpec((None, TOKEN_TILE, d), lambda b, i: (b, i, 0))
    lay = lambda n: pl.BlockSpec((None, 1, n), lambda b, i: (layer, 0, 0))
    single = dict(pipeline_mode=pl.Buffered(1))
    const = lambda shape: pl.BlockSpec(shape, lambda b, i: (0,) * len(shape), **single)
    in_specs = [row, row, row, zg, lay(bw), lay(bw), row, row, const((N_BRANCH, bw, d)), const((d, d))]
    args = [yf, yb, xbc, pnat, d_full, ssd_norm, ym, yh, wb, wo]
    aliases = {}
    if x_ctx is None:
        in_specs.append(xrow)
        args.append(x)
        if not final:
            aliases = {len(args) - 1: 0}
    else:
        in_specs += [pl.BlockSpec((None, TOKEN_TILE, d), lambda b, i: (b, jnp.minimum(i, n_lat_tiles - 1), 0)),
                     pl.BlockSpec((None, TOKEN_TILE, d), lambda b, i: (b, jnp.maximum(i - n_lat_tiles, 0), 0))]
        args += [x, x_ctx]
    in_specs += [pl.BlockSpec((None, 6, d), _mod_row(nb, n_lat_tiles)), lay(d),
                 const((d, hidden)), const((hidden, d)), pl.BlockSpec((1, d), lambda b, i: (0, 0))]
    args += [mod, norm_w, w1, w2, norm_f]

    def body(*refs):
        head, tail = refs[:11], refs[11:]
        xc = None if x_ctx is None else tail[0]
        _merge_mlp_kernel(*head, xc, *tail[0 if x_ctx is None else 1:], final=final, n_lat_tiles=n_lat_tiles)

    return pl.pallas_call(
        body,
        out_shape=jax.ShapeDtypeStruct((nb, n_tiles * TOKEN_TILE if final else s, d), F32),
        grid=(nb, n_tiles),
        in_specs=in_specs,
        out_specs=xrow,
        input_output_aliases=aliases,
        compiler_params=_params(("parallel", "parallel")),
        name="merge_mlp",
    )(*args)


def _dir_rows(g, nb, s, per_dir):
    g = g[:, :2 * per_dir].reshape(nb, s, 2, per_dir)
    return jnp.transpose(g, (2, 0, 3, 1))


def kernel(x, c, ctx, c_ctx, norm1_w, mod_w, mod_b, w_in, ssd_conv_w, ssd_conv_b, ssd_dt_bias, ssd_a_log,
           ssd_d, ssd_norm_w, ml_conv_w, ml_conv_b, ml_gate_b, ml_norm_w, hy_conv_w, hy_conv_b, hy_ffn_w1,
           hy_ffn_b1, hy_ffn_w2, hy_ffn_b2, hy_ffn_w3, hy_decay, hy_skip, w_branch, w_out, norm2_w,
           mlp_w1, mlp_w2, norm_f_w):
    nb, seq, d = x.shape
    ctx_len = ctx.shape[1]
    depth = w_in.shape[0]
    s = seq + ctx_len
    assert seq % ctx_len == 0 and ctx_len % TOKEN_TILE == 0 and seq % GRID_W == 0
    n_lat_tiles, n_tiles = seq // TOKEN_TILE, s // TOKEN_TILE
    n_lat_ch, n_ch = seq // CHUNK, s // CHUNK
    assert n_lat_ch % SCAN_CHUNKS == 0 and n_ch % SCAN_CHUNKS == 0
    rows_g = seq // GRID_W

    ssd_heads = ssd_d.shape[-1]
    ssd_inner = ssd_norm_w.shape[-1]
    ssd_hd = ssd_inner // ssd_heads
    ssd_conv_ch = ssd_conv_w.shape[-1]
    ssd_ds = (ssd_conv_ch - ssd_inner) // (2 * SSD_GROUPS)
    ml_heads = ml_gate_b.shape[-1]
    ml_inner = ml_heads * ML_HEADDIM
    hy_inner = hy_skip.shape[-1]
    ssd_cols = ssd_conv_ch + ssd_inner + 2 * ssd_heads
    ml_cols = 4 * ml_inner + 4 * ml_heads
    rec_cols = ssd_cols + ml_cols
    hy_cols = (HY_ORDER + 1) * hy_inner

    o_z = ssd_conv_ch
    o_dt = ssd_conv_ch + ssd_inner
    o_ml = ssd_cols
    o_mlg = ssd_cols + 4 * ml_inner
    o_hy = rec_cols
    o_g = rec_cols + hy_cols

    xa, xa_ctx = x, ctx
    rpad = (-(nb + 1)) % 8
    c_all = jnp.concatenate([c, c_ctx[None], jnp.zeros((rpad, d), F32)], axis=0)

    tab_lat = _dft_tables(seq)
    tab_ctx = _dft_tables(ctx_len)
    k_scale = jnp.concatenate([jnp.ones((1, ml_inner), F32),
                               jnp.full((1, ml_inner), ML_HEADDIM ** -0.5, F32)], axis=1)

    norm1 = norm1_w[:, None, :]
    norm2 = norm2_w[:, None, :]
    ssd_norm = ssd_norm_w[:, None, :]
    ml_norm = ml_norm_w[:, None, :]
    d_full = jnp.repeat(ssd_d, ssd_hd, axis=-1)[:, None, :]
    w3 = hy_ffn_w3.reshape(depth, hy_ffn_w3.shape[1], HY_ORDER, 2, hy_inner).transpose(0, 2, 3, 1, 4)
    decay = hy_decay[:, :, :, None, :]
    skip = hy_skip[:, :, None, :]
    hb1 = hy_ffn_b1[:, None, :]
    hb2 = hy_ffn_b2[:, None, :]

    for l in range(depth):
        need_ctx = l < depth - 1
        used_tiles = n_tiles if need_ctx else n_lat_tiles
        mod = _mod_vectors(c_all, mod_w, mod_b[:, None, :], l).reshape(-1, 6, d)

        wl = w_in[l].astype(BF16)
        w_nat = jnp.concatenate([wl[:, o_z:o_dt], wl[:, o_g:], wl[:, o_hy:o_g]], axis=1)
        w_xbc = wl[:, :o_z]
        w_ml = wl[:, o_ml:o_mlg]
        gpad = lambda w: jnp.pad(w, ((0, 0), (0, 128 - w.shape[1])))
        hn, hn_cm = _normmod(xa, xa_ctx, norm1, l, mod, 0, rows_g, n_lat_tiles, n_tiles)
        pnat = _matmul(hn.reshape(nb * s, d), w_nat, BF16, "proj_nat").reshape(nb, s, -1)
        c_g = ssd_inner
        c_hy = c_g + N_BRANCH * d

        xbc, p_dt = _proj_conv(hn, w_xbc, gpad(wl[:, o_dt:o_ml]), ssd_conv_w[l], ssd_conv_b[l][None], None,
                               n_lat_tiles, n_tiles, "proj_xbc")
        dt_r = _dir_rows(p_dt.reshape(nb * s, -1), nb, s, ssd_heads)
        y_f, y_b = _ssd_scan(xbc, dt_r, ssd_dt_bias[l][:, :, None], ssd_a_log[l][:, :, None],
                             ssd_heads, ssd_hd, ssd_ds, n_lat_ch, n_ch)

        pml, p_mlg = _proj_conv(hn_cm, w_ml, gpad(wl[:, o_mlg:o_hy]), ml_conv_w[l], ml_conv_b[l][None],
                                k_scale, n_lat_tiles, n_tiles, "proj_ml")
        g_r = _dir_rows(p_mlg.reshape(nb * s, -1), nb, s, 2 * ml_heads)
        gate_b = ml_gate_b[l]
        c_f, c_b = _ml_scan(pml, pml, g_r[:, :, :ml_heads], g_r[:, :, ml_heads:],
                            gate_b[:, 0, :, None], gate_b[:, 1, :, None], ml_heads, n_lat_ch, n_ch)
        ym = _ml_out(c_f, c_b, pml, ml_norm, l, ml_heads, rows_g, n_lat_tiles, used_tiles)

        hcw, hcb = hy_conv_w[l], hy_conv_b[l][None]
        fl = _hy_filters(seq, l, hy_ffn_w1, hb1, hy_ffn_w2, hb2, w3, decay, tab_lat)
        rows = s if need_ctx else seq
        yh = _hy_conv(pnat, c_hy, hy_inner, hcw, hcb, None, rows, 0, seq, l, *fl, skip, tab_lat)
        if need_ctx:
            fc = _hy_filters(ctx_len, l, hy_ffn_w1, hb1, hy_ffn_w2, hb2, w3, decay, tab_ctx)
            yh = _hy_conv(pnat, c_hy, hy_inner, hcw, hcb, yh, rows, seq // ctx_len, ctx_len, l, *fc, skip,
                          tab_ctx)

        xa = _merge_mlp(y_f, y_b, xbc, pnat, d_full, ssd_norm, ym, yh, w_branch[l].astype(BF16),
                        w_out[l].astype(BF16), xa, xa_ctx, mod, norm2, l, mlp_w1[l].astype(BF16),
                        mlp_w2[l].astype(BF16), norm_f_w[None], not need_ctx, n_lat_tiles, used_tiles)
        xa_ctx = None

    return xa
```
